```python
import jax
import jax.numpy as jnp
from jax import lax
import numpy as np

D_MODEL = 1024
BATCH = 8
SEQ = 4096
DEPTH = 1

GRID_W = 64
CTX_LEN = 256
HEAD_DIM = 64
ATT_HEADS = 8
ATT_KV_HEADS = 2
ATT_GROUPS = ATT_HEADS // ATT_KV_HEADS
ATT_DIM = ATT_HEADS * HEAD_DIM
KV_DIM = ATT_KV_HEADS * HEAD_DIM
WINDOW = 128
BLOCK = 128
ROPE_BASE = 10000.0
ATT_SCALE = HEAD_DIM ** -0.5
NEG_INF = -1e30
RWKV_HEADS = 8
RWKV_DIM = RWKV_HEADS * HEAD_DIM
DECAY_LORA = 64
ICL_LORA = 64
GATE_LORA = 128
GN_EPS = 64e-5
N_EXPERTS = 32
TOP_K = 4
D_EXPERT = D_MODEL
SWIGLU_LIMIT = 7.0
SWIGLU_ALPHA = 1.702
MOE_BLOCK = 128
RMS_EPS = 1e-6
ATT_COLS = ATT_DIM + 2 * KV_DIM
RWKV_SIZES = (RWKV_DIM, RWKV_DIM, RWKV_DIM, DECAY_LORA, DECAY_LORA, ICL_LORA, ICL_LORA, GATE_LORA)
RWKV_COLS = sum(RWKV_SIZES)
RWKV_OFFSETS = tuple(int(o) for o in np.cumsum(RWKV_SIZES)[:-1])
GATE_COLS = 2 * D_MODEL
PROJ_COLS = ATT_COLS + RWKV_COLS + GATE_COLS
F32 = jnp.float32

kernel_name = 'hybrid_gqa_rwkv7_moe_diffusion_layer'


def rms_norm(x, g):
    xf = x.astype(F32)
    y = xf * lax.rsqrt(jnp.mean(xf * xf, axis=-1, keepdims=True) + RMS_EPS)
    return y.astype(x.dtype) * g


def modulate(h, shift, scale):
    return h * (1.0 + scale) + shift


def _rotate_half(t, ang):
    cos = jnp.cos(ang)[None, :, None, :].astype(t.dtype)
    sin = jnp.sin(ang)[None, :, None, :].astype(t.dtype)
    t1, t2 = jnp.split(t, 2, axis=-1)
    return jnp.concatenate([t1 * cos - t2 * sin, t2 * cos + t1 * sin], axis=-1)


def axial_rope(t):
    L = t.shape[1]
    n_rows = L // GRID_W
    row = jnp.repeat(jnp.arange(n_rows, dtype=F32), GRID_W, total_repeat_length=L)
    col = jnp.tile(jnp.arange(GRID_W, dtype=F32), n_rows)
    half = HEAD_DIM // 2
    inv_freq = ROPE_BASE ** (-jnp.arange(0, half, 2, dtype=F32) / half)
    t_row, t_col = jnp.split(t, 2, axis=-1)
    return jnp.concatenate([_rotate_half(t_row, row[:, None] * inv_freq[None, :]),
                            _rotate_half(t_col, col[:, None] * inv_freq[None, :])], axis=-1)


def token_shift(z, mu_prev, mu_next):
    prev = jnp.pad(z[:, :-1], ((0, 0), (1, 0), (0, 0)))
    nxt = jnp.pad(z[:, 1:], ((0, 0), (0, 1), (0, 0)))
    return z + mu_prev * (prev - z) + mu_next * (nxt - z)


def rwkv_features(z, p):
    B, L, _ = z.shape
    r, k, v, wl_f, wl_b, al_f, al_b, gl = jnp.split(z, RWKV_OFFSETS, axis=-1)

    def heads(t):
        return t.reshape(B, L, RWKV_HEADS, HEAD_DIM)

    def decay(w_low, w0, w2):
        w = (w0 + jnp.tanh(w_low) @ w2).astype(F32)
        w = -jax.nn.softplus(-w) - 0.5
        return heads(jnp.exp(-jnp.exp(w)))

    icl_f = jax.nn.sigmoid(p['a0_f'] + al_f @ p['a2_f'])
    icl_b = jax.nn.sigmoid(p['a0_b'] + al_b @ p['a2_b'])
    kk = heads(k * p['k_k']).astype(F32)
    kk = (kk / jnp.maximum(jnp.sqrt(jnp.sum(kk * kk, axis=-1, keepdims=True)), 1e-12)).astype(z.dtype)
    return dict(
        r=heads(r), v=heads(v),
        k_f=heads(k * (1.0 + (icl_f - 1.0) * p['k_a'])),
        k_b=heads(k * (1.0 + (icl_b - 1.0) * p['k_a'])),
        dec_f=decay(wl_f, p['w0_f'], p['w2_f']),
        dec_b=decay(wl_b, p['w0_b'], p['w2_b']),
        a=-kk, b_f=kk * heads(icl_f), b_b=kk * heads(icl_b),
        gate=jax.nn.sigmoid(gl) @ p['g2'])


def wkv_scan(r, w, k, v, a, b, state0, reverse):
    def step(S, inp):
        r_t, w_t, k_t, v_t, a_t, b_t = inp
        sa = jnp.einsum('bhij,bhj->bhi', S, a_t)
        S = S * w_t[:, :, None, :] + sa[..., None] * b_t[:, :, None, :] + v_t[..., None] * k_t[:, :, None, :]
        return S, jnp.einsum('bhij,bhj->bhi', S, r_t)

    xs = tuple(jnp.moveaxis(t.astype(F32), 1, 0) for t in (r, w, k, v, a, b))
    S, ys = lax.scan(step, state0, xs, reverse=reverse)
    return jnp.moveaxis(ys, 0, 1), S


def bidir_scan(f, s0_f, s0_b):
    y_f, s_f = wkv_scan(f['r'], f['dec_f'], f['k_f'], f['v'], f['a'], f['b_f'], s0_f, False)
    y_b, s_b = wkv_scan(f['r'], f['dec_b'], f['k_b'], f['v'], f['a'], f['b_b'], s0_b, True)
    return y_f + y_b, s_f, s_b


def rwkv_output(y, f, p):
    B, L = y.shape[:2]
    mean = jnp.mean(y, axis=-1, keepdims=True)
    var = jnp.mean(jnp.square(y - mean), axis=-1, keepdims=True)
    yn = ((y - mean) * lax.rsqrt(var + GN_EPS)).reshape(B, L, RWKV_DIM)
    yn = (yn * p['ln_x_w'] + p['ln_x_b']).astype(f['v'].dtype)
    bonus = jnp.sum(f['r'] * (f['k_f'] + f['k_b']) * p['r_k'], axis=-1, keepdims=True) * f['v']
    return (yn + bonus.reshape(B, L, RWKV_DIM)) * f['gate']


def project(h, p):
    B, L, _ = h.shape
    z = h @ p['w_in'] + p['b_in']
    z_att, z_rwkv, z_gate = jnp.split(z, [ATT_COLS, ATT_COLS + RWKV_COLS], axis=-1)
    q, k, v = jnp.split(z_att, [ATT_DIM, ATT_DIM + KV_DIM], axis=-1)
    q = q.reshape(B, L, ATT_HEADS, HEAD_DIM)
    k = k.reshape(B, L, ATT_KV_HEADS, HEAD_DIM)
    v = v.reshape(B, L, ATT_KV_HEADS, HEAD_DIM)
    feats = rwkv_features(token_shift(z_rwkv, p['mu_prev'], p['mu_next']), p)
    return q, k, v, feats, z_gate


def sink_softmax(logits, sinks):
    sink = jnp.broadcast_to(sinks.reshape(ATT_KV_HEADS, ATT_GROUPS, 1, 1).astype(F32), logits.shape[:-1] + (1,))
    return jax.nn.softmax(jnp.concatenate([logits, sink], axis=-1), axis=-1)[..., :-1]


def window_attention(q, k, v, k_ctx, v_ctx, sinks):
    B, L = q.shape[:2]
    nb = L // BLOCK
    band = BLOCK + 2 * WINDOW
    qb = jnp.moveaxis(q.reshape(B, nb, BLOCK, ATT_KV_HEADS, ATT_GROUPS, HEAD_DIM), 1, 0)
    kp = jnp.pad(k, ((0, 0), (WINDOW, WINDOW), (0, 0), (0, 0)))
    vp = jnp.pad(v, ((0, 0), (WINDOW, WINDOW), (0, 0), (0, 0)))
    k_off = jnp.arange(band) - WINDOW
    rel = k_off[None, :] - jnp.arange(BLOCK)[:, None]

    def one_block(args):
        n, q_n = args
        k_n = lax.dynamic_slice_in_dim(kp, n * BLOCK, band, axis=1)
        v_n = lax.dynamic_slice_in_dim(vp, n * BLOCK, band, axis=1)
        kpos = n * BLOCK + k_off
        valid = (jnp.abs(rel) <= WINDOW) & ((kpos >= 0) & (kpos < L))[None, :]
        s_lat = jnp.einsum('bqkgd,bskd->bkgqs', q_n, k_n).astype(F32) * ATT_SCALE
        s_lat = jnp.where(valid, s_lat, NEG_INF)
        s_ctx = jnp.einsum('bqkgd,bckd->bkgqc', q_n, k_ctx).astype(F32) * ATT_SCALE
        prob = sink_softmax(jnp.concatenate([s_lat, s_ctx], axis=-1), sinks).astype(v.dtype)
        return (jnp.einsum('bkgqs,bskd->bqkgd', prob[..., :band], v_n)
                + jnp.einsum('bkgqc,bckd->bqkgd', prob[..., band:], v_ctx))

    o = lax.map(one_block, (jnp.arange(nb), qb))
    return jnp.moveaxis(o, 0, 1).reshape(B, L, ATT_DIM)


def context_attention(q, k, v, sinks):
    B, C = q.shape[:2]
    qg = q.reshape(B, C, ATT_KV_HEADS, ATT_GROUPS, HEAD_DIM)
    s = jnp.einsum('bqkgd,bckd->bkgqc', qg, k).astype(F32) * ATT_SCALE
    prob = sink_softmax(s, sinks).astype(v.dtype)
    return jnp.einsum('bkgqc,bckd->bqkgd', prob, v).reshape(B, C, ATT_DIM)


def merge_branches(att, rwk, z_gate, p):
    g_att, g_rwkv = jnp.split(jax.nn.sigmoid(z_gate), 2, axis=-1)
    merged = g_att * (att @ p['w_up_att']) + g_rwkv * (rwk @ p['w_up_rwkv'])
    return merged @ p['w_out']


def moe_ffn(h, p):
    B, L, D = h.shape
    xt = h.reshape(-1, D)
    T = xt.shape[0]
    TK = T * TOP_K
    logits = (xt @ p['w_router'] + p['b_router']).astype(F32)
    top_val, top_idx = lax.top_k(logits, TOP_K)
    gates = jax.nn.softmax(top_val, axis=-1).astype(h.dtype)
    flat_e = top_idx.reshape(-1).astype(jnp.int32)
    flat_tok = jnp.repeat(jnp.arange(T, dtype=jnp.int32), TOP_K)
    order = jnp.argsort(flat_e)
    e_sorted = flat_e[order]
    counts = jnp.zeros((N_EXPERTS,), jnp.int32).at[flat_e].add(1)
    padded = ((counts + MOE_BLOCK - 1) // MOE_BLOCK) * MOE_BLOCK
    start = jnp.cumsum(counts) - counts
    p_end = jnp.cumsum(padded)
    p_start = p_end - padded
    dest = p_start[e_sorted] + jnp.arange(TK, dtype=jnp.int32) - start[e_sorted]
    n_blocks = -(-TK // MOE_BLOCK) + N_EXPERTS
    n_slots = n_blocks * MOE_BLOCK
    slot_tok = jnp.full((n_slots,), T, jnp.int32).at[dest].set(flat_tok[order])
    slot_g = jnp.zeros((n_slots,), h.dtype).at[dest].set(gates.reshape(-1)[order])
    block_exp = jnp.minimum(jnp.searchsorted(p_end, jnp.arange(n_blocks, dtype=jnp.int32) * MOE_BLOCK, side='right'),
                            N_EXPERTS - 1).astype(jnp.int32)
    x_pad = jnp.concatenate([xt, jnp.zeros((1, D), xt.dtype)], axis=0)
    xb = x_pad[slot_tok].reshape(n_blocks, MOE_BLOCK, D)
    w_gu, b_gu, w_dn, b_dn = p['w_gate_up'], p['b_gate_up'], p['w_down'], p['b_down']

    def expert_block(args):
        e, x_b = args
        gu = x_b @ w_gu[e] + b_gu[e]
        gate = jnp.minimum(gu[..., ::2], SWIGLU_LIMIT)
        up = jnp.clip(gu[..., 1::2], -SWIGLU_LIMIT, SWIGLU_LIMIT)
        return ((up + 1.0) * (gate * jax.nn.sigmoid(SWIGLU_ALPHA * gate))) @ w_dn[e] + b_dn[e]

    yb = lax.map(expert_block, (block_exp, xb)).reshape(n_slots, D)
    y = jnp.zeros((T + 1, D), yb.dtype).at[slot_tok].add(yb * slot_g[:, None])[:T]
    return y.reshape(B, L, D)


def setup_inputs(seed: int = 0) -> dict:
    key = jax.random.key(seed)
    ks = iter(jax.random.split(key, 48))
    D = D_MODEL

    def nrm(shape, scale):
        return scale * jax.random.normal(next(ks), shape, F32)

    def gain(shape):
        return 1.0 + nrm(shape, 0.05)

    return {
        'x': nrm((BATCH, SEQ, D), 1.0),
        'c': nrm((BATCH, D), 1.0),
        'ctx': nrm((BATCH, CTX_LEN, D), 1.0),
        'c_ctx': nrm((D,), 1.0),
        'w_ada': nrm((DEPTH, D, 6 * D), 0.5 * D ** -0.5),
        'b_ada': nrm((DEPTH, 6 * D), 0.02),
        'g_pre_mix': gain((DEPTH, D)),
        'g_post_mix': gain((DEPTH, D)),
        'g_pre_ffn': gain((DEPTH, D)),
        'g_post_ffn': gain((DEPTH, D)),
        'w_in': nrm((DEPTH, D, PROJ_COLS), D ** -0.5),
        'b_in': nrm((DEPTH, PROJ_COLS), 0.02),
        'mu_prev': jax.random.uniform(next(ks), (DEPTH, RWKV_COLS), F32, 0.0, 0.5),
        'mu_next': jax.random.uniform(next(ks), (DEPTH, RWKV_COLS), F32, 0.0, 0.5),
        'att_sinks': nrm((DEPTH, ATT_HEADS), 0.5),
        'w0_f': jax.random.uniform(next(ks), (DEPTH, RWKV_DIM), F32, -5.0, 0.0),
        'w0_b': jax.random.uniform(next(ks), (DEPTH, RWKV_DIM), F32, -5.0, 0.0),
        'w2_f': nrm((DEPTH, DECAY_LORA, RWKV_DIM), 0.5 * DECAY_LORA ** -0.5),
        'w2_b': nrm((DEPTH, DECAY_LORA, RWKV_DIM), 0.5 * DECAY_LORA ** -0.5),
        'a0_f': nrm((DEPTH, RWKV_DIM), 0.5),
        'a0_b': nrm((DEPTH, RWKV_DIM), 0.5),
        'a2_f': nrm((DEPTH, ICL_LORA, RWKV_DIM), 0.5 * ICL_LORA ** -0.5),
        'a2_b': nrm((DEPTH, ICL_LORA, RWKV_DIM), 0.5 * ICL_LORA ** -0.5),
        'g2': nrm((DEPTH, GATE_LORA, RWKV_DIM), GATE_LORA ** -0.5),
        'k_k': 0.85 + nrm((DEPTH, RWKV_DIM), 0.05),
        'k_a': gain((DEPTH, RWKV_DIM)),
        'r_k': nrm((DEPTH, RWKV_HEADS, HEAD_DIM), 0.1),
        'ln_x_w': gain((DEPTH, RWKV_DIM)),
        'ln_x_b': nrm((DEPTH, RWKV_DIM), 0.02),
        'w_up_att': nrm((DEPTH, ATT_DIM, D), ATT_DIM ** -0.5),
        'w_up_rwkv': nrm((DEPTH, RWKV_DIM, D), RWKV_DIM ** -0.5),
        'w_out': nrm((DEPTH, D, D), D ** -0.5),
        'w_router': nrm((DEPTH, D, N_EXPERTS), D ** -0.5),
        'b_router': nrm((DEPTH, N_EXPERTS), 0.01),
        'w_gate_up': nrm((DEPTH, N_EXPERTS, D, 2 * D_EXPERT), D ** -0.5),
        'b_gate_up': nrm((DEPTH, N_EXPERTS, 2 * D_EXPERT), 0.02),
        'w_down': nrm((DEPTH, N_EXPERTS, D_EXPERT, D), D_EXPERT ** -0.5),
        'b_down': nrm((DEPTH, N_EXPERTS, D), 0.02),
    }


def reference(x, c, ctx, c_ctx, w_ada, b_ada, g_pre_mix, g_post_mix, g_pre_ffn, g_post_ffn,
              w_in, b_in, mu_prev, mu_next, att_sinks, w0_f, w0_b, w2_f, w2_b, a0_f, a0_b,
              a2_f, a2_b, g2, k_k, k_a, r_k, ln_x_w, ln_x_b, w_up_att, w_up_rwkv, w_out,
              w_router, b_router, w_gate_up, b_gate_up, w_down, b_down):
    B = x.shape[0]
    silu_c = jax.nn.silu(c)
    silu_cc = jax.nn.silu(c_ctx)
    for l in range(DEPTH):
        p = dict(w_in=w_in[l], b_in=b_in[l], mu_prev=mu_prev[l], mu_next=mu_next[l], sinks=att_sinks[l],
                 w0_f=w0_f[l], w0_b=w0_b[l], w2_f=w2_f[l], w2_b=w2_b[l], a0_f=a0_f[l], a0_b=a0_b[l],
                 a2_f=a2_f[l], a2_b=a2_b[l], g2=g2[l], k_k=k_k[l], k_a=k_a[l], r_k=r_k[l],
                 ln_x_w=ln_x_w[l], ln_x_b=ln_x_b[l], w_up_att=w_up_att[l], w_up_rwkv=w_up_rwkv[l],
                 w_out=w_out[l], w_router=w_router[l], b_router=b_router[l], w_gate_up=w_gate_up[l],
                 b_gate_up=b_gate_up[l], w_down=w_down[l], b_down=b_down[l])
        mx = [m[:, None, :] for m in jnp.split(silu_c @ w_ada[l] + b_ada[l], 6, axis=-1)]
        mc = jnp.split(silu_cc @ w_ada[l] + b_ada[l], 6, axis=-1)
        hc = modulate(rms_norm(ctx, g_pre_mix[l]), mc[0], mc[1])
        qc, kc, vc, fc, zc_gate = project(hc, p)
        zero_state = jnp.zeros((B, RWKV_HEADS, HEAD_DIM, HEAD_DIM), F32)
        yc, sc_f, sc_b = bidir_scan(fc, zero_state, zero_state)
        hx = modulate(rms_norm(x, g_pre_mix[l]), mx[0], mx[1])
        q, k, v, fx, zx_gate = project(hx, p)
        att_x = window_attention(axial_rope(q), axial_rope(k), v, kc, vc, p['sinks'])
        yx, _, _ = bidir_scan(fx, sc_f, sc_b)
        mix_x = merge_branches(att_x, rwkv_output(yx, fx, p), zx_gate, p)
        x = x + mx[2] * rms_norm(mix_x, g_post_mix[l])
        hf = modulate(rms_norm(x, g_pre_ffn[l]), mx[3], mx[4])
        x = x + mx[5] * rms_norm(moe_ffn(hf, p), g_post_ffn[l])
        if l + 1 < DEPTH:
            mix_c = merge_branches(context_attention(qc, kc, vc, p['sinks']), rwkv_output(yc, fc, p), zc_gate, p)
            ctx = ctx + mc[2] * rms_norm(mix_c, g_post_mix[l])
            hcf = modulate(rms_norm(ctx, g_pre_ffn[l]), mc[3], mc[4])
            ctx = ctx + mc[5] * rms_norm(moe_ffn(hcf, p), g_post_ffn[l])
    return x
```

```python
import functools

import jax
import jax.numpy as jnp
import numpy as np
from jax import lax
from jax.experimental import pallas as pl
from jax.experimental.pallas import tpu as pltpu

F32 = jnp.float32
BF16 = jnp.bfloat16

GRID_W = 64
HEAD_DIM = 64
ATT_HEADS = 8
ATT_KV_HEADS = 2
ATT_GROUPS = ATT_HEADS // ATT_KV_HEADS
ATT_DIM = ATT_HEADS * HEAD_DIM
KV_DIM = ATT_KV_HEADS * HEAD_DIM
WINDOW = 128
BLOCK = 128
ROPE_BASE = 10000.0
ATT_SCALE = HEAD_DIM ** -0.5
NEG_INF = -1e30
RWKV_HEADS = 8
RWKV_DIM = RWKV_HEADS * HEAD_DIM
DECAY_LORA = 64
ICL_LORA = 64
GATE_LORA = 128
GN_EPS = 64e-5
N_EXPERTS = 32
TOP_K = 4
SWIGLU_LIMIT = 7.0
SWIGLU_ALPHA = 1.702
RMS_EPS = 1e-6
ATT_COLS = ATT_DIM + 2 * KV_DIM
RWKV_COLS = 3 * RWKV_DIM + 2 * DECAY_LORA + 2 * ICL_LORA + GATE_LORA
ROPE_COLS = ATT_DIM + KV_DIM

LANES = 128
SUBLANES = 8
VMEM_LIMIT = 48 * 1024 * 1024

ROW_TILE = 256
SCAN_CHUNK = 64
MOE_ROWS = 512


def _cparams(*sem):
    return pltpu.CompilerParams(dimension_semantics=sem, vmem_limit_bytes=VMEM_LIMIT)


def _mm(a, b):
    return jnp.dot(a.astype(BF16), b.astype(BF16), preferred_element_type=F32)


def _mm_nt(a, b):
    return lax.dot_general(a.astype(BF16), b.astype(BF16), (((1,), (1,)), ((), ())),
                           preferred_element_type=F32)


def _split2(x):
    hi = x.astype(BF16)
    lo = (x - hi.astype(F32)).astype(BF16)
    return hi, lo


def _split3(x):
    hi = x.astype(BF16)
    r1 = x - hi.astype(F32)
    mid = r1.astype(BF16)
    lo = (r1 - mid.astype(F32)).astype(BF16)
    return hi, mid, lo


def _mm_f32(a, b):
    ah, al = _split2(a)
    bh, bl = _split2(b)
    d = functools.partial(jnp.dot, preferred_element_type=F32)
    return d(ah, bh) + d(ah, bl) + d(al, bh)


def _seg_sum(x, eseg):
    hi, lo = _split2(x)
    d = functools.partial(jnp.dot, preferred_element_type=F32)
    return d(hi, eseg) + d(lo, eseg)


def _rms(x):
    return x * lax.rsqrt(jnp.mean(x * x, axis=-1, keepdims=True) + RMS_EPS)


def _sigmoid(x):
    return 1.0 / (1.0 + jnp.exp(-x))


def _ada_kernel(c_ref, w_ref, b_ref, o_ref):
    c = c_ref[...]
    o_ref[...] = _mm_f32(c * _sigmoid(c), w_ref[...]) + b_ref[...]


def _ada(c_rows, w, b):
    m, d = c_rows.shape
    n = w.shape[1]
    tn = 1536
    return pl.pallas_call(
        _ada_kernel,
        grid=(n // tn,),
        in_specs=[pl.BlockSpec((m, d), lambda j: (0, 0)),
                  pl.BlockSpec((d, tn), lambda j: (0, j)),
                  pl.BlockSpec((1, tn), lambda j: (0, j))],
        out_specs=pl.BlockSpec((m, tn), lambda j: (0, j)),
        out_shape=jax.ShapeDtypeStruct((m, n), F32),
        compiler_params=_cparams("arbitrary"),
        name="ada",
    )(c_rows, w, b.reshape(1, n))


def _proj_kernel(*refs, rope):
    if rope:
        (x_ref, mod_ref, g_ref, wa_ref, wr_ref, wg_ref, ba_ref, br_ref, bg_ref, cos_ref, sin_ref,
         za_ref, zr_ref, zg_ref) = refs
    else:
        (x_ref, mod_ref, g_ref, wa_ref, wr_ref, wg_ref, ba_ref, br_ref, bg_ref,
         za_ref, zr_ref, zg_ref) = refs
    mod = mod_ref[0]
    h = _rms(x_ref[...]) * g_ref[...] * (1.0 + mod[1:2]) + mod[0:1]
    hb = h.astype(BF16)
    za = jnp.dot(hb, wa_ref[...], preferred_element_type=F32) + ba_ref[...]
    if rope:
        qk = za[:, :ROPE_COLS]
        lane = lax.broadcasted_iota(jnp.int32, qk.shape, 1)
        low = (lane % 32) < 16
        partner = jnp.where(low, pltpu.roll(qk, ROPE_COLS - 16, 1), pltpu.roll(qk, 16, 1))
        za_ref[:, :ROPE_COLS] = (qk * cos_ref[...] + partner * sin_ref[...]).astype(za_ref.dtype)
        za_ref[:, ROPE_COLS:] = za[:, ROPE_COLS:].astype(za_ref.dtype)
    else:
        za_ref[...] = za.astype(za_ref.dtype)
    zr_ref[...] = jnp.dot(hb, wr_ref[...], preferred_element_type=F32) + br_ref[...]
    zg_ref[...] = (jnp.dot(hb, wg_ref[...], preferred_element_type=F32) + bg_ref[...]).astype(zg_ref.dtype)


def _rope_tables(seq):
    n_rows = seq // GRID_W
    row = jnp.repeat(jnp.arange(n_rows, dtype=F32), GRID_W, total_repeat_length=seq)
    col = jnp.tile(jnp.arange(GRID_W, dtype=F32), n_rows)
    half = HEAD_DIM // 2
    inv_freq = ROPE_BASE ** (-jnp.arange(0, half, 2, dtype=F32) / half)
    ang_r = row[:, None] * inv_freq[None, :]
    ang_c = col[:, None] * inv_freq[None, :]
    cos_h = jnp.concatenate([jnp.cos(ang_r), jnp.cos(ang_r), jnp.cos(ang_c), jnp.cos(ang_c)], axis=1)
    sin_h = jnp.concatenate([-jnp.sin(ang_r), jnp.sin(ang_r), -jnp.sin(ang_c), jnp.sin(ang_c)], axis=1)
    reps = ROPE_COLS // HEAD_DIM
    return jnp.tile(cos_h, (1, reps)), jnp.tile(sin_h, (1, reps))


def _project(x2, mods, mod_row, seq, g_pre, w_att, w_rwkv, w_gate, b_att, b_rwkv, b_gate, rope):
    rows, d = x2.shape
    tm = ROW_TILE
    tps = seq // tm
    const = lambda i: (0, 0)
    in_specs = [pl.BlockSpec((tm, d), lambda i: (i, 0)),
                pl.BlockSpec((1, SUBLANES, d), lambda i: (mod_row(i // tps), 0, 0)),
                pl.BlockSpec((1, d), const),
                pl.BlockSpec(w_att.shape, const), pl.BlockSpec(w_rwkv.shape, const),
                pl.BlockSpec(w_gate.shape, const),
                pl.BlockSpec((1, ATT_COLS), const), pl.BlockSpec((1, RWKV_COLS), const),
                pl.BlockSpec((1, w_gate.shape[1]), const)]
    args = [x2, mods, g_pre, w_att, w_rwkv, w_gate, b_att, b_rwkv, b_gate]
    if rope:
        cos, sin = _rope_tables(seq)
        in_specs += [pl.BlockSpec((tm, ROPE_COLS), lambda i: (i % tps, 0))] * 2
        args += [cos, sin]
    return pl.pallas_call(
        functools.partial(_proj_kernel, rope=rope),
        grid=(rows // tm,),
        in_specs=in_specs,
        out_specs=[pl.BlockSpec((tm, ATT_COLS), lambda i: (i, 0)),
                   pl.BlockSpec((tm, RWKV_COLS), lambda i: (i, 0)),
                   pl.BlockSpec((tm, w_gate.shape[1]), lambda i: (i, 0))],
        out_shape=[jax.ShapeDtypeStruct((rows, ATT_COLS), BF16),
                   jax.ShapeDtypeStruct((rows, RWKV_COLS), F32),
                   jax.ShapeDtypeStruct((rows, w_gate.shape[1]), BF16)],
        compiler_params=_cparams("parallel"),
        name="proj",
    )(*args)


def _feat_kernel(z_ref, zp_ref, zn_ref, mup_ref, mun_ref, w2_ref, w0_ref, a2_ref, a0_ref, g2_ref,
                 kk_ref, ka_ref, eseg_ref,
                 r_o, v_o, kf_o, kb_o, lwf_o, lwb_o, kkn_o, bf_o, bb_o, gate_o, *, tiles_per_seq):
    ti = pl.program_id(0) % tiles_per_seq
    z = z_ref[...]
    tm = z.shape[0]
    row = lax.broadcasted_iota(jnp.int32, (tm, 1), 0)
    prev_halo = jnp.where(ti == 0, 0.0, zp_ref[SUBLANES - 1:SUBLANES, :])
    next_halo = jnp.where(ti == tiles_per_seq - 1, 0.0, zn_ref[0:1, :])
    prev = jnp.where(row == 0, prev_halo, pltpu.roll(z, 1, 0))
    nxt = jnp.where(row == tm - 1, next_halo, pltpu.roll(z, tm - 1, 0))
    zs = z + mup_ref[...] * (prev - z) + mun_ref[...] * (nxt - z)

    d = RWKV_DIM
    r = zs[:, 0:d]
    k = zs[:, d:2 * d]
    v = zs[:, 2 * d:3 * d]
    o = 3 * d
    wl = zs[:, o:o + 2 * DECAY_LORA]
    al = zs[:, o + 2 * DECAY_LORA:o + 2 * DECAY_LORA + 2 * ICL_LORA]
    gl = zs[:, o + 2 * DECAY_LORA + 2 * ICL_LORA:]

    w = w0_ref[...] + _mm(jnp.tanh(wl), w2_ref[...])
    nw = -w
    softplus = jnp.maximum(nw, 0.0) + jnp.log(1.0 + jnp.exp(-jnp.abs(nw)))
    lw = -jnp.exp(-softplus - 0.5)
    icl = _sigmoid(a0_ref[...] + _mm(al, a2_ref[...]))
    kk0 = k * kk_ref[...]
    ss = _seg_sum(kk0 * kk0, eseg_ref[...])
    kk = kk0 / jnp.maximum(jnp.sqrt(ss), 1e-12)
    ka = ka_ref[...]
    icl_f = icl[:, :d]
    icl_b = icl[:, d:]

    r_o[...] = r
    v_o[...] = v
    kf_o[...] = k * (1.0 + (icl_f - 1.0) * ka)
    kb_o[...] = k * (1.0 + (icl_b - 1.0) * ka)
    lwf_o[...] = lw[:, :d]
    lwb_o[...] = lw[:, d:]
    kkn_o[...] = kk
    bf_o[...] = kk * icl_f
    bb_o[...] = kk * icl_b
    gate_o[...] = _mm(_sigmoid(gl), g2_ref[...])


def _block_diag2(a, b):
    za = jnp.zeros_like(a)
    zb = jnp.zeros_like(b)
    return jnp.concatenate([jnp.concatenate([a, zb], axis=1), jnp.concatenate([za, b], axis=1)], axis=0)


def _seg_matrix():
    h = np.arange(RWKV_DIM) // HEAD_DIM
    return jnp.asarray((h[:, None] == h[None, :]).astype(np.float32), BF16)


def _features(z_rwkv, seq, p):
    rows = z_rwkv.shape[0]
    tm = ROW_TILE
    tps = seq // tm
    hb = tm // SUBLANES
    nhb = rows // SUBLANES
    d = RWKV_DIM
    const = lambda i: (0, 0)
    w2 = _block_diag2(p['w2_f'], p['w2_b']).astype(BF16)
    a2 = _block_diag2(p['a2_f'], p['a2_b']).astype(BF16)
    w0 = jnp.concatenate([p['w0_f'], p['w0_b']]).reshape(1, 2 * d)
    a0 = jnp.concatenate([p['a0_f'], p['a0_b']]).reshape(1, 2 * d)
    out_spec = pl.BlockSpec((tm, d), lambda i: (i, 0))
    return pl.pallas_call(
        functools.partial(_feat_kernel, tiles_per_seq=tps),
        grid=(rows // tm,),
        in_specs=[pl.BlockSpec((tm, RWKV_COLS), lambda i: (i, 0)),
                  pl.BlockSpec((SUBLANES, RWKV_COLS), lambda i: (jnp.maximum(i * hb - 1, 0), 0)),
                  pl.BlockSpec((SUBLANES, RWKV_COLS), lambda i: (jnp.minimum((i + 1) * hb, nhb - 1), 0)),
                  pl.BlockSpec((1, RWKV_COLS), const), pl.BlockSpec((1, RWKV_COLS), const),
                  pl.BlockSpec(w2.shape, const), pl.BlockSpec((1, 2 * d), const),
                  pl.BlockSpec(a2.shape, const), pl.BlockSpec((1, 2 * d), const),
                  pl.BlockSpec((GATE_LORA, d), const),
                  pl.BlockSpec((1, d), const), pl.BlockSpec((1, d), const),
                  pl.BlockSpec((d, d), const)],
        out_specs=[out_spec] * 10,
        out_shape=[jax.ShapeDtypeStruct((rows, d), F32)] * 10,
        compiler_params=_cparams("parallel"),
        name="feat",
    )(z_rwkv, z_rwkv, z_rwkv, p['mu_prev'].reshape(1, -1), p['mu_next'].reshape(1, -1),
      w2, w0, a2, a0, p['g2'].astype(BF16), p['k_k'].reshape(1, d), p['k_a'].reshape(1, d), _seg_matrix())


def _scan_kernel(rf, vf, kf, lwf, kkf, bf, rb, vb, kb, lwb, kkb, bb, s0f_ref, s0b_ref,
                 yf_ref, yb_ref, sTf_ref, sTb_ref, state_ref, *, chunk, n_chunks):
    c = pl.program_id(1)
    C = chunk

    @pl.when(c == 0)
    def _():
        state_ref[0] = s0f_ref[0]
        state_ref[1] = s0b_ref[0]

    ti = lax.broadcasted_iota(jnp.int32, (C, C), 0)
    si = lax.broadcasted_iota(jnp.int32, (C, C), 1)
    eye = (ti == si).astype(F32)
    n_double = int(np.log2(C)) - 1

    dirs = ((rf, vf, kf, lwf, kkf, bf, yf_ref), (rb, vb, kb, lwb, kkb, bb, yb_ref))
    for d, (r_ref, v_ref, k_ref, lw_ref, kk_ref, b_ref, y_ref) in enumerate(dirs):
        incl = (si <= ti) if d == 0 else (si >= ti)
        strict = (si < ti) if d == 0 else (si > ti)
        tri = incl.astype(BF16)
        lw = lw_ref[...]
        lh, lm, ll = _split3(lw)
        dd = functools.partial(jnp.dot, preferred_element_type=F32)
        cum = dd(tri, lh) + dd(tri, lm) + dd(tri, ll)
        cumx = cum - lw
        cum_end = cum[C - 1:C] if d == 0 else cum[0:1]
        e_neg = jnp.exp(-cum)
        e_end = jnp.exp(cum_end - cum)
        kkv = kk_ref[...]
        bv = b_ref[...]
        kv = k_ref[...]
        vv = v_ref[...]
        a_t = -kkv * jnp.exp(cumx)
        r_t = r_ref[...] * jnp.exp(cum)
        b_t = bv * e_neg
        k_t = kv * e_neg
        b_q = bv * e_end
        k_q = kv * e_end
        g_end = jnp.exp(cum_end)
        ys = []
        for h in range(RWKV_HEADS):
            sl = slice(h * HEAD_DIM, (h + 1) * HEAD_DIM)
            P = jnp.concatenate([a_t[:, sl], r_t[:, sl]], axis=0)
            Q = jnp.concatenate([b_t[:, sl], k_t[:, sl]], axis=0)
            G = _mm_nt(P, Q)
            a_ab = jnp.where(strict, G[:C, :C], 0.0)
            a_ak = jnp.where(strict, G[:C, C:], 0.0)
            a_rb = jnp.where(incl, G[C:, :C], 0.0)
            a_rk = jnp.where(incl, G[C:, C:], 0.0)
            T = eye + a_ab
            Pw = a_ab
            for _ in range(n_double):
                Pw = _mm(Pw, Pw)
                T = T + _mm(T, Pw)
            S0 = state_ref[d, h]
            PH = _mm_nt(P, S0)
            V = vv[:, sl]
            U = _mm(T, PH[:C] + _mm(a_ak, V))
            UV = jnp.concatenate([U, V], axis=0)
            ys.append(PH[C:] + _mm(jnp.concatenate([a_rb, a_rk], axis=1), UV))
            Qq = jnp.concatenate([b_q[:, sl], k_q[:, sl]], axis=0)
            state_ref[d, h] = S0 * g_end[:, sl] + _mm(UV.T, Qq)
        y_ref[...] = jnp.concatenate(ys, axis=1)

    @pl.when(c == n_chunks - 1)
    def _():
        sTf_ref[0] = state_ref[0]
        sTb_ref[0] = state_ref[1]


def _scan(f, batch, seq, s0_f, s0_b):
    C = SCAN_CHUNK
    nC = seq // C
    d = RWKV_DIM
    fwd = pl.BlockSpec((C, d), lambda b, c: (b * nC + c, 0))
    bwd = pl.BlockSpec((C, d), lambda b, c: (b * nC + nC - 1 - c, 0))
    st = pl.BlockSpec((1, RWKV_HEADS, HEAD_DIM, HEAD_DIM), lambda b, c: (b, 0, 0, 0))
    st_shape = jax.ShapeDtypeStruct((batch, RWKV_HEADS, HEAD_DIM, HEAD_DIM), F32)
    y_shape = jax.ShapeDtypeStruct((batch * seq, d), F32)
    return pl.pallas_call(
        functools.partial(_scan_kernel, chunk=C, n_chunks=nC),
        grid=(batch, nC),
        in_specs=[fwd] * 6 + [bwd] * 6 + [st, st],
        out_specs=[fwd, bwd, st, st],
        out_shape=[y_shape, y_shape, st_shape, st_shape],
        scratch_shapes=[pltpu.VMEM((2, RWKV_HEADS, HEAD_DIM, HEAD_DIM), F32)],
        compiler_params=_cparams("parallel", "arbitrary"),
        name="scan",
    )(f['r'], f['v'], f['kf'], f['lwf'], f['kk'], f['bf'],
      f['r'], f['v'], f['kb'], f['lwb'], f['kk'], f['bb'], s0_f, s0_b)


def _attn_kernel(sink_ref, q_ref, kp_ref, km_ref, kn_ref, vp_ref, vm_ref, vn_ref, kc_ref, vc_ref, o_ref,
                 *, n_blocks):
    n = pl.program_id(1)
    q = q_ref[...]
    qi = lax.broadcasted_iota(jnp.int32, (BLOCK, BLOCK), 0)
    kj = lax.broadcasted_iota(jnp.int32, (BLOCK, BLOCK), 1)
    ctx_len = kc_ref.shape[0]
    bias_prev = jnp.where((kj >= qi) & (n > 0), 0.0, NEG_INF)
    bias_next = jnp.where((kj <= qi) & (n < n_blocks - 1), 0.0, NEG_INF)
    valid1 = jnp.concatenate([bias_prev, jnp.zeros((BLOCK, BLOCK), F32), bias_next,
                              jnp.zeros((BLOCK, ctx_len), F32)], axis=1) == 0.0
    valid = jnp.concatenate([valid1] * ATT_GROUPS, axis=0)
    rowh = lax.broadcasted_iota(jnp.int32, (ATT_GROUPS * BLOCK, 1), 0) // BLOCK
    outs = [None] * ATT_HEADS
    for g in range(ATT_KV_HEADS):
        ks = slice(g * HEAD_DIM, (g + 1) * HEAD_DIM)
        K = jnp.concatenate([kp_ref[:, ks], km_ref[:, ks], kn_ref[:, ks], kc_ref[:, ks]], axis=0)
        V = jnp.concatenate([vp_ref[:, ks], vm_ref[:, ks], vn_ref[:, ks], vc_ref[:, ks]], axis=0)
        heads = [g * ATT_GROUPS + j for j in range(ATT_GROUPS)]
        Qs = jnp.concatenate([q[:, h * HEAD_DIM:(h + 1) * HEAD_DIM] for h in heads], axis=0)
        S = _mm_nt(Qs, K) * ATT_SCALE
        S = jnp.where(valid, S, NEG_INF)
        sink = jnp.zeros((ATT_GROUPS * BLOCK, 1), F32)
        for j, h in enumerate(heads):
            sink = jnp.where(rowh == j, sink_ref[h], sink)
        m = jnp.maximum(jnp.max(S, axis=-1, keepdims=True), sink)
        e = jnp.exp(S - m)
        den = jnp.sum(e, axis=-1, keepdims=True) + jnp.exp(sink - m)
        O = _mm(e / den, V)
        for j, h in enumerate(heads):
            outs[h] = O[j * BLOCK:(j + 1) * BLOCK]
    o_ref[...] = jnp.concatenate(outs, axis=1).astype(o_ref.dtype)


def _attention(z_att, zc_att, sinks, batch, seq):
    nb = seq // BLOCK
    ctx_len = zc_att.shape[0] // batch
    kcol = ATT_DIM // KV_DIM
    vcol = kcol + 1

    def kv_spec(col, off):
        return pl.BlockSpec((BLOCK, KV_DIM), lambda b, n: (b * nb + jnp.clip(n + off, 0, nb - 1), col))

    return pl.pallas_call(
        functools.partial(_attn_kernel, n_blocks=nb),
        grid=(batch, nb),
        in_specs=[pl.BlockSpec(memory_space=pltpu.SMEM),
                  pl.BlockSpec((BLOCK, ATT_DIM), lambda b, n: (b * nb + n, 0)),
                  kv_spec(kcol, -1), kv_spec(kcol, 0), kv_spec(kcol, 1),
                  kv_spec(vcol, -1), kv_spec(vcol, 0), kv_spec(vcol, 1),
                  pl.BlockSpec((ctx_len, KV_DIM), lambda b, n: (b, kcol)),
                  pl.BlockSpec((ctx_len, KV_DIM), lambda b, n: (b, vcol))],
        out_specs=pl.BlockSpec((BLOCK, ATT_DIM), lambda b, n: (b * nb + n, 0)),
        out_shape=jax.ShapeDtypeStruct((batch * seq, ATT_DIM), BF16),
        compiler_params=_cparams("parallel", "parallel"),
        name="attn",
    )(sinks, z_att, z_att, z_att, z_att, z_att, z_att, z_att, zc_att, zc_att)


def _merge_kernel(yf_ref, yb_ref, r_ref, kf_ref, kb_ref, v_ref, gate_ref, att_ref, zg_ref, x_ref, mod_ref,
                  lnw_ref, lnb_ref, rk_ref, eseg_ref, wua_ref, wur_ref, wo_ref, gpm_ref, gpf_ref,
                  wr_ref, br_ref,
                  x1_ref, hf_ref, idx_ref, gt_ref):
    eseg = eseg_ref[...]
    inv_n = 1.0 / HEAD_DIM
    y = yf_ref[...] + yb_ref[...]
    mean = _seg_sum(y, eseg) * inv_n
    dy = y - mean
    var = _seg_sum(dy * dy, eseg) * inv_n
    yn = dy * lax.rsqrt(var + GN_EPS) * lnw_ref[...] + lnb_ref[...]
    v = v_ref[...]
    bonus = _seg_sum(r_ref[...] * (kf_ref[...] + kb_ref[...]) * rk_ref[...], eseg) * v
    rwk = (yn + bonus) * gate_ref[...]

    d = x_ref.shape[1]
    zg = zg_ref[...].astype(F32)
    merged = (_sigmoid(zg[:, :d]) * jnp.dot(att_ref[...], wua_ref[...], preferred_element_type=F32)
              + _sigmoid(zg[:, d:]) * _mm(rwk, wur_ref[...]))
    mix = _mm(merged, wo_ref[...])
    mod = mod_ref[0]
    x1 = x_ref[...] + mod[2:3] * (_rms(mix) * gpm_ref[...])
    x1_ref[...] = x1
    hf = _rms(x1) * gpf_ref[...] * (1.0 + mod[4:5]) + mod[3:4]
    hf_ref[...] = hf.astype(hf_ref.dtype)

    logits = _mm_f32(hf, wr_ref[...]) + br_ref[...]
    lane = lax.broadcasted_iota(jnp.int32, logits.shape, 1).astype(F32)
    idx_out = jnp.zeros(logits.shape, F32)
    val_out = jnp.zeros(logits.shape, F32)
    top = None
    den = None
    for kth in range(TOP_K):
        m = jnp.max(logits, axis=-1, keepdims=True)
        sel = jnp.min(jnp.where(logits == m, lane, float(LANES)), axis=-1, keepdims=True)
        logits = jnp.where(lane == sel, -jnp.inf, logits)
        if kth == 0:
            top = m
        e = jnp.exp(m - top)
        den = e if kth == 0 else den + e
        idx_out = jnp.where(lane == float(kth), sel, idx_out)
        val_out = jnp.where(lane == float(kth), e, val_out)
    idx_ref[...] = idx_out.astype(jnp.int32)
    gt_ref[...] = val_out / den


def _merge(y_f, y_b, f, att, z_gate, x2, mods, seq, p):
    rows, d = x2.shape
    tm = ROW_TILE
    tps = seq // tm
    rd = RWKV_DIM
    const = lambda i: (0, 0)
    row = lambda w: pl.BlockSpec((tm, w), lambda i: (i, 0))
    vec = lambda w: pl.BlockSpec((1, w), const)
    w_router = jnp.pad(p['w_router'], ((0, 0), (0, LANES - N_EXPERTS)))
    b_router = jnp.pad(p['b_router'], (0, LANES - N_EXPERTS), constant_values=NEG_INF).reshape(1, LANES)
    return pl.pallas_call(
        _merge_kernel,
        grid=(rows // tm,),
        in_specs=[row(rd)] * 7 + [row(ATT_DIM), row(2 * d), row(d),
                  pl.BlockSpec((1, SUBLANES, d), lambda i: (i // tps, 0, 0)),
                  vec(rd), vec(rd), vec(rd), pl.BlockSpec((rd, rd), const),
                  pl.BlockSpec((ATT_DIM, d), const), pl.BlockSpec((rd, d), const), pl.BlockSpec((d, d), const),
                  vec(d), vec(d), pl.BlockSpec((d, LANES), const), vec(LANES)],
        out_specs=[row(d), row(d), row(LANES), row(LANES)],
        out_shape=[jax.ShapeDtypeStruct((rows, d), F32), jax.ShapeDtypeStruct((rows, d), BF16),
                   jax.ShapeDtypeStruct((rows, LANES), jnp.int32), jax.ShapeDtypeStruct((rows, LANES), F32)],
        compiler_params=_cparams("parallel"),
        name="merge",
    )(y_f, y_b, f['r'], f['kf'], f['kb'], f['v'], f['gate'], att, z_gate, x2, mods,
      p['ln_x_w'].reshape(1, rd), p['ln_x_b'].reshape(1, rd), p['r_k'].reshape(1, rd), _seg_matrix(),
      p['w_up_att'].astype(BF16), p['w_up_rwkv'].astype(BF16), p['w_out'].astype(BF16),
      p['g_post_mix'].reshape(1, d), p['g_pre_ffn'].reshape(1, d), w_router, b_router)


def _moe_kernel(be_ref, nused_ref, x_ref, wgu_ref, bgu_ref, wdn_ref, bdn_ref, o_ref):
    i = pl.program_id(0)

    @pl.when(i < nused_ref[0])
    def _():
        de = wdn_ref.shape[0]
        gu = jnp.dot(x_ref[...], wgu_ref[...], preferred_element_type=F32) + bgu_ref[...]
        gate = jnp.minimum(gu[:, :de], SWIGLU_LIMIT)
        up = jnp.clip(gu[:, de:], -SWIGLU_LIMIT, SWIGLU_LIMIT)
        act = (up + 1.0) * (gate * _sigmoid(SWIGLU_ALPHA * gate))
        o_ref[...] = _mm(act, wdn_ref[...]) + bdn_ref[...]

    @pl.when(i >= nused_ref[0])
    def _():
        o_ref[...] = jnp.zeros(o_ref.shape, o_ref.dtype)


def _experts(xb, block_exp, n_used, w_gu, b_gu, w_dn, b_dn):
    n_slots, d = xb.shape
    bm = MOE_ROWS
    n_blocks = n_slots // bm
    de2 = w_gu.shape[2]
    de = w_dn.shape[1]
    grid_spec = pltpu.PrefetchScalarGridSpec(
        num_scalar_prefetch=2,
        grid=(n_blocks,),
        in_specs=[pl.BlockSpec((bm, d), lambda i, be, nu: (i, 0)),
                  pl.BlockSpec((None, d, de2), lambda i, be, nu: (be[i], 0, 0)),
                  pl.BlockSpec((None, 1, de2), lambda i, be, nu: (be[i], 0, 0)),
                  pl.BlockSpec((None, de, d), lambda i, be, nu: (be[i], 0, 0)),
                  pl.BlockSpec((None, 1, d), lambda i, be, nu: (be[i], 0, 0))],
        out_specs=pl.BlockSpec((bm, d), lambda i, be, nu: (i, 0)),
    )
    return pl.pallas_call(
        _moe_kernel,
        grid_spec=grid_spec,
        out_shape=jax.ShapeDtypeStruct((n_slots, d), F32),
        compiler_params=_cparams("arbitrary"),
        name="moe",
    )(block_exp, n_used, xb, w_gu, b_gu, w_dn, b_dn)


def _final_kernel(x1_ref, y_ref, mod_ref, g_ref, o_ref):
    mod = mod_ref[0]
    o_ref[...] = x1_ref[...] + mod[5:6] * (_rms(y_ref[...]) * g_ref[...])


def _final(x1, y, mods, seq, g_post):
    rows, d = x1.shape
    tm = ROW_TILE
    tps = seq // tm
    row = pl.BlockSpec((tm, d), lambda i: (i, 0))
    return pl.pallas_call(
        _final_kernel,
        grid=(rows // tm,),
        in_specs=[row, row, pl.BlockSpec((1, SUBLANES, d), lambda i: (i // tps, 0, 0)),
                  pl.BlockSpec((1, d), lambda i: (0, 0))],
        out_specs=row,
        out_shape=jax.ShapeDtypeStruct((rows, d), F32),
        compiler_params=_cparams("parallel"),
        name="final",
    )(x1, y, mods, g_post.reshape(1, d))


def _route_slots(top_idx, n_tok):
    bm = MOE_ROWS
    tk = n_tok * TOP_K
    flat_e = top_idx.reshape(-1)
    order = jnp.argsort(flat_e)
    e_sorted = flat_e[order]
    counts = jnp.zeros((N_EXPERTS,), jnp.int32).at[flat_e].add(1)
    padded = ((counts + bm - 1) // bm) * bm
    start = jnp.cumsum(counts) - counts
    p_end = jnp.cumsum(padded)
    p_start = p_end - padded
    dest = p_start[e_sorted] + jnp.arange(tk, dtype=jnp.int32) - start[e_sorted]
    n_blocks = -(-tk // bm) + N_EXPERTS
    block_exp = jnp.minimum(jnp.searchsorted(p_end, jnp.arange(n_blocks, dtype=jnp.int32) * bm, side='right'),
                            N_EXPERTS - 1).astype(jnp.int32)
    n_used = (p_end[-1] // bm).astype(jnp.int32).reshape(1)
    return order, dest, n_blocks, block_exp, n_used


def _moe(hf, top_idx, gates, p):
    n_tok, d = hf.shape
    bm = MOE_ROWS
    order, dest, n_blocks, block_exp, n_used = _route_slots(top_idx, n_tok)
    n_slots = n_blocks * bm
    flat_tok = jnp.repeat(jnp.arange(n_tok, dtype=jnp.int32), TOP_K)
    slot_tok = jnp.full((n_slots,), n_tok, jnp.int32).at[dest].set(flat_tok[order])
    slot_g = jnp.zeros((n_slots,), F32).at[dest].set(gates.reshape(-1)[order])
    x_pad = jnp.concatenate([hf, jnp.zeros((1, d), hf.dtype)], axis=0)
    xb = x_pad[slot_tok]
    w_gu = p['w_gate_up']
    w_gu = jnp.concatenate([w_gu[..., 0::2], w_gu[..., 1::2]], axis=-1).astype(BF16)
    b_gu = p['b_gate_up']
    b_gu = jnp.concatenate([b_gu[..., 0::2], b_gu[..., 1::2]], axis=-1)[:, None, :]
    yb = _experts(xb, block_exp, n_used, w_gu, b_gu, p['w_down'].astype(BF16), p['b_down'][:, None, :])
    return jnp.zeros((n_tok + 1, d), F32).at[slot_tok].add(yb * slot_g[:, None])[:n_tok]


def _layer(x, c, ctx, c_ctx, p):
    batch, seq, d = x.shape
    ctx_len = ctx.shape[1]
    n_mod = p['w_ada'].shape[1] // d

    c_rows = jnp.zeros((2 * SUBLANES, d), F32).at[:batch].set(c).at[batch].set(c_ctx)
    mods = _ada(c_rows, p['w_ada'], p['b_ada']).reshape(2 * SUBLANES, n_mod, d)
    mods = jnp.pad(mods, ((0, 0), (0, SUBLANES - n_mod), (0, 0)))

    w_in = p['w_in'].astype(BF16)
    w_att, w_rwkv, w_gate = (w_in[:, :ATT_COLS], w_in[:, ATT_COLS:ATT_COLS + RWKV_COLS],
                             w_in[:, ATT_COLS + RWKV_COLS:])
    b_in = p['b_in'].reshape(1, -1)
    b_att, b_rwkv, b_gate = (b_in[:, :ATT_COLS], b_in[:, ATT_COLS:ATT_COLS + RWKV_COLS],
                             b_in[:, ATT_COLS + RWKV_COLS:])
    g_pre = p['g_pre_mix'].reshape(1, d)
    proj = functools.partial(_project, g_pre=g_pre, w_att=w_att, w_rwkv=w_rwkv, w_gate=w_gate,
                             b_att=b_att, b_rwkv=b_rwkv, b_gate=b_gate)

    zc_att, zc_rwkv, _ = proj(ctx.reshape(batch * ctx_len, d), mods, lambda b: batch, ctx_len, rope=False)
    fc = dict(zip(('r', 'v', 'kf', 'kb', 'lwf', 'lwb', 'kk', 'bf', 'bb', 'gate'), _features(zc_rwkv, ctx_len, p)))
    zero_state = jnp.zeros((batch, RWKV_HEADS, HEAD_DIM, HEAD_DIM), F32)
    _, _, sc_f, sc_b = _scan(fc, batch, ctx_len, zero_state, zero_state)

    x2 = x.reshape(batch * seq, d)
    z_att, z_rwkv, z_gate = proj(x2, mods, lambda b: b, seq, rope=True)
    fx = dict(zip(('r', 'v', 'kf', 'kb', 'lwf', 'lwb', 'kk', 'bf', 'bb', 'gate'), _features(z_rwkv, seq, p)))
    y_f, y_b, _, _ = _scan(fx, batch, seq, sc_f, sc_b)
    att = _attention(z_att, zc_att, p['att_sinks'], batch, seq)
    x1, hf, idx, gt = _merge(y_f, y_b, fx, att, z_gate, x2, mods, seq, p)

    y = _moe(hf, idx[:, :TOP_K], gt[:, :TOP_K], p)
    out = _final(x1, y, mods, seq, p['g_post_ffn'])
    return out.reshape(batch, seq, d)


def kernel(x, c, ctx, c_ctx, w_ada, b_ada, g_pre_mix, g_post_mix, g_pre_ffn, g_post_ffn, w_in, b_in, mu_prev, mu_next, att_sinks, w0_f, w0_b, w2_f, w2_b, a0_f, a0_b, a2_f, a2_b, g2, k_k, k_a, r_k, ln_x_w, ln_x_b, w_up_att, w_up_rwkv, w_out, w_router, b_router, w_gate_up, b_gate_up, w_down, b_down):
    assert w_ada.shape[0] == 1, "single-layer problem: the context stream update is never consumed"
    p = dict(w_ada=w_ada[0], b_ada=b_ada[0], g_pre_mix=g_pre_mix[0], g_post_mix=g_post_mix[0],
             g_pre_ffn=g_pre_ffn[0], g_post_ffn=g_post_ffn[0], w_in=w_in[0], b_in=b_in[0],
             mu_prev=mu_prev[0], mu_next=mu_next[0], att_sinks=att_sinks[0], w0_f=w0_f[0], w0_b=w0_b[0],
             w2_f=w2_f[0], w2_b=w2_b[0], a0_f=a0_f[0], a0_b=a0_b[0], a2_f=a2_f[0], a2_b=a2_b[0], g2=g2[0],
             k_k=k_k[0], k_a=k_a[0], r_k=r_k[0].reshape(-1), ln_x_w=ln_x_w[0], ln_x_b=ln_x_b[0],
             w_up_att=w_up_att[0], w_up_rwkv=w_up_rwkv[0], w_out=w_out[0], w_router=w_router[0],
             b_router=b_router[0], w_gate_up=w_gate_up[0], b_gate_up=b_gate_up[0], w_down=w_down[0],
             b_down=b_down[0])
    return _layer(x, c, ctx, c_ctx, p)
```

```python
import functools

import jax
import jax.numpy as jnp
import numpy as np
from jax import lax
from jax.experimental import pallas as pl
from jax.experimental.pallas import tpu as pltpu

F32 = jnp.float32
BF16 = jnp.bfloat16

GRID_W = 64
HEAD_DIM = 64
ATT_HEADS = 8
ATT_KV_HEADS = 2
ATT_GROUPS = ATT_HEADS // ATT_KV_HEADS
ATT_DIM = ATT_HEADS * HEAD_DIM
KV_DIM = ATT_KV_HEADS * HEAD_DIM
WINDOW = 128
BLOCK = 128
ROPE_BASE = 10000.0
ATT_SCALE = HEAD_DIM ** -0.5
NEG_INF = -1e30
RWKV_HEADS = 8
RWKV_DIM = RWKV_HEADS * HEAD_DIM
DECAY_LORA = 64
ICL_LORA = 64
GATE_LORA = 128
GN_EPS = 64e-5
N_EXPERTS = 32
TOP_K = 4
SWIGLU_LIMIT = 7.0
SWIGLU_ALPHA = 1.702
RMS_EPS = 1e-6
ATT_COLS = ATT_DIM + 2 * KV_DIM
RWKV_COLS = 3 * RWKV_DIM + 2 * DECAY_LORA + 2 * ICL_LORA + GATE_LORA
ROPE_COLS = ATT_DIM + KV_DIM

LANES = 128
SUBLANES = 8
VMEM_LIMIT = 48 * 1024 * 1024

ROW_TILE = 256
SCAN_CHUNK = 64
MOE_ROWS = 512


def _cparams(*sem):
    return pltpu.CompilerParams(dimension_semantics=sem, vmem_limit_bytes=VMEM_LIMIT)


def _mm(a, b):
    return jnp.dot(a.astype(BF16), b.astype(BF16), preferred_element_type=F32)


def _mm_nt(a, b):
    return lax.dot_general(a.astype(BF16), b.astype(BF16), (((1,), (1,)), ((), ())),
                           preferred_element_type=F32)


def _split2(x):
    hi = x.astype(BF16)
    lo = (x - hi.astype(F32)).astype(BF16)
    return hi, lo


def _split3(x):
    hi = x.astype(BF16)
    r1 = x - hi.astype(F32)
    mid = r1.astype(BF16)
    lo = (r1 - mid.astype(F32)).astype(BF16)
    return hi, mid, lo


def _mm_f32(a, b):
    ah, al = _split2(a)
    bh, bl = _split2(b)
    d = functools.partial(jnp.dot, preferred_element_type=F32)
    return d(ah, bh) + d(ah, bl) + d(al, bh)


def _seg_sum(x, eseg):
    hi, lo = _split2(x)
    d = functools.partial(jnp.dot, preferred_element_type=F32)
    return d(hi, eseg) + d(lo, eseg)


def _rms(x):
    return x * lax.rsqrt(jnp.mean(x * x, axis=-1, keepdims=True) + RMS_EPS)


def _sigmoid(x):
    return 1.0 / (1.0 + jnp.exp(-x))


def _ada_kernel(c_ref, w_ref, b_ref, o_ref):
    c = c_ref[...]
    o_ref[...] = _mm_f32(c * _sigmoid(c), w_ref[...]) + b_ref[...]


def _ada(c_rows, w, b):
    m, d = c_rows.shape
    n = w.shape[1]
    tn = 1536
    return pl.pallas_call(
        _ada_kernel,
        grid=(n // tn,),
        in_specs=[pl.BlockSpec((m, d), lambda j: (0, 0)),
                  pl.BlockSpec((d, tn), lambda j: (0, j)),
                  pl.BlockSpec((1, tn), lambda j: (0, j))],
        out_specs=pl.BlockSpec((m, tn), lambda j: (0, j)),
        out_shape=jax.ShapeDtypeStruct((m, n), F32),
        compiler_params=_cparams("arbitrary"),
        name="ada",
    )(c_rows, w, b.reshape(1, n))


def _proj_kernel(*refs, rope):
    if rope:
        (x_ref, mod_ref, g_ref, wa_ref, wr_ref, wg_ref, ba_ref, br_ref, bg_ref, cos_ref, sin_ref,
         za_ref, zr_ref, zg_ref) = refs
    else:
        (x_ref, mod_ref, g_ref, wa_ref, wr_ref, wg_ref, ba_ref, br_ref, bg_ref,
         za_ref, zr_ref, zg_ref) = refs
    mod = mod_ref[0]
    h = _rms(x_ref[...]) * g_ref[...] * (1.0 + mod[1:2]) + mod[0:1]
    hb = h.astype(BF16)
    za = jnp.dot(hb, wa_ref[...], preferred_element_type=F32) + ba_ref[...]
    if rope:
        qk = za[:, :ROPE_COLS]
        lane = lax.broadcasted_iota(jnp.int32, qk.shape, 1)
        low = (lane % 32) < 16
        partner = jnp.where(low, pltpu.roll(qk, ROPE_COLS - 16, 1), pltpu.roll(qk, 16, 1))
        za_ref[:, :ROPE_COLS] = (qk * cos_ref[...] + partner * sin_ref[...]).astype(za_ref.dtype)
        za_ref[:, ROPE_COLS:] = za[:, ROPE_COLS:].astype(za_ref.dtype)
    else:
        za_ref[...] = za.astype(za_ref.dtype)
    zr_ref[...] = jnp.dot(hb, wr_ref[...], preferred_element_type=F32) + br_ref[...]
    zg_ref[...] = (jnp.dot(hb, wg_ref[...], preferred_element_type=F32) + bg_ref[...]).astype(zg_ref.dtype)


def _rope_tables(seq):
    n_rows = seq // GRID_W
    row = jnp.repeat(jnp.arange(n_rows, dtype=F32), GRID_W, total_repeat_length=seq)
    col = jnp.tile(jnp.arange(GRID_W, dtype=F32), n_rows)
    half = HEAD_DIM // 2
    inv_freq = ROPE_BASE ** (-jnp.arange(0, half, 2, dtype=F32) / half)
    ang_r = row[:, None] * inv_freq[None, :]
    ang_c = col[:, None] * inv_freq[None, :]
    cos_h = jnp.concatenate([jnp.cos(ang_r), jnp.cos(ang_r), jnp.cos(ang_c), jnp.cos(ang_c)], axis=1)
    sin_h = jnp.concatenate([-jnp.sin(ang_r), jnp.sin(ang_r), -jnp.sin(ang_c), jnp.sin(ang_c)], axis=1)
    reps = ROPE_COLS // HEAD_DIM
    return jnp.tile(cos_h, (1, reps)), jnp.tile(sin_h, (1, reps))


def _project(x2, mods, mod_row, seq, g_pre, w_att, w_rwkv, w_gate, b_att, b_rwkv, b_gate, rope):
    rows, d = x2.shape
    tm = ROW_TILE
    tps = seq // tm
    const = lambda i: (0, 0)
    in_specs = [pl.BlockSpec((tm, d), lambda i: (i, 0)),
                pl.BlockSpec((1, SUBLANES, d), lambda i: (mod_row(i // tps), 0, 0)),
                pl.BlockSpec((1, d), const),
                pl.BlockSpec(w_att.shape, const), pl.BlockSpec(w_rwkv.shape, const),
                pl.BlockSpec(w_gate.shape, const),
                pl.BlockSpec((1, ATT_COLS), const), pl.BlockSpec((1, RWKV_COLS), const),
                pl.BlockSpec((1, w_gate.shape[1]), const)]
    args = [x2, mods, g_pre, w_att, w_rwkv, w_gate, b_att, b_rwkv, b_gate]
    if rope:
        cos, sin = _rope_tables(seq)
        in_specs += [pl.BlockSpec((tm, ROPE_COLS), lambda i: (i % tps, 0))] * 2
        args += [cos, sin]
    return pl.pallas_call(
        functools.partial(_proj_kernel, rope=rope),
        grid=(rows // tm,),
        in_specs=in_specs,
        out_specs=[pl.BlockSpec((tm, ATT_COLS), lambda i: (i, 0)),
                   pl.BlockSpec((tm, RWKV_COLS), lambda i: (i, 0)),
                   pl.BlockSpec((tm, w_gate.shape[1]), lambda i: (i, 0))],
        out_shape=[jax.ShapeDtypeStruct((rows, ATT_COLS), BF16),
                   jax.ShapeDtypeStruct((rows, RWKV_COLS), F32),
                   jax.ShapeDtypeStruct((rows, w_gate.shape[1]), BF16)],
        compiler_params=_cparams("parallel"),
        name="proj",
    )(*args)


def _feat_kernel(z_ref, zp_ref, zn_ref, mup_ref, mun_ref, w2_ref, w0_ref, a2_ref, a0_ref, g2_ref,
                 kk_ref, ka_ref, eseg_ref,
                 r_o, v_o, kf_o, kb_o, lwf_o, lwb_o, kkn_o, bf_o, bb_o, gate_o, *, tiles_per_seq):
    ti = pl.program_id(0) % tiles_per_seq
    z = z_ref[...]
    tm = z.shape[0]
    row = lax.broadcasted_iota(jnp.int32, (tm, 1), 0)
    prev_halo = jnp.where(ti == 0, 0.0, zp_ref[SUBLANES - 1:SUBLANES, :])
    next_halo = jnp.where(ti == tiles_per_seq - 1, 0.0, zn_ref[0:1, :])
    prev = jnp.where(row == 0, prev_halo, pltpu.roll(z, 1, 0))
    nxt = jnp.where(row == tm - 1, next_halo, pltpu.roll(z, tm - 1, 0))
    zs = z + mup_ref[...] * (prev - z) + mun_ref[...] * (nxt - z)

    d = RWKV_DIM
    r = zs[:, 0:d]
    k = zs[:, d:2 * d]
    v = zs[:, 2 * d:3 * d]
    o = 3 * d
    wl = zs[:, o:o + 2 * DECAY_LORA]
    al = zs[:, o + 2 * DECAY_LORA:o + 2 * DECAY_LORA + 2 * ICL_LORA]
    gl = zs[:, o + 2 * DECAY_LORA + 2 * ICL_LORA:]

    w = w0_ref[...] + _mm(jnp.tanh(wl), w2_ref[...])
    nw = -w
    softplus = jnp.maximum(nw, 0.0) + jnp.log(1.0 + jnp.exp(-jnp.abs(nw)))
    lw = -jnp.exp(-softplus - 0.5)
    icl = _sigmoid(a0_ref[...] + _mm(al, a2_ref[...]))
    kk0 = k * kk_ref[...]
    ss = _seg_sum(kk0 * kk0, eseg_ref[...])
    kk = kk0 / jnp.maximum(jnp.sqrt(ss), 1e-12)
    ka = ka_ref[...]
    icl_f = icl[:, :d]
    icl_b = icl[:, d:]

    r_o[...] = r
    v_o[...] = v
    kf_o[...] = k * (1.0 + (icl_f - 1.0) * ka)
    kb_o[...] = k * (1.0 + (icl_b - 1.0) * ka)
    lwf_o[...] = lw[:, :d]
    lwb_o[...] = lw[:, d:]
    kkn_o[...] = kk
    bf_o[...] = kk * icl_f
    bb_o[...] = kk * icl_b
    gate_o[...] = _mm(_sigmoid(gl), g2_ref[...])


def _block_diag2(a, b):
    za = jnp.zeros_like(a)
    zb = jnp.zeros_like(b)
    return jnp.concatenate([jnp.concatenate([a, zb], axis=1), jnp.concatenate([za, b], axis=1)], axis=0)


def _seg_matrix():
    h = np.arange(RWKV_DIM) // HEAD_DIM
    return jnp.asarray((h[:, None] == h[None, :]).astype(np.float32), BF16)


def _features(z_rwkv, seq, p):
    rows = z_rwkv.shape[0]
    tm = ROW_TILE
    tps = seq // tm
    hb = tm // SUBLANES
    nhb = rows // SUBLANES
    d = RWKV_DIM
    const = lambda i: (0, 0)
    w2 = _block_diag2(p['w2_f'], p['w2_b']).astype(BF16)
    a2 = _block_diag2(p['a2_f'], p['a2_b']).astype(BF16)
    w0 = jnp.concatenate([p['w0_f'], p['w0_b']]).reshape(1, 2 * d)
    a0 = jnp.concatenate([p['a0_f'], p['a0_b']]).reshape(1, 2 * d)
    out_spec = pl.BlockSpec((tm, d), lambda i: (i, 0))
    return pl.pallas_call(
        functools.partial(_feat_kernel, tiles_per_seq=tps),
        grid=(rows // tm,),
        in_specs=[pl.BlockSpec((tm, RWKV_COLS), lambda i: (i, 0)),
                  pl.BlockSpec((SUBLANES, RWKV_COLS), lambda i: (jnp.maximum(i * hb - 1, 0), 0)),
                  pl.BlockSpec((SUBLANES, RWKV_COLS), lambda i: (jnp.minimum((i + 1) * hb, nhb - 1), 0)),
                  pl.BlockSpec((1, RWKV_COLS), const), pl.BlockSpec((1, RWKV_COLS), const),
                  pl.BlockSpec(w2.shape, const), pl.BlockSpec((1, 2 * d), const),
                  pl.BlockSpec(a2.shape, const), pl.BlockSpec((1, 2 * d), const),
                  pl.BlockSpec((GATE_LORA, d), const),
                  pl.BlockSpec((1, d), const), pl.BlockSpec((1, d), const),
                  pl.BlockSpec((d, d), const)],
        out_specs=[out_spec] * 10,
        out_shape=[jax.ShapeDtypeStruct((rows, d), F32)] * 10,
        compiler_params=_cparams("parallel"),
        name="feat",
    )(z_rwkv, z_rwkv, z_rwkv, p['mu_prev'].reshape(1, -1), p['mu_next'].reshape(1, -1),
      w2, w0, a2, a0, p['g2'].astype(BF16), p['k_k'].reshape(1, d), p['k_a'].reshape(1, d), _seg_matrix())


def _scan_kernel(rf, vf, kf, lwf, kkf, bf, rb, vb, kb, lwb, kkb, bb, s0f_ref, s0b_ref,
                 yf_ref, yb_ref, sTf_ref, sTb_ref, state_ref, *, chunk, n_chunks):
    c = pl.program_id(1)
    C = chunk

    @pl.when(c == 0)
    def _():
        state_ref[0] = s0f_ref[0]
        state_ref[1] = s0b_ref[0]

    ti = lax.broadcasted_iota(jnp.int32, (C, C), 0)
    si = lax.broadcasted_iota(jnp.int32, (C, C), 1)
    ti2 = lax.broadcasted_iota(jnp.int32, (C, 2 * C), 0)
    si2 = lax.broadcasted_iota(jnp.int32, (C, 2 * C), 1) % C
    eye = (ti == si).astype(F32)
    n_double = int(np.log2(C)) - 1

    dirs = ((rf, vf, kf, lwf, kkf, bf), (rb, vb, kb, lwb, kkb, bb))
    units = []
    for d, (r_ref, v_ref, k_ref, lw_ref, kk_ref, b_ref) in enumerate(dirs):
        incl = (si <= ti) if d == 0 else (si >= ti)
        strict = (si < ti) if d == 0 else (si > ti)
        incl2 = (si2 <= ti2) if d == 0 else (si2 >= ti2)
        tri = incl.astype(BF16)
        lw = lw_ref[...]
        lh, lm, ll = _split3(lw)
        dd = functools.partial(jnp.dot, preferred_element_type=F32)
        cum = dd(tri, lh) + dd(tri, lm) + dd(tri, ll)
        cumx = cum - lw
        cum_end = cum[C - 1:C] if d == 0 else cum[0:1]
        e_neg = jnp.exp(-cum)
        e_end = jnp.exp(cum_end - cum)
        bv = b_ref[...]
        kv = k_ref[...]
        vv = v_ref[...]
        a_t = (-kk_ref[...] * jnp.exp(cumx)).astype(BF16)
        r_t = (r_ref[...] * jnp.exp(cum)).astype(BF16)
        b_t = (bv * e_neg).astype(BF16)
        k_t = (kv * e_neg).astype(BF16)
        b_q = (bv * e_end).astype(BF16)
        k_q = (kv * e_end).astype(BF16)
        g_end = jnp.exp(cum_end)
        for h in range(RWKV_HEADS):
            sl = slice(h * HEAD_DIM, (h + 1) * HEAD_DIM)
            units.append(dict(
                d=d, h=h, incl2=incl2, strict=strict,
                P=jnp.concatenate([a_t[:, sl], r_t[:, sl]], axis=0),
                Q=jnp.concatenate([b_t[:, sl], k_t[:, sl]], axis=0),
                Qq=jnp.concatenate([b_q[:, sl], k_q[:, sl]], axis=0),
                V=vv[:, sl], g_end=g_end[:, sl]))

    for u in units:
        u['G'] = _mm_nt(u['P'], u['Q'])
    for u in units:
        u['S0'] = state_ref[u['d'], u['h']]
        u['PH'] = _mm_nt(u['P'], u['S0'])
    for u in units:
        G = u.pop('G')
        u['N'] = jnp.where(u['strict'], G[:C, :C], 0.0)
        u['a_ak'] = jnp.where(u['strict'], G[:C, C:], 0.0).astype(BF16)
        u['a_r'] = jnp.where(u['incl2'], G[C:, :], 0.0).astype(BF16)
    for u in units:
        nb = u['N'].astype(BF16)
        u['T'] = eye + u['N']
        u['Pw'] = _mm(nb, nb)
        u['rhs'] = u['PH'][:C] + _mm(u['a_ak'], u['V'])
    for lvl in range(n_double):
        for u in units:
            pw = u['Pw'].astype(BF16)
            u['T'] = u['T'] + _mm(u['T'], pw)
            if lvl + 1 < n_double:
                u['Pw'] = _mm(pw, pw)
    for u in units:
        u['UV'] = jnp.concatenate([_mm(u['T'], u['rhs']), u['V']], axis=0)
    ys = [[None] * RWKV_HEADS, [None] * RWKV_HEADS]
    for u in units:
        ys[u['d']][u['h']] = u['PH'][C:] + _mm(u['a_r'], u['UV'])
        state_ref[u['d'], u['h']] = u['S0'] * u['g_end'] + _mm(u['UV'].T, u['Qq'])
    yf_ref[...] = jnp.concatenate(ys[0], axis=1)
    yb_ref[...] = jnp.concatenate(ys[1], axis=1)

    @pl.when(c == n_chunks - 1)
    def _():
        sTf_ref[0] = state_ref[0]
        sTb_ref[0] = state_ref[1]


def _scan(f, batch, seq, s0_f, s0_b):
    C = SCAN_CHUNK
    nC = seq // C
    d = RWKV_DIM
    fwd = pl.BlockSpec((C, d), lambda b, c: (b * nC + c, 0))
    bwd = pl.BlockSpec((C, d), lambda b, c: (b * nC + nC - 1 - c, 0))
    st = pl.BlockSpec((1, RWKV_HEADS, HEAD_DIM, HEAD_DIM), lambda b, c: (b, 0, 0, 0))
    st_shape = jax.ShapeDtypeStruct((batch, RWKV_HEADS, HEAD_DIM, HEAD_DIM), F32)
    y_shape = jax.ShapeDtypeStruct((batch * seq, d), F32)
    return pl.pallas_call(
        functools.partial(_scan_kernel, chunk=C, n_chunks=nC),
        grid=(batch, nC),
        in_specs=[fwd] * 6 + [bwd] * 6 + [st, st],
        out_specs=[fwd, bwd, st, st],
        out_shape=[y_shape, y_shape, st_shape, st_shape],
        scratch_shapes=[pltpu.VMEM((2, RWKV_HEADS, HEAD_DIM, HEAD_DIM), F32)],
        compiler_params=_cparams("parallel", "arbitrary"),
        name="scan",
    )(f['r'], f['v'], f['kf'], f['lwf'], f['kk'], f['bf'],
      f['r'], f['v'], f['kb'], f['lwb'], f['kk'], f['bb'], s0_f, s0_b)


def _attn_kernel(sink_ref, q_ref, kp_ref, km_ref, kn_ref, vp_ref, vm_ref, vn_ref, kc_ref, vc_ref, o_ref,
                 *, n_blocks):
    n = pl.program_id(1)
    q = q_ref[...]
    qi = lax.broadcasted_iota(jnp.int32, (BLOCK, BLOCK), 0)
    kj = lax.broadcasted_iota(jnp.int32, (BLOCK, BLOCK), 1)
    ctx_len = kc_ref.shape[0]
    bias_prev = jnp.where((kj >= qi) & (n > 0), 0.0, NEG_INF)
    bias_next = jnp.where((kj <= qi) & (n < n_blocks - 1), 0.0, NEG_INF)
    valid1 = jnp.concatenate([bias_prev, jnp.zeros((BLOCK, BLOCK), F32), bias_next,
                              jnp.zeros((BLOCK, ctx_len), F32)], axis=1) == 0.0
    valid = jnp.concatenate([valid1] * ATT_GROUPS, axis=0)
    rowh = lax.broadcasted_iota(jnp.int32, (ATT_GROUPS * BLOCK, 1), 0) // BLOCK
    outs = [None] * ATT_HEADS
    for g in range(ATT_KV_HEADS):
        ks = slice(g * HEAD_DIM, (g + 1) * HEAD_DIM)
        K = jnp.concatenate([kp_ref[:, ks], km_ref[:, ks], kn_ref[:, ks], kc_ref[:, ks]], axis=0)
        V = jnp.concatenate([vp_ref[:, ks], vm_ref[:, ks], vn_ref[:, ks], vc_ref[:, ks]], axis=0)
        heads = [g * ATT_GROUPS + j for j in range(ATT_GROUPS)]
        Qs = jnp.concatenate([q[:, h * HEAD_DIM:(h + 1) * HEAD_DIM] for h in heads], axis=0)
        S = _mm_nt(Qs, K) * ATT_SCALE
        S = jnp.where(valid, S, NEG_INF)
        sink = jnp.zeros((ATT_GROUPS * BLOCK, 1), F32)
        for j, h in enumerate(heads):
            sink = jnp.where(rowh == j, sink_ref[h], sink)
        m = jnp.maximum(jnp.max(S, axis=-1, keepdims=True), sink)
        e = jnp.exp(S - m)
        den = jnp.sum(e, axis=-1, keepdims=True) + jnp.exp(sink - m)
        O = _mm(e / den, V)
        for j, h in enumerate(heads):
            outs[h] = O[j * BLOCK:(j + 1) * BLOCK]
    o_ref[...] = jnp.concatenate(outs, axis=1).astype(o_ref.dtype)


def _attention(z_att, zc_att, sinks, batch, seq):
    nb = seq // BLOCK
    ctx_len = zc_att.shape[0] // batch
    kcol = ATT_DIM // KV_DIM
    vcol = kcol + 1

    def kv_spec(col, off):
        return pl.BlockSpec((BLOCK, KV_DIM), lambda b, n: (b * nb + jnp.clip(n + off, 0, nb - 1), col))

    return pl.pallas_call(
        functools.partial(_attn_kernel, n_blocks=nb),
        grid=(batch, nb),
        in_specs=[pl.BlockSpec(memory_space=pltpu.SMEM),
                  pl.BlockSpec((BLOCK, ATT_DIM), lambda b, n: (b * nb + n, 0)),
                  kv_spec(kcol, -1), kv_spec(kcol, 0), kv_spec(kcol, 1),
                  kv_spec(vcol, -1), kv_spec(vcol, 0), kv_spec(vcol, 1),
                  pl.BlockSpec((ctx_len, KV_DIM), lambda b, n: (b, kcol)),
                  pl.BlockSpec((ctx_len, KV_DIM), lambda b, n: (b, vcol))],
        out_specs=pl.BlockSpec((BLOCK, ATT_DIM), lambda b, n: (b * nb + n, 0)),
        out_shape=jax.ShapeDtypeStruct((batch * seq, ATT_DIM), BF16),
        compiler_params=_cparams("parallel", "parallel"),
        name="attn",
    )(sinks, z_att, z_att, z_att, z_att, z_att, z_att, z_att, zc_att, zc_att)


def _merge_kernel(yf_ref, yb_ref, r_ref, kf_ref, kb_ref, v_ref, gate_ref, att_ref, zg_ref, x_ref, mod_ref,
                  lnw_ref, lnb_ref, rk_ref, eseg_ref, wua_ref, wur_ref, wo_ref, gpm_ref, gpf_ref,
                  wr_ref, br_ref,
                  x1_ref, hf_ref, idx_ref, gt_ref):
    eseg = eseg_ref[...]
    inv_n = 1.0 / HEAD_DIM
    y = yf_ref[...] + yb_ref[...]
    mean = _seg_sum(y, eseg) * inv_n
    dy = y - mean
    var = _seg_sum(dy * dy, eseg) * inv_n
    yn = dy * lax.rsqrt(var + GN_EPS) * lnw_ref[...] + lnb_ref[...]
    v = v_ref[...]
    bonus = _seg_sum(r_ref[...] * (kf_ref[...] + kb_ref[...]) * rk_ref[...], eseg) * v
    rwk = (yn + bonus) * gate_ref[...]

    d = x_ref.shape[1]
    zg = zg_ref[...].astype(F32)
    merged = (_sigmoid(zg[:, :d]) * jnp.dot(att_ref[...], wua_ref[...], preferred_element_type=F32)
              + _sigmoid(zg[:, d:]) * _mm(rwk, wur_ref[...]))
    mix = _mm(merged, wo_ref[...])
    mod = mod_ref[0]
    x1 = x_ref[...] + mod[2:3] * (_rms(mix) * gpm_ref[...])
    x1_ref[...] = x1
    hf = _rms(x1) * gpf_ref[...] * (1.0 + mod[4:5]) + mod[3:4]
    hf_ref[...] = hf.astype(hf_ref.dtype)

    logits = _mm_f32(hf, wr_ref[...]) + br_ref[...]
    lane = lax.broadcasted_iota(jnp.int32, logits.shape, 1).astype(F32)
    idx_out = jnp.zeros(logits.shape, F32)
    val_out = jnp.zeros(logits.shape, F32)
    top = None
    den = None
    for kth in range(TOP_K):
        m = jnp.max(logits, axis=-1, keepdims=True)
        sel = jnp.min(jnp.where(logits == m, lane, float(LANES)), axis=-1, keepdims=True)
        logits = jnp.where(lane == sel, -jnp.inf, logits)
        if kth == 0:
            top = m
        e = jnp.exp(m - top)
        den = e if kth == 0 else den + e
        idx_out = jnp.where(lane == float(kth), sel, idx_out)
        val_out = jnp.where(lane == float(kth), e, val_out)
    idx_ref[...] = idx_out.astype(jnp.int32)
    gt_ref[...] = val_out / den


def _merge(y_f, y_b, f, att, z_gate, x2, mods, seq, p):
    rows, d = x2.shape
    tm = ROW_TILE
    tps = seq // tm
    rd = RWKV_DIM
    const = lambda i: (0, 0)
    row = lambda w: pl.BlockSpec((tm, w), lambda i: (i, 0))
    vec = lambda w: pl.BlockSpec((1, w), const)
    w_router = jnp.pad(p['w_router'], ((0, 0), (0, LANES - N_EXPERTS)))
    b_router = jnp.pad(p['b_router'], (0, LANES - N_EXPERTS), constant_values=NEG_INF).reshape(1, LANES)
    return pl.pallas_call(
        _merge_kernel,
        grid=(rows // tm,),
        in_specs=[row(rd)] * 7 + [row(ATT_DIM), row(2 * d), row(d),
                  pl.BlockSpec((1, SUBLANES, d), lambda i: (i // tps, 0, 0)),
                  vec(rd), vec(rd), vec(rd), pl.BlockSpec((rd, rd), const),
                  pl.BlockSpec((ATT_DIM, d), const), pl.BlockSpec((rd, d), const), pl.BlockSpec((d, d), const),
                  vec(d), vec(d), pl.BlockSpec((d, LANES), const), vec(LANES)],
        out_specs=[row(d), row(d), row(LANES), row(LANES)],
        out_shape=[jax.ShapeDtypeStruct((rows, d), F32), jax.ShapeDtypeStruct((rows, d), BF16),
                   jax.ShapeDtypeStruct((rows, LANES), jnp.int32), jax.ShapeDtypeStruct((rows, LANES), F32)],
        compiler_params=_cparams("parallel"),
        name="merge",
    )(y_f, y_b, f['r'], f['kf'], f['kb'], f['v'], f['gate'], att, z_gate, x2, mods,
      p['ln_x_w'].reshape(1, rd), p['ln_x_b'].reshape(1, rd), p['r_k'].reshape(1, rd), _seg_matrix(),
      p['w_up_att'].astype(BF16), p['w_up_rwkv'].astype(BF16), p['w_out'].astype(BF16),
      p['g_post_mix'].reshape(1, d), p['g_pre_ffn'].reshape(1, d), w_router, b_router)


def _moe_kernel(be_ref, nused_ref, x_ref, wgu_ref, bgu_ref, wdn_ref, bdn_ref, o_ref):
    i = pl.program_id(0)

    @pl.when(i < nused_ref[0])
    def _():
        de = wdn_ref.shape[0]
        d = x_ref.shape[1]
        x = x_ref[...]
        bgu = bgu_ref[...]
        gate = jnp.minimum(_mm_nt(x, wgu_ref[:, :d]) + bgu[:, :de], SWIGLU_LIMIT)
        up = jnp.clip(_mm_nt(x, wgu_ref[:, d:]) + bgu[:, de:], -SWIGLU_LIMIT, SWIGLU_LIMIT)
        act = (up + 1.0) * (gate * _sigmoid(SWIGLU_ALPHA * gate))
        o_ref[...] = _mm(act, wdn_ref[...]) + bdn_ref[...]

    @pl.when(i >= nused_ref[0])
    def _():
        o_ref[...] = jnp.zeros(o_ref.shape, o_ref.dtype)


def _experts(xb, block_exp, n_used, w_gu, b_gu, w_dn, b_dn):
    n_slots, d = xb.shape
    bm = MOE_ROWS
    n_blocks = n_slots // bm
    de = w_dn.shape[1]
    de2 = 2 * de
    grid_spec = pltpu.PrefetchScalarGridSpec(
        num_scalar_prefetch=2,
        grid=(n_blocks,),
        in_specs=[pl.BlockSpec((bm, d), lambda i, be, nu: (i, 0)),
                  pl.BlockSpec((None, de, 2 * d), lambda i, be, nu: (be[i], 0, 0)),
                  pl.BlockSpec((None, 1, de2), lambda i, be, nu: (be[i], 0, 0)),
                  pl.BlockSpec((None, de, d), lambda i, be, nu: (be[i], 0, 0)),
                  pl.BlockSpec((None, 1, d), lambda i, be, nu: (be[i], 0, 0))],
        out_specs=pl.BlockSpec((bm, d), lambda i, be, nu: (i, 0)),
    )
    return pl.pallas_call(
        _moe_kernel,
        grid_spec=grid_spec,
        out_shape=jax.ShapeDtypeStruct((n_slots, d), F32),
        compiler_params=_cparams("arbitrary"),
        name="moe",
    )(block_exp, n_used, xb, w_gu, b_gu, w_dn, b_dn)


def _final_kernel(x1_ref, y_ref, mod_ref, g_ref, o_ref):
    mod = mod_ref[0]
    o_ref[...] = x1_ref[...] + mod[5:6] * (_rms(y_ref[...]) * g_ref[...])


def _final(x1, y, mods, seq, g_post):
    rows, d = x1.shape
    tm = ROW_TILE
    tps = seq // tm
    row = pl.BlockSpec((tm, d), lambda i: (i, 0))
    return pl.pallas_call(
        _final_kernel,
        grid=(rows // tm,),
        in_specs=[row, row, pl.BlockSpec((1, SUBLANES, d), lambda i: (i // tps, 0, 0)),
                  pl.BlockSpec((1, d), lambda i: (0, 0))],
        out_specs=row,
        out_shape=jax.ShapeDtypeStruct((rows, d), F32),
        compiler_params=_cparams("parallel"),
        name="final",
    )(x1, y, mods, g_post.reshape(1, d))


def _route_slots(top_idx, n_tok):
    bm = MOE_ROWS
    tk = n_tok * TOP_K
    flat_e = top_idx.reshape(-1)
    order = jnp.argsort(flat_e)
    e_sorted = flat_e[order]
    counts = jnp.zeros((N_EXPERTS,), jnp.int32).at[flat_e].add(1)
    padded = ((counts + bm - 1) // bm) * bm
    start = jnp.cumsum(counts) - counts
    p_end = jnp.cumsum(padded)
    p_start = p_end - padded
    dest = p_start[e_sorted] + jnp.arange(tk, dtype=jnp.int32) - start[e_sorted]
    n_blocks = -(-tk // bm) + N_EXPERTS
    block_exp = jnp.minimum(jnp.searchsorted(p_end, jnp.arange(n_blocks, dtype=jnp.int32) * bm, side='right'),
                            N_EXPERTS - 1).astype(jnp.int32)
    n_used = (p_end[-1] // bm).astype(jnp.int32).reshape(1)
    return order, dest, n_blocks, block_exp, n_used


def _moe(hf, top_idx, gates, p):
    n_tok, d = hf.shape
    bm = MOE_ROWS
    order, dest, n_blocks, block_exp, n_used = _route_slots(top_idx, n_tok)
    n_slots = n_blocks * bm
    flat_tok = jnp.repeat(jnp.arange(n_tok, dtype=jnp.int32), TOP_K)
    slot_tok = jnp.full((n_slots,), n_tok, jnp.int32).at[dest].set(flat_tok[order])
    slot_g = jnp.zeros((n_slots,), F32).at[dest].set(gates.reshape(-1)[order])
    x_pad = jnp.concatenate([hf, jnp.zeros((1, d), hf.dtype)], axis=0)
    xb = x_pad[slot_tok]
    w_gu = p['w_gate_up']
    n_exp, _, de2 = w_gu.shape
    w_gu = jnp.swapaxes(w_gu, 1, 2).astype(BF16).reshape(n_exp, de2 // 2, 2 * d)
    b_gu = p['b_gate_up']
    b_gu = jnp.concatenate([b_gu[..., 0::2], b_gu[..., 1::2]], axis=-1)[:, None, :]
    yb = _experts(xb, block_exp, n_used, w_gu, b_gu, p['w_down'].astype(BF16), p['b_down'][:, None, :])
    return jnp.zeros((n_tok + 1, d), F32).at[slot_tok].add(yb * slot_g[:, None])[:n_tok]


def _layer(x, c, ctx, c_ctx, p):
    batch, seq, d = x.shape
    ctx_len = ctx.shape[1]
    n_mod = p['w_ada'].shape[1] // d

    c_rows = jnp.zeros((2 * SUBLANES, d), F32).at[:batch].set(c).at[batch].set(c_ctx)
    mods = _ada(c_rows, p['w_ada'], p['b_ada']).reshape(2 * SUBLANES, n_mod, d)
    mods = jnp.pad(mods, ((0, 0), (0, SUBLANES - n_mod), (0, 0)))

    w_in = p['w_in'].astype(BF16)
    w_att, w_rwkv, w_gate = (w_in[:, :ATT_COLS], w_in[:, ATT_COLS:ATT_COLS + RWKV_COLS],
                             w_in[:, ATT_COLS + RWKV_COLS:])
    b_in = p['b_in'].reshape(1, -1)
    b_att, b_rwkv, b_gate = (b_in[:, :ATT_COLS], b_in[:, ATT_COLS:ATT_COLS + RWKV_COLS],
                             b_in[:, ATT_COLS + RWKV_COLS:])
    g_pre = p['g_pre_mix'].reshape(1, d)
    proj = functools.partial(_project, g_pre=g_pre, w_att=w_att, w_rwkv=w_rwkv, w_gate=w_gate,
                             b_att=b_att, b_rwkv=b_rwkv, b_gate=b_gate)

    zc_att, zc_rwkv, _ = proj(ctx.reshape(batch * ctx_len, d), mods, lambda b: batch, ctx_len, rope=False)
    fc = dict(zip(('r', 'v', 'kf', 'kb', 'lwf', 'lwb', 'kk', 'bf', 'bb', 'gate'), _features(zc_rwkv, ctx_len, p)))
    zero_state = jnp.zeros((batch, RWKV_HEADS, HEAD_DIM, HEAD_DIM), F32)
    _, _, sc_f, sc_b = _scan(fc, batch, ctx_len, zero_state, zero_state)

    x2 = x.reshape(batch * seq, d)
    z_att, z_rwkv, z_gate = proj(x2, mods, lambda b: b, seq, rope=True)
    fx = dict(zip(('r', 'v', 'kf', 'kb', 'lwf', 'lwb', 'kk', 'bf', 'bb', 'gate'), _features(z_rwkv, seq, p)))
    y_f, y_b, _, _ = _scan(fx, batch, seq, sc_f, sc_b)
    att = _attention(z_att, zc_att, p['att_sinks'], batch, seq)
    x1, hf, idx, gt = _merge(y_f, y_b, fx, att, z_gate, x2, mods, seq, p)

    y = _moe(hf, idx[:, :TOP_K], gt[:, :TOP_K], p)
    out = _final(x1, y, mods, seq, p['g_post_ffn'])
    return out.reshape(batch, seq, d)


def kernel(x, c, ctx, c_ctx, w_ada, b_ada, g_pre_mix, g_post_mix, g_pre_ffn, g_post_ffn, w_in, b_in, mu_prev, mu_next, att_sinks, w0_f, w0_b, w2_f, w2_b, a0_f, a0_b, a2_f, a2_b, g2, k_k, k_a, r_k, ln_x_w, ln_x_b, w_up_att, w_up_rwkv, w_out, w_router, b_router, w_gate_up, b_gate_up, w_down, b_down):
    assert w_ada.shape[0] == 1, "single-layer problem: the context stream update is never consumed"
    p = dict(w_ada=w_ada[0], b_ada=b_ada[0], g_pre_mix=g_pre_mix[0], g_post_mix=g_post_mix[0],
             g_pre_ffn=g_pre_ffn[0], g_post_ffn=g_post_ffn[0], w_in=w_in[0], b_in=b_in[0],
             mu_prev=mu_prev[0], mu_next=mu_next[0], att_sinks=att_sinks[0], w0_f=w0_f[0], w0_b=w0_b[0],
             w2_f=w2_f[0], w2_b=w2_b[0], a0_f=a0_f[0], a0_b=a0_b[0], a2_f=a2_f[0], a2_b=a2_b[0], g2=g2[0],
             k_k=k_k[0], k_a=k_a[0], r_k=r_k[0].reshape(-1), ln_x_w=ln_x_w[0], ln_x_b=ln_x_b[0],
             w_up_att=w_up_att[0], w_up_rwkv=w_up_rwkv[0], w_out=w_out[0], w_router=w_router[0],
             b_router=b_router[0], w_gate_up=w_gate_up[0], b_gate_up=b_gate_up[0], w_down=w_down[0],
             b_down=b_down[0])
    return _layer(x, c, ctx, c_ctx, p)
```

```python
import functools

import jax
import jax.numpy as jnp
import numpy as np
from jax import lax
from jax.experimental import pallas as pl
from jax.experimental.pallas import tpu as pltpu

F32 = jnp.float32
BF16 = jnp.bfloat16

GRID_W = 64
HEAD_DIM = 64
ATT_HEADS = 8
ATT_KV_HEADS = 2
ATT_GROUPS = ATT_HEADS // ATT_KV_HEADS
ATT_DIM = ATT_HEADS * HEAD_DIM
KV_DIM = ATT_KV_HEADS * HEAD_DIM
WINDOW = 128
BLOCK = 128
ROPE_BASE = 10000.0
ATT_SCALE = HEAD_DIM ** -0.5
NEG_INF = -1e30
RWKV_HEADS = 8
RWKV_DIM = RWKV_HEADS * HEAD_DIM
DECAY_LORA = 64
ICL_LORA = 64
GATE_LORA = 128
GN_EPS = 64e-5
N_EXPERTS = 32
TOP_K = 4
SWIGLU_LIMIT = 7.0
SWIGLU_ALPHA = 1.702
RMS_EPS = 1e-6
ATT_COLS = ATT_DIM + 2 * KV_DIM
RWKV_COLS = 3 * RWKV_DIM + 2 * DECAY_LORA + 2 * ICL_LORA + GATE_LORA
ROPE_COLS = ATT_DIM + KV_DIM

LANES = 128
SUBLANES = 8
VMEM_LIMIT = 48 * 1024 * 1024

ROW_TILE = 256
SCAN_CHUNK = 64
MOE_ROWS = 512


def _cparams(*sem):
    return pltpu.CompilerParams(dimension_semantics=sem, vmem_limit_bytes=VMEM_LIMIT)


def _mm(a, b):
    return jnp.dot(a.astype(BF16), b.astype(BF16), preferred_element_type=F32)


def _mm_nt(a, b):
    return lax.dot_general(a.astype(BF16), b.astype(BF16), (((1,), (1,)), ((), ())),
                           preferred_element_type=F32)


def _split2(x):
    hi = x.astype(BF16)
    lo = (x - hi.astype(F32)).astype(BF16)
    return hi, lo


def _split3(x):
    hi = x.astype(BF16)
    r1 = x - hi.astype(F32)
    mid = r1.astype(BF16)
    lo = (r1 - mid.astype(F32)).astype(BF16)
    return hi, mid, lo


def _mm_f32(a, b):
    ah, al = _split2(a)
    bh, bl = _split2(b)
    d = functools.partial(jnp.dot, preferred_element_type=F32)
    return d(ah, bh) + d(ah, bl) + d(al, bh)


def _seg_sum(x, eseg):
    hi, lo = _split2(x)
    d = functools.partial(jnp.dot, preferred_element_type=F32)
    return d(hi, eseg) + d(lo, eseg)


def _rms(x):
    return x * lax.rsqrt(jnp.mean(x * x, axis=-1, keepdims=True) + RMS_EPS)


def _sigmoid(x):
    return 1.0 / (1.0 + jnp.exp(-x))


def _ada_kernel(c_ref, w_ref, b_ref, o_ref):
    c = c_ref[...]
    o_ref[...] = _mm_f32(c * _sigmoid(c), w_ref[...]) + b_ref[...]


def _ada(c_rows, w, b):
    m, d = c_rows.shape
    n = w.shape[1]
    tn = 1536
    return pl.pallas_call(
        _ada_kernel,
        grid=(n // tn,),
        in_specs=[pl.BlockSpec((m, d), lambda j: (0, 0)),
                  pl.BlockSpec((d, tn), lambda j: (0, j)),
                  pl.BlockSpec((1, tn), lambda j: (0, j))],
        out_specs=pl.BlockSpec((m, tn), lambda j: (0, j)),
        out_shape=jax.ShapeDtypeStruct((m, n), F32),
        compiler_params=_cparams("arbitrary"),
        name="ada",
    )(c_rows, w, b.reshape(1, n))


def _proj_kernel(*refs, rope):
    if rope:
        (x_ref, mod_ref, g_ref, wa_ref, wr_ref, wg_ref, ba_ref, br_ref, bg_ref, cos_ref, sin_ref,
         za_ref, zr_ref, zg_ref) = refs
    else:
        (x_ref, mod_ref, g_ref, wa_ref, wr_ref, wg_ref, ba_ref, br_ref, bg_ref,
         za_ref, zr_ref, zg_ref) = refs
    mod = mod_ref[0]
    h = _rms(x_ref[...]) * g_ref[...] * (1.0 + mod[1:2]) + mod[0:1]
    hb = h.astype(BF16)
    za = jnp.dot(hb, wa_ref[...], preferred_element_type=F32) + ba_ref[...]
    if rope:
        qk = za[:, :ROPE_COLS]
        lane = lax.broadcasted_iota(jnp.int32, qk.shape, 1)
        low = (lane % 32) < 16
        partner = jnp.where(low, pltpu.roll(qk, ROPE_COLS - 16, 1), pltpu.roll(qk, 16, 1))
        za_ref[:, :ROPE_COLS] = (qk * cos_ref[...] + partner * sin_ref[...]).astype(za_ref.dtype)
        za_ref[:, ROPE_COLS:] = za[:, ROPE_COLS:].astype(za_ref.dtype)
    else:
        za_ref[...] = za.astype(za_ref.dtype)
    zr_ref[...] = jnp.dot(hb, wr_ref[...], preferred_element_type=F32) + br_ref[...]
    zg_ref[...] = (jnp.dot(hb, wg_ref[...], preferred_element_type=F32) + bg_ref[...]).astype(zg_ref.dtype)


def _rope_tables(seq):
    n_rows = seq // GRID_W
    row = jnp.repeat(jnp.arange(n_rows, dtype=F32), GRID_W, total_repeat_length=seq)
    col = jnp.tile(jnp.arange(GRID_W, dtype=F32), n_rows)
    half = HEAD_DIM // 2
    inv_freq = ROPE_BASE ** (-jnp.arange(0, half, 2, dtype=F32) / half)
    ang_r = row[:, None] * inv_freq[None, :]
    ang_c = col[:, None] * inv_freq[None, :]
    cos_h = jnp.concatenate([jnp.cos(ang_r), jnp.cos(ang_r), jnp.cos(ang_c), jnp.cos(ang_c)], axis=1)
    sin_h = jnp.concatenate([-jnp.sin(ang_r), jnp.sin(ang_r), -jnp.sin(ang_c), jnp.sin(ang_c)], axis=1)
    reps = ROPE_COLS // HEAD_DIM
    return jnp.tile(cos_h, (1, reps)), jnp.tile(sin_h, (1, reps))


def _project(x2, mods, mod_row, seq, g_pre, w_att, w_rwkv, w_gate, b_att, b_rwkv, b_gate, rope):
    rows, d = x2.shape
    tm = ROW_TILE
    tps = seq // tm
    const = lambda i: (0, 0)
    in_specs = [pl.BlockSpec((tm, d), lambda i: (i, 0)),
                pl.BlockSpec((1, SUBLANES, d), lambda i: (mod_row(i // tps), 0, 0)),
                pl.BlockSpec((1, d), const),
                pl.BlockSpec(w_att.shape, const), pl.BlockSpec(w_rwkv.shape, const),
                pl.BlockSpec(w_gate.shape, const),
                pl.BlockSpec((1, ATT_COLS), const), pl.BlockSpec((1, RWKV_COLS), const),
                pl.BlockSpec((1, w_gate.shape[1]), const)]
    args = [x2, mods, g_pre, w_att, w_rwkv, w_gate, b_att, b_rwkv, b_gate]
    if rope:
        cos, sin = _rope_tables(seq)
        in_specs += [pl.BlockSpec((tm, ROPE_COLS), lambda i: (i % tps, 0))] * 2
        args += [cos, sin]
    return pl.pallas_call(
        functools.partial(_proj_kernel, rope=rope),
        grid=(rows // tm,),
        in_specs=in_specs,
        out_specs=[pl.BlockSpec((tm, ATT_COLS), lambda i: (i, 0)),
                   pl.BlockSpec((tm, RWKV_COLS), lambda i: (i, 0)),
                   pl.BlockSpec((tm, w_gate.shape[1]), lambda i: (i, 0))],
        out_shape=[jax.ShapeDtypeStruct((rows, ATT_COLS), BF16),
                   jax.ShapeDtypeStruct((rows, RWKV_COLS), F32),
                   jax.ShapeDtypeStruct((rows, w_gate.shape[1]), BF16)],
        compiler_params=_cparams("parallel"),
        name="proj",
    )(*args)


def _feat_kernel(z_ref, zp_ref, zn_ref, mup_ref, mun_ref, w2_ref, w0_ref, a2_ref, a0_ref, g2_ref,
                 kk_ref, ka_ref, eseg_ref,
                 r_o, v_o, kf_o, kb_o, lwf_o, lwb_o, kkn_o, bf_o, bb_o, gate_o, *, tiles_per_seq):
    ti = pl.program_id(0) % tiles_per_seq
    z = z_ref[...]
    tm = z.shape[0]
    row = lax.broadcasted_iota(jnp.int32, (tm, 1), 0)
    prev_halo = jnp.where(ti == 0, 0.0, zp_ref[SUBLANES - 1:SUBLANES, :])
    next_halo = jnp.where(ti == tiles_per_seq - 1, 0.0, zn_ref[0:1, :])
    prev = jnp.where(row == 0, prev_halo, pltpu.roll(z, 1, 0))
    nxt = jnp.where(row == tm - 1, next_halo, pltpu.roll(z, tm - 1, 0))
    zs = z + mup_ref[...] * (prev - z) + mun_ref[...] * (nxt - z)

    d = RWKV_DIM
    r = zs[:, 0:d]
    k = zs[:, d:2 * d]
    v = zs[:, 2 * d:3 * d]
    o = 3 * d
    wl = zs[:, o:o + 2 * DECAY_LORA]
    al = zs[:, o + 2 * DECAY_LORA:o + 2 * DECAY_LORA + 2 * ICL_LORA]
    gl = zs[:, o + 2 * DECAY_LORA + 2 * ICL_LORA:]

    w = w0_ref[...] + _mm(jnp.tanh(wl), w2_ref[...])
    nw = -w
    softplus = jnp.maximum(nw, 0.0) + jnp.log(1.0 + jnp.exp(-jnp.abs(nw)))
    lw = -jnp.exp(-softplus - 0.5)
    icl = _sigmoid(a0_ref[...] + _mm(al, a2_ref[...]))
    kk0 = k * kk_ref[...]
    ss = _seg_sum(kk0 * kk0, eseg_ref[...])
    kk = kk0 / jnp.maximum(jnp.sqrt(ss), 1e-12)
    ka = ka_ref[...]
    icl_f = icl[:, :d]
    icl_b = icl[:, d:]

    r_o[...] = r
    v_o[...] = v
    kf_o[...] = k * (1.0 + (icl_f - 1.0) * ka)
    kb_o[...] = k * (1.0 + (icl_b - 1.0) * ka)
    lwf_o[...] = lw[:, :d]
    lwb_o[...] = lw[:, d:]
    kkn_o[...] = kk
    bf_o[...] = kk * icl_f
    bb_o[...] = kk * icl_b
    gate_o[...] = _mm(_sigmoid(gl), g2_ref[...])


def _block_diag2(a, b):
    za = jnp.zeros_like(a)
    zb = jnp.zeros_like(b)
    return jnp.concatenate([jnp.concatenate([a, zb], axis=1), jnp.concatenate([za, b], axis=1)], axis=0)


def _seg_matrix():
    h = np.arange(RWKV_DIM) // HEAD_DIM
    return jnp.asarray((h[:, None] == h[None, :]).astype(np.float32), BF16)


def _features(z_rwkv, seq, p):
    rows = z_rwkv.shape[0]
    tm = ROW_TILE
    tps = seq // tm
    hb = tm // SUBLANES
    nhb = rows // SUBLANES
    d = RWKV_DIM
    const = lambda i: (0, 0)
    w2 = _block_diag2(p['w2_f'], p['w2_b']).astype(BF16)
    a2 = _block_diag2(p['a2_f'], p['a2_b']).astype(BF16)
    w0 = jnp.concatenate([p['w0_f'], p['w0_b']]).reshape(1, 2 * d)
    a0 = jnp.concatenate([p['a0_f'], p['a0_b']]).reshape(1, 2 * d)
    out_spec = pl.BlockSpec((tm, d), lambda i: (i, 0))
    return pl.pallas_call(
        functools.partial(_feat_kernel, tiles_per_seq=tps),
        grid=(rows // tm,),
        in_specs=[pl.BlockSpec((tm, RWKV_COLS), lambda i: (i, 0)),
                  pl.BlockSpec((SUBLANES, RWKV_COLS), lambda i: (jnp.maximum(i * hb - 1, 0), 0)),
                  pl.BlockSpec((SUBLANES, RWKV_COLS), lambda i: (jnp.minimum((i + 1) * hb, nhb - 1), 0)),
                  pl.BlockSpec((1, RWKV_COLS), const), pl.BlockSpec((1, RWKV_COLS), const),
                  pl.BlockSpec(w2.shape, const), pl.BlockSpec((1, 2 * d), const),
                  pl.BlockSpec(a2.shape, const), pl.BlockSpec((1, 2 * d), const),
                  pl.BlockSpec((GATE_LORA, d), const),
                  pl.BlockSpec((1, d), const), pl.BlockSpec((1, d), const),
                  pl.BlockSpec((d, d), const)],
        out_specs=[out_spec] * 10,
        out_shape=[jax.ShapeDtypeStruct((rows, d), F32)] * 10,
        compiler_params=_cparams("parallel"),
        name="feat",
    )(z_rwkv, z_rwkv, z_rwkv, p['mu_prev'].reshape(1, -1), p['mu_next'].reshape(1, -1),
      w2, w0, a2, a0, p['g2'].astype(BF16), p['k_k'].reshape(1, d), p['k_a'].reshape(1, d), _seg_matrix())


def _scan_kernel(rf, vf, kf, lwf, kkf, bf, rb, vb, kb, lwb, kkb, bb, s0f_ref, s0b_ref,
                 yf_ref, yb_ref, sTf_ref, sTb_ref, state_ref, *, chunk, n_chunks):
    c = pl.program_id(1)
    C = chunk

    @pl.when(c == 0)
    def _():
        state_ref[0] = s0f_ref[0]
        state_ref[1] = s0b_ref[0]

    ti = lax.broadcasted_iota(jnp.int32, (C, C), 0)
    si = lax.broadcasted_iota(jnp.int32, (C, C), 1)
    ti2 = lax.broadcasted_iota(jnp.int32, (C, 2 * C), 0)
    si2 = lax.broadcasted_iota(jnp.int32, (C, 2 * C), 1) % C
    eye = (ti == si).astype(F32)
    n_double = int(np.log2(C)) - 1

    dirs = ((rf, vf, kf, lwf, kkf, bf), (rb, vb, kb, lwb, kkb, bb))
    units = []
    for d, (r_ref, v_ref, k_ref, lw_ref, kk_ref, b_ref) in enumerate(dirs):
        incl = (si <= ti) if d == 0 else (si >= ti)
        strict = (si < ti) if d == 0 else (si > ti)
        incl2 = (si2 <= ti2) if d == 0 else (si2 >= ti2)
        tri = incl.astype(BF16)
        lw = lw_ref[...]
        lh, lm, ll = _split3(lw)
        dd = functools.partial(jnp.dot, preferred_element_type=F32)
        cum = dd(tri, lh) + dd(tri, lm) + dd(tri, ll)
        cumx = cum - lw
        cum_end = cum[C - 1:C] if d == 0 else cum[0:1]
        e_neg = jnp.exp(-cum)
        e_end = jnp.exp(cum_end - cum)
        bv = b_ref[...]
        kv = k_ref[...]
        vv = v_ref[...]
        a_t = (-kk_ref[...] * jnp.exp(cumx)).astype(BF16)
        r_t = (r_ref[...] * jnp.exp(cum)).astype(BF16)
        b_t = (bv * e_neg).astype(BF16)
        k_t = (kv * e_neg).astype(BF16)
        b_q = (bv * e_end).astype(BF16)
        k_q = (kv * e_end).astype(BF16)
        g_end = jnp.exp(cum_end)
        for h in range(RWKV_HEADS):
            sl = slice(h * HEAD_DIM, (h + 1) * HEAD_DIM)
            units.append(dict(
                d=d, h=h, incl2=incl2, strict=strict,
                P=jnp.concatenate([a_t[:, sl], r_t[:, sl]], axis=0),
                Q=jnp.concatenate([b_t[:, sl], k_t[:, sl]], axis=0),
                Qq=jnp.concatenate([b_q[:, sl], k_q[:, sl]], axis=0),
                V=vv[:, sl], g_end=g_end[:, sl]))

    for u in units:
        u['G'] = _mm_nt(u['P'], u['Q'])
    for u in units:
        u['S0'] = state_ref[u['d'], u['h']]
        u['PH'] = _mm_nt(u['P'], u['S0'])
    for u in units:
        G = u.pop('G')
        u['N'] = jnp.where(u['strict'], G[:C, :C], 0.0)
        u['a_ak'] = jnp.where(u['strict'], G[:C, C:], 0.0).astype(BF16)
        u['a_r'] = jnp.where(u['incl2'], G[C:, :], 0.0).astype(BF16)
    for u in units:
        nb = u['N'].astype(BF16)
        u['T'] = eye + u['N']
        u['Pw'] = _mm(nb, nb)
        u['rhs'] = u['PH'][:C] + _mm(u['a_ak'], u['V'])
    for lvl in range(n_double):
        for u in units:
            pw = u['Pw'].astype(BF16)
            u['T'] = u['T'] + _mm(u['T'], pw)
            if lvl + 1 < n_double:
                u['Pw'] = _mm(pw, pw)
    for u in units:
        u['UV'] = jnp.concatenate([_mm(u['T'], u['rhs']), u['V']], axis=0)
    ys = [[None] * RWKV_HEADS, [None] * RWKV_HEADS]
    for u in units:
        ys[u['d']][u['h']] = u['PH'][C:] + _mm(u['a_r'], u['UV'])
        state_ref[u['d'], u['h']] = u['S0'] * u['g_end'] + _mm(u['UV'].T, u['Qq'])
    yf_ref[...] = jnp.concatenate(ys[0], axis=1)
    yb_ref[...] = jnp.concatenate(ys[1], axis=1)

    @pl.when(c == n_chunks - 1)
    def _():
        sTf_ref[0] = state_ref[0]
        sTb_ref[0] = state_ref[1]


def _scan(f, batch, seq, s0_f, s0_b):
    C = SCAN_CHUNK
    nC = seq // C
    d = RWKV_DIM
    fwd = pl.BlockSpec((C, d), lambda b, c: (b * nC + c, 0))
    bwd = pl.BlockSpec((C, d), lambda b, c: (b * nC + nC - 1 - c, 0))
    st = pl.BlockSpec((1, RWKV_HEADS, HEAD_DIM, HEAD_DIM), lambda b, c: (b, 0, 0, 0))
    st_shape = jax.ShapeDtypeStruct((batch, RWKV_HEADS, HEAD_DIM, HEAD_DIM), F32)
    y_shape = jax.ShapeDtypeStruct((batch * seq, d), F32)
    return pl.pallas_call(
        functools.partial(_scan_kernel, chunk=C, n_chunks=nC),
        grid=(batch, nC),
        in_specs=[fwd] * 6 + [bwd] * 6 + [st, st],
        out_specs=[fwd, bwd, st, st],
        out_shape=[y_shape, y_shape, st_shape, st_shape],
        scratch_shapes=[pltpu.VMEM((2, RWKV_HEADS, HEAD_DIM, HEAD_DIM), F32)],
        compiler_params=_cparams("parallel", "arbitrary"),
        name="scan",
    )(f['r'], f['v'], f['kf'], f['lwf'], f['kk'], f['bf'],
      f['r'], f['v'], f['kb'], f['lwb'], f['kk'], f['bb'], s0_f, s0_b)


def _attn_kernel(sink_ref, q_ref, kp_ref, km_ref, kn_ref, vp_ref, vm_ref, vn_ref, kc_ref, vc_ref, o_ref,
                 *, n_blocks):
    n = pl.program_id(1)
    q = q_ref[...]
    qi = lax.broadcasted_iota(jnp.int32, (BLOCK, BLOCK), 0)
    kj = lax.broadcasted_iota(jnp.int32, (BLOCK, BLOCK), 1)
    ctx_len = kc_ref.shape[0]
    bias_prev = jnp.where((kj >= qi) & (n > 0), 0.0, NEG_INF)
    bias_next = jnp.where((kj <= qi) & (n < n_blocks - 1), 0.0, NEG_INF)
    valid1 = jnp.concatenate([bias_prev, jnp.zeros((BLOCK, BLOCK), F32), bias_next,
                              jnp.zeros((BLOCK, ctx_len), F32)], axis=1) == 0.0
    valid = jnp.concatenate([valid1] * ATT_GROUPS, axis=0)
    rowh = lax.broadcasted_iota(jnp.int32, (ATT_GROUPS * BLOCK, 1), 0) // BLOCK
    outs = [None] * ATT_HEADS
    for g in range(ATT_KV_HEADS):
        ks = slice(g * HEAD_DIM, (g + 1) * HEAD_DIM)
        K = jnp.concatenate([kp_ref[:, ks], km_ref[:, ks], kn_ref[:, ks], kc_ref[:, ks]], axis=0)
        V = jnp.concatenate([vp_ref[:, ks], vm_ref[:, ks], vn_ref[:, ks], vc_ref[:, ks]], axis=0)
        heads = [g * ATT_GROUPS + j for j in range(ATT_GROUPS)]
        Qs = jnp.concatenate([q[:, h * HEAD_DIM:(h + 1) * HEAD_DIM] for h in heads], axis=0)
        S = _mm_nt(Qs, K) * ATT_SCALE
        S = jnp.where(valid, S, NEG_INF)
        sink = jnp.zeros((ATT_GROUPS * BLOCK, 1), F32)
        for j, h in enumerate(heads):
            sink = jnp.where(rowh == j, sink_ref[h], sink)
        m = jnp.maximum(jnp.max(S, axis=-1, keepdims=True), sink)
        e = jnp.exp(S - m)
        den = jnp.sum(e, axis=-1, keepdims=True) + jnp.exp(sink - m)
        O = _mm(e / den, V)
        for j, h in enumerate(heads):
            outs[h] = O[j * BLOCK:(j + 1) * BLOCK]
    o_ref[...] = jnp.concatenate(outs, axis=1).astype(o_ref.dtype)


def _attention(z_att, zc_att, sinks, batch, seq):
    nb = seq // BLOCK
    ctx_len = zc_att.shape[0] // batch
    kcol = ATT_DIM // KV_DIM
    vcol = kcol + 1

    def kv_spec(col, off):
        return pl.BlockSpec((BLOCK, KV_DIM), lambda b, n: (b * nb + jnp.clip(n + off, 0, nb - 1), col))

    return pl.pallas_call(
        functools.partial(_attn_kernel, n_blocks=nb),
        grid=(batch, nb),
        in_specs=[pl.BlockSpec(memory_space=pltpu.SMEM),
                  pl.BlockSpec((BLOCK, ATT_DIM), lambda b, n: (b * nb + n, 0)),
                  kv_spec(kcol, -1), kv_spec(kcol, 0), kv_spec(kcol, 1),
                  kv_spec(vcol, -1), kv_spec(vcol, 0), kv_spec(vcol, 1),
                  pl.BlockSpec((ctx_len, KV_DIM), lambda b, n: (b, kcol)),
                  pl.BlockSpec((ctx_len, KV_DIM), lambda b, n: (b, vcol))],
        out_specs=pl.BlockSpec((BLOCK, ATT_DIM), lambda b, n: (b * nb + n, 0)),
        out_shape=jax.ShapeDtypeStruct((batch * seq, ATT_DIM), BF16),
        compiler_params=_cparams("parallel", "parallel"),
        name="attn",
    )(sinks, z_att, z_att, z_att, z_att, z_att, z_att, z_att, zc_att, zc_att)


def _merge_kernel(yf_ref, yb_ref, r_ref, kf_ref, kb_ref, v_ref, gate_ref, att_ref, zg_ref, x_ref, mod_ref,
                  lnw_ref, lnb_ref, rk_ref, eseg_ref, wua_ref, wur_ref, wo_ref, gpm_ref, gpf_ref,
                  wr_ref, br_ref,
                  x1_ref, hf_ref, idx_ref, gt_ref, rank_ref, cnt_ref):
    eseg = eseg_ref[...]
    inv_n = 1.0 / HEAD_DIM
    y = yf_ref[...] + yb_ref[...]
    mean = _seg_sum(y, eseg) * inv_n
    dy = y - mean
    var = _seg_sum(dy * dy, eseg) * inv_n
    yn = dy * lax.rsqrt(var + GN_EPS) * lnw_ref[...] + lnb_ref[...]
    v = v_ref[...]
    bonus = _seg_sum(r_ref[...] * (kf_ref[...] + kb_ref[...]) * rk_ref[...], eseg) * v
    rwk = (yn + bonus) * gate_ref[...]

    d = x_ref.shape[1]
    zg = zg_ref[...].astype(F32)
    merged = (_sigmoid(zg[:, :d]) * jnp.dot(att_ref[...], wua_ref[...], preferred_element_type=F32)
              + _sigmoid(zg[:, d:]) * _mm(rwk, wur_ref[...]))
    mix = _mm(merged, wo_ref[...])
    mod = mod_ref[0]
    x1 = x_ref[...] + mod[2:3] * (_rms(mix) * gpm_ref[...])
    x1_ref[...] = x1
    hf = _rms(x1) * gpf_ref[...] * (1.0 + mod[4:5]) + mod[3:4]
    hf_ref[...] = hf.astype(hf_ref.dtype)

    logits = _mm_f32(hf, wr_ref[...]) + br_ref[...]
    lane = lax.broadcasted_iota(jnp.int32, logits.shape, 1).astype(F32)
    idx_out = jnp.zeros(logits.shape, F32)
    val_out = jnp.zeros(logits.shape, F32)
    picked = jnp.zeros(logits.shape, F32)
    sels = []
    top = None
    den = None
    for kth in range(TOP_K):
        m = jnp.max(logits, axis=-1, keepdims=True)
        sel = jnp.min(jnp.where(logits == m, lane, float(LANES)), axis=-1, keepdims=True)
        hit = lane == sel
        logits = jnp.where(hit, -jnp.inf, logits)
        picked = jnp.where(hit, 1.0, picked)
        sels.append(hit)
        if kth == 0:
            top = m
        e = jnp.exp(m - top)
        den = e if kth == 0 else den + e
        idx_out = jnp.where(lane == float(kth), sel, idx_out)
        val_out = jnp.where(lane == float(kth), e, val_out)
    idx_ref[...] = idx_out.astype(jnp.int32)
    gt_ref[...] = val_out / den

    tm = picked.shape[0]
    earlier = (lax.broadcasted_iota(jnp.int32, (tm, tm), 1)
               < lax.broadcasted_iota(jnp.int32, (tm, tm), 0)).astype(BF16)
    before = jnp.dot(earlier, picked.astype(BF16), preferred_element_type=F32)
    rank_out = jnp.zeros(logits.shape, F32)
    for kth, hit in enumerate(sels):
        rk = jnp.sum(jnp.where(hit, before, 0.0), axis=-1, keepdims=True)
        rank_out = jnp.where(lane == float(kth), rk, rank_out)
    rank_ref[...] = rank_out.astype(jnp.int32)
    cnt_ref[0] = jnp.broadcast_to(jnp.sum(picked, axis=0, keepdims=True), cnt_ref.shape[1:]).astype(jnp.int32)


def _merge(y_f, y_b, f, att, z_gate, x2, mods, seq, p):
    rows, d = x2.shape
    tm = ROW_TILE
    tps = seq // tm
    rd = RWKV_DIM
    const = lambda i: (0, 0)
    row = lambda w: pl.BlockSpec((tm, w), lambda i: (i, 0))
    vec = lambda w: pl.BlockSpec((1, w), const)
    w_router = jnp.pad(p['w_router'], ((0, 0), (0, LANES - N_EXPERTS)))
    b_router = jnp.pad(p['b_router'], (0, LANES - N_EXPERTS), constant_values=NEG_INF).reshape(1, LANES)
    return pl.pallas_call(
        _merge_kernel,
        grid=(rows // tm,),
        in_specs=[row(rd)] * 7 + [row(ATT_DIM), row(2 * d), row(d),
                  pl.BlockSpec((1, SUBLANES, d), lambda i: (i // tps, 0, 0)),
                  vec(rd), vec(rd), vec(rd), pl.BlockSpec((rd, rd), const),
                  pl.BlockSpec((ATT_DIM, d), const), pl.BlockSpec((rd, d), const), pl.BlockSpec((d, d), const),
                  vec(d), vec(d), pl.BlockSpec((d, LANES), const), vec(LANES)],
        out_specs=[row(d), row(d), row(LANES), row(LANES), row(LANES),
                   pl.BlockSpec((1, SUBLANES, LANES), lambda i: (i, 0, 0))],
        out_shape=[jax.ShapeDtypeStruct((rows, d), F32), jax.ShapeDtypeStruct((rows, d), F32),
                   jax.ShapeDtypeStruct((rows, LANES), jnp.int32), jax.ShapeDtypeStruct((rows, LANES), F32),
                   jax.ShapeDtypeStruct((rows, LANES), jnp.int32),
                   jax.ShapeDtypeStruct((rows // tm, SUBLANES, LANES), jnp.int32)],
        compiler_params=_cparams("parallel"),
        name="merge",
    )(y_f, y_b, f['r'], f['kf'], f['kb'], f['v'], f['gate'], att, z_gate, x2, mods,
      p['ln_x_w'].reshape(1, rd), p['ln_x_b'].reshape(1, rd), p['r_k'].reshape(1, rd), _seg_matrix(),
      p['w_up_att'].astype(BF16), p['w_up_rwkv'].astype(BF16), p['w_out'].astype(BF16),
      p['g_post_mix'].reshape(1, d), p['g_pre_ffn'].reshape(1, d), w_router, b_router)


def _moe_kernel(be_ref, nused_ref, x_ref, wgu_ref, bgu_ref, wdn_ref, bdn_ref, o_ref):
    i = pl.program_id(0)

    @pl.when(i < nused_ref[0])
    def _():
        de = wdn_ref.shape[0]
        d = x_ref.shape[1]
        x = x_ref[...]
        bgu = bgu_ref[...]
        gate = jnp.minimum(_mm_nt(x, wgu_ref[:, :d]) + bgu[:, :de], SWIGLU_LIMIT)
        up = jnp.clip(_mm_nt(x, wgu_ref[:, d:]) + bgu[:, de:], -SWIGLU_LIMIT, SWIGLU_LIMIT)
        act = (up + 1.0) * (gate * _sigmoid(SWIGLU_ALPHA * gate))
        o_ref[...] = _mm(act, wdn_ref[...]) + bdn_ref[...]

    @pl.when(i >= nused_ref[0])
    def _():
        o_ref[...] = jnp.zeros(o_ref.shape, o_ref.dtype)


def _experts(xb, block_exp, n_used, w_gu, b_gu, w_dn, b_dn):
    n_slots, d = xb.shape
    bm = MOE_ROWS
    n_blocks = n_slots // bm
    de = w_dn.shape[1]
    de2 = 2 * de
    grid_spec = pltpu.PrefetchScalarGridSpec(
        num_scalar_prefetch=2,
        grid=(n_blocks,),
        in_specs=[pl.BlockSpec((bm, d), lambda i, be, nu: (i, 0)),
                  pl.BlockSpec((None, de, 2 * d), lambda i, be, nu: (be[i], 0, 0)),
                  pl.BlockSpec((None, 1, de2), lambda i, be, nu: (be[i], 0, 0)),
                  pl.BlockSpec((None, de, d), lambda i, be, nu: (be[i], 0, 0)),
                  pl.BlockSpec((None, 1, d), lambda i, be, nu: (be[i], 0, 0))],
        out_specs=pl.BlockSpec((bm, d), lambda i, be, nu: (i, 0)),
    )
    return pl.pallas_call(
        _moe_kernel,
        grid_spec=grid_spec,
        out_shape=jax.ShapeDtypeStruct((n_slots, d), F32),
        compiler_params=_cparams("arbitrary"),
        name="moe",
    )(block_exp, n_used, xb, w_gu, b_gu, w_dn, b_dn)


def _dispatch_kernel(dest_ref, hf_ref, xb_in_ref, xb_ref, sem):
    del xb_in_ref
    tm = hf_ref.shape[0]
    base = pl.program_id(0) * (tm * TOP_K)

    def row_copy(j, slot):
        return pltpu.make_async_copy(hf_ref.at[pl.ds(j, 1), :], xb_ref.at[pl.ds(slot, 1), :], sem)

    def body(j, carry):
        for k in range(TOP_K):
            row_copy(j, dest_ref[base + j * TOP_K + k]).start()
        return carry

    lax.fori_loop(0, tm, body, 0, unroll=8)
    for _ in range(TOP_K):
        pltpu.make_async_copy(hf_ref, hf_ref, sem).wait()


def _dispatch(hf, dest, n_slots):
    rows, d = hf.shape
    tm = ROW_TILE
    grid_spec = pltpu.PrefetchScalarGridSpec(
        num_scalar_prefetch=1,
        grid=(rows // tm,),
        in_specs=[pl.BlockSpec((tm, d), lambda i, dst: (i, 0)),
                  pl.BlockSpec(memory_space=pl.ANY)],
        out_specs=pl.BlockSpec(memory_space=pl.ANY),
        scratch_shapes=[pltpu.SemaphoreType.DMA(())],
    )
    return pl.pallas_call(
        _dispatch_kernel,
        grid_spec=grid_spec,
        out_shape=jax.ShapeDtypeStruct((n_slots, d), F32),
        input_output_aliases={2: 0},
        compiler_params=_cparams("arbitrary"),
        name="dispatch",
    )(dest, hf, jnp.zeros((n_slots, d), F32))


def _combine_kernel(dest_ref, yb_ref, gt_ref, x1_ref, mod_ref, g_ref, o_ref, buf, sem):
    i = pl.program_id(0)
    n = pl.num_programs(0)
    tm = x1_ref.shape[0]

    def issue(tile, slot):
        base = tile * (tm * TOP_K)

        def body(j, carry):
            for k in range(TOP_K):
                src = dest_ref[base + j * TOP_K + k]
                pltpu.make_async_copy(yb_ref.at[pl.ds(src, 1), :], buf.at[slot, k, pl.ds(j, 1), :],
                                      sem.at[slot]).start()
            return carry

        lax.fori_loop(0, tm, body, 0, unroll=8)

    @pl.when(i == 0)
    def _():
        issue(0, 0)

    @pl.when(i + 1 < n)
    def _():
        issue(i + 1, (i + 1) % 2)

    slot = i % 2
    for k in range(TOP_K):
        pltpu.make_async_copy(buf.at[slot, k], buf.at[slot, k], sem.at[slot]).wait()
    gt = gt_ref[...]
    y = gt[:, 0:1] * buf[slot, 0]
    for k in range(1, TOP_K):
        y = y + gt[:, k:k + 1] * buf[slot, k]
    mod = mod_ref[0]
    o_ref[...] = x1_ref[...] + mod[5:6] * (_rms(y) * g_ref[...])


def _combine(yb, dest, gt, x1, mods, seq, g_post):
    rows, d = x1.shape
    tm = ROW_TILE
    tps = seq // tm
    row = lambda w: pl.BlockSpec((tm, w), lambda i, dst: (i, 0))
    grid_spec = pltpu.PrefetchScalarGridSpec(
        num_scalar_prefetch=1,
        grid=(rows // tm,),
        in_specs=[pl.BlockSpec(memory_space=pl.ANY), row(LANES), row(d),
                  pl.BlockSpec((1, SUBLANES, d), lambda i, dst: (i // tps, 0, 0)),
                  pl.BlockSpec((1, d), lambda i, dst: (0, 0))],
        out_specs=row(d),
        scratch_shapes=[pltpu.VMEM((2, TOP_K, tm, d), F32), pltpu.SemaphoreType.DMA((2,))],
    )
    return pl.pallas_call(
        _combine_kernel,
        grid_spec=grid_spec,
        out_shape=jax.ShapeDtypeStruct((rows, d), F32),
        compiler_params=_cparams("arbitrary"),
        name="combine",
    )(dest, yb, gt, x1, mods, g_post.reshape(1, d))


def _route_slots(top_idx, rank, tile_counts, n_tok):
    bm = MOE_ROWS
    tm = n_tok // tile_counts.shape[0]
    before_tile = jnp.cumsum(tile_counts, axis=0) - tile_counts
    counts = jnp.sum(tile_counts, axis=0)
    padded = ((counts + bm - 1) // bm) * bm
    p_end = jnp.cumsum(padded)
    p_start = p_end - padded
    table = jnp.repeat(p_start[None, :] + before_tile, tm, axis=0)
    hit = top_idx[:, :, None] == jnp.arange(N_EXPERTS, dtype=jnp.int32)[None, None, :]
    dest = rank + jnp.sum(jnp.where(hit, table[:, None, :], 0), axis=-1)
    n_blocks = -(-(n_tok * TOP_K) // bm) + N_EXPERTS
    starts = jnp.arange(n_blocks, dtype=jnp.int32) * bm
    block_exp = jnp.minimum(jnp.sum(p_end[None, :] <= starts[:, None], axis=1), N_EXPERTS - 1).astype(jnp.int32)
    n_used = (p_end[-1] // bm).astype(jnp.int32).reshape(1)
    return dest.reshape(-1).astype(jnp.int32), n_blocks, block_exp, n_used


def _moe(hf, top_idx, rank, tile_counts, p):
    n_tok, d = hf.shape
    dest, n_blocks, block_exp, n_used = _route_slots(top_idx, rank, tile_counts, n_tok)
    xb = _dispatch(hf, dest, n_blocks * MOE_ROWS)
    w_gu = p['w_gate_up']
    n_exp, _, de2 = w_gu.shape
    w_gu = jnp.swapaxes(w_gu, 1, 2).astype(BF16).reshape(n_exp, de2 // 2, 2 * d)
    b_gu = p['b_gate_up']
    b_gu = jnp.concatenate([b_gu[..., 0::2], b_gu[..., 1::2]], axis=-1)[:, None, :]
    yb = _experts(xb, block_exp, n_used, w_gu, b_gu, p['w_down'].astype(BF16), p['b_down'][:, None, :])
    return yb, dest


def _layer(x, c, ctx, c_ctx, p):
    batch, seq, d = x.shape
    ctx_len = ctx.shape[1]
    n_mod = p['w_ada'].shape[1] // d

    c_rows = jnp.zeros((2 * SUBLANES, d), F32).at[:batch].set(c).at[batch].set(c_ctx)
    mods = _ada(c_rows, p['w_ada'], p['b_ada']).reshape(2 * SUBLANES, n_mod, d)
    mods = jnp.pad(mods, ((0, 0), (0, SUBLANES - n_mod), (0, 0)))

    w_in = p['w_in'].astype(BF16)
    w_att, w_rwkv, w_gate = (w_in[:, :ATT_COLS], w_in[:, ATT_COLS:ATT_COLS + RWKV_COLS],
                             w_in[:, ATT_COLS + RWKV_COLS:])
    b_in = p['b_in'].reshape(1, -1)
    b_att, b_rwkv, b_gate = (b_in[:, :ATT_COLS], b_in[:, ATT_COLS:ATT_COLS + RWKV_COLS],
                             b_in[:, ATT_COLS + RWKV_COLS:])
    g_pre = p['g_pre_mix'].reshape(1, d)
    proj = functools.partial(_project, g_pre=g_pre, w_att=w_att, w_rwkv=w_rwkv, w_gate=w_gate,
                             b_att=b_att, b_rwkv=b_rwkv, b_gate=b_gate)

    zc_att, zc_rwkv, _ = proj(ctx.reshape(batch * ctx_len, d), mods, lambda b: batch, ctx_len, rope=False)
    fc = dict(zip(('r', 'v', 'kf', 'kb', 'lwf', 'lwb', 'kk', 'bf', 'bb', 'gate'), _features(zc_rwkv, ctx_len, p)))
    zero_state = jnp.zeros((batch, RWKV_HEADS, HEAD_DIM, HEAD_DIM), F32)
    _, _, sc_f, sc_b = _scan(fc, batch, ctx_len, zero_state, zero_state)

    x2 = x.reshape(batch * seq, d)
    z_att, z_rwkv, z_gate = proj(x2, mods, lambda b: b, seq, rope=True)
    fx = dict(zip(('r', 'v', 'kf', 'kb', 'lwf', 'lwb', 'kk', 'bf', 'bb', 'gate'), _features(z_rwkv, seq, p)))
    y_f, y_b, _, _ = _scan(fx, batch, seq, sc_f, sc_b)
    att = _attention(z_att, zc_att, p['att_sinks'], batch, seq)
    x1, hf, idx, gt, rank, cnt = _merge(y_f, y_b, fx, att, z_gate, x2, mods, seq, p)

    yb, dest = _moe(hf, idx[:, :TOP_K], rank[:, :TOP_K], cnt[:, 0, :N_EXPERTS], p)
    out = _combine(yb, dest, gt, x1, mods, seq, p['g_post_ffn'])
    return out.reshape(batch, seq, d)


def kernel(x, c, ctx, c_ctx, w_ada, b_ada, g_pre_mix, g_post_mix, g_pre_ffn, g_post_ffn, w_in, b_in, mu_prev, mu_next, att_sinks, w0_f, w0_b, w2_f, w2_b, a0_f, a0_b, a2_f, a2_b, g2, k_k, k_a, r_k, ln_x_w, ln_x_b, w_up_att, w_up_rwkv, w_out, w_router, b_router, w_gate_up, b_gate_up, w_down, b_down):
    assert w_ada.shape[0] == 1, "single-layer problem: the context stream update is never consumed"
    p = dict(w_ada=w_ada[0], b_ada=b_ada[0], g_pre_mix=g_pre_mix[0], g_post_mix=g_post_mix[0],
             g_pre_ffn=g_pre_ffn[0], g_post_ffn=g_post_ffn[0], w_in=w_in[0], b_in=b_in[0],
             mu_prev=mu_prev[0], mu_next=mu_next[0], att_sinks=att_sinks[0], w0_f=w0_f[0], w0_b=w0_b[0],
             w2_f=w2_f[0], w2_b=w2_b[0], a0_f=a0_f[0], a0_b=a0_b[0], a2_f=a2_f[0], a2_b=a2_b[0], g2=g2[0],
             k_k=k_k[0], k_a=k_a[0], r_k=r_k[0].reshape(-1), ln_x_w=ln_x_w[0], ln_x_b=ln_x_b[0],
             w_up_att=w_up_att[0], w_up_rwkv=w_up_rwkv[0], w_out=w_out[0], w_router=w_router[0],
             b_router=b_router[0], w_gate_up=w_gate_up[0], b_gate_up=b_gate_up[0], w_down=w_down[0],
             b_down=b_down[0])
    return _layer(x, c, ctx, c_ctx, p)
```

```python
import functools

import jax
import jax.numpy as jnp
import numpy as np
from jax import lax
from jax.experimental import pallas as pl
from jax.experimental.pallas import tpu as pltpu

F32 = jnp.float32
BF16 = jnp.bfloat16

GRID_W = 64
HEAD_DIM = 64
ATT_HEADS = 8
ATT_KV_HEADS = 2
ATT_GROUPS = ATT_HEADS // ATT_KV_HEADS
ATT_DIM = ATT_HEADS * HEAD_DIM
KV_DIM = ATT_KV_HEADS * HEAD_DIM
WINDOW = 128
BLOCK = 128
ROPE_BASE = 10000.0
ATT_SCALE = HEAD_DIM ** -0.5
NEG_INF = -1e30
RWKV_HEADS = 8
RWKV_DIM = RWKV_HEADS * HEAD_DIM
DECAY_LORA = 64
ICL_LORA = 64
GATE_LORA = 128
GN_EPS = 64e-5
N_EXPERTS = 32
TOP_K = 4
SWIGLU_LIMIT = 7.0
SWIGLU_ALPHA = 1.702
RMS_EPS = 1e-6
ATT_COLS = ATT_DIM + 2 * KV_DIM
RWKV_COLS = 3 * RWKV_DIM + 2 * DECAY_LORA + 2 * ICL_LORA + GATE_LORA
ROPE_COLS = ATT_DIM + KV_DIM

LANES = 128
SUBLANES = 8
VMEM_LIMIT = 48 * 1024 * 1024

ROW_TILE = 256
SCAN_CHUNK = 64
MOE_ROWS = 512


def _cparams(*sem):
    return pltpu.CompilerParams(dimension_semantics=sem, vmem_limit_bytes=VMEM_LIMIT)


def _mm(a, b):
    return jnp.dot(a.astype(BF16), b.astype(BF16), preferred_element_type=F32)


def _mm_nt(a, b):
    return lax.dot_general(a.astype(BF16), b.astype(BF16), (((1,), (1,)), ((), ())),
                           preferred_element_type=F32)


def _split2(x):
    hi = x.astype(BF16)
    lo = (x - hi.astype(F32)).astype(BF16)
    return hi, lo


def _split3(x):
    hi = x.astype(BF16)
    r1 = x - hi.astype(F32)
    mid = r1.astype(BF16)
    lo = (r1 - mid.astype(F32)).astype(BF16)
    return hi, mid, lo


def _mm_f32(a, b):
    ah, al = _split2(a)
    bh, bl = _split2(b)
    d = functools.partial(jnp.dot, preferred_element_type=F32)
    return d(ah, bh) + d(ah, bl) + d(al, bh)


def _seg_sum(x, eseg):
    hi, lo = _split2(x)
    d = functools.partial(jnp.dot, preferred_element_type=F32)
    return d(hi, eseg) + d(lo, eseg)


def _rms(x):
    return x * lax.rsqrt(jnp.mean(x * x, axis=-1, keepdims=True) + RMS_EPS)


def _sigmoid(x):
    return 1.0 / (1.0 + jnp.exp(-x))


def _ada_kernel(c_ref, w_ref, b_ref, o_ref):
    c = c_ref[...]
    o_ref[...] = _mm_f32(c * _sigmoid(c), w_ref[...]) + b_ref[...]


def _ada(c_rows, w, b):
    m, d = c_rows.shape
    n = w.shape[1]
    tn = 1536
    return pl.pallas_call(
        _ada_kernel,
        grid=(n // tn,),
        in_specs=[pl.BlockSpec((m, d), lambda j: (0, 0)),
                  pl.BlockSpec((d, tn), lambda j: (0, j)),
                  pl.BlockSpec((1, tn), lambda j: (0, j))],
        out_specs=pl.BlockSpec((m, tn), lambda j: (0, j)),
        out_shape=jax.ShapeDtypeStruct((m, n), F32),
        compiler_params=_cparams("arbitrary"),
        name="ada",
    )(c_rows, w, b.reshape(1, n))


def _proj_kernel(*refs, rope):
    if rope:
        (x_ref, mod_ref, g_ref, wa_ref, wr_ref, wg_ref, ba_ref, br_ref, bg_ref, cos_ref, sin_ref,
         za_ref, zr_ref, zg_ref) = refs
    else:
        (x_ref, mod_ref, g_ref, wa_ref, wr_ref, wg_ref, ba_ref, br_ref, bg_ref,
         za_ref, zr_ref, zg_ref) = refs
    mod = mod_ref[0]
    h = _rms(x_ref[...]) * g_ref[...] * (1.0 + mod[1:2]) + mod[0:1]
    hb = h.astype(BF16)
    za = jnp.dot(hb, wa_ref[...], preferred_element_type=F32) + ba_ref[...]
    if rope:
        qk = za[:, :ROPE_COLS]
        lane = lax.broadcasted_iota(jnp.int32, qk.shape, 1)
        low = (lane % 32) < 16
        partner = jnp.where(low, pltpu.roll(qk, ROPE_COLS - 16, 1), pltpu.roll(qk, 16, 1))
        za_ref[:, :ROPE_COLS] = (qk * cos_ref[...] + partner * sin_ref[...]).astype(za_ref.dtype)
        za_ref[:, ROPE_COLS:] = za[:, ROPE_COLS:].astype(za_ref.dtype)
    else:
        za_ref[...] = za.astype(za_ref.dtype)
    zr_ref[...] = jnp.dot(hb, wr_ref[...], preferred_element_type=F32) + br_ref[...]
    zg_ref[...] = (jnp.dot(hb, wg_ref[...], preferred_element_type=F32) + bg_ref[...]).astype(zg_ref.dtype)


def _rope_tables(seq):
    n_rows = seq // GRID_W
    row = jnp.repeat(jnp.arange(n_rows, dtype=F32), GRID_W, total_repeat_length=seq)
    col = jnp.tile(jnp.arange(GRID_W, dtype=F32), n_rows)
    half = HEAD_DIM // 2
    inv_freq = ROPE_BASE ** (-jnp.arange(0, half, 2, dtype=F32) / half)
    ang_r = row[:, None] * inv_freq[None, :]
    ang_c = col[:, None] * inv_freq[None, :]
    cos_h = jnp.concatenate([jnp.cos(ang_r), jnp.cos(ang_r), jnp.cos(ang_c), jnp.cos(ang_c)], axis=1)
    sin_h = jnp.concatenate([-jnp.sin(ang_r), jnp.sin(ang_r), -jnp.sin(ang_c), jnp.sin(ang_c)], axis=1)
    reps = ROPE_COLS // HEAD_DIM
    return jnp.tile(cos_h, (1, reps)), jnp.tile(sin_h, (1, reps))


def _project(x2, mods, mod_row, seq, g_pre, w_att, w_rwkv, w_gate, b_att, b_rwkv, b_gate, rope):
    rows, d = x2.shape
    tm = ROW_TILE
    tps = seq // tm
    const = lambda i: (0, 0)
    in_specs = [pl.BlockSpec((tm, d), lambda i: (i, 0)),
                pl.BlockSpec((1, SUBLANES, d), lambda i: (mod_row(i // tps), 0, 0)),
                pl.BlockSpec((1, d), const),
                pl.BlockSpec(w_att.shape, const), pl.BlockSpec(w_rwkv.shape, const),
                pl.BlockSpec(w_gate.shape, const),
                pl.BlockSpec((1, ATT_COLS), const), pl.BlockSpec((1, RWKV_COLS), const),
                pl.BlockSpec((1, w_gate.shape[1]), const)]
    args = [x2, mods, g_pre, w_att, w_rwkv, w_gate, b_att, b_rwkv, b_gate]
    if rope:
        cos, sin = _rope_tables(seq)
        in_specs += [pl.BlockSpec((tm, ROPE_COLS), lambda i: (i % tps, 0))] * 2
        args += [cos, sin]
    return pl.pallas_call(
        functools.partial(_proj_kernel, rope=rope),
        grid=(rows // tm,),
        in_specs=in_specs,
        out_specs=[pl.BlockSpec((tm, ATT_COLS), lambda i: (i, 0)),
                   pl.BlockSpec((tm, RWKV_COLS), lambda i: (i, 0)),
                   pl.BlockSpec((tm, w_gate.shape[1]), lambda i: (i, 0))],
        out_shape=[jax.ShapeDtypeStruct((rows, ATT_COLS), BF16),
                   jax.ShapeDtypeStruct((rows, RWKV_COLS), F32),
                   jax.ShapeDtypeStruct((rows, w_gate.shape[1]), BF16)],
        compiler_params=_cparams("parallel"),
        name="proj",
    )(*args)


def _feat_kernel(z_ref, zp_ref, zn_ref, mup_ref, mun_ref, w2_ref, w0_ref, a2_ref, a0_ref, g2_ref,
                 kk_ref, ka_ref, eseg_ref,
                 r_o, v_o, kf_o, kb_o, lwf_o, lwb_o, kkn_o, bf_o, bb_o, gate_o, *, tiles_per_seq):
    ti = pl.program_id(0) % tiles_per_seq
    z = z_ref[...]
    tm = z.shape[0]
    row = lax.broadcasted_iota(jnp.int32, (tm, 1), 0)
    prev_halo = jnp.where(ti == 0, 0.0, zp_ref[SUBLANES - 1:SUBLANES, :])
    next_halo = jnp.where(ti == tiles_per_seq - 1, 0.0, zn_ref[0:1, :])
    prev = jnp.where(row == 0, prev_halo, pltpu.roll(z, 1, 0))
    nxt = jnp.where(row == tm - 1, next_halo, pltpu.roll(z, tm - 1, 0))
    zs = z + mup_ref[...] * (prev - z) + mun_ref[...] * (nxt - z)

    d = RWKV_DIM
    r = zs[:, 0:d]
    k = zs[:, d:2 * d]
    v = zs[:, 2 * d:3 * d]
    o = 3 * d
    wl = zs[:, o:o + 2 * DECAY_LORA]
    al = zs[:, o + 2 * DECAY_LORA:o + 2 * DECAY_LORA + 2 * ICL_LORA]
    gl = zs[:, o + 2 * DECAY_LORA + 2 * ICL_LORA:]

    w = w0_ref[...] + _mm(jnp.tanh(wl), w2_ref[...])
    nw = -w
    softplus = jnp.maximum(nw, 0.0) + jnp.log(1.0 + jnp.exp(-jnp.abs(nw)))
    lw = -jnp.exp(-softplus - 0.5)
    icl = _sigmoid(a0_ref[...] + _mm(al, a2_ref[...]))
    kk0 = k * kk_ref[...]
    ss = _seg_sum(kk0 * kk0, eseg_ref[...])
    kk = kk0 / jnp.maximum(jnp.sqrt(ss), 1e-12)
    ka = ka_ref[...]
    icl_f = icl[:, :d]
    icl_b = icl[:, d:]

    r_o[...] = r.astype(r_o.dtype)
    v_o[...] = v.astype(v_o.dtype)
    kf_o[...] = (k * (1.0 + (icl_f - 1.0) * ka)).astype(kf_o.dtype)
    kb_o[...] = (k * (1.0 + (icl_b - 1.0) * ka)).astype(kb_o.dtype)
    lwf_o[...] = lw[:, :d]
    lwb_o[...] = lw[:, d:]
    kkn_o[...] = kk.astype(kkn_o.dtype)
    bf_o[...] = (kk * icl_f).astype(bf_o.dtype)
    bb_o[...] = (kk * icl_b).astype(bb_o.dtype)
    gate_o[...] = _mm(_sigmoid(gl), g2_ref[...]).astype(gate_o.dtype)


FEAT_NAMES = ('r', 'v', 'kf', 'kb', 'lwf', 'lwb', 'kk', 'bf', 'bb', 'gate')
FEAT_BF16 = ('r', 'v', 'kf', 'kb', 'kk', 'bf', 'bb', 'gate')


def _block_diag2(a, b):
    za = jnp.zeros_like(a)
    zb = jnp.zeros_like(b)
    return jnp.concatenate([jnp.concatenate([a, zb], axis=1), jnp.concatenate([za, b], axis=1)], axis=0)


def _seg_matrix():
    h = np.arange(RWKV_DIM) // HEAD_DIM
    return jnp.asarray((h[:, None] == h[None, :]).astype(np.float32), BF16)


def _features(z_rwkv, seq, p):
    rows = z_rwkv.shape[0]
    tm = ROW_TILE
    tps = seq // tm
    hb = tm // SUBLANES
    nhb = rows // SUBLANES
    d = RWKV_DIM
    const = lambda i: (0, 0)
    w2 = _block_diag2(p['w2_f'], p['w2_b']).astype(BF16)
    a2 = _block_diag2(p['a2_f'], p['a2_b']).astype(BF16)
    w0 = jnp.concatenate([p['w0_f'], p['w0_b']]).reshape(1, 2 * d)
    a0 = jnp.concatenate([p['a0_f'], p['a0_b']]).reshape(1, 2 * d)
    out_spec = pl.BlockSpec((tm, d), lambda i: (i, 0))
    return pl.pallas_call(
        functools.partial(_feat_kernel, tiles_per_seq=tps),
        grid=(rows // tm,),
        in_specs=[pl.BlockSpec((tm, RWKV_COLS), lambda i: (i, 0)),
                  pl.BlockSpec((SUBLANES, RWKV_COLS), lambda i: (jnp.maximum(i * hb - 1, 0), 0)),
                  pl.BlockSpec((SUBLANES, RWKV_COLS), lambda i: (jnp.minimum((i + 1) * hb, nhb - 1), 0)),
                  pl.BlockSpec((1, RWKV_COLS), const), pl.BlockSpec((1, RWKV_COLS), const),
                  pl.BlockSpec(w2.shape, const), pl.BlockSpec((1, 2 * d), const),
                  pl.BlockSpec(a2.shape, const), pl.BlockSpec((1, 2 * d), const),
                  pl.BlockSpec((GATE_LORA, d), const),
                  pl.BlockSpec((1, d), const), pl.BlockSpec((1, d), const),
                  pl.BlockSpec((d, d), const)],
        out_specs=[out_spec] * 10,
        out_shape=[jax.ShapeDtypeStruct((rows, d), BF16 if name in FEAT_BF16 else F32) for name in FEAT_NAMES],
        compiler_params=_cparams("parallel"),
        name="feat",
    )(z_rwkv, z_rwkv, z_rwkv, p['mu_prev'].reshape(1, -1), p['mu_next'].reshape(1, -1),
      w2, w0, a2, a0, p['g2'].astype(BF16), p['k_k'].reshape(1, d), p['k_a'].reshape(1, d), _seg_matrix())


def _scan_kernel(rf, vf, kf, lwf, kkf, bf, rb, vb, kb, lwb, kkb, bb, s0f_ref, s0b_ref,
                 yf_ref, yb_ref, sTf_ref, sTb_ref, state_ref, *, chunk, n_chunks):
    c = pl.program_id(1)
    C = chunk

    @pl.when(c == 0)
    def _():
        state_ref[0] = s0f_ref[0]
        state_ref[1] = s0b_ref[0]

    ti = lax.broadcasted_iota(jnp.int32, (C, C), 0)
    si = lax.broadcasted_iota(jnp.int32, (C, C), 1)
    ti2 = lax.broadcasted_iota(jnp.int32, (C, 2 * C), 0)
    si2 = lax.broadcasted_iota(jnp.int32, (C, 2 * C), 1) % C
    eye = (ti == si).astype(F32)
    n_double = int(np.log2(C)) - 1

    dirs = ((rf, vf, kf, lwf, kkf, bf), (rb, vb, kb, lwb, kkb, bb))
    units = []
    for d, (r_ref, v_ref, k_ref, lw_ref, kk_ref, b_ref) in enumerate(dirs):
        incl = (si <= ti) if d == 0 else (si >= ti)
        strict = (si < ti) if d == 0 else (si > ti)
        incl2 = (si2 <= ti2) if d == 0 else (si2 >= ti2)
        tri = incl.astype(BF16)
        lw = lw_ref[...]
        lh, lm, ll = _split3(lw)
        dd = functools.partial(jnp.dot, preferred_element_type=F32)
        cum = dd(tri, lh) + dd(tri, lm) + dd(tri, ll)
        cumx = cum - lw
        cum_end = cum[C - 1:C] if d == 0 else cum[0:1]
        e_neg = jnp.exp(-cum)
        e_end = jnp.exp(cum_end - cum)
        bv = b_ref[...].astype(F32)
        kv = k_ref[...].astype(F32)
        vv = v_ref[...].astype(F32)
        a_t = (-kk_ref[...].astype(F32) * jnp.exp(cumx)).astype(BF16)
        r_t = (r_ref[...].astype(F32) * jnp.exp(cum)).astype(BF16)
        b_t = (bv * e_neg).astype(BF16)
        k_t = (kv * e_neg).astype(BF16)
        b_q = (bv * e_end).astype(BF16)
        k_q = (kv * e_end).astype(BF16)
        g_end = jnp.exp(cum_end)
        for h in range(RWKV_HEADS):
            sl = slice(h * HEAD_DIM, (h + 1) * HEAD_DIM)
            units.append(dict(
                d=d, h=h, incl2=incl2, strict=strict,
                P=jnp.concatenate([a_t[:, sl], r_t[:, sl]], axis=0),
                Q=jnp.concatenate([b_t[:, sl], k_t[:, sl]], axis=0),
                Qq=jnp.concatenate([b_q[:, sl], k_q[:, sl]], axis=0),
                V=vv[:, sl], g_end=g_end[:, sl]))

    for u in units:
        u['G'] = _mm_nt(u['P'], u['Q'])
    for u in units:
        u['S0'] = state_ref[u['d'], u['h']]
        u['PH'] = _mm_nt(u['P'], u['S0'])
    for u in units:
        G = u.pop('G')
        u['N'] = jnp.where(u['strict'], G[:C, :C], 0.0)
        u['a_ak'] = jnp.where(u['strict'], G[:C, C:], 0.0).astype(BF16)
        u['a_r'] = jnp.where(u['incl2'], G[C:, :], 0.0).astype(BF16)
    for u in units:
        nb = u['N'].astype(BF16)
        u['T'] = eye + u['N']
        u['Pw'] = _mm(nb, nb)
        u['rhs'] = u['PH'][:C] + _mm(u['a_ak'], u['V'])
    for lvl in range(n_double):
        for u in units:
            pw = u['Pw'].astype(BF16)
            u['T'] = u['T'] + _mm(u['T'], pw)
            if lvl + 1 < n_double:
                u['Pw'] = _mm(pw, pw)
    for u in units:
        u['UV'] = jnp.concatenate([_mm(u['T'], u['rhs']), u['V']], axis=0)
    ys = [[None] * RWKV_HEADS, [None] * RWKV_HEADS]
    for u in units:
        ys[u['d']][u['h']] = u['PH'][C:] + _mm(u['a_r'], u['UV'])
        state_ref[u['d'], u['h']] = u['S0'] * u['g_end'] + _mm(u['UV'].T, u['Qq'])
    yf_ref[...] = jnp.concatenate(ys[0], axis=1)
    yb_ref[...] = jnp.concatenate(ys[1], axis=1)

    @pl.when(c == n_chunks - 1)
    def _():
        sTf_ref[0] = state_ref[0]
        sTb_ref[0] = state_ref[1]


def _scan(f, batch, seq, s0_f, s0_b):
    C = SCAN_CHUNK
    nC = seq // C
    d = RWKV_DIM
    fwd = pl.BlockSpec((C, d), lambda b, c: (b * nC + c, 0))
    bwd = pl.BlockSpec((C, d), lambda b, c: (b * nC + nC - 1 - c, 0))
    st = pl.BlockSpec((1, RWKV_HEADS, HEAD_DIM, HEAD_DIM), lambda b, c: (b, 0, 0, 0))
    st_shape = jax.ShapeDtypeStruct((batch, RWKV_HEADS, HEAD_DIM, HEAD_DIM), F32)
    y_shape = jax.ShapeDtypeStruct((batch * seq, d), F32)
    return pl.pallas_call(
        functools.partial(_scan_kernel, chunk=C, n_chunks=nC),
        grid=(batch, nC),
        in_specs=[fwd] * 6 + [bwd] * 6 + [st, st],
        out_specs=[fwd, bwd, st, st],
        out_shape=[y_shape, y_shape, st_shape, st_shape],
        scratch_shapes=[pltpu.VMEM((2, RWKV_HEADS, HEAD_DIM, HEAD_DIM), F32)],
        compiler_params=_cparams("parallel", "arbitrary"),
        name="scan",
    )(f['r'], f['v'], f['kf'], f['lwf'], f['kk'], f['bf'],
      f['r'], f['v'], f['kb'], f['lwb'], f['kk'], f['bb'], s0_f, s0_b)


def _attn_kernel(sink_ref, q_ref, kp_ref, km_ref, kn_ref, vp_ref, vm_ref, vn_ref, kc_ref, vc_ref, o_ref,
                 *, n_blocks):
    n = pl.program_id(1)
    rows = ATT_GROUPS * BLOCK
    q = q_ref[...] * ATT_SCALE
    qi = lax.broadcasted_iota(jnp.int32, (rows, BLOCK), 0) % BLOCK
    kj = lax.broadcasted_iota(jnp.int32, (rows, BLOCK), 1)
    ok_prev = (kj >= qi) & (n > 0)
    ok_next = (kj <= qi) & (n < n_blocks - 1)
    rowh = lax.broadcasted_iota(jnp.int32, (rows, 1), 0) // BLOCK
    outs = [None] * ATT_HEADS
    for g in range(ATT_KV_HEADS):
        ks = slice(g * HEAD_DIM, (g + 1) * HEAD_DIM)
        heads = [g * ATT_GROUPS + j for j in range(ATT_GROUPS)]
        Qs = jnp.concatenate([q[:, h * HEAD_DIM:(h + 1) * HEAD_DIM] for h in heads], axis=0)
        scores = [jnp.where(ok_prev, _mm_nt(Qs, kp_ref[:, ks]), NEG_INF),
                  _mm_nt(Qs, km_ref[:, ks]),
                  jnp.where(ok_next, _mm_nt(Qs, kn_ref[:, ks]), NEG_INF),
                  _mm_nt(Qs, kc_ref[:, ks])]
        values = [vp_ref, vm_ref, vn_ref, vc_ref]
        sink = jnp.zeros((rows, 1), F32)
        for j, h in enumerate(heads):
            sink = jnp.where(rowh == j, sink_ref[h], sink)
        m = sink
        for sc in scores:
            m = jnp.maximum(m, jnp.max(sc, axis=-1, keepdims=True))
        acc = jnp.zeros((rows, 2 * HEAD_DIM), F32)
        for sc, v_ref in zip(scores, values):
            one_col = (lax.broadcasted_iota(jnp.int32, (v_ref.shape[0], HEAD_DIM), 1) == 0).astype(BF16)
            v_ext = jnp.concatenate([v_ref[:, ks], one_col], axis=1)
            acc = acc + jnp.dot(jnp.exp(sc - m).astype(BF16), v_ext, preferred_element_type=F32)
        den = acc[:, HEAD_DIM:HEAD_DIM + 1] + jnp.exp(sink - m)
        O = acc[:, :HEAD_DIM] / den
        for j, h in enumerate(heads):
            outs[h] = O[j * BLOCK:(j + 1) * BLOCK]
    o_ref[...] = jnp.concatenate(outs, axis=1).astype(o_ref.dtype)


def _attention(z_att, zc_att, sinks, batch, seq):
    nb = seq // BLOCK
    ctx_len = zc_att.shape[0] // batch
    kcol = ATT_DIM // KV_DIM
    vcol = kcol + 1

    def kv_spec(col, off):
        return pl.BlockSpec((BLOCK, KV_DIM), lambda b, n: (b * nb + jnp.clip(n + off, 0, nb - 1), col))

    return pl.pallas_call(
        functools.partial(_attn_kernel, n_blocks=nb),
        grid=(batch, nb),
        in_specs=[pl.BlockSpec(memory_space=pltpu.SMEM),
                  pl.BlockSpec((BLOCK, ATT_DIM), lambda b, n: (b * nb + n, 0)),
                  kv_spec(kcol, -1), kv_spec(kcol, 0), kv_spec(kcol, 1),
                  kv_spec(vcol, -1), kv_spec(vcol, 0), kv_spec(vcol, 1),
                  pl.BlockSpec((ctx_len, KV_DIM), lambda b, n: (b, kcol)),
                  pl.BlockSpec((ctx_len, KV_DIM), lambda b, n: (b, vcol))],
        out_specs=pl.BlockSpec((BLOCK, ATT_DIM), lambda b, n: (b * nb + n, 0)),
        out_shape=jax.ShapeDtypeStruct((batch * seq, ATT_DIM), BF16),
        compiler_params=_cparams("parallel", "parallel"),
        name="attn",
    )(sinks, z_att, z_att, z_att, z_att, z_att, z_att, z_att, zc_att, zc_att)


def _merge_kernel(yf_ref, yb_ref, r_ref, kf_ref, kb_ref, v_ref, gate_ref, att_ref, zg_ref, x_ref, mod_ref,
                  lnw_ref, lnb_ref, rk_ref, eseg_ref, wua_ref, wur_ref, wo_ref, gpm_ref, gpf_ref,
                  wr_ref, br_ref,
                  x1_ref, hf_ref, idx_ref, gt_ref, rank_ref, cnt_ref, slots_ref):
    eseg = eseg_ref[...]
    inv_n = 1.0 / HEAD_DIM
    y = yf_ref[...] + yb_ref[...]
    mean = _seg_sum(y, eseg) * inv_n
    dy = y - mean
    var = _seg_sum(dy * dy, eseg) * inv_n
    yn = dy * lax.rsqrt(var + GN_EPS) * lnw_ref[...] + lnb_ref[...]
    f32 = lambda ref: ref[...].astype(F32)
    bonus = _seg_sum(f32(r_ref) * (f32(kf_ref) + f32(kb_ref)) * rk_ref[...], eseg) * f32(v_ref)
    rwk = (yn + bonus) * f32(gate_ref)

    d = x_ref.shape[1]
    zg = zg_ref[...].astype(F32)
    merged = (_sigmoid(zg[:, :d]) * jnp.dot(att_ref[...], wua_ref[...], preferred_element_type=F32)
              + _sigmoid(zg[:, d:]) * _mm(rwk, wur_ref[...]))
    mix = _mm(merged, wo_ref[...])
    mod = mod_ref[0]
    x1 = x_ref[...] + mod[2:3] * (_rms(mix) * gpm_ref[...])
    x1_ref[...] = x1
    hf = _rms(x1) * gpf_ref[...] * (1.0 + mod[4:5]) + mod[3:4]
    hf_ref[...] = hf.astype(hf_ref.dtype)

    logits = _mm_f32(hf, wr_ref[...]) + br_ref[...]
    lane = lax.broadcasted_iota(jnp.int32, logits.shape, 1).astype(F32)
    idx_out = jnp.zeros(logits.shape, F32)
    val_out = jnp.zeros(logits.shape, F32)
    picked = jnp.zeros(logits.shape, F32)
    sels = []
    top = None
    den = None
    for kth in range(TOP_K):
        m = jnp.max(logits, axis=-1, keepdims=True)
        sel = jnp.min(jnp.where(logits == m, lane, float(LANES)), axis=-1, keepdims=True)
        hit = lane == sel
        logits = jnp.where(hit, -jnp.inf, logits)
        picked = jnp.where(hit, 1.0, picked)
        sels.append(hit)
        if kth == 0:
            top = m
        e = jnp.exp(m - top)
        den = e if kth == 0 else den + e
        idx_out = jnp.where(lane == float(kth), sel, idx_out)
        val_out = jnp.where(lane == float(kth), e, val_out)
    idx_ref[...] = idx_out.astype(jnp.int32)
    gt_ref[...] = val_out / den

    tm = picked.shape[0]
    earlier = (lax.broadcasted_iota(jnp.int32, (tm, tm), 1)
               < lax.broadcasted_iota(jnp.int32, (tm, tm), 0)).astype(BF16)
    before = jnp.dot(earlier, picked.astype(BF16), preferred_element_type=F32)
    rank_out = jnp.zeros(logits.shape, F32)
    for kth, hit in enumerate(sels):
        rk = jnp.sum(jnp.where(hit, before, 0.0), axis=-1, keepdims=True)
        rank_out = jnp.where(lane == float(kth), rk, rank_out)
    rank_ref[...] = rank_out.astype(jnp.int32)
    cnt_ref[0] = jnp.broadcast_to(jnp.sum(picked, axis=0, keepdims=True), cnt_ref.shape[1:]).astype(jnp.int32)
    slots_ref[...] = jnp.zeros(slots_ref.shape, slots_ref.dtype)


def _moe_blocks(n_tok):
    return -(-(n_tok * TOP_K) // MOE_ROWS) + N_EXPERTS


def _merge(y_f, y_b, f, att, z_gate, x2, mods, seq, p):
    rows, d = x2.shape
    tm = ROW_TILE
    tps = seq // tm
    rd = RWKV_DIM
    const = lambda i: (0, 0)
    row = lambda w: pl.BlockSpec((tm, w), lambda i: (i, 0))
    vec = lambda w: pl.BlockSpec((1, w), const)
    w_router = jnp.pad(p['w_router'], ((0, 0), (0, LANES - N_EXPERTS)))
    b_router = jnp.pad(p['b_router'], (0, LANES - N_EXPERTS), constant_values=NEG_INF).reshape(1, LANES)
    n_slots = _moe_blocks(rows) * MOE_ROWS
    slab = n_slots // (rows // tm)
    assert slab * (rows // tm) == n_slots and slab % SUBLANES == 0
    return pl.pallas_call(
        _merge_kernel,
        grid=(rows // tm,),
        in_specs=[row(rd)] * 7 + [row(ATT_DIM), row(2 * d), row(d),
                  pl.BlockSpec((1, SUBLANES, d), lambda i: (i // tps, 0, 0)),
                  vec(rd), vec(rd), vec(rd), pl.BlockSpec((rd, rd), const),
                  pl.BlockSpec((ATT_DIM, d), const), pl.BlockSpec((rd, d), const), pl.BlockSpec((d, d), const),
                  vec(d), vec(d), pl.BlockSpec((d, LANES), const), vec(LANES)],
        out_specs=[row(d), row(d), row(LANES), row(LANES), row(LANES),
                   pl.BlockSpec((1, SUBLANES, LANES), lambda i: (i, 0, 0)),
                   pl.BlockSpec((slab, d), lambda i: (i, 0))],
        out_shape=[jax.ShapeDtypeStruct((rows, d), F32), jax.ShapeDtypeStruct((rows, d), F32),
                   jax.ShapeDtypeStruct((rows, LANES), jnp.int32), jax.ShapeDtypeStruct((rows, LANES), F32),
                   jax.ShapeDtypeStruct((rows, LANES), jnp.int32),
                   jax.ShapeDtypeStruct((rows // tm, SUBLANES, LANES), jnp.int32),
                   jax.ShapeDtypeStruct((n_slots, d), F32)],
        compiler_params=_cparams("parallel"),
        name="merge",
    )(y_f, y_b, f['r'], f['kf'], f['kb'], f['v'], f['gate'], att, z_gate, x2, mods,
      p['ln_x_w'].reshape(1, rd), p['ln_x_b'].reshape(1, rd), p['r_k'].reshape(1, rd), _seg_matrix(),
      p['w_up_att'].astype(BF16), p['w_up_rwkv'].astype(BF16), p['w_out'].astype(BF16),
      p['g_post_mix'].reshape(1, d), p['g_pre_ffn'].reshape(1, d), w_router, b_router)


def _moe_kernel(be_ref, nused_ref, x_ref, wgu_ref, bgu_ref, wdn_f32_ref, bdn_ref, o_ref, wdn_ref):
    i = pl.program_id(0)

    @pl.when((i == 0) | (be_ref[i] != be_ref[jnp.maximum(i - 1, 0)]))
    def _():
        wdn_ref[...] = wdn_f32_ref[...].astype(BF16)

    @pl.when(i < nused_ref[0])
    def _():
        de = wdn_ref.shape[0]
        d = x_ref.shape[1]
        x = x_ref[...]
        bgu = bgu_ref[...]
        gate = jnp.minimum(_mm_nt(x, wgu_ref[:, :d]) + bgu[:, :de], SWIGLU_LIMIT)
        up = jnp.clip(_mm_nt(x, wgu_ref[:, d:]) + bgu[:, de:], -SWIGLU_LIMIT, SWIGLU_LIMIT)
        act = (up + 1.0) * (gate * _sigmoid(SWIGLU_ALPHA * gate))
        o_ref[...] = _mm(act, wdn_ref[...]) + bdn_ref[...]

    @pl.when(i >= nused_ref[0])
    def _():
        o_ref[...] = jnp.zeros(o_ref.shape, o_ref.dtype)


def _experts(xb, block_exp, n_used, w_gu, b_gu, w_dn, b_dn):
    n_slots, d = xb.shape
    bm = MOE_ROWS
    n_blocks = n_slots // bm
    de = w_dn.shape[1]
    de2 = 2 * de
    grid_spec = pltpu.PrefetchScalarGridSpec(
        num_scalar_prefetch=2,
        grid=(n_blocks,),
        in_specs=[pl.BlockSpec((bm, d), lambda i, be, nu: (i, 0)),
                  pl.BlockSpec((None, de, 2 * d), lambda i, be, nu: (be[i], 0, 0)),
                  pl.BlockSpec((None, 1, de2), lambda i, be, nu: (be[i], 0, 0)),
                  pl.BlockSpec((None, de, d), lambda i, be, nu: (be[i], 0, 0)),
                  pl.BlockSpec((None, 1, d), lambda i, be, nu: (be[i], 0, 0))],
        out_specs=pl.BlockSpec((bm, d), lambda i, be, nu: (i, 0)),
        scratch_shapes=[pltpu.VMEM((de, d), BF16)],
    )
    return pl.pallas_call(
        _moe_kernel,
        grid_spec=grid_spec,
        out_shape=jax.ShapeDtypeStruct((n_slots, d), F32),
        compiler_params=_cparams("arbitrary"),
        name="moe",
    )(block_exp, n_used, xb, w_gu, b_gu, w_dn, b_dn)


def _dispatch_kernel(dest_ref, hf_ref, xb_in_ref, xb_ref, sem):
    del xb_in_ref
    tm = hf_ref.shape[0]
    base = pl.program_id(0) * (tm * TOP_K)

    def row_copy(j, slot):
        return pltpu.make_async_copy(hf_ref.at[pl.ds(j, 1), :], xb_ref.at[pl.ds(slot, 1), :], sem)

    def body(j, carry):
        for k in range(TOP_K):
            row_copy(j, dest_ref[base + j * TOP_K + k]).start()
        return carry

    lax.fori_loop(0, tm, body, 0, unroll=8)
    for _ in range(TOP_K):
        pltpu.make_async_copy(hf_ref, hf_ref, sem).wait()


def _dispatch(hf, dest, slots):
    rows, d = hf.shape
    n_slots = slots.shape[0]
    tm = ROW_TILE
    grid_spec = pltpu.PrefetchScalarGridSpec(
        num_scalar_prefetch=1,
        grid=(rows // tm,),
        in_specs=[pl.BlockSpec((tm, d), lambda i, dst: (i, 0)),
                  pl.BlockSpec(memory_space=pl.ANY)],
        out_specs=pl.BlockSpec(memory_space=pl.ANY),
        scratch_shapes=[pltpu.SemaphoreType.DMA(())],
    )
    return pl.pallas_call(
        _dispatch_kernel,
        grid_spec=grid_spec,
        out_shape=jax.ShapeDtypeStruct((n_slots, d), F32),
        input_output_aliases={2: 0},
        compiler_params=_cparams("arbitrary"),
        name="dispatch",
    )(dest, hf, slots)


def _combine_kernel(dest_ref, yb_ref, gt_ref, x1_ref, mod_ref, g_ref, o_ref, buf, sem):
    i = pl.program_id(0)
    n = pl.num_programs(0)
    tm = x1_ref.shape[0]

    def issue(tile, slot):
        base = tile * (tm * TOP_K)

        def body(j, carry):
            for k in range(TOP_K):
                src = dest_ref[base + j * TOP_K + k]
                pltpu.make_async_copy(yb_ref.at[pl.ds(src, 1), :], buf.at[slot, k, pl.ds(j, 1), :],
                                      sem.at[slot]).start()
            return carry

        lax.fori_loop(0, tm, body, 0, unroll=8)

    @pl.when(i == 0)
    def _():
        issue(0, 0)

    @pl.when(i + 1 < n)
    def _():
        issue(i + 1, (i + 1) % 2)

    slot = i % 2
    for k in range(TOP_K):
        pltpu.make_async_copy(buf.at[slot, k], buf.at[slot, k], sem.at[slot]).wait()
    gt = gt_ref[...]
    y = gt[:, 0:1] * buf[slot, 0]
    for k in range(1, TOP_K):
        y = y + gt[:, k:k + 1] * buf[slot, k]
    mod = mod_ref[0]
    o_ref[...] = x1_ref[...] + mod[5:6] * (_rms(y) * g_ref[...])


def _combine(yb, dest, gt, x1, mods, seq, g_post):
    rows, d = x1.shape
    tm = ROW_TILE
    tps = seq // tm
    row = lambda w: pl.BlockSpec((tm, w), lambda i, dst: (i, 0))
    grid_spec = pltpu.PrefetchScalarGridSpec(
        num_scalar_prefetch=1,
        grid=(rows // tm,),
        in_specs=[pl.BlockSpec(memory_space=pl.ANY), row(LANES), row(d),
                  pl.BlockSpec((1, SUBLANES, d), lambda i, dst: (i // tps, 0, 0)),
                  pl.BlockSpec((1, d), lambda i, dst: (0, 0))],
        out_specs=row(d),
        scratch_shapes=[pltpu.VMEM((2, TOP_K, tm, d), F32), pltpu.SemaphoreType.DMA((2,))],
    )
    return pl.pallas_call(
        _combine_kernel,
        grid_spec=grid_spec,
        out_shape=jax.ShapeDtypeStruct((rows, d), F32),
        compiler_params=_cparams("arbitrary"),
        name="combine",
    )(dest, yb, gt, x1, mods, g_post.reshape(1, d))


def _route_slots(top_idx, rank, tile_counts, n_tok):
    bm = MOE_ROWS
    tm = n_tok // tile_counts.shape[0]
    before_tile = jnp.cumsum(tile_counts, axis=0) - tile_counts
    counts = jnp.sum(tile_counts, axis=0)
    padded = ((counts + bm - 1) // bm) * bm
    p_end = jnp.cumsum(padded)
    p_start = p_end - padded
    table = jnp.repeat(p_start[None, :] + before_tile, tm, axis=0)
    hit = top_idx[:, :, None] == jnp.arange(N_EXPERTS, dtype=jnp.int32)[None, None, :]
    dest = rank + jnp.sum(jnp.where(hit, table[:, None, :], 0), axis=-1)
    n_blocks = _moe_blocks(n_tok)
    starts = jnp.arange(n_blocks, dtype=jnp.int32) * bm
    block_exp = jnp.minimum(jnp.sum(p_end[None, :] <= starts[:, None], axis=1), N_EXPERTS - 1).astype(jnp.int32)
    n_used = (p_end[-1] // bm).astype(jnp.int32).reshape(1)
    return dest.reshape(-1).astype(jnp.int32), n_blocks, block_exp, n_used


def _moe(hf, slots, top_idx, rank, tile_counts, p):
    n_tok, d = hf.shape
    dest, n_blocks, block_exp, n_used = _route_slots(top_idx, rank, tile_counts, n_tok)
    xb = _dispatch(hf, dest, slots)
    w_gu = p['w_gate_up']
    n_exp, _, de2 = w_gu.shape
    w_gu = jnp.swapaxes(w_gu, 1, 2).astype(BF16).reshape(n_exp, de2 // 2, 2 * d)
    b_gu = p['b_gate_up']
    b_gu = jnp.concatenate([b_gu[..., 0::2], b_gu[..., 1::2]], axis=-1)[:, None, :]
    yb = _experts(xb, block_exp, n_used, w_gu, b_gu, p['w_down'], p['b_down'][:, None, :])
    return yb, dest


def _layer(x, c, ctx, c_ctx, p):
    batch, seq, d = x.shape
    ctx_len = ctx.shape[1]
    n_mod = p['w_ada'].shape[1] // d

    c_rows = jnp.zeros((2 * SUBLANES, d), F32).at[:batch].set(c).at[batch].set(c_ctx)
    mods = _ada(c_rows, p['w_ada'], p['b_ada']).reshape(2 * SUBLANES, n_mod, d)
    mods = jnp.pad(mods, ((0, 0), (0, SUBLANES - n_mod), (0, 0)))

    w_in = p['w_in'].astype(BF16)
    w_att, w_rwkv, w_gate = (w_in[:, :ATT_COLS], w_in[:, ATT_COLS:ATT_COLS + RWKV_COLS],
                             w_in[:, ATT_COLS + RWKV_COLS:])
    b_in = p['b_in'].reshape(1, -1)
    b_att, b_rwkv, b_gate = (b_in[:, :ATT_COLS], b_in[:, ATT_COLS:ATT_COLS + RWKV_COLS],
                             b_in[:, ATT_COLS + RWKV_COLS:])
    g_pre = p['g_pre_mix'].reshape(1, d)
    proj = functools.partial(_project, g_pre=g_pre, w_att=w_att, w_rwkv=w_rwkv, w_gate=w_gate,
                             b_att=b_att, b_rwkv=b_rwkv, b_gate=b_gate)

    zc_att, zc_rwkv, _ = proj(ctx.reshape(batch * ctx_len, d), mods, lambda b: batch, ctx_len, rope=False)
    fc = dict(zip(FEAT_NAMES,_features(zc_rwkv, ctx_len, p)))
    zero_state = jnp.zeros((batch, RWKV_HEADS, HEAD_DIM, HEAD_DIM), F32)
    _, _, sc_f, sc_b = _scan(fc, batch, ctx_len, zero_state, zero_state)

    x2 = x.reshape(batch * seq, d)
    z_att, z_rwkv, z_gate = proj(x2, mods, lambda b: b, seq, rope=True)
    fx = dict(zip(FEAT_NAMES,_features(z_rwkv, seq, p)))
    y_f, y_b, _, _ = _scan(fx, batch, seq, sc_f, sc_b)
    att = _attention(z_att, zc_att, p['att_sinks'], batch, seq)
    x1, hf, idx, gt, rank, cnt, slots = _merge(y_f, y_b, fx, att, z_gate, x2, mods, seq, p)

    yb, dest = _moe(hf, slots, idx[:, :TOP_K], rank[:, :TOP_K], cnt[:, 0, :N_EXPERTS], p)
    out = _combine(yb, dest, gt, x1, mods, seq, p['g_post_ffn'])
    return out.reshape(batch, seq, d)


def kernel(x, c, ctx, c_ctx, w_ada, b_ada, g_pre_mix, g_post_mix, g_pre_ffn, g_post_ffn, w_in, b_in, mu_prev, mu_next, att_sinks, w0_f, w0_b, w2_f, w2_b, a0_f, a0_b, a2_f, a2_b, g2, k_k, k_a, r_k, ln_x_w, ln_x_b, w_up_att, w_up_rwkv, w_out, w_router, b_router, w_gate_up, b_gate_up, w_down, b_down):
    assert w_ada.shape[0] == 1, "single-layer problem: the context stream update is never consumed"
    p = dict(w_ada=w_ada[0], b_ada=b_ada[0], g_pre_mix=g_pre_mix[0], g_post_mix=g_post_mix[0],
             g_pre_ffn=g_pre_ffn[0], g_post_ffn=g_post_ffn[0], w_in=w_in[0], b_in=b_in[0],
             mu_prev=mu_prev[0], mu_next=mu_next[0], att_sinks=att_sinks[0], w0_f=w0_f[0], w0_b=w0_b[0],
             w2_f=w2_f[0], w2_b=w2_b[0], a0_f=a0_f[0], a0_b=a0_b[0], a2_f=a2_f[0], a2_b=a2_b[0], g2=g2[0],
             k_k=k_k[0], k_a=k_a[0], r_k=r_k[0].reshape(-1), ln_x_w=ln_x_w[0], ln_x_b=ln_x_b[0],
             w_up_att=w_up_att[0], w_up_rwkv=w_up_rwkv[0], w_out=w_out[0], w_router=w_router[0],
             b_router=b_router[0], w_gate_up=w_gate_up[0], b_gate_up=b_gate_up[0], w_down=w_down[0],
             b_down=b_down[0])
    return _layer(x, c, ctx, c_ctx, p)
```

```python
import functools

import jax
import jax.numpy as jnp
import numpy as np
from jax import lax
from jax.experimental import pallas as pl
from jax.experimental.pallas import tpu as pltpu

F32 = jnp.float32
BF16 = jnp.bfloat16

GRID_W = 64
HEAD_DIM = 64
ATT_HEADS = 8
ATT_KV_HEADS = 2
ATT_GROUPS = ATT_HEADS // ATT_KV_HEADS
ATT_DIM = ATT_HEADS * HEAD_DIM
KV_DIM = ATT_KV_HEADS * HEAD_DIM
WINDOW = 128
BLOCK = 128
ROPE_BASE = 10000.0
ATT_SCALE = HEAD_DIM ** -0.5
NEG_INF = -1e30
RWKV_HEADS = 8
RWKV_DIM = RWKV_HEADS * HEAD_DIM
DECAY_LORA = 64
ICL_LORA = 64
GATE_LORA = 128
GN_EPS = 64e-5
N_EXPERTS = 32
TOP_K = 4
SWIGLU_LIMIT = 7.0
SWIGLU_ALPHA = 1.702
RMS_EPS = 1e-6
ATT_COLS = ATT_DIM + 2 * KV_DIM
RWKV_COLS = 3 * RWKV_DIM + 2 * DECAY_LORA + 2 * ICL_LORA + GATE_LORA
ROPE_COLS = ATT_DIM + KV_DIM

LANES = 128
SUBLANES = 8
VMEM_LIMIT = 48 * 1024 * 1024

ROW_TILE = 256
SCAN_CHUNK = 64
SCAN_GROUP = 4
SCAN_BATCH = 4
MOE_ROWS = 512


def _cparams(*sem):
    return pltpu.CompilerParams(dimension_semantics=sem, vmem_limit_bytes=VMEM_LIMIT)


def _mm(a, b):
    return jnp.dot(a.astype(BF16), b.astype(BF16), preferred_element_type=F32)


def _mm_nt(a, b):
    return lax.dot_general(a.astype(BF16), b.astype(BF16), (((1,), (1,)), ((), ())),
                           preferred_element_type=F32)


def _split2(x):
    hi = x.astype(BF16)
    lo = (x - hi.astype(F32)).astype(BF16)
    return hi, lo


def _split3(x):
    hi = x.astype(BF16)
    r1 = x - hi.astype(F32)
    mid = r1.astype(BF16)
    lo = (r1 - mid.astype(F32)).astype(BF16)
    return hi, mid, lo


def _mm_f32(a, b):
    ah, al = _split2(a)
    bh, bl = _split2(b)
    d = functools.partial(jnp.dot, preferred_element_type=F32)
    return d(ah, bh) + d(ah, bl) + d(al, bh)


def _seg_sum(x, eseg):
    hi, lo = _split2(x)
    d = functools.partial(jnp.dot, preferred_element_type=F32)
    return d(hi, eseg) + d(lo, eseg)


def _rms(x):
    return x * lax.rsqrt(jnp.mean(x * x, axis=-1, keepdims=True) + RMS_EPS)


def _sigmoid(x):
    return 1.0 / (1.0 + jnp.exp(-x))


def _ada_kernel(c_ref, w_ref, b_ref, o_ref):
    c = c_ref[...]
    o_ref[...] = _mm_f32(c * _sigmoid(c), w_ref[...]) + b_ref[...]


def _ada(c_rows, w, b):
    m, d = c_rows.shape
    n = w.shape[1]
    tn = 1536
    return pl.pallas_call(
        _ada_kernel,
        grid=(n // tn,),
        in_specs=[pl.BlockSpec((m, d), lambda j: (0, 0)),
                  pl.BlockSpec((d, tn), lambda j: (0, j)),
                  pl.BlockSpec((1, tn), lambda j: (0, j))],
        out_specs=pl.BlockSpec((m, tn), lambda j: (0, j)),
        out_shape=jax.ShapeDtypeStruct((m, n), F32),
        compiler_params=_cparams("arbitrary"),
        name="ada",
    )(c_rows, w, b.reshape(1, n))


def _proj_kernel(*refs, rope):
    if rope:
        (x_ref, mod_ref, g_ref, wa_ref, wr_ref, wg_ref, ba_ref, br_ref, bg_ref, cos_ref, sin_ref,
         za_ref, zr_ref, zg_ref) = refs
    else:
        (x_ref, mod_ref, g_ref, wa_ref, wr_ref, wg_ref, ba_ref, br_ref, bg_ref,
         za_ref, zr_ref, zg_ref) = refs
    mod = mod_ref[0]
    h = _rms(x_ref[...]) * g_ref[...] * (1.0 + mod[1:2]) + mod[0:1]
    hb = h.astype(BF16)
    za = jnp.dot(hb, wa_ref[...], preferred_element_type=F32) + ba_ref[...]
    if rope:
        qk = za[:, :ROPE_COLS]
        lane = lax.broadcasted_iota(jnp.int32, qk.shape, 1)
        low = (lane % 32) < 16
        partner = jnp.where(low, pltpu.roll(qk, ROPE_COLS - 16, 1), pltpu.roll(qk, 16, 1))
        za_ref[:, :ROPE_COLS] = (qk * cos_ref[...] + partner * sin_ref[...]).astype(za_ref.dtype)
        za_ref[:, ROPE_COLS:] = za[:, ROPE_COLS:].astype(za_ref.dtype)
    else:
        za_ref[...] = za.astype(za_ref.dtype)
    zr_ref[...] = jnp.dot(hb, wr_ref[...], preferred_element_type=F32) + br_ref[...]
    zg_ref[...] = (jnp.dot(hb, wg_ref[...], preferred_element_type=F32) + bg_ref[...]).astype(zg_ref.dtype)


def _rope_tables(seq):
    n_rows = seq // GRID_W
    row = jnp.repeat(jnp.arange(n_rows, dtype=F32), GRID_W, total_repeat_length=seq)
    col = jnp.tile(jnp.arange(GRID_W, dtype=F32), n_rows)
    half = HEAD_DIM // 2
    inv_freq = ROPE_BASE ** (-jnp.arange(0, half, 2, dtype=F32) / half)
    ang_r = row[:, None] * inv_freq[None, :]
    ang_c = col[:, None] * inv_freq[None, :]
    cos_h = jnp.concatenate([jnp.cos(ang_r), jnp.cos(ang_r), jnp.cos(ang_c), jnp.cos(ang_c)], axis=1)
    sin_h = jnp.concatenate([-jnp.sin(ang_r), jnp.sin(ang_r), -jnp.sin(ang_c), jnp.sin(ang_c)], axis=1)
    reps = ROPE_COLS // HEAD_DIM
    return jnp.tile(cos_h, (1, reps)), jnp.tile(sin_h, (1, reps))


def _project(x2, mods, mod_row, seq, g_pre, w_att, w_rwkv, w_gate, b_att, b_rwkv, b_gate, rope):
    rows, d = x2.shape
    tm = ROW_TILE
    tps = seq // tm
    const = lambda i: (0, 0)
    in_specs = [pl.BlockSpec((tm, d), lambda i: (i, 0)),
                pl.BlockSpec((1, SUBLANES, d), lambda i: (mod_row(i // tps), 0, 0)),
                pl.BlockSpec((1, d), const),
                pl.BlockSpec(w_att.shape, const), pl.BlockSpec(w_rwkv.shape, const),
                pl.BlockSpec(w_gate.shape, const),
                pl.BlockSpec((1, ATT_COLS), const), pl.BlockSpec((1, RWKV_COLS), const),
                pl.BlockSpec((1, w_gate.shape[1]), const)]
    args = [x2, mods, g_pre, w_att, w_rwkv, w_gate, b_att, b_rwkv, b_gate]
    if rope:
        cos, sin = _rope_tables(seq)
        in_specs += [pl.BlockSpec((tm, ROPE_COLS), lambda i: (i % tps, 0))] * 2
        args += [cos, sin]
    return pl.pallas_call(
        functools.partial(_proj_kernel, rope=rope),
        grid=(rows // tm,),
        in_specs=in_specs,
        out_specs=[pl.BlockSpec((tm, ATT_COLS), lambda i: (i, 0)),
                   pl.BlockSpec((tm, RWKV_COLS), lambda i: (i, 0)),
                   pl.BlockSpec((tm, w_gate.shape[1]), lambda i: (i, 0))],
        out_shape=[jax.ShapeDtypeStruct((rows, ATT_COLS), BF16),
                   jax.ShapeDtypeStruct((rows, RWKV_COLS), F32),
                   jax.ShapeDtypeStruct((rows, w_gate.shape[1]), BF16)],
        compiler_params=_cparams("parallel"),
        name="proj",
    )(*args)


def _feat_kernel(z_ref, zp_ref, zn_ref, mup_ref, mun_ref, w2_ref, w0_ref, a2_ref, a0_ref, g2_ref,
                 kk_ref, ka_ref, eseg_ref,
                 r_o, v_o, kf_o, kb_o, lwf_o, lwb_o, kkn_o, bf_o, bb_o, gate_o, *, tiles_per_seq):
    ti = pl.program_id(0) % tiles_per_seq
    z = z_ref[...]
    tm = z.shape[0]
    row = lax.broadcasted_iota(jnp.int32, (tm, 1), 0)
    prev_halo = jnp.where(ti == 0, 0.0, zp_ref[SUBLANES - 1:SUBLANES, :])
    next_halo = jnp.where(ti == tiles_per_seq - 1, 0.0, zn_ref[0:1, :])
    prev = jnp.where(row == 0, prev_halo, pltpu.roll(z, 1, 0))
    nxt = jnp.where(row == tm - 1, next_halo, pltpu.roll(z, tm - 1, 0))
    zs = z + mup_ref[...] * (prev - z) + mun_ref[...] * (nxt - z)

    d = RWKV_DIM
    r = zs[:, 0:d]
    k = zs[:, d:2 * d]
    v = zs[:, 2 * d:3 * d]
    o = 3 * d
    wl = zs[:, o:o + 2 * DECAY_LORA]
    al = zs[:, o + 2 * DECAY_LORA:o + 2 * DECAY_LORA + 2 * ICL_LORA]
    gl = zs[:, o + 2 * DECAY_LORA + 2 * ICL_LORA:]

    w = w0_ref[...] + _mm(jnp.tanh(wl), w2_ref[...])
    nw = -w
    softplus = jnp.maximum(nw, 0.0) + jnp.log(1.0 + jnp.exp(-jnp.abs(nw)))
    lw = -jnp.exp(-softplus - 0.5)
    icl = _sigmoid(a0_ref[...] + _mm(al, a2_ref[...]))
    kk0 = k * kk_ref[...]
    ss = _seg_sum(kk0 * kk0, eseg_ref[...])
    kk = kk0 / jnp.maximum(jnp.sqrt(ss), 1e-12)
    ka = ka_ref[...]
    icl_f = icl[:, :d]
    icl_b = icl[:, d:]

    r_o[...] = r.astype(r_o.dtype)
    v_o[...] = v.astype(v_o.dtype)
    kf_o[...] = (k * (1.0 + (icl_f - 1.0) * ka)).astype(kf_o.dtype)
    kb_o[...] = (k * (1.0 + (icl_b - 1.0) * ka)).astype(kb_o.dtype)
    lwf_o[...] = lw[:, :d]
    lwb_o[...] = lw[:, d:]
    kkn_o[...] = kk.astype(kkn_o.dtype)
    bf_o[...] = (kk * icl_f).astype(bf_o.dtype)
    bb_o[...] = (kk * icl_b).astype(bb_o.dtype)
    gate_o[...] = _mm(_sigmoid(gl), g2_ref[...]).astype(gate_o.dtype)


FEAT_NAMES = ('r', 'v', 'kf', 'kb', 'lwf', 'lwb', 'kk', 'bf', 'bb', 'gate')
FEAT_BF16 = ('r', 'v', 'kf', 'kb', 'kk', 'bf', 'bb', 'gate')


def _block_diag2(a, b):
    za = jnp.zeros_like(a)
    zb = jnp.zeros_like(b)
    return jnp.concatenate([jnp.concatenate([a, zb], axis=1), jnp.concatenate([za, b], axis=1)], axis=0)


def _seg_matrix():
    h = np.arange(RWKV_DIM) // HEAD_DIM
    return jnp.asarray((h[:, None] == h[None, :]).astype(np.float32), BF16)


def _features(z_rwkv, seq, p):
    rows = z_rwkv.shape[0]
    tm = ROW_TILE
    tps = seq // tm
    hb = tm // SUBLANES
    nhb = rows // SUBLANES
    d = RWKV_DIM
    const = lambda i: (0, 0)
    w2 = _block_diag2(p['w2_f'], p['w2_b']).astype(BF16)
    a2 = _block_diag2(p['a2_f'], p['a2_b']).astype(BF16)
    w0 = jnp.concatenate([p['w0_f'], p['w0_b']]).reshape(1, 2 * d)
    a0 = jnp.concatenate([p['a0_f'], p['a0_b']]).reshape(1, 2 * d)
    out_spec = pl.BlockSpec((tm, d), lambda i: (i, 0))
    return pl.pallas_call(
        functools.partial(_feat_kernel, tiles_per_seq=tps),
        grid=(rows // tm,),
        in_specs=[pl.BlockSpec((tm, RWKV_COLS), lambda i: (i, 0)),
                  pl.BlockSpec((SUBLANES, RWKV_COLS), lambda i: (jnp.maximum(i * hb - 1, 0), 0)),
                  pl.BlockSpec((SUBLANES, RWKV_COLS), lambda i: (jnp.minimum((i + 1) * hb, nhb - 1), 0)),
                  pl.BlockSpec((1, RWKV_COLS), const), pl.BlockSpec((1, RWKV_COLS), const),
                  pl.BlockSpec(w2.shape, const), pl.BlockSpec((1, 2 * d), const),
                  pl.BlockSpec(a2.shape, const), pl.BlockSpec((1, 2 * d), const),
                  pl.BlockSpec((GATE_LORA, d), const),
                  pl.BlockSpec((1, d), const), pl.BlockSpec((1, d), const),
                  pl.BlockSpec((d, d), const)],
        out_specs=[out_spec] * 10,
        out_shape=[jax.ShapeDtypeStruct((rows, d), BF16 if name in FEAT_BF16 else F32) for name in FEAT_NAMES],
        compiler_params=_cparams("parallel"),
        name="feat",
    )(z_rwkv, z_rwkv, z_rwkv, p['mu_prev'].reshape(1, -1), p['mu_next'].reshape(1, -1),
      w2, w0, a2, a0, p['g2'].astype(BF16), p['k_k'].reshape(1, d), p['k_a'].reshape(1, d), _seg_matrix())


def _scan_kernel(rf, vf, kf, lwf, kkf, bf, rb, vb, kb, lwb, kkb, bb, s0f_ref, s0b_ref,
                 yf_ref, yb_ref, sTf_ref, sTb_ref, state_ref, *, chunk, n_chunks):
    c = pl.program_id(1)
    C = chunk
    hd = HEAD_DIM
    gw = SCAN_GROUP * hd
    n_groups = RWKV_HEADS // SCAN_GROUP
    assert C == hd, "the triangular masks below are shared between time and channel blocks"

    n_rows = state_ref.shape[0]

    @pl.when(c == 0)
    def _():
        for n in range(n_rows):
            for d, s0_ref in enumerate((s0f_ref, s0b_ref)):
                for g in range(n_groups):
                    state_ref[n, d, g] = jnp.zeros((gw, gw), F32)
                    for j in range(SCAN_GROUP):
                        state_ref[n, d, g, j * hd:(j + 1) * hd, j * hd:(j + 1) * hd] = (
                            s0_ref[n, g * SCAN_GROUP + j])

    ti = lax.broadcasted_iota(jnp.int32, (C, C), 0)
    si = lax.broadcasted_iota(jnp.int32, (C, C), 1)
    tg = lax.broadcasted_iota(jnp.int32, (C, gw), 0)
    sg = lax.broadcasted_iota(jnp.int32, (C, gw), 1) % C
    eye = (tg == sg).astype(F32)
    same_head = (lax.broadcasted_iota(jnp.int32, (gw, gw), 0) // hd
                 == lax.broadcasted_iota(jnp.int32, (gw, gw), 1) // hd)
    same_head_b = same_head.astype(BF16)
    n_double = int(np.log2(C)) - 1

    def bdiag(x_cat):
        return jnp.concatenate([x_cat.astype(BF16)] * SCAN_GROUP, axis=0) * same_head_b

    dirs = ((rf, vf, kf, lwf, kkf, bf), (rb, vb, kb, lwb, kkb, bb))
    units = []
    masks = [(((si <= ti) if d == 0 else (si >= ti)).astype(BF16),
              (sg <= tg) if d == 0 else (sg >= tg),
              (sg < tg) if d == 0 else (sg > tg)) for d in range(2)]
    for n, d in [(n, d) for n in range(n_rows) for d in range(2)]:
        r_ref, v_ref, k_ref, lw_ref, kk_ref, b_ref = (ref.at[n] for ref in dirs[d])
        tri, incl_g, strict_g = masks[d]
        lw = lw_ref[...]
        lh, lm, ll = _split3(lw)
        dd = functools.partial(jnp.dot, preferred_element_type=F32)
        cum = dd(tri, lh) + dd(tri, lm) + dd(tri, ll)
        cumx = cum - lw
        cum_end = cum[C - 1:C] if d == 0 else cum[0:1]
        e_neg = jnp.exp(-cum)
        e_end = jnp.exp(cum_end - cum)
        bv = b_ref[...].astype(F32)
        kv = k_ref[...].astype(F32)
        vv = v_ref[...].astype(F32)
        a_t = (-kk_ref[...].astype(F32) * jnp.exp(cumx)).astype(BF16)
        r_t = (r_ref[...].astype(F32) * jnp.exp(cum)).astype(BF16)
        b_t = (bv * e_neg).astype(BF16)
        k_t = (kv * e_neg).astype(BF16)
        b_q = (bv * e_end).astype(BF16)
        k_q = (kv * e_end).astype(BF16)
        g_end = jnp.exp(cum_end)
        for g in range(n_groups):
            sl = slice(g * gw, (g + 1) * gw)
            units.append(dict(
                n=n, d=d, g=g, sl=sl, incl=incl_g, strict=strict_g,
                P=jnp.concatenate([a_t[:, sl], r_t[:, sl]], axis=0),
                Q=jnp.concatenate([bdiag(b_t[:, sl]), bdiag(k_t[:, sl])], axis=0),
                Qq=jnp.concatenate([b_q[:, sl], k_q[:, sl]], axis=0),
                V=vv[:, sl], g_end=g_end[:, sl]))

    for u in units:
        u['G'] = _mm_nt(u['P'], u['Q'])
    for u in units:
        u['S0'] = state_ref[u['n'], u['d'], u['g']]
        u['PH'] = _mm_nt(u['P'], u['S0'])
        u['Vd'] = bdiag(u['V'])
    for u in units:
        G = u.pop('G')
        u['N'] = jnp.where(u['strict'], G[:C, :gw], 0.0)
        u['a_ak'] = jnp.where(u['strict'], G[:C, gw:], 0.0)
        u['a_rb'] = jnp.where(u['incl'], G[C:, :gw], 0.0)
        u['a_rk'] = jnp.where(u['incl'], G[C:, gw:], 0.0)
    for u in units:
        u['T'] = eye + u['N']
        u['Pw'] = _mm(u['N'], bdiag(u['N']))
        u['rhs'] = u['PH'][:C] + _mm(u['a_ak'], u['Vd'])
    for lvl in range(n_double):
        for u in units:
            pw = bdiag(u['Pw'])
            u['T'] = u['T'] + _mm(u['T'], pw)
            if lvl + 1 < n_double:
                u['Pw'] = _mm(u['Pw'], pw)
    for u in units:
        u['U'] = _mm(u['T'], bdiag(u['rhs']))
    y_refs = (yf_ref, yb_ref)
    for u in units:
        y_refs[u['d']][u['n'], :, u['sl']] = (u['PH'][C:] + _mm(u['a_rb'], bdiag(u['U']))
                                              + _mm(u['a_rk'], u['Vd']))
        uv = jnp.concatenate([u['U'], u['V']], axis=0)
        state_ref[u['n'], u['d'], u['g']] = jnp.where(
            same_head, u['S0'] * u['g_end'] + _mm(uv.T, u['Qq']), 0.0)

    @pl.when(c == n_chunks - 1)
    def _():
        for n in range(n_rows):
            for d, sT_ref in enumerate((sTf_ref, sTb_ref)):
                for g in range(n_groups):
                    for j in range(SCAN_GROUP):
                        sT_ref[n, g * SCAN_GROUP + j] = (
                            state_ref[n, d, g, j * hd:(j + 1) * hd, j * hd:(j + 1) * hd])


def _scan(f, batch, seq, s0_f, s0_b):
    C = SCAN_CHUNK
    nC = seq // C
    d = RWKV_DIM
    nb = max(n for n in range(1, SCAN_BATCH + 1) if batch % n == 0)
    fwd = pl.BlockSpec((nb, C, d), lambda b, c: (b, c, 0))
    bwd = pl.BlockSpec((nb, C, d), lambda b, c: (b, nC - 1 - c, 0))
    st = pl.BlockSpec((nb, RWKV_HEADS, HEAD_DIM, HEAD_DIM), lambda b, c: (b, 0, 0, 0))
    st_shape = jax.ShapeDtypeStruct((batch, RWKV_HEADS, HEAD_DIM, HEAD_DIM), F32)
    y_shape = jax.ShapeDtypeStruct((batch, seq, d), F32)
    gw = SCAN_GROUP * HEAD_DIM
    arr = lambda name: f[name].reshape(batch, seq, d)
    y_f, y_b, sT_f, sT_b = pl.pallas_call(
        functools.partial(_scan_kernel, chunk=C, n_chunks=nC),
        grid=(batch // nb, nC),
        in_specs=[fwd] * 6 + [bwd] * 6 + [st, st],
        out_specs=[fwd, bwd, st, st],
        out_shape=[y_shape, y_shape, st_shape, st_shape],
        scratch_shapes=[pltpu.VMEM((nb, 2, RWKV_HEADS // SCAN_GROUP, gw, gw), F32)],
        compiler_params=_cparams("parallel", "arbitrary"),
        name="scan",
    )(arr('r'), arr('v'), arr('kf'), arr('lwf'), arr('kk'), arr('bf'),
      arr('r'), arr('v'), arr('kb'), arr('lwb'), arr('kk'), arr('bb'), s0_f, s0_b)
    return y_f.reshape(batch * seq, d), y_b.reshape(batch * seq, d), sT_f, sT_b


def _attn_kernel(sink_ref, q_ref, kp_ref, km_ref, kn_ref, vp_ref, vm_ref, vn_ref, kc_ref, vc_ref, o_ref,
                 *, n_blocks):
    n = pl.program_id(1)
    rows = ATT_GROUPS * BLOCK
    q = q_ref[...] * ATT_SCALE
    qi = lax.broadcasted_iota(jnp.int32, (rows, BLOCK), 0) % BLOCK
    kj = lax.broadcasted_iota(jnp.int32, (rows, BLOCK), 1)
    ok_prev = (kj >= qi) & (n > 0)
    ok_next = (kj <= qi) & (n < n_blocks - 1)
    rowh = lax.broadcasted_iota(jnp.int32, (rows, 1), 0) // BLOCK
    outs = [None] * ATT_HEADS
    for g in range(ATT_KV_HEADS):
        ks = slice(g * HEAD_DIM, (g + 1) * HEAD_DIM)
        heads = [g * ATT_GROUPS + j for j in range(ATT_GROUPS)]
        Qs = jnp.concatenate([q[:, h * HEAD_DIM:(h + 1) * HEAD_DIM] for h in heads], axis=0)
        scores = [jnp.where(ok_prev, _mm_nt(Qs, kp_ref[:, ks]), NEG_INF),
                  _mm_nt(Qs, km_ref[:, ks]),
                  jnp.where(ok_next, _mm_nt(Qs, kn_ref[:, ks]), NEG_INF),
                  _mm_nt(Qs, kc_ref[:, ks])]
        values = [vp_ref, vm_ref, vn_ref, vc_ref]
        sink = jnp.zeros((rows, 1), F32)
        for j, h in enumerate(heads):
            sink = jnp.where(rowh == j, sink_ref[h], sink)
        folded = None
        for sc in scores:
            for c0 in range(0, sc.shape[1], BLOCK):
                blk = sc[:, c0:c0 + BLOCK]
                folded = blk if folded is None else jnp.maximum(folded, blk)
        m = jnp.maximum(sink, jnp.max(folded, axis=-1, keepdims=True))
        acc = jnp.zeros((rows, 2 * HEAD_DIM), F32)
        for sc, v_ref in zip(scores, values):
            one_col = (lax.broadcasted_iota(jnp.int32, (v_ref.shape[0], HEAD_DIM), 1) == 0).astype(BF16)
            v_ext = jnp.concatenate([v_ref[:, ks], one_col], axis=1)
            acc = acc + jnp.dot(jnp.exp((sc - m).astype(BF16)), v_ext, preferred_element_type=F32)
        den = acc[:, HEAD_DIM:HEAD_DIM + 1] + jnp.exp(sink - m)
        O = acc[:, :HEAD_DIM] / den
        for j, h in enumerate(heads):
            outs[h] = O[j * BLOCK:(j + 1) * BLOCK]
    o_ref[...] = jnp.concatenate(outs, axis=1).astype(o_ref.dtype)


def _attention(z_att, zc_att, sinks, batch, seq):
    nb = seq // BLOCK
    ctx_len = zc_att.shape[0] // batch
    kcol = ATT_DIM // KV_DIM
    vcol = kcol + 1

    def kv_spec(col, off):
        return pl.BlockSpec((BLOCK, KV_DIM), lambda b, n: (b * nb + jnp.clip(n + off, 0, nb - 1), col))

    return pl.pallas_call(
        functools.partial(_attn_kernel, n_blocks=nb),
        grid=(batch, nb),
        in_specs=[pl.BlockSpec(memory_space=pltpu.SMEM),
                  pl.BlockSpec((BLOCK, ATT_DIM), lambda b, n: (b * nb + n, 0)),
                  kv_spec(kcol, -1), kv_spec(kcol, 0), kv_spec(kcol, 1),
                  kv_spec(vcol, -1), kv_spec(vcol, 0), kv_spec(vcol, 1),
                  pl.BlockSpec((ctx_len, KV_DIM), lambda b, n: (b, kcol)),
                  pl.BlockSpec((ctx_len, KV_DIM), lambda b, n: (b, vcol))],
        out_specs=pl.BlockSpec((BLOCK, ATT_DIM), lambda b, n: (b * nb + n, 0)),
        out_shape=jax.ShapeDtypeStruct((batch * seq, ATT_DIM), BF16),
        compiler_params=_cparams("parallel", "parallel"),
        name="attn",
    )(sinks, z_att, z_att, z_att, z_att, z_att, z_att, z_att, zc_att, zc_att)


def _merge_kernel(yf_ref, yb_ref, r_ref, kf_ref, kb_ref, v_ref, gate_ref, att_ref, zg_ref, x_ref, mod_ref,
                  lnw_ref, lnb_ref, rk_ref, eseg_ref, wua_ref, wur_ref, wo_ref, gpm_ref, gpf_ref,
                  wr_ref, br_ref,
                  x1_ref, hf_ref, idx_ref, gt_ref, rank_ref, cnt_ref, slots_ref):
    eseg = eseg_ref[...]
    inv_n = 1.0 / HEAD_DIM
    y = yf_ref[...] + yb_ref[...]
    mean = _seg_sum(y, eseg) * inv_n
    dy = y - mean
    var = _seg_sum(dy * dy, eseg) * inv_n
    yn = dy * lax.rsqrt(var + GN_EPS) * lnw_ref[...] + lnb_ref[...]
    f32 = lambda ref: ref[...].astype(F32)
    bonus = _seg_sum(f32(r_ref) * (f32(kf_ref) + f32(kb_ref)) * rk_ref[...], eseg) * f32(v_ref)
    rwk = (yn + bonus) * f32(gate_ref)

    d = x_ref.shape[1]
    zg = zg_ref[...].astype(F32)
    merged = (_sigmoid(zg[:, :d]) * jnp.dot(att_ref[...], wua_ref[...], preferred_element_type=F32)
              + _sigmoid(zg[:, d:]) * _mm(rwk, wur_ref[...]))
    mix = _mm(merged, wo_ref[...])
    mod = mod_ref[0]
    x1 = x_ref[...] + mod[2:3] * (_rms(mix) * gpm_ref[...])
    x1_ref[...] = x1
    hf = _rms(x1) * gpf_ref[...] * (1.0 + mod[4:5]) + mod[3:4]
    hf_ref[...] = hf.astype(hf_ref.dtype)

    logits = _mm_f32(hf, wr_ref[...]) + br_ref[...]
    lane = lax.broadcasted_iota(jnp.int32, logits.shape, 1).astype(F32)
    idx_out = jnp.zeros(logits.shape, F32)
    val_out = jnp.zeros(logits.shape, F32)
    picked = jnp.zeros(logits.shape, F32)
    sels = []
    top = None
    den = None
    for kth in range(TOP_K):
        m = jnp.max(logits, axis=-1, keepdims=True)
        sel = jnp.min(jnp.where(logits == m, lane, float(LANES)), axis=-1, keepdims=True)
        hit = lane == sel
        logits = jnp.where(hit, -jnp.inf, logits)
        picked = jnp.where(hit, 1.0, picked)
        sels.append(hit)
        if kth == 0:
            top = m
        e = jnp.exp(m - top)
        den = e if kth == 0 else den + e
        idx_out = jnp.where(lane == float(kth), sel, idx_out)
        val_out = jnp.where(lane == float(kth), e, val_out)
    idx_ref[...] = idx_out.astype(jnp.int32)
    gt_ref[...] = val_out / den

    tm = picked.shape[0]
    earlier = (lax.broadcasted_iota(jnp.int32, (tm, tm), 1)
               < lax.broadcasted_iota(jnp.int32, (tm, tm), 0)).astype(BF16)
    before = jnp.dot(earlier, picked.astype(BF16), preferred_element_type=F32)
    rank_out = jnp.zeros(logits.shape, F32)
    for kth, hit in enumerate(sels):
        rk = jnp.sum(jnp.where(hit, before, 0.0), axis=-1, keepdims=True)
        rank_out = jnp.where(lane == float(kth), rk, rank_out)
    rank_ref[...] = rank_out.astype(jnp.int32)
    cnt_ref[0] = jnp.broadcast_to(jnp.sum(picked, axis=0, keepdims=True), cnt_ref.shape[1:]).astype(jnp.int32)
    slots_ref[...] = jnp.zeros(slots_ref.shape, slots_ref.dtype)


def _moe_blocks(n_tok):
    return -(-(n_tok * TOP_K) // MOE_ROWS) + N_EXPERTS


def _merge(y_f, y_b, f, att, z_gate, x2, mods, seq, p):
    rows, d = x2.shape
    tm = ROW_TILE
    tps = seq // tm
    rd = RWKV_DIM
    const = lambda i: (0, 0)
    row = lambda w: pl.BlockSpec((tm, w), lambda i: (i, 0))
    vec = lambda w: pl.BlockSpec((1, w), const)
    w_router = jnp.pad(p['w_router'], ((0, 0), (0, LANES - N_EXPERTS)))
    b_router = jnp.pad(p['b_router'], (0, LANES - N_EXPERTS), constant_values=NEG_INF).reshape(1, LANES)
    n_slots = _moe_blocks(rows) * MOE_ROWS
    slab = n_slots // (rows // tm)
    assert slab * (rows // tm) == n_slots and slab % SUBLANES == 0
    return pl.pallas_call(
        _merge_kernel,
        grid=(rows // tm,),
        in_specs=[row(rd)] * 7 + [row(ATT_DIM), row(2 * d), row(d),
                  pl.BlockSpec((1, SUBLANES, d), lambda i: (i // tps, 0, 0)),
                  vec(rd), vec(rd), vec(rd), pl.BlockSpec((rd, rd), const),
                  pl.BlockSpec((ATT_DIM, d), const), pl.BlockSpec((rd, d), const), pl.BlockSpec((d, d), const),
                  vec(d), vec(d), pl.BlockSpec((d, LANES), const), vec(LANES)],
        out_specs=[row(d), row(d), row(LANES), row(LANES), row(LANES),
                   pl.BlockSpec((1, SUBLANES, LANES), lambda i: (i, 0, 0)),
                   pl.BlockSpec((slab, d), lambda i: (i, 0))],
        out_shape=[jax.ShapeDtypeStruct((rows, d), F32), jax.ShapeDtypeStruct((rows, d), F32),
                   jax.ShapeDtypeStruct((rows, LANES), jnp.int32), jax.ShapeDtypeStruct((rows, LANES), F32),
                   jax.ShapeDtypeStruct((rows, LANES), jnp.int32),
                   jax.ShapeDtypeStruct((rows // tm, SUBLANES, LANES), jnp.int32),
                   jax.ShapeDtypeStruct((n_slots, d), F32)],
        compiler_params=_cparams("parallel"),
        name="merge",
    )(y_f, y_b, f['r'], f['kf'], f['kb'], f['v'], f['gate'], att, z_gate, x2, mods,
      p['ln_x_w'].reshape(1, rd), p['ln_x_b'].reshape(1, rd), p['r_k'].reshape(1, rd), _seg_matrix(),
      p['w_up_att'].astype(BF16), p['w_up_rwkv'].astype(BF16), p['w_out'].astype(BF16),
      p['g_post_mix'].reshape(1, d), p['g_pre_ffn'].reshape(1, d), w_router, b_router)


def _moe_kernel(be_ref, nused_ref, x_ref, wgu_ref, bgu_ref, wdn_f32_ref, bdn_ref, o_ref, wdn_ref):
    i = pl.program_id(0)

    @pl.when((i == 0) | (be_ref[i] != be_ref[jnp.maximum(i - 1, 0)]))
    def _():
        wdn_ref[...] = wdn_f32_ref[...].astype(BF16)

    @pl.when(i < nused_ref[0])
    def _():
        de = wdn_ref.shape[0]
        d = x_ref.shape[1]
        x = x_ref[...]
        bgu = bgu_ref[...]
        gate = jnp.minimum(_mm_nt(x, wgu_ref[:, :d]) + bgu[:, :de], SWIGLU_LIMIT)
        up = jnp.clip(_mm_nt(x, wgu_ref[:, d:]) + bgu[:, de:], -SWIGLU_LIMIT, SWIGLU_LIMIT)
        act = (up + 1.0) * (gate * _sigmoid(SWIGLU_ALPHA * gate))
        o_ref[...] = _mm(act, wdn_ref[...]) + bdn_ref[...]

    @pl.when(i >= nused_ref[0])
    def _():
        o_ref[...] = jnp.zeros(o_ref.shape, o_ref.dtype)


def _experts(xb, block_exp, n_used, w_gu, b_gu, w_dn, b_dn):
    n_slots, d = xb.shape
    bm = MOE_ROWS
    n_blocks = n_slots // bm
    de = w_dn.shape[1]
    de2 = 2 * de
    grid_spec = pltpu.PrefetchScalarGridSpec(
        num_scalar_prefetch=2,
        grid=(n_blocks,),
        in_specs=[pl.BlockSpec((bm, d), lambda i, be, nu: (i, 0)),
                  pl.BlockSpec((None, de, 2 * d), lambda i, be, nu: (be[i], 0, 0)),
                  pl.BlockSpec((None, 1, de2), lambda i, be, nu: (be[i], 0, 0)),
                  pl.BlockSpec((None, de, d), lambda i, be, nu: (be[i], 0, 0)),
                  pl.BlockSpec((None, 1, d), lambda i, be, nu: (be[i], 0, 0))],
        out_specs=pl.BlockSpec((bm, d), lambda i, be, nu: (i, 0)),
        scratch_shapes=[pltpu.VMEM((de, d), BF16)],
    )
    return pl.pallas_call(
        _moe_kernel,
        grid_spec=grid_spec,
        out_shape=jax.ShapeDtypeStruct((n_slots, d), F32),
        compiler_params=_cparams("arbitrary"),
        name="moe",
    )(block_exp, n_used, xb, w_gu, b_gu, w_dn, b_dn)


def _dispatch_kernel(dest_ref, hf_ref, xb_in_ref, xb_ref, sem):
    del xb_in_ref
    tm = hf_ref.shape[0]
    base = pl.program_id(0) * (tm * TOP_K)

    def row_copy(j, slot):
        return pltpu.make_async_copy(hf_ref.at[pl.ds(j, 1), :], xb_ref.at[pl.ds(slot, 1), :], sem)

    def body(j, carry):
        for k in range(TOP_K):
            row_copy(j, dest_ref[base + j * TOP_K + k]).start()
        return carry

    lax.fori_loop(0, tm, body, 0, unroll=8)
    for _ in range(TOP_K):
        pltpu.make_async_copy(hf_ref, hf_ref, sem).wait()


def _dispatch(hf, dest, slots):
    rows, d = hf.shape
    n_slots = slots.shape[0]
    tm = ROW_TILE
    grid_spec = pltpu.PrefetchScalarGridSpec(
        num_scalar_prefetch=1,
        grid=(rows // tm,),
        in_specs=[pl.BlockSpec((tm, d), lambda i, dst: (i, 0)),
                  pl.BlockSpec(memory_space=pl.ANY)],
        out_specs=pl.BlockSpec(memory_space=pl.ANY),
        scratch_shapes=[pltpu.SemaphoreType.DMA(())],
    )
    return pl.pallas_call(
        _dispatch_kernel,
        grid_spec=grid_spec,
        out_shape=jax.ShapeDtypeStruct((n_slots, d), F32),
        input_output_aliases={2: 0},
        compiler_params=_cparams("arbitrary"),
        name="dispatch",
    )(dest, hf, slots)


def _combine_kernel(dest_ref, yb_ref, gt_ref, x1_ref, mod_ref, g_ref, o_ref, buf, sem):
    i = pl.program_id(0)
    n = pl.num_programs(0)
    tm = x1_ref.shape[0]

    def issue(tile, slot):
        base = tile * (tm * TOP_K)

        def body(j, carry):
            for k in range(TOP_K):
                src = dest_ref[base + j * TOP_K + k]
                pltpu.make_async_copy(yb_ref.at[pl.ds(src, 1), :], buf.at[slot, k, pl.ds(j, 1), :],
                                      sem.at[slot]).start()
            return carry

        lax.fori_loop(0, tm, body, 0, unroll=8)

    @pl.when(i == 0)
    def _():
        issue(0, 0)

    @pl.when(i + 1 < n)
    def _():
        issue(i + 1, (i + 1) % 2)

    slot = i % 2
    for k in range(TOP_K):
        pltpu.make_async_copy(buf.at[slot, k], buf.at[slot, k], sem.at[slot]).wait()
    gt = gt_ref[...]
    y = gt[:, 0:1] * buf[slot, 0]
    for k in range(1, TOP_K):
        y = y + gt[:, k:k + 1] * buf[slot, k]
    mod = mod_ref[0]
    o_ref[...] = x1_ref[...] + mod[5:6] * (_rms(y) * g_ref[...])


def _combine(yb, dest, gt, x1, mods, seq, g_post):
    rows, d = x1.shape
    tm = ROW_TILE
    tps = seq // tm
    row = lambda w: pl.BlockSpec((tm, w), lambda i, dst: (i, 0))
    grid_spec = pltpu.PrefetchScalarGridSpec(
        num_scalar_prefetch=1,
        grid=(rows // tm,),
        in_specs=[pl.BlockSpec(memory_space=pl.ANY), row(LANES), row(d),
                  pl.BlockSpec((1, SUBLANES, d), lambda i, dst: (i // tps, 0, 0)),
                  pl.BlockSpec((1, d), lambda i, dst: (0, 0))],
        out_specs=row(d),
        scratch_shapes=[pltpu.VMEM((2, TOP_K, tm, d), F32), pltpu.SemaphoreType.DMA((2,))],
    )
    return pl.pallas_call(
        _combine_kernel,
        grid_spec=grid_spec,
        out_shape=jax.ShapeDtypeStruct((rows, d), F32),
        compiler_params=_cparams("arbitrary"),
        name="combine",
    )(dest, yb, gt, x1, mods, g_post.reshape(1, d))


def _route_slots(top_idx, rank, tile_counts, n_tok):
    bm = MOE_ROWS
    tm = n_tok // tile_counts.shape[0]
    before_tile = jnp.cumsum(tile_counts, axis=0) - tile_counts
    counts = jnp.sum(tile_counts, axis=0)
    padded = ((counts + bm - 1) // bm) * bm
    p_end = jnp.cumsum(padded)
    p_start = p_end - padded
    table = jnp.repeat(p_start[None, :] + before_tile, tm, axis=0)
    hit = top_idx[:, :, None] == jnp.arange(N_EXPERTS, dtype=jnp.int32)[None, None, :]
    dest = rank + jnp.sum(jnp.where(hit, table[:, None, :], 0), axis=-1)
    n_blocks = _moe_blocks(n_tok)
    starts = jnp.arange(n_blocks, dtype=jnp.int32) * bm
    block_exp = jnp.minimum(jnp.sum(p_end[None, :] <= starts[:, None], axis=1), N_EXPERTS - 1).astype(jnp.int32)
    n_used = (p_end[-1] // bm).astype(jnp.int32).reshape(1)
    return dest.reshape(-1).astype(jnp.int32), n_blocks, block_exp, n_used


def _moe(hf, slots, top_idx, rank, tile_counts, p):
    n_tok, d = hf.shape
    dest, n_blocks, block_exp, n_used = _route_slots(top_idx, rank, tile_counts, n_tok)
    xb = _dispatch(hf, dest, slots)
    w_gu = p['w_gate_up']
    n_exp, _, de2 = w_gu.shape
    w_gu = jnp.swapaxes(w_gu.astype(BF16), 1, 2).reshape(n_exp, de2 // 2, 2 * d)
    b_gu = p['b_gate_up']
    b_gu = jnp.concatenate([b_gu[..., 0::2], b_gu[..., 1::2]], axis=-1)[:, None, :]
    yb = _experts(xb, block_exp, n_used, w_gu, b_gu, p['w_down'], p['b_down'][:, None, :])
    return yb, dest


def _layer(x, c, ctx, c_ctx, p):
    batch, seq, d = x.shape
    ctx_len = ctx.shape[1]
    n_mod = p['w_ada'].shape[1] // d

    c_rows = jnp.zeros((2 * SUBLANES, d), F32).at[:batch].set(c).at[batch].set(c_ctx)
    mods = _ada(c_rows, p['w_ada'], p['b_ada']).reshape(2 * SUBLANES, n_mod, d)
    mods = jnp.pad(mods, ((0, 0), (0, SUBLANES - n_mod), (0, 0)))

    w_in = p['w_in'].astype(BF16)
    w_att, w_rwkv, w_gate = (w_in[:, :ATT_COLS], w_in[:, ATT_COLS:ATT_COLS + RWKV_COLS],
                             w_in[:, ATT_COLS + RWKV_COLS:])
    b_in = p['b_in'].reshape(1, -1)
    b_att, b_rwkv, b_gate = (b_in[:, :ATT_COLS], b_in[:, ATT_COLS:ATT_COLS + RWKV_COLS],
                             b_in[:, ATT_COLS + RWKV_COLS:])
    g_pre = p['g_pre_mix'].reshape(1, d)
    proj = functools.partial(_project, g_pre=g_pre, w_att=w_att, w_rwkv=w_rwkv, w_gate=w_gate,
                             b_att=b_att, b_rwkv=b_rwkv, b_gate=b_gate)

    zc_att, zc_rwkv, _ = proj(ctx.reshape(batch * ctx_len, d), mods, lambda b: batch, ctx_len, rope=False)
    fc = dict(zip(FEAT_NAMES,_features(zc_rwkv, ctx_len, p)))
    zero_state = jnp.zeros((batch, RWKV_HEADS, HEAD_DIM, HEAD_DIM), F32)
    _, _, sc_f, sc_b = _scan(fc, batch, ctx_len, zero_state, zero_state)

    x2 = x.reshape(batch * seq, d)
    z_att, z_rwkv, z_gate = proj(x2, mods, lambda b: b, seq, rope=True)
    fx = dict(zip(FEAT_NAMES,_features(z_rwkv, seq, p)))
    y_f, y_b, _, _ = _scan(fx, batch, seq, sc_f, sc_b)
    att = _attention(z_att, zc_att, p['att_sinks'], batch, seq)
    x1, hf, idx, gt, rank, cnt, slots = _merge(y_f, y_b, fx, att, z_gate, x2, mods, seq, p)

    yb, dest = _moe(hf, slots, idx[:, :TOP_K], rank[:, :TOP_K], cnt[:, 0, :N_EXPERTS], p)
    out = _combine(yb, dest, gt, x1, mods, seq, p['g_post_ffn'])
    return out.reshape(batch, seq, d)


def kernel(x, c, ctx, c_ctx, w_ada, b_ada, g_pre_mix, g_post_mix, g_pre_ffn, g_post_ffn, w_in, b_in, mu_prev, mu_next, att_sinks, w0_f, w0_b, w2_f, w2_b, a0_f, a0_b, a2_f, a2_b, g2, k_k, k_a, r_k, ln_x_w, ln_x_b, w_up_att, w_up_rwkv, w_out, w_router, b_router, w_gate_up, b_gate_up, w_down, b_down):
    assert w_ada.shape[0] == 1, "single-layer problem: the context stream update is never consumed"
    p = dict(w_ada=w_ada[0], b_ada=b_ada[0], g_pre_mix=g_pre_mix[0], g_post_mix=g_post_mix[0],
             g_pre_ffn=g_pre_ffn[0], g_post_ffn=g_post_ffn[0], w_in=w_in[0], b_in=b_in[0],
             mu_prev=mu_prev[0], mu_next=mu_next[0], att_sinks=att_sinks[0], w0_f=w0_f[0], w0_b=w0_b[0],
             w2_f=w2_f[0], w2_b=w2_b[0], a0_f=a0_f[0], a0_b=a0_b[0], a2_f=a2_f[0], a2_b=a2_b[0], g2=g2[0],
             k_k=k_k[0], k_a=k_a[0], r_k=r_k[0].reshape(-1), ln_x_w=ln_x_w[0], ln_x_b=ln_x_b[0],
             w_up_att=w_up_att[0], w_up_rwkv=w_up_rwkv[0], w_out=w_out[0], w_router=w_router[0],
             b_router=b_router[0], w_gate_up=w_gate_up[0], b_gate_up=b_gate_up[0], w_down=w_down[0],
             b_down=b_down[0])
    return _layer(x, c, ctx, c_ctx, p)
```

```python
import functools

import jax
import jax.numpy as jnp
import numpy as np
from jax import lax
from jax.experimental import pallas as pl
from jax.experimental.pallas import tpu as pltpu

F32 = jnp.float32
BF16 = jnp.bfloat16

GRID_W = 64
HEAD_DIM = 64
ATT_HEADS = 8
ATT_KV_HEADS = 2
ATT_GROUPS = ATT_HEADS // ATT_KV_HEADS
ATT_DIM = ATT_HEADS * HEAD_DIM
KV_DIM = ATT_KV_HEADS * HEAD_DIM
WINDOW = 128
BLOCK = 128
ROPE_BASE = 10000.0
ATT_SCALE = HEAD_DIM ** -0.5
NEG_INF = -1e30
RWKV_HEADS = 8
RWKV_DIM = RWKV_HEADS * HEAD_DIM
DECAY_LORA = 64
ICL_LORA = 64
GATE_LORA = 128
GN_EPS = 64e-5
N_EXPERTS = 32
TOP_K = 4
SWIGLU_LIMIT = 7.0
SWIGLU_ALPHA = 1.702
RMS_EPS = 1e-6
ATT_COLS = ATT_DIM + 2 * KV_DIM
RWKV_COLS = 3 * RWKV_DIM + 2 * DECAY_LORA + 2 * ICL_LORA + GATE_LORA
ROPE_COLS = ATT_DIM + KV_DIM

LANES = 128
SUBLANES = 8
VMEM_LIMIT = 48 * 1024 * 1024

ROW_TILE = 256
SCAN_CHUNK = 64
SCAN_GROUP = 4
SCAN_BATCH = 4
MOE_ROWS = 512


def _cparams(*sem):
    return pltpu.CompilerParams(dimension_semantics=sem, vmem_limit_bytes=VMEM_LIMIT)


def _mm(a, b):
    return jnp.dot(a.astype(BF16), b.astype(BF16), preferred_element_type=F32)


def _mm_nt(a, b):
    return lax.dot_general(a.astype(BF16), b.astype(BF16), (((1,), (1,)), ((), ())),
                           preferred_element_type=F32)


def _split2(x):
    hi = x.astype(BF16)
    lo = (x - hi.astype(F32)).astype(BF16)
    return hi, lo


def _split3(x):
    hi = x.astype(BF16)
    r1 = x - hi.astype(F32)
    mid = r1.astype(BF16)
    lo = (r1 - mid.astype(F32)).astype(BF16)
    return hi, mid, lo


def _mm_f32(a, b):
    ah, al = _split2(a)
    bh, bl = _split2(b)
    d = functools.partial(jnp.dot, preferred_element_type=F32)
    return d(ah, bh) + d(ah, bl) + d(al, bh)


def _seg_sum(x, eseg):
    hi, lo = _split2(x)
    d = functools.partial(jnp.dot, preferred_element_type=F32)
    return d(hi, eseg) + d(lo, eseg)


def _rms(x):
    return x * lax.rsqrt(jnp.mean(x * x, axis=-1, keepdims=True) + RMS_EPS)


def _sigmoid(x):
    return 1.0 / (1.0 + jnp.exp(-x))


def _ada_kernel(c_ref, w_ref, b_ref, o_ref):
    c = c_ref[...]
    o_ref[...] = _mm_f32(c * _sigmoid(c), w_ref[...]) + b_ref[...]


def _ada(c_rows, w, b):
    m, d = c_rows.shape
    n = w.shape[1]
    tn = 1536
    return pl.pallas_call(
        _ada_kernel,
        grid=(n // tn,),
        in_specs=[pl.BlockSpec((m, d), lambda j: (0, 0)),
                  pl.BlockSpec((d, tn), lambda j: (0, j)),
                  pl.BlockSpec((1, tn), lambda j: (0, j))],
        out_specs=pl.BlockSpec((m, tn), lambda j: (0, j)),
        out_shape=jax.ShapeDtypeStruct((m, n), F32),
        compiler_params=_cparams("arbitrary"),
        name="ada",
    )(c_rows, w, b.reshape(1, n))


def _proj_kernel(*refs, rope):
    if rope:
        (x_ref, mod_ref, g_ref, wa_ref, wr_ref, wg_ref, ba_ref, br_ref, bg_ref, cos_ref, sin_ref,
         za_ref, zr_ref, zg_ref) = refs
    else:
        (x_ref, mod_ref, g_ref, wa_ref, wr_ref, wg_ref, ba_ref, br_ref, bg_ref,
         za_ref, zr_ref, zg_ref) = refs
    mod = mod_ref[0]
    h = _rms(x_ref[...]) * g_ref[...] * (1.0 + mod[1:2]) + mod[0:1]
    hb = h.astype(BF16)
    za = jnp.dot(hb, wa_ref[...], preferred_element_type=F32) + ba_ref[...]
    if rope:
        qk = za[:, :ROPE_COLS]
        lane = lax.broadcasted_iota(jnp.int32, qk.shape, 1)
        low = (lane % 32) < 16
        partner = jnp.where(low, pltpu.roll(qk, ROPE_COLS - 16, 1), pltpu.roll(qk, 16, 1))
        za_ref[:, :ROPE_COLS] = (qk * cos_ref[...] + partner * sin_ref[...]).astype(za_ref.dtype)
        za_ref[:, ROPE_COLS:] = za[:, ROPE_COLS:].astype(za_ref.dtype)
    else:
        za_ref[...] = za.astype(za_ref.dtype)
    zr_ref[...] = jnp.dot(hb, wr_ref[...], preferred_element_type=F32) + br_ref[...]
    zg_ref[...] = (jnp.dot(hb, wg_ref[...], preferred_element_type=F32) + bg_ref[...]).astype(zg_ref.dtype)


def _rope_tables(seq):
    n_rows = seq // GRID_W
    row = jnp.repeat(jnp.arange(n_rows, dtype=F32), GRID_W, total_repeat_length=seq)
    col = jnp.tile(jnp.arange(GRID_W, dtype=F32), n_rows)
    half = HEAD_DIM // 2
    inv_freq = ROPE_BASE ** (-jnp.arange(0, half, 2, dtype=F32) / half)
    ang_r = row[:, None] * inv_freq[None, :]
    ang_c = col[:, None] * inv_freq[None, :]
    cos_h = jnp.concatenate([jnp.cos(ang_r), jnp.cos(ang_r), jnp.cos(ang_c), jnp.cos(ang_c)], axis=1)
    sin_h = jnp.concatenate([-jnp.sin(ang_r), jnp.sin(ang_r), -jnp.sin(ang_c), jnp.sin(ang_c)], axis=1)
    reps = ROPE_COLS // HEAD_DIM
    return jnp.tile(cos_h, (1, reps)), jnp.tile(sin_h, (1, reps))


def _project(x2, mods, mod_row, seq, g_pre, w_att, w_rwkv, w_gate, b_att, b_rwkv, b_gate, rope):
    rows, d = x2.shape
    tm = ROW_TILE
    tps = seq // tm
    const = lambda i: (0, 0)
    in_specs = [pl.BlockSpec((tm, d), lambda i: (i, 0)),
                pl.BlockSpec((1, SUBLANES, d), lambda i: (mod_row(i // tps), 0, 0)),
                pl.BlockSpec((1, d), const),
                pl.BlockSpec(w_att.shape, const), pl.BlockSpec(w_rwkv.shape, const),
                pl.BlockSpec(w_gate.shape, const),
                pl.BlockSpec((1, ATT_COLS), const), pl.BlockSpec((1, RWKV_COLS), const),
                pl.BlockSpec((1, w_gate.shape[1]), const)]
    args = [x2, mods, g_pre, w_att, w_rwkv, w_gate, b_att, b_rwkv, b_gate]
    if rope:
        cos, sin = _rope_tables(seq)
        in_specs += [pl.BlockSpec((tm, ROPE_COLS), lambda i: (i % tps, 0))] * 2
        args += [cos, sin]
    return pl.pallas_call(
        functools.partial(_proj_kernel, rope=rope),
        grid=(rows // tm,),
        in_specs=in_specs,
        out_specs=[pl.BlockSpec((tm, ATT_COLS), lambda i: (i, 0)),
                   pl.BlockSpec((tm, RWKV_COLS), lambda i: (i, 0)),
                   pl.BlockSpec((tm, w_gate.shape[1]), lambda i: (i, 0))],
        out_shape=[jax.ShapeDtypeStruct((rows, ATT_COLS), BF16),
                   jax.ShapeDtypeStruct((rows, RWKV_COLS), F32),
                   jax.ShapeDtypeStruct((rows, w_gate.shape[1]), BF16)],
        compiler_params=_cparams("parallel"),
        name="proj",
    )(*args)


def _feat_kernel(z_ref, zp_ref, zn_ref, mup_ref, mun_ref, w2_ref, w0_ref, a2_ref, a0_ref, g2_ref,
                 kk_ref, ka_ref, eseg_ref,
                 r_o, v_o, kf_o, kb_o, lwf_o, lwb_o, kkn_o, bf_o, bb_o, gate_o, *, tiles_per_seq):
    ti = pl.program_id(0) % tiles_per_seq
    z = z_ref[...]
    tm = z.shape[0]
    row = lax.broadcasted_iota(jnp.int32, (tm, 1), 0)
    prev_halo = jnp.where(ti == 0, 0.0, zp_ref[SUBLANES - 1:SUBLANES, :])
    next_halo = jnp.where(ti == tiles_per_seq - 1, 0.0, zn_ref[0:1, :])
    prev = jnp.where(row == 0, prev_halo, pltpu.roll(z, 1, 0))
    nxt = jnp.where(row == tm - 1, next_halo, pltpu.roll(z, tm - 1, 0))
    zs = z + mup_ref[...] * (prev - z) + mun_ref[...] * (nxt - z)

    d = RWKV_DIM
    r = zs[:, 0:d]
    k = zs[:, d:2 * d]
    v = zs[:, 2 * d:3 * d]
    o = 3 * d
    wl = zs[:, o:o + 2 * DECAY_LORA]
    al = zs[:, o + 2 * DECAY_LORA:o + 2 * DECAY_LORA + 2 * ICL_LORA]
    gl = zs[:, o + 2 * DECAY_LORA + 2 * ICL_LORA:]

    w = w0_ref[...] + _mm(jnp.tanh(wl), w2_ref[...])
    nw = -w
    softplus = jnp.maximum(nw, 0.0) + jnp.log(1.0 + jnp.exp(-jnp.abs(nw)))
    lw = -jnp.exp(-softplus - 0.5)
    icl = _sigmoid(a0_ref[...] + _mm(al, a2_ref[...]))
    kk0 = k * kk_ref[...]
    ss = _seg_sum(kk0 * kk0, eseg_ref[...])
    kk = kk0 / jnp.maximum(jnp.sqrt(ss), 1e-12)
    ka = ka_ref[...]
    icl_f = icl[:, :d]
    icl_b = icl[:, d:]

    r_o[...] = r.astype(r_o.dtype)
    v_o[...] = v.astype(v_o.dtype)
    kf_o[...] = (k * (1.0 + (icl_f - 1.0) * ka)).astype(kf_o.dtype)
    kb_o[...] = (k * (1.0 + (icl_b - 1.0) * ka)).astype(kb_o.dtype)
    lwf_o[...] = lw[:, :d]
    lwb_o[...] = lw[:, d:]
    kkn_o[...] = kk.astype(kkn_o.dtype)
    bf_o[...] = (kk * icl_f).astype(bf_o.dtype)
    bb_o[...] = (kk * icl_b).astype(bb_o.dtype)
    gate_o[...] = _mm(_sigmoid(gl), g2_ref[...]).astype(gate_o.dtype)


FEAT_NAMES = ('r', 'v', 'kf', 'kb', 'lwf', 'lwb', 'kk', 'bf', 'bb', 'gate')
FEAT_BF16 = ('r', 'v', 'kf', 'kb', 'kk', 'bf', 'bb', 'gate')


def _block_diag2(a, b):
    za = jnp.zeros_like(a)
    zb = jnp.zeros_like(b)
    return jnp.concatenate([jnp.concatenate([a, zb], axis=1), jnp.concatenate([za, b], axis=1)], axis=0)


def _seg_matrix():
    h = np.arange(RWKV_DIM) // HEAD_DIM
    return jnp.asarray((h[:, None] == h[None, :]).astype(np.float32), BF16)


def _features(z_rwkv, seq, p):
    rows = z_rwkv.shape[0]
    tm = ROW_TILE
    tps = seq // tm
    hb = tm // SUBLANES
    nhb = rows // SUBLANES
    d = RWKV_DIM
    const = lambda i: (0, 0)
    w2 = _block_diag2(p['w2_f'], p['w2_b']).astype(BF16)
    a2 = _block_diag2(p['a2_f'], p['a2_b']).astype(BF16)
    w0 = jnp.concatenate([p['w0_f'], p['w0_b']]).reshape(1, 2 * d)
    a0 = jnp.concatenate([p['a0_f'], p['a0_b']]).reshape(1, 2 * d)
    out_spec = pl.BlockSpec((tm, d), lambda i: (i, 0))
    return pl.pallas_call(
        functools.partial(_feat_kernel, tiles_per_seq=tps),
        grid=(rows // tm,),
        in_specs=[pl.BlockSpec((tm, RWKV_COLS), lambda i: (i, 0)),
                  pl.BlockSpec((SUBLANES, RWKV_COLS), lambda i: (jnp.maximum(i * hb - 1, 0), 0)),
                  pl.BlockSpec((SUBLANES, RWKV_COLS), lambda i: (jnp.minimum((i + 1) * hb, nhb - 1), 0)),
                  pl.BlockSpec((1, RWKV_COLS), const), pl.BlockSpec((1, RWKV_COLS), const),
                  pl.BlockSpec(w2.shape, const), pl.BlockSpec((1, 2 * d), const),
                  pl.BlockSpec(a2.shape, const), pl.BlockSpec((1, 2 * d), const),
                  pl.BlockSpec((GATE_LORA, d), const),
                  pl.BlockSpec((1, d), const), pl.BlockSpec((1, d), const),
                  pl.BlockSpec((d, d), const)],
        out_specs=[out_spec] * 10,
        out_shape=[jax.ShapeDtypeStruct((rows, d), BF16 if name in FEAT_BF16 else F32) for name in FEAT_NAMES],
        compiler_params=_cparams("parallel"),
        name="feat",
    )(z_rwkv, z_rwkv, z_rwkv, p['mu_prev'].reshape(1, -1), p['mu_next'].reshape(1, -1),
      w2, w0, a2, a0, p['g2'].astype(BF16), p['k_k'].reshape(1, d), p['k_a'].reshape(1, d), _seg_matrix())


def _scan_kernel(rf, vf, kf, lwf, kkf, bf, rb, vb, kb, lwb, kkb, bb, s0f_ref, s0b_ref,
                 yf_ref, yb_ref, sTf_ref, sTb_ref, state_ref, *, chunk, n_chunks):
    c = pl.program_id(1)
    C = chunk
    hd = HEAD_DIM
    gw = SCAN_GROUP * hd
    n_groups = RWKV_HEADS // SCAN_GROUP
    assert C == hd, "the triangular masks below are shared between time and channel blocks"

    n_rows = state_ref.shape[0]

    @pl.when(c == 0)
    def _():
        for n in range(n_rows):
            for d, s0_ref in enumerate((s0f_ref, s0b_ref)):
                for g in range(n_groups):
                    state_ref[n, d, g] = jnp.zeros((gw, gw), F32)
                    for j in range(SCAN_GROUP):
                        state_ref[n, d, g, j * hd:(j + 1) * hd, j * hd:(j + 1) * hd] = (
                            s0_ref[n, g * SCAN_GROUP + j])

    ti = lax.broadcasted_iota(jnp.int32, (C, C), 0)
    si = lax.broadcasted_iota(jnp.int32, (C, C), 1)
    tg = lax.broadcasted_iota(jnp.int32, (C, gw), 0)
    sg = lax.broadcasted_iota(jnp.int32, (C, gw), 1) % C
    eye = (tg == sg).astype(F32)
    same_head = (lax.broadcasted_iota(jnp.int32, (gw, gw), 0) // hd
                 == lax.broadcasted_iota(jnp.int32, (gw, gw), 1) // hd)
    same_head_b = same_head.astype(BF16)
    n_double = int(np.log2(C)) - 1

    def bdiag(x_cat):
        return jnp.concatenate([x_cat.astype(BF16)] * SCAN_GROUP, axis=0) * same_head_b

    dirs = ((rf, vf, kf, lwf, kkf, bf), (rb, vb, kb, lwb, kkb, bb))
    units = []
    masks = [(((si <= ti) if d == 0 else (si >= ti)).astype(BF16),
              (sg <= tg) if d == 0 else (sg >= tg),
              (sg < tg) if d == 0 else (sg > tg)) for d in range(2)]
    for n, d in [(n, d) for n in range(n_rows) for d in range(2)]:
        r_ref, v_ref, k_ref, lw_ref, kk_ref, b_ref = (ref.at[n] for ref in dirs[d])
        tri, incl_g, strict_g = masks[d]
        lw = lw_ref[...]
        lh, lm, ll = _split3(lw)
        dd = functools.partial(jnp.dot, preferred_element_type=F32)
        cum = dd(tri, lh) + dd(tri, lm) + dd(tri, ll)
        cumx = cum - lw
        cum_end = cum[C - 1:C] if d == 0 else cum[0:1]
        e_neg = jnp.exp(-cum)
        e_end = jnp.exp(cum_end - cum)
        bv = b_ref[...].astype(F32)
        kv = k_ref[...].astype(F32)
        vv = v_ref[...].astype(F32)
        a_t = (-kk_ref[...].astype(F32) * jnp.exp(cumx)).astype(BF16)
        r_t = (r_ref[...].astype(F32) * jnp.exp(cum)).astype(BF16)
        b_t = (bv * e_neg).astype(BF16)
        k_t = (kv * e_neg).astype(BF16)
        b_q = (bv * e_end).astype(BF16)
        k_q = (kv * e_end).astype(BF16)
        g_end = jnp.exp(cum_end)
        for g in range(n_groups):
            sl = slice(g * gw, (g + 1) * gw)
            units.append(dict(
                n=n, d=d, g=g, sl=sl, incl=incl_g, strict=strict_g,
                P=jnp.concatenate([a_t[:, sl], r_t[:, sl]], axis=0),
                Q=jnp.concatenate([bdiag(b_t[:, sl]), bdiag(k_t[:, sl])], axis=0),
                Qq=jnp.concatenate([b_q[:, sl], k_q[:, sl]], axis=0),
                V=vv[:, sl], g_end=g_end[:, sl]))

    for u in units:
        u['G'] = _mm_nt(u['P'], u['Q'])
    for u in units:
        u['S0'] = state_ref[u['n'], u['d'], u['g']]
        u['PH'] = _mm_nt(u['P'], u['S0'])
        u['Vd'] = bdiag(u['V'])
    for u in units:
        G = u.pop('G')
        u['N'] = jnp.where(u['strict'], G[:C, :gw], 0.0)
        u['a_ak'] = jnp.where(u['strict'], G[:C, gw:], 0.0)
        u['a_rb'] = jnp.where(u['incl'], G[C:, :gw], 0.0)
        u['a_rk'] = jnp.where(u['incl'], G[C:, gw:], 0.0)
    for u in units:
        u['T'] = eye + u['N']
        u['Pw'] = _mm(u['N'], bdiag(u['N']))
        u['rhs'] = u['PH'][:C] + _mm(u['a_ak'], u['Vd'])
    for lvl in range(n_double):
        for u in units:
            pw = bdiag(u['Pw'])
            u['T'] = u['T'] + _mm(u['T'], pw)
            if lvl + 1 < n_double:
                u['Pw'] = _mm(u['Pw'], pw)
    for u in units:
        u['U'] = _mm(u['T'], bdiag(u['rhs']))
    y_refs = (yf_ref, yb_ref)
    for u in units:
        y_refs[u['d']][u['n'], :, u['sl']] = (u['PH'][C:] + _mm(u['a_rb'], bdiag(u['U']))
                                              + _mm(u['a_rk'], u['Vd']))
        uv = jnp.concatenate([u['U'], u['V']], axis=0)
        state_ref[u['n'], u['d'], u['g']] = jnp.where(
            same_head, u['S0'] * u['g_end'] + _mm(uv.T, u['Qq']), 0.0)

    @pl.when(c == n_chunks - 1)
    def _():
        for n in range(n_rows):
            for d, sT_ref in enumerate((sTf_ref, sTb_ref)):
                for g in range(n_groups):
                    for j in range(SCAN_GROUP):
                        sT_ref[n, g * SCAN_GROUP + j] = (
                            state_ref[n, d, g, j * hd:(j + 1) * hd, j * hd:(j + 1) * hd])


def _scan(f, batch, seq, s0_f, s0_b):
    C = SCAN_CHUNK
    nC = seq // C
    d = RWKV_DIM
    nb = max(n for n in range(1, SCAN_BATCH + 1) if batch % n == 0)
    fwd = pl.BlockSpec((nb, C, d), lambda b, c: (b, c, 0))
    bwd = pl.BlockSpec((nb, C, d), lambda b, c: (b, nC - 1 - c, 0))
    st = pl.BlockSpec((nb, RWKV_HEADS, HEAD_DIM, HEAD_DIM), lambda b, c: (b, 0, 0, 0))
    st_shape = jax.ShapeDtypeStruct((batch, RWKV_HEADS, HEAD_DIM, HEAD_DIM), F32)
    y_shape = jax.ShapeDtypeStruct((batch, seq, d), F32)
    gw = SCAN_GROUP * HEAD_DIM
    arr = lambda name: f[name].reshape(batch, seq, d)
    y_f, y_b, sT_f, sT_b = pl.pallas_call(
        functools.partial(_scan_kernel, chunk=C, n_chunks=nC),
        grid=(batch // nb, nC),
        in_specs=[fwd] * 6 + [bwd] * 6 + [st, st],
        out_specs=[fwd, bwd, st, st],
        out_shape=[y_shape, y_shape, st_shape, st_shape],
        scratch_shapes=[pltpu.VMEM((nb, 2, RWKV_HEADS // SCAN_GROUP, gw, gw), F32)],
        compiler_params=_cparams("parallel", "arbitrary"),
        name="scan",
    )(arr('r'), arr('v'), arr('kf'), arr('lwf'), arr('kk'), arr('bf'),
      arr('r'), arr('v'), arr('kb'), arr('lwb'), arr('kk'), arr('bb'), s0_f, s0_b)
    return y_f.reshape(batch * seq, d), y_b.reshape(batch * seq, d), sT_f, sT_b


def _attn_kernel(sink_ref, q_ref, kp_ref, km_ref, kn_ref, vp_ref, vm_ref, vn_ref, kc_ref, vc_ref, o_ref,
                 *, n_blocks):
    n = pl.program_id(1)
    rows = ATT_GROUPS * BLOCK
    q = q_ref[...] * ATT_SCALE
    qi = lax.broadcasted_iota(jnp.int32, (rows, BLOCK), 0) % BLOCK
    kj = lax.broadcasted_iota(jnp.int32, (rows, BLOCK), 1)
    ok_prev = (kj >= qi) & (n > 0)
    ok_next = (kj <= qi) & (n < n_blocks - 1)
    rowh = lax.broadcasted_iota(jnp.int32, (rows, 1), 0) // BLOCK
    outs = [None] * ATT_HEADS
    for g in range(ATT_KV_HEADS):
        ks = slice(g * HEAD_DIM, (g + 1) * HEAD_DIM)
        heads = [g * ATT_GROUPS + j for j in range(ATT_GROUPS)]
        Qs = jnp.concatenate([q[:, h * HEAD_DIM:(h + 1) * HEAD_DIM] for h in heads], axis=0)
        scores = [jnp.where(ok_prev, _mm_nt(Qs, kp_ref[:, ks]), NEG_INF),
                  _mm_nt(Qs, km_ref[:, ks]),
                  jnp.where(ok_next, _mm_nt(Qs, kn_ref[:, ks]), NEG_INF),
                  _mm_nt(Qs, kc_ref[:, ks])]
        values = [vp_ref, vm_ref, vn_ref, vc_ref]
        sink = jnp.zeros((rows, 1), F32)
        for j, h in enumerate(heads):
            sink = jnp.where(rowh == j, sink_ref[h], sink)
        folded = None
        for sc in scores:
            for c0 in range(0, sc.shape[1], BLOCK):
                blk = sc[:, c0:c0 + BLOCK]
                folded = blk if folded is None else jnp.maximum(folded, blk)
        m = jnp.maximum(sink, jnp.max(folded, axis=-1, keepdims=True))
        acc = jnp.zeros((rows, 2 * HEAD_DIM), F32)
        for sc, v_ref in zip(scores, values):
            one_col = (lax.broadcasted_iota(jnp.int32, (v_ref.shape[0], HEAD_DIM), 1) == 0).astype(BF16)
            v_ext = jnp.concatenate([v_ref[:, ks], one_col], axis=1)
            acc = acc + jnp.dot(jnp.exp((sc - m).astype(BF16)), v_ext, preferred_element_type=F32)
        den = acc[:, HEAD_DIM:HEAD_DIM + 1] + jnp.exp(sink - m)
        O = acc[:, :HEAD_DIM] / den
        for j, h in enumerate(heads):
            outs[h] = O[j * BLOCK:(j + 1) * BLOCK]
    o_ref[...] = jnp.concatenate(outs, axis=1).astype(o_ref.dtype)


def _attention(z_att, zc_att, sinks, batch, seq):
    nb = seq // BLOCK
    ctx_len = zc_att.shape[0] // batch
    kcol = ATT_DIM // KV_DIM
    vcol = kcol + 1

    def kv_spec(col, off):
        return pl.BlockSpec((BLOCK, KV_DIM), lambda b, n: (b * nb + jnp.clip(n + off, 0, nb - 1), col))

    return pl.pallas_call(
        functools.partial(_attn_kernel, n_blocks=nb),
        grid=(batch, nb),
        in_specs=[pl.BlockSpec(memory_space=pltpu.SMEM),
                  pl.BlockSpec((BLOCK, ATT_DIM), lambda b, n: (b * nb + n, 0)),
                  kv_spec(kcol, -1), kv_spec(kcol, 0), kv_spec(kcol, 1),
                  kv_spec(vcol, -1), kv_spec(vcol, 0), kv_spec(vcol, 1),
                  pl.BlockSpec((ctx_len, KV_DIM), lambda b, n: (b, kcol)),
                  pl.BlockSpec((ctx_len, KV_DIM), lambda b, n: (b, vcol))],
        out_specs=pl.BlockSpec((BLOCK, ATT_DIM), lambda b, n: (b * nb + n, 0)),
        out_shape=jax.ShapeDtypeStruct((batch * seq, ATT_DIM), BF16),
        compiler_params=_cparams("parallel", "parallel"),
        name="attn",
    )(sinks, z_att, z_att, z_att, z_att, z_att, z_att, z_att, zc_att, zc_att)


def _merge_kernel(yf_ref, yb_ref, r_ref, kf_ref, kb_ref, v_ref, gate_ref, att_ref, zg_ref, x_ref, mod_ref,
                  lnw_ref, lnb_ref, rk_ref, eseg_ref, wua_ref, wur_ref, wo_ref, gpm_ref, gpf_ref,
                  wr_ref, br_ref,
                  x1_ref, hf_ref, idx_ref, gt_ref, rank_ref, cnt_ref, slots_ref):
    eseg = eseg_ref[...]
    inv_n = 1.0 / HEAD_DIM
    y = yf_ref[...] + yb_ref[...]
    mean = _seg_sum(y, eseg) * inv_n
    dy = y - mean
    var = _seg_sum(dy * dy, eseg) * inv_n
    yn = dy * lax.rsqrt(var + GN_EPS) * lnw_ref[...] + lnb_ref[...]
    f32 = lambda ref: ref[...].astype(F32)
    bonus = _seg_sum(f32(r_ref) * (f32(kf_ref) + f32(kb_ref)) * rk_ref[...], eseg) * f32(v_ref)
    rwk = (yn + bonus) * f32(gate_ref)

    d = x_ref.shape[1]
    zg = zg_ref[...].astype(F32)
    merged = (_sigmoid(zg[:, :d]) * jnp.dot(att_ref[...], wua_ref[...], preferred_element_type=F32)
              + _sigmoid(zg[:, d:]) * _mm(rwk, wur_ref[...]))
    mix = _mm(merged, wo_ref[...])
    mod = mod_ref[0]
    x1 = x_ref[...] + mod[2:3] * (_rms(mix) * gpm_ref[...])
    x1_ref[...] = x1
    hf = _rms(x1) * gpf_ref[...] * (1.0 + mod[4:5]) + mod[3:4]
    hf_ref[...] = hf.astype(hf_ref.dtype)

    logits = _mm_f32(hf, wr_ref[...]) + br_ref[...]
    lane = lax.broadcasted_iota(jnp.int32, logits.shape, 1).astype(F32)
    idx_out = jnp.zeros(logits.shape, F32)
    val_out = jnp.zeros(logits.shape, F32)
    picked = jnp.zeros(logits.shape, F32)
    sels = []
    top = None
    den = None
    for kth in range(TOP_K):
        m = jnp.max(logits, axis=-1, keepdims=True)
        sel = jnp.min(jnp.where(logits == m, lane, float(LANES)), axis=-1, keepdims=True)
        hit = lane == sel
        logits = jnp.where(hit, -jnp.inf, logits)
        picked = jnp.where(hit, 1.0, picked)
        sels.append(hit)
        if kth == 0:
            top = m
        e = jnp.exp(m - top)
        den = e if kth == 0 else den + e
        idx_out = jnp.where(lane == float(kth), sel, idx_out)
        val_out = jnp.where(lane == float(kth), e, val_out)
    idx_ref[...] = idx_out.astype(jnp.int32)
    gt_ref[...] = val_out / den

    tm = picked.shape[0]
    earlier = (lax.broadcasted_iota(jnp.int32, (tm, tm), 1)
               < lax.broadcasted_iota(jnp.int32, (tm, tm), 0)).astype(BF16)
    before = jnp.dot(earlier, picked.astype(BF16), preferred_element_type=F32)
    rank_out = jnp.zeros(logits.shape, F32)
    for kth, hit in enumerate(sels):
        rk = jnp.sum(jnp.where(hit, before, 0.0), axis=-1, keepdims=True)
        rank_out = jnp.where(lane == float(kth), rk, rank_out)
    rank_ref[...] = rank_out.astype(jnp.int32)
    cnt_ref[0] = jnp.broadcast_to(jnp.sum(picked, axis=0, keepdims=True), cnt_ref.shape[1:]).astype(jnp.int32)
    slots_ref[...] = jnp.zeros(slots_ref.shape, slots_ref.dtype)


def _moe_blocks(n_tok, n_tiles):
    return -(-(n_tok * TOP_K + (SUBLANES - 1) * N_EXPERTS * n_tiles) // MOE_ROWS) + N_EXPERTS


def _merge(y_f, y_b, f, att, z_gate, x2, mods, seq, p):
    rows, d = x2.shape
    tm = ROW_TILE
    tps = seq // tm
    rd = RWKV_DIM
    const = lambda i: (0, 0)
    row = lambda w: pl.BlockSpec((tm, w), lambda i: (i, 0))
    vec = lambda w: pl.BlockSpec((1, w), const)
    w_router = jnp.pad(p['w_router'], ((0, 0), (0, LANES - N_EXPERTS)))
    b_router = jnp.pad(p['b_router'], (0, LANES - N_EXPERTS), constant_values=NEG_INF).reshape(1, LANES)
    n_slots = _moe_blocks(rows, rows // tm) * MOE_ROWS
    slab = n_slots // (rows // tm)
    assert slab * (rows // tm) == n_slots and slab % SUBLANES == 0
    return pl.pallas_call(
        _merge_kernel,
        grid=(rows // tm,),
        in_specs=[row(rd)] * 7 + [row(ATT_DIM), row(2 * d), row(d),
                  pl.BlockSpec((1, SUBLANES, d), lambda i: (i // tps, 0, 0)),
                  vec(rd), vec(rd), vec(rd), pl.BlockSpec((rd, rd), const),
                  pl.BlockSpec((ATT_DIM, d), const), pl.BlockSpec((rd, d), const), pl.BlockSpec((d, d), const),
                  vec(d), vec(d), pl.BlockSpec((d, LANES), const), vec(LANES)],
        out_specs=[row(d), row(d), row(LANES), row(LANES), row(LANES),
                   pl.BlockSpec((1, SUBLANES, LANES), lambda i: (i, 0, 0)),
                   pl.BlockSpec((slab, d), lambda i: (i, 0))],
        out_shape=[jax.ShapeDtypeStruct((rows, d), F32), jax.ShapeDtypeStruct((rows, d), BF16),
                   jax.ShapeDtypeStruct((rows, LANES), jnp.int32), jax.ShapeDtypeStruct((rows, LANES), F32),
                   jax.ShapeDtypeStruct((rows, LANES), jnp.int32),
                   jax.ShapeDtypeStruct((rows // tm, SUBLANES, LANES), jnp.int32),
                   jax.ShapeDtypeStruct((n_slots, d), F32)],
        compiler_params=_cparams("parallel"),
        name="merge",
    )(y_f, y_b, f['r'], f['kf'], f['kb'], f['v'], f['gate'], att, z_gate, x2, mods,
      p['ln_x_w'].reshape(1, rd), p['ln_x_b'].reshape(1, rd), p['r_k'].reshape(1, rd), _seg_matrix(),
      p['w_up_att'].astype(BF16), p['w_up_rwkv'].astype(BF16), p['w_out'].astype(BF16),
      p['g_post_mix'].reshape(1, d), p['g_pre_ffn'].reshape(1, d), w_router, b_router)


def _moe_kernel(be_ref, nused_ref, x_ref, wgu_ref, bgu_ref, wdn_f32_ref, bdn_ref, o_ref, wdn_ref):
    i = pl.program_id(0)

    @pl.when((i == 0) | (be_ref[i] != be_ref[jnp.maximum(i - 1, 0)]))
    def _():
        wdn_ref[...] = wdn_f32_ref[...].astype(BF16)

    @pl.when(i < nused_ref[0])
    def _():
        de = wdn_ref.shape[0]
        d = x_ref.shape[1]
        x = x_ref[...]
        bgu = bgu_ref[...]
        gate = jnp.minimum(_mm_nt(x, wgu_ref[:, :d]) + bgu[:, :de], SWIGLU_LIMIT)
        up = jnp.clip(_mm_nt(x, wgu_ref[:, d:]) + bgu[:, de:], -SWIGLU_LIMIT, SWIGLU_LIMIT)
        act = (up + 1.0) * (gate * _sigmoid(SWIGLU_ALPHA * gate))
        o_ref[...] = _mm(act, wdn_ref[...]) + bdn_ref[...]

    @pl.when(i >= nused_ref[0])
    def _():
        o_ref[...] = jnp.zeros(o_ref.shape, o_ref.dtype)


def _wprep_kernel(w_ref, o_ref, t_ref):
    d = w_ref.shape[1]
    half = o_ref.shape[1]
    wt = w_ref[0].T
    for c in range(d // LANES):
        cols = slice(c * LANES, (c + 1) * LANES)
        t_ref[c] = wt[:, cols]
        o_ref[0, :, cols] = t_ref[c, pl.ds(0, half, stride=2), :].astype(o_ref.dtype)
        o_ref[0, :, d + c * LANES:d + (c + 1) * LANES] = t_ref[c, pl.ds(1, half, stride=2), :].astype(o_ref.dtype)


def _gate_up_rows(w_gu):
    n_exp, d, de2 = w_gu.shape
    tc = 2 * LANES
    return pl.pallas_call(
        _wprep_kernel,
        grid=(n_exp, de2 // tc),
        in_specs=[pl.BlockSpec((1, d, tc), lambda e, j: (e, 0, j))],
        out_specs=pl.BlockSpec((1, tc // 2, 2 * d), lambda e, j: (e, j, 0)),
        out_shape=jax.ShapeDtypeStruct((n_exp, de2 // 2, 2 * d), BF16),
        scratch_shapes=[pltpu.VMEM((d // LANES, tc, LANES), F32)],
        compiler_params=_cparams("parallel", "parallel"),
        name="wprep",
    )(w_gu)


def _experts(xb, block_exp, n_used, w_gu, b_gu, w_dn, b_dn):
    n_slots, d = xb.shape
    bm = MOE_ROWS
    n_blocks = n_slots // bm
    de = w_dn.shape[1]
    de2 = 2 * de
    grid_spec = pltpu.PrefetchScalarGridSpec(
        num_scalar_prefetch=2,
        grid=(n_blocks,),
        in_specs=[pl.BlockSpec((bm, d), lambda i, be, nu: (i, 0)),
                  pl.BlockSpec((None, de, 2 * d), lambda i, be, nu: (be[i], 0, 0)),
                  pl.BlockSpec((None, 1, de2), lambda i, be, nu: (be[i], 0, 0)),
                  pl.BlockSpec((None, de, d), lambda i, be, nu: (be[i], 0, 0)),
                  pl.BlockSpec((None, 1, d), lambda i, be, nu: (be[i], 0, 0))],
        out_specs=pl.BlockSpec((bm, d), lambda i, be, nu: (i, 0)),
        scratch_shapes=[pltpu.VMEM((de, d), BF16)],
    )
    return pl.pallas_call(
        _moe_kernel,
        grid_spec=grid_spec,
        out_shape=jax.ShapeDtypeStruct((n_slots, d), F32),
        compiler_params=_cparams("arbitrary"),
        name="moe",
    )(block_exp, n_used, xb, w_gu, b_gu, w_dn, b_dn)


def _local_rows(tm):
    rows = tm * TOP_K + (SUBLANES - 1) * N_EXPERTS
    assert rows % SUBLANES == 0
    return rows


def _pick_matrix(loc, weights, n_cols):
    col = lax.broadcasted_iota(jnp.int32, (loc.shape[0], n_cols), 1)
    out = jnp.zeros(col.shape, F32)
    for k in range(TOP_K):
        out = jnp.where(col == loc[:, k:k + 1], weights[k], out)
    return out


def _dispatch_kernel(gdst_ref, nch_ref, hf_ref, loc_ref, xb_in_ref, xb_ref, xc_ref, sem):
    del xb_in_ref
    j = pl.program_id(0)
    n_tiles = pl.num_programs(0)
    n_chunk_max = xc_ref.shape[1] // SUBLANES
    slot = j % 2

    def chunk_copy(tile, s, c):
        src = xc_ref.at[s, pl.ds(pl.multiple_of(c * SUBLANES, SUBLANES), SUBLANES), :]
        row = pl.multiple_of(gdst_ref[tile * n_chunk_max + c], SUBLANES)
        return pltpu.make_async_copy(src, xb_ref.at[pl.ds(row, SUBLANES), :], sem.at[s])

    def drain(tile, s):
        def body(c, carry):
            chunk_copy(tile, s, c).wait()
            return carry
        lax.fori_loop(0, nch_ref[tile], body, 0)

    sel = _pick_matrix(loc_ref[...], [1.0] * TOP_K, xc_ref.shape[1]).astype(BF16)
    xc_ref[slot] = lax.dot_general(sel, hf_ref[...], (((0,), (0,)), ((), ())), preferred_element_type=F32)

    def start(c, carry):
        chunk_copy(j, slot, c).start()
        return carry
    lax.fori_loop(0, nch_ref[j], start, 0)

    @pl.when(j > 0)
    def _():
        drain(j - 1, 1 - slot)

    @pl.when(j == n_tiles - 1)
    def _():
        drain(j, slot)


def _dispatch(hf, loc, gdst, nch, slots):
    rows, d = hf.shape
    n_slots = slots.shape[0]
    tm = ROW_TILE
    grid_spec = pltpu.PrefetchScalarGridSpec(
        num_scalar_prefetch=2,
        grid=(rows // tm,),
        in_specs=[pl.BlockSpec((tm, d), lambda i, gd, nc: (i, 0)),
                  pl.BlockSpec((tm, LANES), lambda i, gd, nc: (i, 0)),
                  pl.BlockSpec(memory_space=pl.ANY)],
        out_specs=pl.BlockSpec(memory_space=pl.ANY),
        scratch_shapes=[pltpu.VMEM((2, _local_rows(tm), d), F32), pltpu.SemaphoreType.DMA((2,))],
    )
    return pl.pallas_call(
        _dispatch_kernel,
        grid_spec=grid_spec,
        out_shape=jax.ShapeDtypeStruct((n_slots, d), F32),
        input_output_aliases={4: 0},
        compiler_params=_cparams("arbitrary"),
        name="dispatch",
    )(gdst, nch, hf, loc, slots)


def _combine_kernel(gdst_ref, nch_ref, yb_ref, loc_ref, gt_ref, x1_ref, mod_ref, g_ref, o_ref, yc_ref, sem):
    j = pl.program_id(0)
    n_tiles = pl.num_programs(0)
    n_chunk_max = yc_ref.shape[1] // SUBLANES

    def chunk_copy(tile, s, c):
        row = pl.multiple_of(gdst_ref[tile * n_chunk_max + c], SUBLANES)
        dst = yc_ref.at[s, pl.ds(pl.multiple_of(c * SUBLANES, SUBLANES), SUBLANES), :]
        return pltpu.make_async_copy(yb_ref.at[pl.ds(row, SUBLANES), :], dst, sem.at[s])

    def issue(tile, s):
        def body(c, carry):
            chunk_copy(tile, s, c).start()
            return carry
        lax.fori_loop(0, nch_ref[tile], body, 0)

    @pl.when(j == 0)
    def _():
        yc_ref[...] = jnp.zeros(yc_ref.shape, yc_ref.dtype)
        issue(0, 0)

    @pl.when(j + 1 < n_tiles)
    def _():
        issue(j + 1, (j + 1) % 2)

    slot = j % 2

    def drain(c, carry):
        chunk_copy(j, slot, c).wait()
        return carry
    lax.fori_loop(0, nch_ref[j], drain, 0)

    gt = gt_ref[...]
    w = _pick_matrix(loc_ref[...], [gt[:, k:k + 1] for k in range(TOP_K)], yc_ref.shape[1])
    w_hi, w_lo = _split2(w)
    rows_b = yc_ref[slot].astype(BF16)
    y = (jnp.dot(w_hi, rows_b, preferred_element_type=F32) + jnp.dot(w_lo, rows_b, preferred_element_type=F32))
    mod = mod_ref[0]
    o_ref[...] = x1_ref[...] + mod[5:6] * (_rms(y) * g_ref[...])


def _combine(yb, loc, gdst, nch, gt, x1, mods, seq, g_post):
    rows, d = x1.shape
    tm = ROW_TILE
    tps = seq // tm
    row = lambda w: pl.BlockSpec((tm, w), lambda i, gd, nc: (i, 0))
    grid_spec = pltpu.PrefetchScalarGridSpec(
        num_scalar_prefetch=2,
        grid=(rows // tm,),
        in_specs=[pl.BlockSpec(memory_space=pl.ANY), row(LANES), row(LANES), row(d),
                  pl.BlockSpec((1, SUBLANES, d), lambda i, gd, nc: (i // tps, 0, 0)),
                  pl.BlockSpec((1, d), lambda i, gd, nc: (0, 0))],
        out_specs=row(d),
        scratch_shapes=[pltpu.VMEM((2, _local_rows(tm), d), F32), pltpu.SemaphoreType.DMA((2,))],
    )
    return pl.pallas_call(
        _combine_kernel,
        grid_spec=grid_spec,
        out_shape=jax.ShapeDtypeStruct((rows, d), F32),
        compiler_params=_cparams("arbitrary"),
        name="combine",
    )(gdst, nch, yb, loc, gt, x1, mods, g_post.reshape(1, d))


def _route_slots(top_idx, rank, tile_counts, n_tok):
    bm = MOE_ROWS
    n_tiles = tile_counts.shape[0]
    tm = n_tok // n_tiles
    seg_len = -(-tile_counts // SUBLANES) * SUBLANES
    seg_off = jnp.cumsum(seg_len, axis=1) - seg_len
    per_expert = jnp.sum(seg_len, axis=0)
    padded = -(-per_expert // bm) * bm
    p_end = jnp.cumsum(padded)
    seg_slot = (p_end - padded)[None, :] + jnp.cumsum(seg_len, axis=0) - seg_len
    hit = top_idx[:, :, None] == jnp.arange(N_EXPERTS, dtype=jnp.int32)[None, None, :]
    loc = rank + jnp.sum(jnp.where(hit, jnp.repeat(seg_off, tm, axis=0)[:, None, :], 0), axis=-1)
    chunk_row = jnp.arange(_local_rows(tm) // SUBLANES, dtype=jnp.int32) * SUBLANES
    inside = ((seg_off[:, None, :] <= chunk_row[None, :, None])
              & (chunk_row[None, :, None] < (seg_off + seg_len)[:, None, :]))
    chunk_slot = jnp.sum(jnp.where(inside, (seg_slot - seg_off)[:, None, :] + chunk_row[None, :, None], 0), axis=-1)
    n_chunks = jnp.sum(seg_len, axis=1) // SUBLANES
    starts = jnp.arange(_moe_blocks(n_tok, n_tiles), dtype=jnp.int32) * bm
    block_exp = jnp.minimum(jnp.sum(p_end[None, :] <= starts[:, None], axis=1), N_EXPERTS - 1).astype(jnp.int32)
    n_used = (p_end[-1] // bm).astype(jnp.int32).reshape(1)
    loc = jnp.pad(loc.astype(jnp.int32), ((0, 0), (0, LANES - TOP_K)))
    return loc, chunk_slot.reshape(-1).astype(jnp.int32), n_chunks.astype(jnp.int32), block_exp, n_used


def _moe(hf, slots, loc, chunk_slot, n_chunks, block_exp, n_used, p):
    xb = _dispatch(hf, loc, chunk_slot, n_chunks, slots)
    w_gu = _gate_up_rows(p['w_gate_up'])
    b_gu = p['b_gate_up']
    b_gu = jnp.concatenate([b_gu[..., 0::2], b_gu[..., 1::2]], axis=-1)[:, None, :]
    return _experts(xb, block_exp, n_used, w_gu, b_gu, p['w_down'], p['b_down'][:, None, :])


def _layer(x, c, ctx, c_ctx, p):
    batch, seq, d = x.shape
    ctx_len = ctx.shape[1]
    n_mod = p['w_ada'].shape[1] // d

    c_rows = jnp.zeros((2 * SUBLANES, d), F32).at[:batch].set(c).at[batch].set(c_ctx)
    mods = _ada(c_rows, p['w_ada'], p['b_ada']).reshape(2 * SUBLANES, n_mod, d)
    mods = jnp.pad(mods, ((0, 0), (0, SUBLANES - n_mod), (0, 0)))

    w_in = p['w_in'].astype(BF16)
    w_att, w_rwkv, w_gate = (w_in[:, :ATT_COLS], w_in[:, ATT_COLS:ATT_COLS + RWKV_COLS],
                             w_in[:, ATT_COLS + RWKV_COLS:])
    b_in = p['b_in'].reshape(1, -1)
    b_att, b_rwkv, b_gate = (b_in[:, :ATT_COLS], b_in[:, ATT_COLS:ATT_COLS + RWKV_COLS],
                             b_in[:, ATT_COLS + RWKV_COLS:])
    g_pre = p['g_pre_mix'].reshape(1, d)
    proj = functools.partial(_project, g_pre=g_pre, w_att=w_att, w_rwkv=w_rwkv, w_gate=w_gate,
                             b_att=b_att, b_rwkv=b_rwkv, b_gate=b_gate)

    zc_att, zc_rwkv, _ = proj(ctx.reshape(batch * ctx_len, d), mods, lambda b: batch, ctx_len, rope=False)
    fc = dict(zip(FEAT_NAMES,_features(zc_rwkv, ctx_len, p)))
    zero_state = jnp.zeros((batch, RWKV_HEADS, HEAD_DIM, HEAD_DIM), F32)
    _, _, sc_f, sc_b = _scan(fc, batch, ctx_len, zero_state, zero_state)

    x2 = x.reshape(batch * seq, d)
    z_att, z_rwkv, z_gate = proj(x2, mods, lambda b: b, seq, rope=True)
    fx = dict(zip(FEAT_NAMES,_features(z_rwkv, seq, p)))
    y_f, y_b, _, _ = _scan(fx, batch, seq, sc_f, sc_b)
    att = _attention(z_att, zc_att, p['att_sinks'], batch, seq)
    x1, hf, idx, gt, rank, cnt, slots = _merge(y_f, y_b, fx, att, z_gate, x2, mods, seq, p)

    loc, chunk_slot, n_chunks, block_exp, n_used = _route_slots(
        idx[:, :TOP_K], rank[:, :TOP_K], cnt[:, 0, :N_EXPERTS], batch * seq)
    yb = _moe(hf, slots, loc, chunk_slot, n_chunks, block_exp, n_used, p)
    out = _combine(yb, loc, chunk_slot, n_chunks, gt, x1, mods, seq, p['g_post_ffn'])
    return out.reshape(batch, seq, d)


def kernel(x, c, ctx, c_ctx, w_ada, b_ada, g_pre_mix, g_post_mix, g_pre_ffn, g_post_ffn, w_in, b_in, mu_prev, mu_next, att_sinks, w0_f, w0_b, w2_f, w2_b, a0_f, a0_b, a2_f, a2_b, g2, k_k, k_a, r_k, ln_x_w, ln_x_b, w_up_att, w_up_rwkv, w_out, w_router, b_router, w_gate_up, b_gate_up, w_down, b_down):
    assert w_ada.shape[0] == 1, "single-layer problem: the context stream update is never consumed"
    p = dict(w_ada=w_ada[0], b_ada=b_ada[0], g_pre_mix=g_pre_mix[0], g_post_mix=g_post_mix[0],
             g_pre_ffn=g_pre_ffn[0], g_post_ffn=g_post_ffn[0], w_in=w_in[0], b_in=b_in[0],
             mu_prev=mu_prev[0], mu_next=mu_next[0], att_sinks=att_sinks[0], w0_f=w0_f[0], w0_b=w0_b[0],
             w2_f=w2_f[0], w2_b=w2_b[0], a0_f=a0_f[0], a0_b=a0_b[0], a2_f=a2_f[0], a2_b=a2_b[0], g2=g2[0],
             k_k=k_k[0], k_a=k_a[0], r_k=r_k[0].reshape(-1), ln_x_w=ln_x_w[0], ln_x_b=ln_x_b[0],
             w_up_att=w_up_att[0], w_up_rwkv=w_up_rwkv[0], w_out=w_out[0], w_router=w_router[0],
             b_router=b_router[0], w_gate_up=w_gate_up[0], b_gate_up=b_gate_up[0], w_down=w_down[0],
             b_down=b_down[0])
    return _layer(x, c, ctx, c_ctx, p)
```

```python
import functools

import jax
import jax.numpy as jnp
import numpy as np
from jax import lax
from jax.experimental import pallas as pl
from jax.experimental.pallas import tpu as pltpu

F32 = jnp.float32
BF16 = jnp.bfloat16

GRID_W = 64
HEAD_DIM = 64
ATT_HEADS = 8
ATT_KV_HEADS = 2
ATT_GROUPS = ATT_HEADS // ATT_KV_HEADS
ATT_DIM = ATT_HEADS * HEAD_DIM
KV_DIM = ATT_KV_HEADS * HEAD_DIM
WINDOW = 128
BLOCK = 128
ROPE_BASE = 10000.0
ATT_SCALE = HEAD_DIM ** -0.5
NEG_INF = -1e30
RWKV_HEADS = 8
RWKV_DIM = RWKV_HEADS * HEAD_DIM
DECAY_LORA = 64
ICL_LORA = 64
GATE_LORA = 128
GN_EPS = 64e-5
N_EXPERTS = 32
TOP_K = 4
SWIGLU_LIMIT = 7.0
SWIGLU_ALPHA = 1.702
RMS_EPS = 1e-6
ATT_COLS = ATT_DIM + 2 * KV_DIM
RWKV_COLS = 3 * RWKV_DIM + 2 * DECAY_LORA + 2 * ICL_LORA + GATE_LORA
ROPE_COLS = ATT_DIM + KV_DIM

LANES = 128
SUBLANES = 8
VMEM_LIMIT = 48 * 1024 * 1024

ROW_TILE = 256
SCAN_CHUNK = 64
SCAN_GROUP = 4
SCAN_BATCH = 4
MOE_ROWS = 512


def _cparams(*sem):
    return pltpu.CompilerParams(dimension_semantics=sem, vmem_limit_bytes=VMEM_LIMIT)


def _mm(a, b):
    return jnp.dot(a.astype(BF16), b.astype(BF16), preferred_element_type=F32)


def _mm_nt(a, b):
    return lax.dot_general(a.astype(BF16), b.astype(BF16), (((1,), (1,)), ((), ())),
                           preferred_element_type=F32)


def _split2(x):
    hi = x.astype(BF16)
    lo = (x - hi.astype(F32)).astype(BF16)
    return hi, lo


def _split3(x):
    hi = x.astype(BF16)
    r1 = x - hi.astype(F32)
    mid = r1.astype(BF16)
    lo = (r1 - mid.astype(F32)).astype(BF16)
    return hi, mid, lo


def _mm_f32(a, b):
    ah, al = _split2(a)
    bh, bl = _split2(b)
    d = functools.partial(jnp.dot, preferred_element_type=F32)
    return d(ah, bh) + d(ah, bl) + d(al, bh)


def _seg_sum(x, eseg):
    hi, lo = _split2(x)
    d = functools.partial(jnp.dot, preferred_element_type=F32)
    return d(hi, eseg) + d(lo, eseg)


def _rms(x):
    return x * lax.rsqrt(jnp.mean(x * x, axis=-1, keepdims=True) + RMS_EPS)


def _sigmoid(x):
    return 1.0 / (1.0 + jnp.exp(-x))


def _ada_kernel(c_ref, w_ref, b_ref, o_ref):
    c = c_ref[...]
    o_ref[...] = _mm_f32(c * _sigmoid(c), w_ref[...]) + b_ref[...]


def _ada(c_rows, w, b):
    m, d = c_rows.shape
    n = w.shape[1]
    tn = 1536
    return pl.pallas_call(
        _ada_kernel,
        grid=(n // tn,),
        in_specs=[pl.BlockSpec((m, d), lambda j: (0, 0)),
                  pl.BlockSpec((d, tn), lambda j: (0, j)),
                  pl.BlockSpec((1, tn), lambda j: (0, j))],
        out_specs=pl.BlockSpec((m, tn), lambda j: (0, j)),
        out_shape=jax.ShapeDtypeStruct((m, n), F32),
        compiler_params=_cparams("arbitrary"),
        name="ada",
    )(c_rows, w, b.reshape(1, n))


def _proj_kernel(*refs, rope):
    if rope:
        (x_ref, mod_ref, g_ref, wa_ref, wr_ref, wg_ref, ba_ref, br_ref, bg_ref, cos_ref, sin_ref,
         za_ref, zr_ref, zg_ref) = refs
    else:
        (x_ref, mod_ref, g_ref, wa_ref, wr_ref, wg_ref, ba_ref, br_ref, bg_ref,
         za_ref, zr_ref, zg_ref) = refs
    mod = mod_ref[0]
    h = _rms(x_ref[...]) * g_ref[...] * (1.0 + mod[1:2]) + mod[0:1]
    hb = h.astype(BF16)
    za = jnp.dot(hb, wa_ref[...], preferred_element_type=F32) + ba_ref[...]
    if rope:
        qk = za[:, :ROPE_COLS]
        lane = lax.broadcasted_iota(jnp.int32, qk.shape, 1)
        low = (lane % 32) < 16
        partner = jnp.where(low, pltpu.roll(qk, ROPE_COLS - 16, 1), pltpu.roll(qk, 16, 1))
        za_ref[:, :ROPE_COLS] = (qk * cos_ref[...] + partner * sin_ref[...]).astype(za_ref.dtype)
        za_ref[:, ROPE_COLS:] = za[:, ROPE_COLS:].astype(za_ref.dtype)
    else:
        za_ref[...] = za.astype(za_ref.dtype)
    zr_ref[...] = jnp.dot(hb, wr_ref[...], preferred_element_type=F32) + br_ref[...]
    zg_ref[...] = (jnp.dot(hb, wg_ref[...], preferred_element_type=F32) + bg_ref[...]).astype(zg_ref.dtype)


def _rope_tables(seq):
    n_rows = seq // GRID_W
    row = jnp.repeat(jnp.arange(n_rows, dtype=F32), GRID_W, total_repeat_length=seq)
    col = jnp.tile(jnp.arange(GRID_W, dtype=F32), n_rows)
    half = HEAD_DIM // 2
    inv_freq = ROPE_BASE ** (-jnp.arange(0, half, 2, dtype=F32) / half)
    ang_r = row[:, None] * inv_freq[None, :]
    ang_c = col[:, None] * inv_freq[None, :]
    cos_h = jnp.concatenate([jnp.cos(ang_r), jnp.cos(ang_r), jnp.cos(ang_c), jnp.cos(ang_c)], axis=1)
    sin_h = jnp.concatenate([-jnp.sin(ang_r), jnp.sin(ang_r), -jnp.sin(ang_c), jnp.sin(ang_c)], axis=1)
    reps = ROPE_COLS // HEAD_DIM
    return jnp.tile(cos_h, (1, reps)), jnp.tile(sin_h, (1, reps))


def _project(x2, mods, mod_row, seq, g_pre, w_att, w_rwkv, w_gate, b_att, b_rwkv, b_gate, rope):
    rows, d = x2.shape
    tm = ROW_TILE
    tps = seq // tm
    const = lambda i: (0, 0)
    in_specs = [pl.BlockSpec((tm, d), lambda i: (i, 0)),
                pl.BlockSpec((1, SUBLANES, d), lambda i: (mod_row(i // tps), 0, 0)),
                pl.BlockSpec((1, d), const),
                pl.BlockSpec(w_att.shape, const), pl.BlockSpec(w_rwkv.shape, const),
                pl.BlockSpec(w_gate.shape, const),
                pl.BlockSpec((1, ATT_COLS), const), pl.BlockSpec((1, RWKV_COLS), const),
                pl.BlockSpec((1, w_gate.shape[1]), const)]
    args = [x2, mods, g_pre, w_att, w_rwkv, w_gate, b_att, b_rwkv, b_gate]
    if rope:
        cos, sin = _rope_tables(seq)
        in_specs += [pl.BlockSpec((tm, ROPE_COLS), lambda i: (i % tps, 0))] * 2
        args += [cos, sin]
    return pl.pallas_call(
        functools.partial(_proj_kernel, rope=rope),
        grid=(rows // tm,),
        in_specs=in_specs,
        out_specs=[pl.BlockSpec((tm, ATT_COLS), lambda i: (i, 0)),
                   pl.BlockSpec((tm, RWKV_COLS), lambda i: (i, 0)),
                   pl.BlockSpec((tm, w_gate.shape[1]), lambda i: (i, 0))],
        out_shape=[jax.ShapeDtypeStruct((rows, ATT_COLS), BF16),
                   jax.ShapeDtypeStruct((rows, RWKV_COLS), F32),
                   jax.ShapeDtypeStruct((rows, w_gate.shape[1]), BF16)],
        compiler_params=_cparams("parallel"),
        name="proj",
    )(*args)


def _feat_kernel(z_ref, zp_ref, zn_ref, mup_ref, mun_ref, w2_ref, w0_ref, a2_ref, a0_ref, g2_ref,
                 kk_ref, ka_ref, eseg_ref,
                 r_o, v_o, kf_o, kb_o, lwf_o, lwb_o, kkn_o, bf_o, bb_o, gate_o, *, tiles_per_seq):
    ti = pl.program_id(0) % tiles_per_seq
    z = z_ref[...]
    tm = z.shape[0]
    row = lax.broadcasted_iota(jnp.int32, (tm, 1), 0)
    prev_halo = jnp.where(ti == 0, 0.0, zp_ref[SUBLANES - 1:SUBLANES, :])
    next_halo = jnp.where(ti == tiles_per_seq - 1, 0.0, zn_ref[0:1, :])
    prev = jnp.where(row == 0, prev_halo, pltpu.roll(z, 1, 0))
    nxt = jnp.where(row == tm - 1, next_halo, pltpu.roll(z, tm - 1, 0))
    zs = z + mup_ref[...] * (prev - z) + mun_ref[...] * (nxt - z)

    d = RWKV_DIM
    r = zs[:, 0:d]
    k = zs[:, d:2 * d]
    v = zs[:, 2 * d:3 * d]
    o = 3 * d
    wl = zs[:, o:o + 2 * DECAY_LORA]
    al = zs[:, o + 2 * DECAY_LORA:o + 2 * DECAY_LORA + 2 * ICL_LORA]
    gl = zs[:, o + 2 * DECAY_LORA + 2 * ICL_LORA:]

    w = w0_ref[...] + _mm(jnp.tanh(wl), w2_ref[...])
    nw = -w
    softplus = jnp.maximum(nw, 0.0) + jnp.log(1.0 + jnp.exp(-jnp.abs(nw)))
    lw = -jnp.exp(-softplus - 0.5)
    icl = _sigmoid(a0_ref[...] + _mm(al, a2_ref[...]))
    kk0 = k * kk_ref[...]
    ss = _seg_sum(kk0 * kk0, eseg_ref[...])
    kk = kk0 / jnp.maximum(jnp.sqrt(ss), 1e-12)
    ka = ka_ref[...]
    icl_f = icl[:, :d]
    icl_b = icl[:, d:]

    r_o[...] = r.astype(r_o.dtype)
    v_o[...] = v.astype(v_o.dtype)
    kf_o[...] = (k * (1.0 + (icl_f - 1.0) * ka)).astype(kf_o.dtype)
    kb_o[...] = (k * (1.0 + (icl_b - 1.0) * ka)).astype(kb_o.dtype)
    lwf_o[...] = lw[:, :d]
    lwb_o[...] = lw[:, d:]
    kkn_o[...] = kk.astype(kkn_o.dtype)
    bf_o[...] = (kk * icl_f).astype(bf_o.dtype)
    bb_o[...] = (kk * icl_b).astype(bb_o.dtype)
    gate_o[...] = _mm(_sigmoid(gl), g2_ref[...]).astype(gate_o.dtype)


FEAT_NAMES = ('r', 'v', 'kf', 'kb', 'lwf', 'lwb', 'kk', 'bf', 'bb', 'gate')
FEAT_BF16 = ('r', 'v', 'kf', 'kb', 'kk', 'bf', 'bb', 'gate')


def _block_diag2(a, b):
    za = jnp.zeros_like(a)
    zb = jnp.zeros_like(b)
    return jnp.concatenate([jnp.concatenate([a, zb], axis=1), jnp.concatenate([za, b], axis=1)], axis=0)


def _seg_matrix():
    h = np.arange(RWKV_DIM) // HEAD_DIM
    return jnp.asarray((h[:, None] == h[None, :]).astype(np.float32), BF16)


def _features(z_rwkv, seq, p):
    rows = z_rwkv.shape[0]
    tm = ROW_TILE
    tps = seq // tm
    hb = tm // SUBLANES
    nhb = rows // SUBLANES
    d = RWKV_DIM
    const = lambda i: (0, 0)
    w2 = _block_diag2(p['w2_f'], p['w2_b']).astype(BF16)
    a2 = _block_diag2(p['a2_f'], p['a2_b']).astype(BF16)
    w0 = jnp.concatenate([p['w0_f'], p['w0_b']]).reshape(1, 2 * d)
    a0 = jnp.concatenate([p['a0_f'], p['a0_b']]).reshape(1, 2 * d)
    out_spec = pl.BlockSpec((tm, d), lambda i: (i, 0))
    return pl.pallas_call(
        functools.partial(_feat_kernel, tiles_per_seq=tps),
        grid=(rows // tm,),
        in_specs=[pl.BlockSpec((tm, RWKV_COLS), lambda i: (i, 0)),
                  pl.BlockSpec((SUBLANES, RWKV_COLS), lambda i: (jnp.maximum(i * hb - 1, 0), 0)),
                  pl.BlockSpec((SUBLANES, RWKV_COLS), lambda i: (jnp.minimum((i + 1) * hb, nhb - 1), 0)),
                  pl.BlockSpec((1, RWKV_COLS), const), pl.BlockSpec((1, RWKV_COLS), const),
                  pl.BlockSpec(w2.shape, const), pl.BlockSpec((1, 2 * d), const),
                  pl.BlockSpec(a2.shape, const), pl.BlockSpec((1, 2 * d), const),
                  pl.BlockSpec((GATE_LORA, d), const),
                  pl.BlockSpec((1, d), const), pl.BlockSpec((1, d), const),
                  pl.BlockSpec((d, d), const)],
        out_specs=[out_spec] * 10,
        out_shape=[jax.ShapeDtypeStruct((rows, d), BF16 if name in FEAT_BF16 else F32) for name in FEAT_NAMES],
        compiler_params=_cparams("parallel"),
        name="feat",
    )(z_rwkv, z_rwkv, z_rwkv, p['mu_prev'].reshape(1, -1), p['mu_next'].reshape(1, -1),
      w2, w0, a2, a0, p['g2'].astype(BF16), p['k_k'].reshape(1, d), p['k_a'].reshape(1, d), _seg_matrix())


def _scan_kernel(rf, vf, kf, lwf, kkf, bf, rb, vb, kb, lwb, kkb, bb, s0f_ref, s0b_ref,
                 yf_ref, yb_ref, sTf_ref, sTb_ref, state_ref, *, chunk, n_chunks):
    c = pl.program_id(1)
    C = chunk
    hd = HEAD_DIM
    gw = SCAN_GROUP * hd
    n_groups = RWKV_HEADS // SCAN_GROUP
    assert C == hd, "the triangular masks below are shared between time and channel blocks"

    n_rows = state_ref.shape[0]

    @pl.when(c == 0)
    def _():
        for n in range(n_rows):
            for d, s0_ref in enumerate((s0f_ref, s0b_ref)):
                for g in range(n_groups):
                    state_ref[n, d, g] = jnp.zeros((gw, gw), F32)
                    for j in range(SCAN_GROUP):
                        state_ref[n, d, g, j * hd:(j + 1) * hd, j * hd:(j + 1) * hd] = (
                            s0_ref[n, g * SCAN_GROUP + j])

    ti = lax.broadcasted_iota(jnp.int32, (C, C), 0)
    si = lax.broadcasted_iota(jnp.int32, (C, C), 1)
    tg = lax.broadcasted_iota(jnp.int32, (C, gw), 0)
    sg = lax.broadcasted_iota(jnp.int32, (C, gw), 1) % C
    eye = (tg == sg).astype(F32)
    same_head = (lax.broadcasted_iota(jnp.int32, (gw, gw), 0) // hd
                 == lax.broadcasted_iota(jnp.int32, (gw, gw), 1) // hd)
    same_head_b = same_head.astype(BF16)
    n_double = int(np.log2(C)) - 1

    def bdiag(x_cat):
        return jnp.concatenate([x_cat.astype(BF16)] * SCAN_GROUP, axis=0) * same_head_b

    dirs = ((rf, vf, kf, lwf, kkf, bf), (rb, vb, kb, lwb, kkb, bb))
    units = []
    masks = [(((si <= ti) if d == 0 else (si >= ti)).astype(BF16),
              (sg <= tg) if d == 0 else (sg >= tg),
              (sg < tg) if d == 0 else (sg > tg)) for d in range(2)]
    for n, d in [(n, d) for n in range(n_rows) for d in range(2)]:
        r_ref, v_ref, k_ref, lw_ref, kk_ref, b_ref = (ref.at[n] for ref in dirs[d])
        tri, incl_g, strict_g = masks[d]
        lw = lw_ref[...]
        lh, lm, ll = _split3(lw)
        dd = functools.partial(jnp.dot, preferred_element_type=F32)
        cum = dd(tri, lh) + dd(tri, lm) + dd(tri, ll)
        cumx = cum - lw
        cum_end = cum[C - 1:C] if d == 0 else cum[0:1]
        e_neg = jnp.exp(-cum)
        e_end = jnp.exp(cum_end - cum)
        bv = b_ref[...].astype(F32)
        kv = k_ref[...].astype(F32)
        vv = v_ref[...].astype(F32)
        a_t = (-kk_ref[...].astype(F32) * jnp.exp(cumx)).astype(BF16)
        r_t = (r_ref[...].astype(F32) * jnp.exp(cum)).astype(BF16)
        b_t = (bv * e_neg).astype(BF16)
        k_t = (kv * e_neg).astype(BF16)
        b_q = (bv * e_end).astype(BF16)
        k_q = (kv * e_end).astype(BF16)
        g_end = jnp.exp(cum_end)
        for g in range(n_groups):
            sl = slice(g * gw, (g + 1) * gw)
            units.append(dict(
                n=n, d=d, g=g, sl=sl, incl=incl_g, strict=strict_g,
                P=jnp.concatenate([a_t[:, sl], r_t[:, sl]], axis=0),
                Q=jnp.concatenate([bdiag(b_t[:, sl]), bdiag(k_t[:, sl])], axis=0),
                Qq=jnp.concatenate([b_q[:, sl], k_q[:, sl]], axis=0),
                V=vv[:, sl], g_end=g_end[:, sl]))

    for u in units:
        u['G'] = _mm_nt(u['P'], u['Q'])
    for u in units:
        u['S0'] = state_ref[u['n'], u['d'], u['g']]
        u['PH'] = _mm_nt(u['P'], u['S0'])
        u['Vd'] = bdiag(u['V'])
    for u in units:
        G = u.pop('G')
        u['N'] = jnp.where(u['strict'], G[:C, :gw], 0.0)
        u['a_ak'] = jnp.where(u['strict'], G[:C, gw:], 0.0)
        u['a_rb'] = jnp.where(u['incl'], G[C:, :gw], 0.0)
        u['a_rk'] = jnp.where(u['incl'], G[C:, gw:], 0.0)
    for u in units:
        u['T'] = eye + u['N']
        u['Pw'] = _mm(u['N'], bdiag(u['N']))
        u['rhs'] = u['PH'][:C] + _mm(u['a_ak'], u['Vd'])
    for lvl in range(n_double):
        for u in units:
            pw = bdiag(u['Pw'])
            u['T'] = u['T'] + _mm(u['T'], pw)
            if lvl + 1 < n_double:
                u['Pw'] = _mm(u['Pw'], pw)
    for u in units:
        u['U'] = _mm(u['T'], bdiag(u['rhs']))
    y_refs = (yf_ref, yb_ref)
    for u in units:
        y_refs[u['d']][u['n'], :, u['sl']] = (u['PH'][C:] + _mm(u['a_rb'], bdiag(u['U']))
                                              + _mm(u['a_rk'], u['Vd']))
        uv = jnp.concatenate([u['U'], u['V']], axis=0)
        state_ref[u['n'], u['d'], u['g']] = jnp.where(
            same_head, u['S0'] * u['g_end'] + _mm(uv.T, u['Qq']), 0.0)

    @pl.when(c == n_chunks - 1)
    def _():
        for n in range(n_rows):
            for d, sT_ref in enumerate((sTf_ref, sTb_ref)):
                for g in range(n_groups):
                    for j in range(SCAN_GROUP):
                        sT_ref[n, g * SCAN_GROUP + j] = (
                            state_ref[n, d, g, j * hd:(j + 1) * hd, j * hd:(j + 1) * hd])


def _scan(f, batch, seq, s0_f, s0_b):
    C = SCAN_CHUNK
    nC = seq // C
    d = RWKV_DIM
    nb = max(n for n in range(1, SCAN_BATCH + 1) if batch % n == 0)
    fwd = pl.BlockSpec((nb, C, d), lambda b, c: (b, c, 0))
    bwd = pl.BlockSpec((nb, C, d), lambda b, c: (b, nC - 1 - c, 0))
    st = pl.BlockSpec((nb, RWKV_HEADS, HEAD_DIM, HEAD_DIM), lambda b, c: (b, 0, 0, 0))
    st_shape = jax.ShapeDtypeStruct((batch, RWKV_HEADS, HEAD_DIM, HEAD_DIM), F32)
    y_shape = jax.ShapeDtypeStruct((batch, seq, d), F32)
    gw = SCAN_GROUP * HEAD_DIM
    arr = lambda name: f[name].reshape(batch, seq, d)
    y_f, y_b, sT_f, sT_b = pl.pallas_call(
        functools.partial(_scan_kernel, chunk=C, n_chunks=nC),
        grid=(batch // nb, nC),
        in_specs=[fwd] * 6 + [bwd] * 6 + [st, st],
        out_specs=[fwd, bwd, st, st],
        out_shape=[y_shape, y_shape, st_shape, st_shape],
        scratch_shapes=[pltpu.VMEM((nb, 2, RWKV_HEADS // SCAN_GROUP, gw, gw), F32)],
        compiler_params=_cparams("parallel", "arbitrary"),
        name="scan",
    )(arr('r'), arr('v'), arr('kf'), arr('lwf'), arr('kk'), arr('bf'),
      arr('r'), arr('v'), arr('kb'), arr('lwb'), arr('kk'), arr('bb'), s0_f, s0_b)
    return y_f.reshape(batch * seq, d), y_b.reshape(batch * seq, d), sT_f, sT_b


def _attn_kernel(sink_ref, q_ref, kp_ref, km_ref, kn_ref, vp_ref, vm_ref, vn_ref, kc_ref, vc_ref, o_ref,
                 *, n_blocks):
    n = pl.program_id(1)
    rows = ATT_GROUPS * BLOCK
    q = q_ref[...] * ATT_SCALE
    qi = lax.broadcasted_iota(jnp.int32, (rows, BLOCK), 0) % BLOCK
    kj = lax.broadcasted_iota(jnp.int32, (rows, BLOCK), 1)
    ok_prev = (kj >= qi) & (n > 0)
    ok_next = (kj <= qi) & (n < n_blocks - 1)
    rowh = lax.broadcasted_iota(jnp.int32, (rows, 1), 0) // BLOCK
    outs = [None] * ATT_HEADS
    for g in range(ATT_KV_HEADS):
        ks = slice(g * HEAD_DIM, (g + 1) * HEAD_DIM)
        heads = [g * ATT_GROUPS + j for j in range(ATT_GROUPS)]
        Qs = jnp.concatenate([q[:, h * HEAD_DIM:(h + 1) * HEAD_DIM] for h in heads], axis=0)
        scores = [jnp.where(ok_prev, _mm_nt(Qs, kp_ref[:, ks]), NEG_INF),
                  _mm_nt(Qs, km_ref[:, ks]),
                  jnp.where(ok_next, _mm_nt(Qs, kn_ref[:, ks]), NEG_INF),
                  _mm_nt(Qs, kc_ref[:, ks])]
        values = [vp_ref, vm_ref, vn_ref, vc_ref]
        sink = jnp.zeros((rows, 1), F32)
        for j, h in enumerate(heads):
            sink = jnp.where(rowh == j, sink_ref[h], sink)
        folded = None
        for sc in scores:
            for c0 in range(0, sc.shape[1], BLOCK):
                blk = sc[:, c0:c0 + BLOCK]
                folded = blk if folded is None else jnp.maximum(folded, blk)
        m = jnp.maximum(sink, jnp.max(folded, axis=-1, keepdims=True))
        acc = jnp.zeros((rows, 2 * HEAD_DIM), F32)
        for sc, v_ref in zip(scores, values):
            one_col = (lax.broadcasted_iota(jnp.int32, (v_ref.shape[0], HEAD_DIM), 1) == 0).astype(BF16)
            v_ext = jnp.concatenate([v_ref[:, ks], one_col], axis=1)
            acc = acc + jnp.dot(jnp.exp((sc - m).astype(BF16)), v_ext, preferred_element_type=F32)
        den = acc[:, HEAD_DIM:HEAD_DIM + 1] + jnp.exp(sink - m)
        O = acc[:, :HEAD_DIM] / den
        for j, h in enumerate(heads):
            outs[h] = O[j * BLOCK:(j + 1) * BLOCK]
    o_ref[...] = jnp.concatenate(outs, axis=1).astype(o_ref.dtype)


def _attention(z_att, zc_att, sinks, batch, seq):
    nb = seq // BLOCK
    ctx_len = zc_att.shape[0] // batch
    kcol = ATT_DIM // KV_DIM
    vcol = kcol + 1

    def kv_spec(col, off):
        return pl.BlockSpec((BLOCK, KV_DIM), lambda b, n: (b * nb + jnp.clip(n + off, 0, nb - 1), col))

    return pl.pallas_call(
        functools.partial(_attn_kernel, n_blocks=nb),
        grid=(batch, nb),
        in_specs=[pl.BlockSpec(memory_space=pltpu.SMEM),
                  pl.BlockSpec((BLOCK, ATT_DIM), lambda b, n: (b * nb + n, 0)),
                  kv_spec(kcol, -1), kv_spec(kcol, 0), kv_spec(kcol, 1),
                  kv_spec(vcol, -1), kv_spec(vcol, 0), kv_spec(vcol, 1),
                  pl.BlockSpec((ctx_len, KV_DIM), lambda b, n: (b, kcol)),
                  pl.BlockSpec((ctx_len, KV_DIM), lambda b, n: (b, vcol))],
        out_specs=pl.BlockSpec((BLOCK, ATT_DIM), lambda b, n: (b * nb + n, 0)),
        out_shape=jax.ShapeDtypeStruct((batch * seq, ATT_DIM), BF16),
        compiler_params=_cparams("parallel", "parallel"),
        name="attn",
    )(sinks, z_att, z_att, z_att, z_att, z_att, z_att, z_att, zc_att, zc_att)


def _merge_kernel(yf_ref, yb_ref, r_ref, kf_ref, kb_ref, v_ref, gate_ref, att_ref, zg_ref, x_ref, mod_ref,
                  lnw_ref, lnb_ref, rk_ref, eseg_ref, wua_ref, wur_ref, wo_ref, gpm_ref, gpf_ref,
                  wr_ref, br_ref,
                  x1_ref, hf_ref, idx_ref, gt_ref, rank_ref, cnt_ref, slots_ref):
    eseg = eseg_ref[...]
    inv_n = 1.0 / HEAD_DIM
    y = yf_ref[...] + yb_ref[...]
    mean = _seg_sum(y, eseg) * inv_n
    dy = y - mean
    var = _seg_sum(dy * dy, eseg) * inv_n
    yn = dy * lax.rsqrt(var + GN_EPS) * lnw_ref[...] + lnb_ref[...]
    f32 = lambda ref: ref[...].astype(F32)
    bonus = _seg_sum(f32(r_ref) * (f32(kf_ref) + f32(kb_ref)) * rk_ref[...], eseg) * f32(v_ref)
    rwk = (yn + bonus) * f32(gate_ref)

    d = x_ref.shape[1]
    zg = zg_ref[...].astype(F32)
    merged = (_sigmoid(zg[:, :d]) * jnp.dot(att_ref[...], wua_ref[...], preferred_element_type=F32)
              + _sigmoid(zg[:, d:]) * _mm(rwk, wur_ref[...]))
    mix = _mm(merged, wo_ref[...])
    mod = mod_ref[0]
    x1 = x_ref[...] + mod[2:3] * (_rms(mix) * gpm_ref[...])
    x1_ref[...] = x1
    hf = _rms(x1) * gpf_ref[...] * (1.0 + mod[4:5]) + mod[3:4]
    hf_ref[...] = hf.astype(hf_ref.dtype)

    logits = _mm_f32(hf, wr_ref[...]) + br_ref[...]
    lane = lax.broadcasted_iota(jnp.int32, logits.shape, 1).astype(F32)
    idx_out = jnp.zeros(logits.shape, F32)
    val_out = jnp.zeros(logits.shape, F32)
    picked = jnp.zeros(logits.shape, F32)
    sels = []
    top = None
    den = None
    for kth in range(TOP_K):
        m = jnp.max(logits, axis=-1, keepdims=True)
        sel = jnp.min(jnp.where(logits == m, lane, float(LANES)), axis=-1, keepdims=True)
        hit = lane == sel
        logits = jnp.where(hit, -jnp.inf, logits)
        picked = jnp.where(hit, 1.0, picked)
        sels.append(hit)
        if kth == 0:
            top = m
        e = jnp.exp(m - top)
        den = e if kth == 0 else den + e
        idx_out = jnp.where(lane == float(kth), sel, idx_out)
        val_out = jnp.where(lane == float(kth), e, val_out)
    idx_ref[...] = idx_out.astype(jnp.int32)
    gt_ref[...] = val_out / den

    tm = picked.shape[0]
    earlier = (lax.broadcasted_iota(jnp.int32, (tm, tm), 1)
               < lax.broadcasted_iota(jnp.int32, (tm, tm), 0)).astype(BF16)
    before = jnp.dot(earlier, picked.astype(BF16), preferred_element_type=F32)
    rank_out = jnp.zeros(logits.shape, F32)
    for kth, hit in enumerate(sels):
        rk = jnp.sum(jnp.where(hit, before, 0.0), axis=-1, keepdims=True)
        rank_out = jnp.where(lane == float(kth), rk, rank_out)
    rank_ref[...] = rank_out.astype(jnp.int32)
    cnt_ref[0] = jnp.broadcast_to(jnp.sum(picked, axis=0, keepdims=True), cnt_ref.shape[1:]).astype(jnp.int32)
    slots_ref[...] = jnp.zeros(slots_ref.shape, slots_ref.dtype)


def _moe_blocks(n_tok, n_tiles):
    return -(-(n_tok * TOP_K + (SUBLANES - 1) * N_EXPERTS * n_tiles) // MOE_ROWS) + N_EXPERTS


def _merge(y_f, y_b, f, att, z_gate, x2, mods, seq, p):
    rows, d = x2.shape
    tm = ROW_TILE
    tps = seq // tm
    rd = RWKV_DIM
    const = lambda i: (0, 0)
    row = lambda w: pl.BlockSpec((tm, w), lambda i: (i, 0))
    vec = lambda w: pl.BlockSpec((1, w), const)
    w_router = jnp.pad(p['w_router'], ((0, 0), (0, LANES - N_EXPERTS)))
    b_router = jnp.pad(p['b_router'], (0, LANES - N_EXPERTS), constant_values=NEG_INF).reshape(1, LANES)
    n_slots = _moe_blocks(rows, rows // tm) * MOE_ROWS
    slab = n_slots // (rows // tm)
    assert slab * (rows // tm) == n_slots and slab % SUBLANES == 0
    return pl.pallas_call(
        _merge_kernel,
        grid=(rows // tm,),
        in_specs=[row(rd)] * 7 + [row(ATT_DIM), row(2 * d), row(d),
                  pl.BlockSpec((1, SUBLANES, d), lambda i: (i // tps, 0, 0)),
                  vec(rd), vec(rd), vec(rd), pl.BlockSpec((rd, rd), const),
                  pl.BlockSpec((ATT_DIM, d), const), pl.BlockSpec((rd, d), const), pl.BlockSpec((d, d), const),
                  vec(d), vec(d), pl.BlockSpec((d, LANES), const), vec(LANES)],
        out_specs=[row(d), row(d), row(LANES), row(LANES), row(LANES),
                   pl.BlockSpec((1, SUBLANES, LANES), lambda i: (i, 0, 0)),
                   pl.BlockSpec((slab, d), lambda i: (i, 0))],
        out_shape=[jax.ShapeDtypeStruct((rows, d), F32), jax.ShapeDtypeStruct((rows, d), BF16),
                   jax.ShapeDtypeStruct((rows, LANES), jnp.int32), jax.ShapeDtypeStruct((rows, LANES), F32),
                   jax.ShapeDtypeStruct((rows, LANES), jnp.int32),
                   jax.ShapeDtypeStruct((rows // tm, SUBLANES, LANES), jnp.int32),
                   jax.ShapeDtypeStruct((n_slots, d), F32)],
        compiler_params=_cparams("parallel"),
        name="merge",
    )(y_f, y_b, f['r'], f['kf'], f['kb'], f['v'], f['gate'], att, z_gate, x2, mods,
      p['ln_x_w'].reshape(1, rd), p['ln_x_b'].reshape(1, rd), p['r_k'].reshape(1, rd), _seg_matrix(),
      p['w_up_att'].astype(BF16), p['w_up_rwkv'].astype(BF16), p['w_out'].astype(BF16),
      p['g_post_mix'].reshape(1, d), p['g_pre_ffn'].reshape(1, d), w_router, b_router)


def _moe_kernel(be_ref, nused_ref, x_ref, wgu_ref, bgu_ref, wdn_f32_ref, bdn_ref, o_ref, wdn_ref):
    i = pl.program_id(0)

    @pl.when((i == 0) | (be_ref[i] != be_ref[jnp.maximum(i - 1, 0)]))
    def _():
        wdn_ref[...] = wdn_f32_ref[...].astype(BF16)

    @pl.when(i < nused_ref[0])
    def _():
        de = wdn_ref.shape[0]
        d = x_ref.shape[1]
        x = x_ref[...]
        bgu = bgu_ref[...]
        gate = jnp.minimum(_mm_nt(x, wgu_ref[:, :d]) + bgu[:, :de], SWIGLU_LIMIT)
        up = jnp.clip(_mm_nt(x, wgu_ref[:, d:]) + bgu[:, de:], -SWIGLU_LIMIT, SWIGLU_LIMIT)
        act = (up + 1.0) * (gate * _sigmoid(SWIGLU_ALPHA * gate))
        o_ref[...] = _mm(act, wdn_ref[...]) + bdn_ref[...]

    @pl.when(i >= nused_ref[0])
    def _():
        o_ref[...] = jnp.zeros(o_ref.shape, o_ref.dtype)


def _wprep_kernel(w_ref, o_ref, t_ref):
    d = w_ref.shape[1]
    half = o_ref.shape[1]
    wt = w_ref[0].T
    for c in range(d // LANES):
        cols = slice(c * LANES, (c + 1) * LANES)
        t_ref[c] = wt[:, cols]
        o_ref[0, :, cols] = t_ref[c, pl.ds(0, half, stride=2), :].astype(o_ref.dtype)
        o_ref[0, :, d + c * LANES:d + (c + 1) * LANES] = t_ref[c, pl.ds(1, half, stride=2), :].astype(o_ref.dtype)


def _gate_up_rows(w_gu):
    n_exp, d, de2 = w_gu.shape
    tc = 4 * LANES
    return pl.pallas_call(
        _wprep_kernel,
        grid=(n_exp, de2 // tc),
        in_specs=[pl.BlockSpec((1, d, tc), lambda e, j: (e, 0, j))],
        out_specs=pl.BlockSpec((1, tc // 2, 2 * d), lambda e, j: (e, j, 0)),
        out_shape=jax.ShapeDtypeStruct((n_exp, de2 // 2, 2 * d), BF16),
        scratch_shapes=[pltpu.VMEM((d // LANES, tc, LANES), F32)],
        compiler_params=_cparams("parallel", "parallel"),
        name="wprep",
    )(w_gu)


def _experts(xb, block_exp, n_used, w_gu, b_gu, w_dn, b_dn):
    n_slots, d = xb.shape
    bm = MOE_ROWS
    n_blocks = n_slots // bm
    de = w_dn.shape[1]
    de2 = 2 * de
    grid_spec = pltpu.PrefetchScalarGridSpec(
        num_scalar_prefetch=2,
        grid=(n_blocks,),
        in_specs=[pl.BlockSpec((bm, d), lambda i, be, nu: (i, 0)),
                  pl.BlockSpec((None, de, 2 * d), lambda i, be, nu: (be[i], 0, 0)),
                  pl.BlockSpec((None, 1, de2), lambda i, be, nu: (be[i], 0, 0)),
                  pl.BlockSpec((None, de, d), lambda i, be, nu: (be[i], 0, 0)),
                  pl.BlockSpec((None, 1, d), lambda i, be, nu: (be[i], 0, 0))],
        out_specs=pl.BlockSpec((bm, d), lambda i, be, nu: (i, 0)),
        scratch_shapes=[pltpu.VMEM((de, d), BF16)],
    )
    return pl.pallas_call(
        _moe_kernel,
        grid_spec=grid_spec,
        out_shape=jax.ShapeDtypeStruct((n_slots, d), F32),
        compiler_params=_cparams("arbitrary"),
        name="moe",
    )(block_exp, n_used, xb, w_gu, b_gu, w_dn, b_dn)


def _local_rows(tm):
    rows = tm * TOP_K + (SUBLANES - 1) * N_EXPERTS
    assert rows % SUBLANES == 0
    return rows


def _pick_matrix(loc, weights, n_cols):
    col = lax.broadcasted_iota(jnp.int32, (loc.shape[0], n_cols), 1)
    out = jnp.zeros(col.shape, F32)
    for k in range(TOP_K):
        out = jnp.where(col == loc[:, k:k + 1], weights[k], out)
    return out


def _segment(seg_ref, tile, e):
    base = (tile * N_EXPERTS + e) * 3
    return tuple(pl.multiple_of(seg_ref[base + i], SUBLANES) for i in range(3))


def _dispatch_kernel(seg_ref, nch_ref, hf_ref, loc_ref, xb_in_ref, xb_ref, xc_ref, sem):
    del xb_in_ref
    j = pl.program_id(0)
    n_tiles = pl.num_programs(0)
    slot = j % 2

    def drain(tile, s):
        rows = pl.multiple_of(nch_ref[tile] * SUBLANES, SUBLANES)
        done = xc_ref.at[s, pl.ds(0, rows), :]
        pltpu.make_async_copy(done, done, sem.at[s]).wait()

    sel = _pick_matrix(loc_ref[...], [1.0] * TOP_K, xc_ref.shape[1]).astype(BF16)
    xc_ref[slot] = lax.dot_general(sel, hf_ref[...], (((0,), (0,)), ((), ())), preferred_element_type=F32)

    def start(e, carry):
        off, row, n = _segment(seg_ref, j, e)

        @pl.when(n > 0)
        def _():
            pltpu.make_async_copy(xc_ref.at[slot, pl.ds(off, n), :], xb_ref.at[pl.ds(row, n), :],
                                  sem.at[slot]).start()
        return carry
    lax.fori_loop(0, N_EXPERTS, start, 0)

    @pl.when(j > 0)
    def _():
        drain(j - 1, 1 - slot)

    @pl.when(j == n_tiles - 1)
    def _():
        drain(j, slot)


def _dispatch(hf, loc, segments, nch, slots):
    rows, d = hf.shape
    n_slots = slots.shape[0]
    tm = ROW_TILE
    grid_spec = pltpu.PrefetchScalarGridSpec(
        num_scalar_prefetch=2,
        grid=(rows // tm,),
        in_specs=[pl.BlockSpec((tm, d), lambda i, gd, nc: (i, 0)),
                  pl.BlockSpec((tm, LANES), lambda i, gd, nc: (i, 0)),
                  pl.BlockSpec(memory_space=pl.ANY)],
        out_specs=pl.BlockSpec(memory_space=pl.ANY),
        scratch_shapes=[pltpu.VMEM((2, _local_rows(tm), d), F32), pltpu.SemaphoreType.DMA((2,))],
    )
    return pl.pallas_call(
        _dispatch_kernel,
        grid_spec=grid_spec,
        out_shape=jax.ShapeDtypeStruct((n_slots, d), F32),
        input_output_aliases={4: 0},
        compiler_params=_cparams("arbitrary"),
        name="dispatch",
    )(segments, nch, hf, loc, slots)


def _combine_kernel(seg_ref, nch_ref, yb_ref, loc_ref, gt_ref, x1_ref, mod_ref, g_ref, o_ref, yc_ref, sem):
    j = pl.program_id(0)
    n_tiles = pl.num_programs(0)

    def issue(tile, s):
        def body(e, carry):
            off, row, n = _segment(seg_ref, tile, e)

            @pl.when(n > 0)
            def _():
                pltpu.make_async_copy(yb_ref.at[pl.ds(row, n), :], yc_ref.at[s, pl.ds(off, n), :],
                                      sem.at[s]).start()
            return carry
        lax.fori_loop(0, N_EXPERTS, body, 0)

    @pl.when(j == 0)
    def _():
        yc_ref[...] = jnp.zeros(yc_ref.shape, yc_ref.dtype)
        issue(0, 0)

    @pl.when(j + 1 < n_tiles)
    def _():
        issue(j + 1, (j + 1) % 2)

    slot = j % 2

    rows = pl.multiple_of(nch_ref[j] * SUBLANES, SUBLANES)
    done = yc_ref.at[slot, pl.ds(0, rows), :]
    pltpu.make_async_copy(done, done, sem.at[slot]).wait()

    gt = gt_ref[...]
    w = _pick_matrix(loc_ref[...], [gt[:, k:k + 1] for k in range(TOP_K)], yc_ref.shape[1])
    w_hi, w_lo = _split2(w)
    rows_b = yc_ref[slot].astype(BF16)
    y = (jnp.dot(w_hi, rows_b, preferred_element_type=F32) + jnp.dot(w_lo, rows_b, preferred_element_type=F32))
    mod = mod_ref[0]
    o_ref[...] = x1_ref[...] + mod[5:6] * (_rms(y) * g_ref[...])


def _combine(yb, loc, segments, nch, gt, x1, mods, seq, g_post):
    rows, d = x1.shape
    tm = ROW_TILE
    tps = seq // tm
    row = lambda w: pl.BlockSpec((tm, w), lambda i, gd, nc: (i, 0))
    grid_spec = pltpu.PrefetchScalarGridSpec(
        num_scalar_prefetch=2,
        grid=(rows // tm,),
        in_specs=[pl.BlockSpec(memory_space=pl.ANY), row(LANES), row(LANES), row(d),
                  pl.BlockSpec((1, SUBLANES, d), lambda i, gd, nc: (i // tps, 0, 0)),
                  pl.BlockSpec((1, d), lambda i, gd, nc: (0, 0))],
        out_specs=row(d),
        scratch_shapes=[pltpu.VMEM((2, _local_rows(tm), d), F32), pltpu.SemaphoreType.DMA((2,))],
    )
    return pl.pallas_call(
        _combine_kernel,
        grid_spec=grid_spec,
        out_shape=jax.ShapeDtypeStruct((rows, d), F32),
        compiler_params=_cparams("arbitrary"),
        name="combine",
    )(segments, nch, yb, loc, gt, x1, mods, g_post.reshape(1, d))


def _route_slots(top_idx, rank, tile_counts, n_tok):
    bm = MOE_ROWS
    n_tiles = tile_counts.shape[0]
    tm = n_tok // n_tiles
    seg_len = -(-tile_counts // SUBLANES) * SUBLANES
    seg_off = jnp.cumsum(seg_len, axis=1) - seg_len
    per_expert = jnp.sum(seg_len, axis=0)
    padded = -(-per_expert // bm) * bm
    p_end = jnp.cumsum(padded)
    seg_slot = (p_end - padded)[None, :] + jnp.cumsum(seg_len, axis=0) - seg_len
    hit = top_idx[:, :, None] == jnp.arange(N_EXPERTS, dtype=jnp.int32)[None, None, :]
    loc = rank + jnp.sum(jnp.where(hit, jnp.repeat(seg_off, tm, axis=0)[:, None, :], 0), axis=-1)
    segments = jnp.stack([seg_off, seg_slot, seg_len], axis=-1).reshape(-1).astype(jnp.int32)
    n_chunks = jnp.sum(seg_len, axis=1) // SUBLANES
    starts = jnp.arange(_moe_blocks(n_tok, n_tiles), dtype=jnp.int32) * bm
    block_exp = jnp.minimum(jnp.sum(p_end[None, :] <= starts[:, None], axis=1), N_EXPERTS - 1).astype(jnp.int32)
    n_used = (p_end[-1] // bm).astype(jnp.int32).reshape(1)
    loc = jnp.pad(loc.astype(jnp.int32), ((0, 0), (0, LANES - TOP_K)))
    return loc, segments, n_chunks.astype(jnp.int32), block_exp, n_used


def _moe(hf, slots, loc, segments, n_chunks, block_exp, n_used, p):
    xb = _dispatch(hf, loc, segments, n_chunks, slots)
    w_gu = _gate_up_rows(p['w_gate_up'])
    b_gu = p['b_gate_up']
    b_gu = jnp.concatenate([b_gu[..., 0::2], b_gu[..., 1::2]], axis=-1)[:, None, :]
    return _experts(xb, block_exp, n_used, w_gu, b_gu, p['w_down'], p['b_down'][:, None, :])


def _layer(x, c, ctx, c_ctx, p):
    batch, seq, d = x.shape
    ctx_len = ctx.shape[1]
    n_mod = p['w_ada'].shape[1] // d

    c_rows = jnp.zeros((2 * SUBLANES, d), F32).at[:batch].set(c).at[batch].set(c_ctx)
    mods = _ada(c_rows, p['w_ada'], p['b_ada']).reshape(2 * SUBLANES, n_mod, d)
    mods = jnp.pad(mods, ((0, 0), (0, SUBLANES - n_mod), (0, 0)))

    w_in = p['w_in'].astype(BF16)
    w_att, w_rwkv, w_gate = (w_in[:, :ATT_COLS], w_in[:, ATT_COLS:ATT_COLS + RWKV_COLS],
                             w_in[:, ATT_COLS + RWKV_COLS:])
    b_in = p['b_in'].reshape(1, -1)
    b_att, b_rwkv, b_gate = (b_in[:, :ATT_COLS], b_in[:, ATT_COLS:ATT_COLS + RWKV_COLS],
                             b_in[:, ATT_COLS + RWKV_COLS:])
    g_pre = p['g_pre_mix'].reshape(1, d)
    proj = functools.partial(_project, g_pre=g_pre, w_att=w_att, w_rwkv=w_rwkv, w_gate=w_gate,
                             b_att=b_att, b_rwkv=b_rwkv, b_gate=b_gate)

    zc_att, zc_rwkv, _ = proj(ctx.reshape(batch * ctx_len, d), mods, lambda b: batch, ctx_len, rope=False)
    fc = dict(zip(FEAT_NAMES,_features(zc_rwkv, ctx_len, p)))
    zero_state = jnp.zeros((batch, RWKV_HEADS, HEAD_DIM, HEAD_DIM), F32)
    _, _, sc_f, sc_b = _scan(fc, batch, ctx_len, zero_state, zero_state)

    x2 = x.reshape(batch * seq, d)
    z_att, z_rwkv, z_gate = proj(x2, mods, lambda b: b, seq, rope=True)
    fx = dict(zip(FEAT_NAMES,_features(z_rwkv, seq, p)))
    y_f, y_b, _, _ = _scan(fx, batch, seq, sc_f, sc_b)
    att = _attention(z_att, zc_att, p['att_sinks'], batch, seq)
    x1, hf, idx, gt, rank, cnt, slots = _merge(y_f, y_b, fx, att, z_gate, x2, mods, seq, p)

    loc, segments, n_chunks, block_exp, n_used = _route_slots(
        idx[:, :TOP_K], rank[:, :TOP_K], cnt[:, 0, :N_EXPERTS], batch * seq)
    yb = _moe(hf, slots, loc, segments, n_chunks, block_exp, n_used, p)
    out = _combine(yb, loc, segments, n_chunks, gt, x1, mods, seq, p['g_post_ffn'])
    return out.reshape(batch, seq, d)


def kernel(x, c, ctx, c_ctx, w_ada, b_ada, g_pre_mix, g_post_mix, g_pre_ffn, g_post_ffn, w_in, b_in, mu_prev, mu_next, att_sinks, w0_f, w0_b, w2_f, w2_b, a0_f, a0_b, a2_f, a2_b, g2, k_k, k_a, r_k, ln_x_w, ln_x_b, w_up_att, w_up_rwkv, w_out, w_router, b_router, w_gate_up, b_gate_up, w_down, b_down):
    assert w_ada.shape[0] == 1, "single-layer problem: the context stream update is never consumed"
    p = dict(w_ada=w_ada[0], b_ada=b_ada[0], g_pre_mix=g_pre_mix[0], g_post_mix=g_post_mix[0],
             g_pre_ffn=g_pre_ffn[0], g_post_ffn=g_post_ffn[0], w_in=w_in[0], b_in=b_in[0],
             mu_prev=mu_prev[0], mu_next=mu_next[0], att_sinks=att_sinks[0], w0_f=w0_f[0], w0_b=w0_b[0],
             w2_f=w2_f[0], w2_b=w2_b[0], a0_f=a0_f[0], a0_b=a0_b[0], a2_f=a2_f[0], a2_b=a2_b[0], g2=g2[0],
             k_k=k_k[0], k_a=k_a[0], r_k=r_k[0].reshape(-1), ln_x_w=ln_x_w[0], ln_x_b=ln_x_b[0],
             w_up_att=w_up_att[0], w_up_rwkv=w_up_rwkv[0], w_out=w_out[0], w_router=w_router[0],
             b_router=b_router[0], w_gate_up=w_gate_up[0], b_gate_up=b_gate_up[0], w_down=w_down[0],
             b_down=b_down[0])
    return _layer(x, c, ctx, c_ctx, p)
```

```python
import functools

import jax
import jax.numpy as jnp
import numpy as np
from jax import lax
from jax.experimental import pallas as pl
from jax.experimental.pallas import tpu as pltpu

F32 = jnp.float32
BF16 = jnp.bfloat16

GRID_W = 64
HEAD_DIM = 64
ATT_HEADS = 8
ATT_KV_HEADS = 2
ATT_GROUPS = ATT_HEADS // ATT_KV_HEADS
ATT_DIM = ATT_HEADS * HEAD_DIM
KV_DIM = ATT_KV_HEADS * HEAD_DIM
WINDOW = 128
BLOCK = 128
ROPE_BASE = 10000.0
ATT_SCALE = HEAD_DIM ** -0.5
NEG_INF = -1e30
RWKV_HEADS = 8
RWKV_DIM = RWKV_HEADS * HEAD_DIM
DECAY_LORA = 64
ICL_LORA = 64
GATE_LORA = 128
GN_EPS = 64e-5
N_EXPERTS = 32
TOP_K = 4
SWIGLU_LIMIT = 7.0
SWIGLU_ALPHA = 1.702
RMS_EPS = 1e-6
ATT_COLS = ATT_DIM + 2 * KV_DIM
RWKV_COLS = 3 * RWKV_DIM + 2 * DECAY_LORA + 2 * ICL_LORA + GATE_LORA
ROPE_COLS = ATT_DIM + KV_DIM

LANES = 128
SUBLANES = 8
VMEM_LIMIT = 48 * 1024 * 1024

ROW_TILE = 256
SCAN_CHUNK = 64
SCAN_GROUP = 4
SCAN_BATCH = 4
MOE_ROWS = 512


def _cparams(*sem):
    return pltpu.CompilerParams(dimension_semantics=sem, vmem_limit_bytes=VMEM_LIMIT)


def _mm(a, b):
    return jnp.dot(a.astype(BF16), b.astype(BF16), preferred_element_type=F32)


def _mm_nt(a, b):
    return lax.dot_general(a.astype(BF16), b.astype(BF16), (((1,), (1,)), ((), ())),
                           preferred_element_type=F32)


def _split2(x):
    hi = x.astype(BF16)
    lo = (x - hi.astype(F32)).astype(BF16)
    return hi, lo


def _split3(x):
    hi = x.astype(BF16)
    r1 = x - hi.astype(F32)
    mid = r1.astype(BF16)
    lo = (r1 - mid.astype(F32)).astype(BF16)
    return hi, mid, lo


def _mm_f32(a, b):
    ah, al = _split2(a)
    bh, bl = _split2(b)
    d = functools.partial(jnp.dot, preferred_element_type=F32)
    return d(ah, bh) + d(ah, bl) + d(al, bh)


def _seg_sum(x, eseg):
    hi, lo = _split2(x)
    d = functools.partial(jnp.dot, preferred_element_type=F32)
    return d(hi, eseg) + d(lo, eseg)


def _rms(x):
    return x * lax.rsqrt(jnp.mean(x * x, axis=-1, keepdims=True) + RMS_EPS)


def _sigmoid(x):
    return 1.0 / (1.0 + jnp.exp(-x))


def _ada_kernel(c_ref, w_ref, b_ref, o_ref):
    c = c_ref[...]
    o_ref[...] = _mm_f32(c * _sigmoid(c), w_ref[...]) + b_ref[...]


def _ada(c_rows, w, b):
    m, d = c_rows.shape
    n = w.shape[1]
    tn = 1536
    return pl.pallas_call(
        _ada_kernel,
        grid=(n // tn,),
        in_specs=[pl.BlockSpec((m, d), lambda j: (0, 0)),
                  pl.BlockSpec((d, tn), lambda j: (0, j)),
                  pl.BlockSpec((1, tn), lambda j: (0, j))],
        out_specs=pl.BlockSpec((m, tn), lambda j: (0, j)),
        out_shape=jax.ShapeDtypeStruct((m, n), F32),
        compiler_params=_cparams("arbitrary"),
        name="ada",
    )(c_rows, w, b.reshape(1, n))


def _proj_kernel(*refs, rope, tiles_per_seq):
    n_rope = 2 if rope else 0
    (x_ref, xp_ref, xn_ref, mod_ref, g_ref, wa_ref, wr_ref, wg_ref, ba_ref, br_ref, bg_ref) = refs[:11]
    cos_ref, sin_ref = refs[11:11 + n_rope] if rope else (None, None)
    (mup_ref, mun_ref, w2_ref, w0_ref, a2_ref, a0_ref, g2_ref, kk_ref, ka_ref, eseg_ref) = refs[11 + n_rope:21 + n_rope]
    za_ref, zg_ref = refs[21 + n_rope:23 + n_rope]
    r_o, v_o, kf_o, kb_o, lwf_o, lwb_o, kkn_o, bf_o, bb_o, gate_o = refs[23 + n_rope:]
    mod = mod_ref[0]
    modulated = lambda xv: _rms(xv) * g_ref[...] * (1.0 + mod[1:2]) + mod[0:1]
    h = modulated(x_ref[...])
    hb = h.astype(BF16)
    za = jnp.dot(hb, wa_ref[...], preferred_element_type=F32) + ba_ref[...]
    if rope:
        qk = za[:, :ROPE_COLS]
        lane = lax.broadcasted_iota(jnp.int32, qk.shape, 1)
        low = (lane % 32) < 16
        partner = jnp.where(low, pltpu.roll(qk, ROPE_COLS - 16, 1), pltpu.roll(qk, 16, 1))
        za_ref[:, :ROPE_COLS] = (qk * cos_ref[...] + partner * sin_ref[...]).astype(za_ref.dtype)
        za_ref[:, ROPE_COLS:] = za[:, ROPE_COLS:].astype(za_ref.dtype)
    else:
        za_ref[...] = za.astype(za_ref.dtype)
    zg_ref[...] = (jnp.dot(hb, wg_ref[...], preferred_element_type=F32) + bg_ref[...]).astype(zg_ref.dtype)

    tm = x_ref.shape[0]
    h_ext = jnp.concatenate([modulated(xp_ref[...]), h, modulated(xn_ref[...])], axis=0).astype(BF16)
    z_ext = jnp.dot(h_ext, wr_ref[...], preferred_element_type=F32) + br_ref[...]
    z = z_ext[SUBLANES:SUBLANES + tm]
    ti = pl.program_id(0) % tiles_per_seq
    row = lax.broadcasted_iota(jnp.int32, (tm, 1), 0)
    prev_halo = jnp.where(ti == 0, 0.0, z_ext[SUBLANES - 1:SUBLANES])
    next_halo = jnp.where(ti == tiles_per_seq - 1, 0.0, z_ext[SUBLANES + tm:SUBLANES + tm + 1])
    prev = jnp.where(row == 0, prev_halo, pltpu.roll(z, 1, 0))
    nxt = jnp.where(row == tm - 1, next_halo, pltpu.roll(z, tm - 1, 0))
    zs = z + mup_ref[...] * (prev - z) + mun_ref[...] * (nxt - z)

    d = RWKV_DIM
    r = zs[:, 0:d]
    k = zs[:, d:2 * d]
    v = zs[:, 2 * d:3 * d]
    o = 3 * d
    wl = zs[:, o:o + 2 * DECAY_LORA]
    al = zs[:, o + 2 * DECAY_LORA:o + 2 * DECAY_LORA + 2 * ICL_LORA]
    gl = zs[:, o + 2 * DECAY_LORA + 2 * ICL_LORA:]

    w = w0_ref[...] + _mm(jnp.tanh(wl), w2_ref[...])
    nw = -w
    softplus = jnp.maximum(nw, 0.0) + jnp.log(1.0 + jnp.exp(-jnp.abs(nw)))
    lw = -jnp.exp(-softplus - 0.5)
    icl = _sigmoid(a0_ref[...] + _mm(al, a2_ref[...]))
    kk0 = k * kk_ref[...]
    ss = _seg_sum(kk0 * kk0, eseg_ref[...])
    kk = kk0 / jnp.maximum(jnp.sqrt(ss), 1e-12)
    ka = ka_ref[...]
    icl_f = icl[:, :d]
    icl_b = icl[:, d:]

    r_o[...] = r.astype(r_o.dtype)
    v_o[...] = v.astype(v_o.dtype)
    kf_o[...] = (k * (1.0 + (icl_f - 1.0) * ka)).astype(kf_o.dtype)
    kb_o[...] = (k * (1.0 + (icl_b - 1.0) * ka)).astype(kb_o.dtype)
    lwf_o[...] = lw[:, :d]
    lwb_o[...] = lw[:, d:]
    kkn_o[...] = kk.astype(kkn_o.dtype)
    bf_o[...] = (kk * icl_f).astype(bf_o.dtype)
    bb_o[...] = (kk * icl_b).astype(bb_o.dtype)
    gate_o[...] = _mm(_sigmoid(gl), g2_ref[...]).astype(gate_o.dtype)


def _rope_tables(seq):
    n_rows = seq // GRID_W
    row = jnp.repeat(jnp.arange(n_rows, dtype=F32), GRID_W, total_repeat_length=seq)
    col = jnp.tile(jnp.arange(GRID_W, dtype=F32), n_rows)
    half = HEAD_DIM // 2
    inv_freq = ROPE_BASE ** (-jnp.arange(0, half, 2, dtype=F32) / half)
    ang_r = row[:, None] * inv_freq[None, :]
    ang_c = col[:, None] * inv_freq[None, :]
    cos_h = jnp.concatenate([jnp.cos(ang_r), jnp.cos(ang_r), jnp.cos(ang_c), jnp.cos(ang_c)], axis=1)
    sin_h = jnp.concatenate([-jnp.sin(ang_r), jnp.sin(ang_r), -jnp.sin(ang_c), jnp.sin(ang_c)], axis=1)
    reps = ROPE_COLS // HEAD_DIM
    return jnp.tile(cos_h, (1, reps)), jnp.tile(sin_h, (1, reps))


def _project(x2, mods, mod_row, seq, g_pre, w_att, w_rwkv, w_gate, b_att, b_rwkv, b_gate, feat_params, rope):
    rows, d = x2.shape
    tm = ROW_TILE
    tps = seq // tm
    hb = tm // SUBLANES
    nhb = rows // SUBLANES
    rd = RWKV_DIM
    const = lambda i: (0, 0)
    in_specs = [pl.BlockSpec((tm, d), lambda i: (i, 0)),
                pl.BlockSpec((SUBLANES, d), lambda i: (jnp.maximum(i * hb - 1, 0), 0)),
                pl.BlockSpec((SUBLANES, d), lambda i: (jnp.minimum((i + 1) * hb, nhb - 1), 0)),
                pl.BlockSpec((1, SUBLANES, d), lambda i: (mod_row(i // tps), 0, 0)),
                pl.BlockSpec((1, d), const),
                pl.BlockSpec(w_att.shape, const), pl.BlockSpec(w_rwkv.shape, const),
                pl.BlockSpec(w_gate.shape, const),
                pl.BlockSpec((1, ATT_COLS), const), pl.BlockSpec((1, RWKV_COLS), const),
                pl.BlockSpec((1, w_gate.shape[1]), const)]
    args = [x2, x2, x2, mods, g_pre, w_att, w_rwkv, w_gate, b_att, b_rwkv, b_gate]
    if rope:
        cos, sin = _rope_tables(seq)
        in_specs += [pl.BlockSpec((tm, ROPE_COLS), lambda i: (i % tps, 0))] * 2
        args += [cos, sin]
    in_specs += [pl.BlockSpec(a.shape, const) for a in feat_params]
    args += feat_params
    feat_spec = pl.BlockSpec((tm, rd), lambda i: (i, 0))
    outs = pl.pallas_call(
        functools.partial(_proj_kernel, rope=rope, tiles_per_seq=tps),
        grid=(rows // tm,),
        in_specs=in_specs,
        out_specs=[pl.BlockSpec((tm, ATT_COLS), lambda i: (i, 0)),
                   pl.BlockSpec((tm, w_gate.shape[1]), lambda i: (i, 0))] + [feat_spec] * len(FEAT_NAMES),
        out_shape=[jax.ShapeDtypeStruct((rows, ATT_COLS), BF16),
                   jax.ShapeDtypeStruct((rows, w_gate.shape[1]), BF16)]
                  + [jax.ShapeDtypeStruct((rows, rd), BF16 if name in FEAT_BF16 else F32) for name in FEAT_NAMES],
        compiler_params=_cparams("parallel"),
        name="proj",
    )(*args)
    return outs[0], outs[1], dict(zip(FEAT_NAMES, outs[2:]))


FEAT_NAMES = ('r', 'v', 'kf', 'kb', 'lwf', 'lwb', 'kk', 'bf', 'bb', 'gate')
FEAT_BF16 = ('r', 'v', 'kf', 'kb', 'kk', 'bf', 'bb', 'gate')


def _block_diag2(a, b):
    za = jnp.zeros_like(a)
    zb = jnp.zeros_like(b)
    return jnp.concatenate([jnp.concatenate([a, zb], axis=1), jnp.concatenate([za, b], axis=1)], axis=0)


def _seg_matrix():
    h = np.arange(RWKV_DIM) // HEAD_DIM
    return jnp.asarray((h[:, None] == h[None, :]).astype(np.float32), BF16)


def _feature_params(p):
    d = RWKV_DIM
    return [p['mu_prev'].reshape(1, -1), p['mu_next'].reshape(1, -1),
            _block_diag2(p['w2_f'], p['w2_b']).astype(BF16),
            jnp.concatenate([p['w0_f'], p['w0_b']]).reshape(1, 2 * d),
            _block_diag2(p['a2_f'], p['a2_b']).astype(BF16),
            jnp.concatenate([p['a0_f'], p['a0_b']]).reshape(1, 2 * d),
            p['g2'].astype(BF16), p['k_k'].reshape(1, d), p['k_a'].reshape(1, d), _seg_matrix()]


def _scan_kernel(rf, vf, kf, lwf, kkf, bf, rb, vb, kb, lwb, kkb, bb, s0f_ref, s0b_ref,
                 yf_ref, yb_ref, sTf_ref, sTb_ref, state_ref, *, chunk, n_chunks):
    c = pl.program_id(1)
    C = chunk
    hd = HEAD_DIM
    gw = SCAN_GROUP * hd
    n_groups = RWKV_HEADS // SCAN_GROUP
    assert C == hd, "the triangular masks below are shared between time and channel blocks"

    n_rows = state_ref.shape[0]

    @pl.when(c == 0)
    def _():
        for n in range(n_rows):
            for d, s0_ref in enumerate((s0f_ref, s0b_ref)):
                for g in range(n_groups):
                    state_ref[n, d, g] = jnp.zeros((gw, gw), F32)
                    for j in range(SCAN_GROUP):
                        state_ref[n, d, g, j * hd:(j + 1) * hd, j * hd:(j + 1) * hd] = (
                            s0_ref[n, g * SCAN_GROUP + j])

    ti = lax.broadcasted_iota(jnp.int32, (C, C), 0)
    si = lax.broadcasted_iota(jnp.int32, (C, C), 1)
    tg = lax.broadcasted_iota(jnp.int32, (C, gw), 0)
    sg = lax.broadcasted_iota(jnp.int32, (C, gw), 1) % C
    eye = (tg == sg).astype(F32)
    same_head = (lax.broadcasted_iota(jnp.int32, (gw, gw), 0) // hd
                 == lax.broadcasted_iota(jnp.int32, (gw, gw), 1) // hd)
    same_head_b = same_head.astype(BF16)
    n_double = int(np.log2(C)) - 1

    def bdiag(x_cat):
        return jnp.concatenate([x_cat.astype(BF16)] * SCAN_GROUP, axis=0) * same_head_b

    dirs = ((rf, vf, kf, lwf, kkf, bf), (rb, vb, kb, lwb, kkb, bb))
    units = []
    masks = [(((si <= ti) if d == 0 else (si >= ti)).astype(BF16),
              (sg <= tg) if d == 0 else (sg >= tg),
              (sg < tg) if d == 0 else (sg > tg)) for d in range(2)]
    for n, d in [(n, d) for n in range(n_rows) for d in range(2)]:
        r_ref, v_ref, k_ref, lw_ref, kk_ref, b_ref = (ref.at[n] for ref in dirs[d])
        tri, incl_g, strict_g = masks[d]
        lw = lw_ref[...]
        lh, lm, ll = _split3(lw)
        dd = functools.partial(jnp.dot, preferred_element_type=F32)
        cum = dd(tri, lh) + dd(tri, lm) + dd(tri, ll)
        cumx = cum - lw
        cum_end = cum[C - 1:C] if d == 0 else cum[0:1]
        e_neg = jnp.exp(-cum)
        e_end = jnp.exp(cum_end - cum)
        bv = b_ref[...].astype(F32)
        kv = k_ref[...].astype(F32)
        vv = v_ref[...].astype(F32)
        a_t = (-kk_ref[...].astype(F32) * jnp.exp(cumx)).astype(BF16)
        r_t = (r_ref[...].astype(F32) * jnp.exp(cum)).astype(BF16)
        b_t = (bv * e_neg).astype(BF16)
        k_t = (kv * e_neg).astype(BF16)
        b_q = (bv * e_end).astype(BF16)
        k_q = (kv * e_end).astype(BF16)
        g_end = jnp.exp(cum_end)
        for g in range(n_groups):
            sl = slice(g * gw, (g + 1) * gw)
            units.append(dict(
                n=n, d=d, g=g, sl=sl, incl=incl_g, strict=strict_g,
                P=jnp.concatenate([a_t[:, sl], r_t[:, sl]], axis=0),
                Q=jnp.concatenate([bdiag(b_t[:, sl]), bdiag(k_t[:, sl])], axis=0),
                Qq=jnp.concatenate([b_q[:, sl], k_q[:, sl]], axis=0),
                V=vv[:, sl], g_end=g_end[:, sl]))

    for u in units:
        u['G'] = _mm_nt(u['P'], u['Q'])
    for u in units:
        u['S0'] = state_ref[u['n'], u['d'], u['g']]
        u['PH'] = _mm_nt(u['P'], u['S0'])
        u['Vd'] = bdiag(u['V'])
    for u in units:
        G = u.pop('G')
        u['N'] = jnp.where(u['strict'], G[:C, :gw], 0.0)
        u['a_ak'] = jnp.where(u['strict'], G[:C, gw:], 0.0)
        u['a_rb'] = jnp.where(u['incl'], G[C:, :gw], 0.0)
        u['a_rk'] = jnp.where(u['incl'], G[C:, gw:], 0.0)
    for u in units:
        u['T'] = eye + u['N']
        u['Pw'] = _mm(u['N'], bdiag(u['N']))
        u['rhs'] = u['PH'][:C] + _mm(u['a_ak'], u['Vd'])
    for lvl in range(n_double):
        for u in units:
            pw = bdiag(u['Pw'])
            u['T'] = u['T'] + _mm(u['T'], pw)
            if lvl + 1 < n_double:
                u['Pw'] = _mm(u['Pw'], pw)
    for u in units:
        u['U'] = _mm(u['T'], bdiag(u['rhs']))
    y_refs = (yf_ref, yb_ref)
    for u in units:
        y_refs[u['d']][u['n'], :, u['sl']] = (u['PH'][C:] + _mm(u['a_rb'], bdiag(u['U']))
                                              + _mm(u['a_rk'], u['Vd']))
        uv = jnp.concatenate([u['U'], u['V']], axis=0)
        state_ref[u['n'], u['d'], u['g']] = jnp.where(
            same_head, u['S0'] * u['g_end'] + _mm(uv.T, u['Qq']), 0.0)

    @pl.when(c == n_chunks - 1)
    def _():
        for n in range(n_rows):
            for d, sT_ref in enumerate((sTf_ref, sTb_ref)):
                for g in range(n_groups):
                    for j in range(SCAN_GROUP):
                        sT_ref[n, g * SCAN_GROUP + j] = (
                            state_ref[n, d, g, j * hd:(j + 1) * hd, j * hd:(j + 1) * hd])


def _scan(f, batch, seq, s0_f, s0_b):
    C = SCAN_CHUNK
    nC = seq // C
    d = RWKV_DIM
    nb = max(n for n in range(1, SCAN_BATCH + 1) if batch % n == 0)
    fwd = pl.BlockSpec((nb, C, d), lambda b, c: (b, c, 0))
    bwd = pl.BlockSpec((nb, C, d), lambda b, c: (b, nC - 1 - c, 0))
    st = pl.BlockSpec((nb, RWKV_HEADS, HEAD_DIM, HEAD_DIM), lambda b, c: (b, 0, 0, 0))
    st_shape = jax.ShapeDtypeStruct((batch, RWKV_HEADS, HEAD_DIM, HEAD_DIM), F32)
    y_shape = jax.ShapeDtypeStruct((batch, seq, d), F32)
    gw = SCAN_GROUP * HEAD_DIM
    arr = lambda name: f[name].reshape(batch, seq, d)
    y_f, y_b, sT_f, sT_b = pl.pallas_call(
        functools.partial(_scan_kernel, chunk=C, n_chunks=nC),
        grid=(batch // nb, nC),
        in_specs=[fwd] * 6 + [bwd] * 6 + [st, st],
        out_specs=[fwd, bwd, st, st],
        out_shape=[y_shape, y_shape, st_shape, st_shape],
        scratch_shapes=[pltpu.VMEM((nb, 2, RWKV_HEADS // SCAN_GROUP, gw, gw), F32)],
        compiler_params=_cparams("parallel", "arbitrary"),
        name="scan",
    )(arr('r'), arr('v'), arr('kf'), arr('lwf'), arr('kk'), arr('bf'),
      arr('r'), arr('v'), arr('kb'), arr('lwb'), arr('kk'), arr('bb'), s0_f, s0_b)
    return y_f.reshape(batch * seq, d), y_b.reshape(batch * seq, d), sT_f, sT_b


def _attn_kernel(sink_ref, q_ref, kp_ref, km_ref, kn_ref, vp_ref, vm_ref, vn_ref, kc_ref, vc_ref, o_ref,
                 *, n_blocks):
    n = pl.program_id(1)
    rows = ATT_GROUPS * BLOCK
    q = q_ref[...] * ATT_SCALE
    qi = lax.broadcasted_iota(jnp.int32, (rows, BLOCK), 0) % BLOCK
    kj = lax.broadcasted_iota(jnp.int32, (rows, BLOCK), 1)
    ok_prev = (kj >= qi) & (n > 0)
    ok_next = (kj <= qi) & (n < n_blocks - 1)
    rowh = lax.broadcasted_iota(jnp.int32, (rows, 1), 0) // BLOCK
    outs = [None] * ATT_HEADS
    for g in range(ATT_KV_HEADS):
        ks = slice(g * HEAD_DIM, (g + 1) * HEAD_DIM)
        heads = [g * ATT_GROUPS + j for j in range(ATT_GROUPS)]
        Qs = jnp.concatenate([q[:, h * HEAD_DIM:(h + 1) * HEAD_DIM] for h in heads], axis=0)
        scores = [jnp.where(ok_prev, _mm_nt(Qs, kp_ref[:, ks]), NEG_INF),
                  _mm_nt(Qs, km_ref[:, ks]),
                  jnp.where(ok_next, _mm_nt(Qs, kn_ref[:, ks]), NEG_INF),
                  _mm_nt(Qs, kc_ref[:, ks])]
        values = [vp_ref, vm_ref, vn_ref, vc_ref]
        sink = jnp.zeros((rows, 1), F32)
        for j, h in enumerate(heads):
            sink = jnp.where(rowh == j, sink_ref[h], sink)
        folded = None
        for sc in scores:
            for c0 in range(0, sc.shape[1], BLOCK):
                blk = sc[:, c0:c0 + BLOCK]
                folded = blk if folded is None else jnp.maximum(folded, blk)
        m = jnp.maximum(sink, jnp.max(folded, axis=-1, keepdims=True))
        acc = jnp.zeros((rows, 2 * HEAD_DIM), F32)
        for sc, v_ref in zip(scores, values):
            one_col = (lax.broadcasted_iota(jnp.int32, (v_ref.shape[0], HEAD_DIM), 1) == 0).astype(BF16)
            v_ext = jnp.concatenate([v_ref[:, ks], one_col], axis=1)
            acc = acc + jnp.dot(jnp.exp((sc - m).astype(BF16)), v_ext, preferred_element_type=F32)
        den = acc[:, HEAD_DIM:HEAD_DIM + 1] + jnp.exp(sink - m)
        O = acc[:, :HEAD_DIM] / den
        for j, h in enumerate(heads):
            outs[h] = O[j * BLOCK:(j + 1) * BLOCK]
    o_ref[...] = jnp.concatenate(outs, axis=1).astype(o_ref.dtype)


def _attention(z_att, zc_att, sinks, batch, seq):
    nb = seq // BLOCK
    ctx_len = zc_att.shape[0] // batch
    kcol = ATT_DIM // KV_DIM
    vcol = kcol + 1

    def kv_spec(col, off):
        return pl.BlockSpec((BLOCK, KV_DIM), lambda b, n: (b * nb + jnp.clip(n + off, 0, nb - 1), col))

    return pl.pallas_call(
        functools.partial(_attn_kernel, n_blocks=nb),
        grid=(batch, nb),
        in_specs=[pl.BlockSpec(memory_space=pltpu.SMEM),
                  pl.BlockSpec((BLOCK, ATT_DIM), lambda b, n: (b * nb + n, 0)),
                  kv_spec(kcol, -1), kv_spec(kcol, 0), kv_spec(kcol, 1),
                  kv_spec(vcol, -1), kv_spec(vcol, 0), kv_spec(vcol, 1),
                  pl.BlockSpec((ctx_len, KV_DIM), lambda b, n: (b, kcol)),
                  pl.BlockSpec((ctx_len, KV_DIM), lambda b, n: (b, vcol))],
        out_specs=pl.BlockSpec((BLOCK, ATT_DIM), lambda b, n: (b * nb + n, 0)),
        out_shape=jax.ShapeDtypeStruct((batch * seq, ATT_DIM), BF16),
        compiler_params=_cparams("parallel", "parallel"),
        name="attn",
    )(sinks, z_att, z_att, z_att, z_att, z_att, z_att, z_att, zc_att, zc_att)


def _merge_kernel(yf_ref, yb_ref, r_ref, kf_ref, kb_ref, v_ref, gate_ref, att_ref, zg_ref, x_ref, mod_ref,
                  lnw_ref, lnb_ref, rk_ref, eseg_ref, wua_ref, wur_ref, wo_ref, gpm_ref, gpf_ref,
                  wr_ref, br_ref,
                  x1_ref, hf_ref, idx_ref, gt_ref, rank_ref, cnt_ref, slots_ref):
    eseg = eseg_ref[...]
    inv_n = 1.0 / HEAD_DIM
    y = yf_ref[...] + yb_ref[...]
    mean = _seg_sum(y, eseg) * inv_n
    dy = y - mean
    var = _seg_sum(dy * dy, eseg) * inv_n
    yn = dy * lax.rsqrt(var + GN_EPS) * lnw_ref[...] + lnb_ref[...]
    f32 = lambda ref: ref[...].astype(F32)
    bonus = _seg_sum(f32(r_ref) * (f32(kf_ref) + f32(kb_ref)) * rk_ref[...], eseg) * f32(v_ref)
    rwk = (yn + bonus) * f32(gate_ref)

    d = x_ref.shape[1]
    zg = zg_ref[...].astype(F32)
    merged = (_sigmoid(zg[:, :d]) * jnp.dot(att_ref[...], wua_ref[...], preferred_element_type=F32)
              + _sigmoid(zg[:, d:]) * _mm(rwk, wur_ref[...]))
    mix = _mm(merged, wo_ref[...])
    mod = mod_ref[0]
    x1 = x_ref[...] + mod[2:3] * (_rms(mix) * gpm_ref[...])
    x1_ref[...] = x1
    hf = _rms(x1) * gpf_ref[...] * (1.0 + mod[4:5]) + mod[3:4]
    hf_ref[...] = hf.astype(hf_ref.dtype)

    logits = _mm_f32(hf, wr_ref[...]) + br_ref[...]
    lane = lax.broadcasted_iota(jnp.int32, logits.shape, 1).astype(F32)
    idx_out = jnp.zeros(logits.shape, F32)
    val_out = jnp.zeros(logits.shape, F32)
    picked = jnp.zeros(logits.shape, F32)
    sels = []
    top = None
    den = None
    for kth in range(TOP_K):
        m = jnp.max(logits, axis=-1, keepdims=True)
        sel = jnp.min(jnp.where(logits == m, lane, float(LANES)), axis=-1, keepdims=True)
        hit = lane == sel
        logits = jnp.where(hit, -jnp.inf, logits)
        picked = jnp.where(hit, 1.0, picked)
        sels.append(hit)
        if kth == 0:
            top = m
        e = jnp.exp(m - top)
        den = e if kth == 0 else den + e
        idx_out = jnp.where(lane == float(kth), sel, idx_out)
        val_out = jnp.where(lane == float(kth), e, val_out)
    idx_ref[...] = idx_out.astype(jnp.int32)
    gt_ref[...] = val_out / den

    tm = picked.shape[0]
    earlier = (lax.broadcasted_iota(jnp.int32, (tm, tm), 1)
               < lax.broadcasted_iota(jnp.int32, (tm, tm), 0)).astype(BF16)
    before = jnp.dot(earlier, picked.astype(BF16), preferred_element_type=F32)
    rank_out = jnp.zeros(logits.shape, F32)
    for kth, hit in enumerate(sels):
        rk = jnp.sum(jnp.where(hit, before, 0.0), axis=-1, keepdims=True)
        rank_out = jnp.where(lane == float(kth), rk, rank_out)
    rank_ref[...] = rank_out.astype(jnp.int32)
    cnt_ref[0] = jnp.broadcast_to(jnp.sum(picked, axis=0, keepdims=True), cnt_ref.shape[1:]).astype(jnp.int32)
    slots_ref[...] = jnp.zeros(slots_ref.shape, slots_ref.dtype)


def _moe_blocks(n_tok, n_tiles):
    return -(-(n_tok * TOP_K + (SUBLANES - 1) * N_EXPERTS * n_tiles) // MOE_ROWS) + N_EXPERTS


def _merge(y_f, y_b, f, att, z_gate, x2, mods, seq, p):
    rows, d = x2.shape
    tm = ROW_TILE
    tps = seq // tm
    rd = RWKV_DIM
    const = lambda i: (0, 0)
    row = lambda w: pl.BlockSpec((tm, w), lambda i: (i, 0))
    vec = lambda w: pl.BlockSpec((1, w), const)
    w_router = jnp.pad(p['w_router'], ((0, 0), (0, LANES - N_EXPERTS)))
    b_router = jnp.pad(p['b_router'], (0, LANES - N_EXPERTS), constant_values=NEG_INF).reshape(1, LANES)
    n_slots = _moe_blocks(rows, rows // tm) * MOE_ROWS
    slab = n_slots // (rows // tm)
    assert slab * (rows // tm) == n_slots and slab % SUBLANES == 0
    return pl.pallas_call(
        _merge_kernel,
        grid=(rows // tm,),
        in_specs=[row(rd)] * 7 + [row(ATT_DIM), row(2 * d), row(d),
                  pl.BlockSpec((1, SUBLANES, d), lambda i: (i // tps, 0, 0)),
                  vec(rd), vec(rd), vec(rd), pl.BlockSpec((rd, rd), const),
                  pl.BlockSpec((ATT_DIM, d), const), pl.BlockSpec((rd, d), const), pl.BlockSpec((d, d), const),
                  vec(d), vec(d), pl.BlockSpec((d, LANES), const), vec(LANES)],
        out_specs=[row(d), row(d), row(LANES), row(LANES), row(LANES),
                   pl.BlockSpec((1, SUBLANES, LANES), lambda i: (i, 0, 0)),
                   pl.BlockSpec((slab, d), lambda i: (i, 0))],
        out_shape=[jax.ShapeDtypeStruct((rows, d), F32), jax.ShapeDtypeStruct((rows, d), BF16),
                   jax.ShapeDtypeStruct((rows, LANES), jnp.int32), jax.ShapeDtypeStruct((rows, LANES), F32),
                   jax.ShapeDtypeStruct((rows, LANES), jnp.int32),
                   jax.ShapeDtypeStruct((rows // tm, SUBLANES, LANES), jnp.int32),
                   jax.ShapeDtypeStruct((n_slots, d), F32)],
        compiler_params=_cparams("parallel"),
        name="merge",
    )(y_f, y_b, f['r'], f['kf'], f['kb'], f['v'], f['gate'], att, z_gate, x2, mods,
      p['ln_x_w'].reshape(1, rd), p['ln_x_b'].reshape(1, rd), p['r_k'].reshape(1, rd), _seg_matrix(),
      p['w_up_att'].astype(BF16), p['w_up_rwkv'].astype(BF16), p['w_out'].astype(BF16),
      p['g_post_mix'].reshape(1, d), p['g_pre_ffn'].reshape(1, d), w_router, b_router)


def _moe_kernel(be_ref, nused_ref, x_ref, wgu_ref, bgu_ref, wdn_f32_ref, bdn_ref, o_ref, wdn_ref):
    i = pl.program_id(0)

    @pl.when((i == 0) | (be_ref[i] != be_ref[jnp.maximum(i - 1, 0)]))
    def _():
        wdn_ref[...] = wdn_f32_ref[...].astype(BF16)

    @pl.when(i < nused_ref[0])
    def _():
        de = wdn_ref.shape[0]
        d = x_ref.shape[1]
        x = x_ref[...]
        bgu = bgu_ref[...]
        gate = jnp.minimum(_mm_nt(x, wgu_ref[:, :d]) + bgu[:, :de], SWIGLU_LIMIT)
        up = jnp.clip(_mm_nt(x, wgu_ref[:, d:]) + bgu[:, de:], -SWIGLU_LIMIT, SWIGLU_LIMIT)
        act = (up + 1.0) * (gate * _sigmoid(SWIGLU_ALPHA * gate))
        o_ref[...] = _mm(act, wdn_ref[...]) + bdn_ref[...]

    @pl.when(i >= nused_ref[0])
    def _():
        o_ref[...] = jnp.zeros(o_ref.shape, o_ref.dtype)


def _wprep_kernel(w_ref, o_ref, t_ref):
    d = w_ref.shape[1]
    half = o_ref.shape[1]
    wt = w_ref[0].T
    for c in range(d // LANES):
        cols = slice(c * LANES, (c + 1) * LANES)
        t_ref[c] = wt[:, cols]
        o_ref[0, :, cols] = t_ref[c, pl.ds(0, half, stride=2), :].astype(o_ref.dtype)
        o_ref[0, :, d + c * LANES:d + (c + 1) * LANES] = t_ref[c, pl.ds(1, half, stride=2), :].astype(o_ref.dtype)


def _gate_up_rows(w_gu):
    n_exp, d, de2 = w_gu.shape
    tc = 4 * LANES
    return pl.pallas_call(
        _wprep_kernel,
        grid=(n_exp, de2 // tc),
        in_specs=[pl.BlockSpec((1, d, tc), lambda e, j: (e, 0, j))],
        out_specs=pl.BlockSpec((1, tc // 2, 2 * d), lambda e, j: (e, j, 0)),
        out_shape=jax.ShapeDtypeStruct((n_exp, de2 // 2, 2 * d), BF16),
        scratch_shapes=[pltpu.VMEM((d // LANES, tc, LANES), F32)],
        compiler_params=_cparams("parallel", "parallel"),
        name="wprep",
    )(w_gu)


def _experts(xb, block_exp, n_used, w_gu, b_gu, w_dn, b_dn):
    n_slots, d = xb.shape
    bm = MOE_ROWS
    n_blocks = n_slots // bm
    de = w_dn.shape[1]
    de2 = 2 * de
    grid_spec = pltpu.PrefetchScalarGridSpec(
        num_scalar_prefetch=2,
        grid=(n_blocks,),
        in_specs=[pl.BlockSpec((bm, d), lambda i, be, nu: (i, 0)),
                  pl.BlockSpec((None, de, 2 * d), lambda i, be, nu: (be[i], 0, 0)),
                  pl.BlockSpec((None, 1, de2), lambda i, be, nu: (be[i], 0, 0)),
                  pl.BlockSpec((None, de, d), lambda i, be, nu: (be[i], 0, 0)),
                  pl.BlockSpec((None, 1, d), lambda i, be, nu: (be[i], 0, 0))],
        out_specs=pl.BlockSpec((bm, d), lambda i, be, nu: (i, 0)),
        scratch_shapes=[pltpu.VMEM((de, d), BF16)],
    )
    return pl.pallas_call(
        _moe_kernel,
        grid_spec=grid_spec,
        out_shape=jax.ShapeDtypeStruct((n_slots, d), F32),
        compiler_params=_cparams("arbitrary"),
        name="moe",
    )(block_exp, n_used, xb, w_gu, b_gu, w_dn, b_dn)


def _local_rows(tm):
    rows = tm * TOP_K + (SUBLANES - 1) * N_EXPERTS
    assert rows % SUBLANES == 0
    return rows


def _pick_matrix(loc, weights, n_cols):
    col = lax.broadcasted_iota(jnp.int32, (loc.shape[0], n_cols), 1)
    out = jnp.zeros(col.shape, F32)
    for k in range(TOP_K):
        out = jnp.where(col == loc[:, k:k + 1], weights[k], out)
    return out


def _segment(seg_ref, tile, e):
    base = (tile * N_EXPERTS + e) * 3
    return tuple(pl.multiple_of(seg_ref[base + i], SUBLANES) for i in range(3))


def _dispatch_kernel(seg_ref, nch_ref, hf_ref, loc_ref, xb_in_ref, xb_ref, xc_ref, sem):
    del xb_in_ref
    j = pl.program_id(0)
    n_tiles = pl.num_programs(0)
    slot = j % 2

    def drain(tile, s):
        rows = pl.multiple_of(nch_ref[tile] * SUBLANES, SUBLANES)
        done = xc_ref.at[s, pl.ds(0, rows), :]
        pltpu.make_async_copy(done, done, sem.at[s]).wait()

    sel = _pick_matrix(loc_ref[...], [1.0] * TOP_K, xc_ref.shape[1]).astype(BF16)
    xc_ref[slot] = lax.dot_general(sel, hf_ref[...], (((0,), (0,)), ((), ())), preferred_element_type=F32)

    def start(e, carry):
        off, row, n = _segment(seg_ref, j, e)

        @pl.when(n > 0)
        def _():
            pltpu.make_async_copy(xc_ref.at[slot, pl.ds(off, n), :], xb_ref.at[pl.ds(row, n), :],
                                  sem.at[slot]).start()
        return carry
    lax.fori_loop(0, N_EXPERTS, start, 0)

    @pl.when(j > 0)
    def _():
        drain(j - 1, 1 - slot)

    @pl.when(j == n_tiles - 1)
    def _():
        drain(j, slot)


def _dispatch(hf, loc, segments, nch, slots):
    rows, d = hf.shape
    n_slots = slots.shape[0]
    tm = ROW_TILE
    grid_spec = pltpu.PrefetchScalarGridSpec(
        num_scalar_prefetch=2,
        grid=(rows // tm,),
        in_specs=[pl.BlockSpec((tm, d), lambda i, gd, nc: (i, 0)),
                  pl.BlockSpec((tm, LANES), lambda i, gd, nc: (i, 0)),
                  pl.BlockSpec(memory_space=pl.ANY)],
        out_specs=pl.BlockSpec(memory_space=pl.ANY),
        scratch_shapes=[pltpu.VMEM((2, _local_rows(tm), d), F32), pltpu.SemaphoreType.DMA((2,))],
    )
    return pl.pallas_call(
        _dispatch_kernel,
        grid_spec=grid_spec,
        out_shape=jax.ShapeDtypeStruct((n_slots, d), F32),
        input_output_aliases={4: 0},
        compiler_params=_cparams("arbitrary"),
        name="dispatch",
    )(segments, nch, hf, loc, slots)


def _combine_kernel(seg_ref, nch_ref, yb_ref, loc_ref, gt_ref, x1_ref, mod_ref, g_ref, o_ref, yc_ref, sem):
    j = pl.program_id(0)
    n_tiles = pl.num_programs(0)

    def issue(tile, s):
        def body(e, carry):
            off, row, n = _segment(seg_ref, tile, e)

            @pl.when(n > 0)
            def _():
                pltpu.make_async_copy(yb_ref.at[pl.ds(row, n), :], yc_ref.at[s, pl.ds(off, n), :],
                                      sem.at[s]).start()
            return carry
        lax.fori_loop(0, N_EXPERTS, body, 0)

    @pl.when(j == 0)
    def _():
        yc_ref[...] = jnp.zeros(yc_ref.shape, yc_ref.dtype)
        issue(0, 0)

    @pl.when(j + 1 < n_tiles)
    def _():
        issue(j + 1, (j + 1) % 2)

    slot = j % 2

    rows = pl.multiple_of(nch_ref[j] * SUBLANES, SUBLANES)
    done = yc_ref.at[slot, pl.ds(0, rows), :]
    pltpu.make_async_copy(done, done, sem.at[slot]).wait()

    gt = gt_ref[...]
    w = _pick_matrix(loc_ref[...], [gt[:, k:k + 1] for k in range(TOP_K)], yc_ref.shape[1])
    w_hi, w_lo = _split2(w)
    rows_b = yc_ref[slot].astype(BF16)
    y = (jnp.dot(w_hi, rows_b, preferred_element_type=F32) + jnp.dot(w_lo, rows_b, preferred_element_type=F32))
    mod = mod_ref[0]
    o_ref[...] = x1_ref[...] + mod[5:6] * (_rms(y) * g_ref[...])


def _combine(yb, loc, segments, nch, gt, x1, mods, seq, g_post):
    rows, d = x1.shape
    tm = ROW_TILE
    tps = seq // tm
    row = lambda w: pl.BlockSpec((tm, w), lambda i, gd, nc: (i, 0))
    grid_spec = pltpu.PrefetchScalarGridSpec(
        num_scalar_prefetch=2,
        grid=(rows // tm,),
        in_specs=[pl.BlockSpec(memory_space=pl.ANY), row(LANES), row(LANES), row(d),
                  pl.BlockSpec((1, SUBLANES, d), lambda i, gd, nc: (i // tps, 0, 0)),
                  pl.BlockSpec((1, d), lambda i, gd, nc: (0, 0))],
        out_specs=row(d),
        scratch_shapes=[pltpu.VMEM((2, _local_rows(tm), d), F32), pltpu.SemaphoreType.DMA((2,))],
    )
    return pl.pallas_call(
        _combine_kernel,
        grid_spec=grid_spec,
        out_shape=jax.ShapeDtypeStruct((rows, d), F32),
        compiler_params=_cparams("arbitrary"),
        name="combine",
    )(segments, nch, yb, loc, gt, x1, mods, g_post.reshape(1, d))


def _route_slots(top_idx, rank, tile_counts, n_tok):
    bm = MOE_ROWS
    n_tiles = tile_counts.shape[0]
    tm = n_tok // n_tiles
    seg_len = -(-tile_counts // SUBLANES) * SUBLANES
    seg_off = jnp.cumsum(seg_len, axis=1) - seg_len
    per_expert = jnp.sum(seg_len, axis=0)
    padded = -(-per_expert // bm) * bm
    p_end = jnp.cumsum(padded)
    seg_slot = (p_end - padded)[None, :] + jnp.cumsum(seg_len, axis=0) - seg_len
    hit = top_idx[:, :, None] == jnp.arange(N_EXPERTS, dtype=jnp.int32)[None, None, :]
    loc = rank + jnp.sum(jnp.where(hit, jnp.repeat(seg_off, tm, axis=0)[:, None, :], 0), axis=-1)
    segments = jnp.stack([seg_off, seg_slot, seg_len], axis=-1).reshape(-1).astype(jnp.int32)
    n_chunks = jnp.sum(seg_len, axis=1) // SUBLANES
    starts = jnp.arange(_moe_blocks(n_tok, n_tiles), dtype=jnp.int32) * bm
    block_exp = jnp.minimum(jnp.sum(p_end[None, :] <= starts[:, None], axis=1), N_EXPERTS - 1).astype(jnp.int32)
    n_used = (p_end[-1] // bm).astype(jnp.int32).reshape(1)
    loc = jnp.pad(loc.astype(jnp.int32), ((0, 0), (0, LANES - TOP_K)))
    return loc, segments, n_chunks.astype(jnp.int32), block_exp, n_used


def _moe(hf, slots, loc, segments, n_chunks, block_exp, n_used, p):
    xb = _dispatch(hf, loc, segments, n_chunks, slots)
    w_gu = _gate_up_rows(p['w_gate_up'])
    b_gu = p['b_gate_up']
    b_gu = jnp.concatenate([b_gu[..., 0::2], b_gu[..., 1::2]], axis=-1)[:, None, :]
    return _experts(xb, block_exp, n_used, w_gu, b_gu, p['w_down'], p['b_down'][:, None, :])


def _layer(x, c, ctx, c_ctx, p):
    batch, seq, d = x.shape
    ctx_len = ctx.shape[1]
    n_mod = p['w_ada'].shape[1] // d

    c_rows = jnp.zeros((2 * SUBLANES, d), F32).at[:batch].set(c).at[batch].set(c_ctx)
    mods = _ada(c_rows, p['w_ada'], p['b_ada']).reshape(2 * SUBLANES, n_mod, d)
    mods = jnp.pad(mods, ((0, 0), (0, SUBLANES - n_mod), (0, 0)))

    w_in = p['w_in'].astype(BF16)
    w_att, w_rwkv, w_gate = (w_in[:, :ATT_COLS], w_in[:, ATT_COLS:ATT_COLS + RWKV_COLS],
                             w_in[:, ATT_COLS + RWKV_COLS:])
    b_in = p['b_in'].reshape(1, -1)
    b_att, b_rwkv, b_gate = (b_in[:, :ATT_COLS], b_in[:, ATT_COLS:ATT_COLS + RWKV_COLS],
                             b_in[:, ATT_COLS + RWKV_COLS:])
    g_pre = p['g_pre_mix'].reshape(1, d)
    proj = functools.partial(_project, g_pre=g_pre, w_att=w_att, w_rwkv=w_rwkv, w_gate=w_gate,
                             b_att=b_att, b_rwkv=b_rwkv, b_gate=b_gate, feat_params=_feature_params(p))

    zc_att, _, fc = proj(ctx.reshape(batch * ctx_len, d), mods, lambda b: batch, ctx_len, rope=False)
    zero_state = jnp.zeros((batch, RWKV_HEADS, HEAD_DIM, HEAD_DIM), F32)
    _, _, sc_f, sc_b = _scan(fc, batch, ctx_len, zero_state, zero_state)

    x2 = x.reshape(batch * seq, d)
    z_att, z_gate, fx = proj(x2, mods, lambda b: b, seq, rope=True)
    y_f, y_b, _, _ = _scan(fx, batch, seq, sc_f, sc_b)
    att = _attention(z_att, zc_att, p['att_sinks'], batch, seq)
    x1, hf, idx, gt, rank, cnt, slots = _merge(y_f, y_b, fx, att, z_gate, x2, mods, seq, p)

    loc, segments, n_chunks, block_exp, n_used = _route_slots(
        idx[:, :TOP_K], rank[:, :TOP_K], cnt[:, 0, :N_EXPERTS], batch * seq)
    yb = _moe(hf, slots, loc, segments, n_chunks, block_exp, n_used, p)
    out = _combine(yb, loc, segments, n_chunks, gt, x1, mods, seq, p['g_post_ffn'])
    return out.reshape(batch, seq, d)


def kernel(x, c, ctx, c_ctx, w_ada, b_ada, g_pre_mix, g_post_mix, g_pre_ffn, g_post_ffn, w_in, b_in, mu_prev, mu_next, att_sinks, w0_f, w0_b, w2_f, w2_b, a0_f, a0_b, a2_f, a2_b, g2, k_k, k_a, r_k, ln_x_w, ln_x_b, w_up_att, w_up_rwkv, w_out, w_router, b_router, w_gate_up, b_gate_up, w_down, b_down):
    assert w_ada.shape[0] == 1, "single-layer problem: the context stream update is never consumed"
    p = dict(w_ada=w_ada[0], b_ada=b_ada[0], g_pre_mix=g_pre_mix[0], g_post_mix=g_post_mix[0],
             g_pre_ffn=g_pre_ffn[0], g_post_ffn=g_post_ffn[0], w_in=w_in[0], b_in=b_in[0],
             mu_prev=mu_prev[0], mu_next=mu_next[0], att_sinks=att_sinks[0], w0_f=w0_f[0], w0_b=w0_b[0],
             w2_f=w2_f[0], w2_b=w2_b[0], a0_f=a0_f[0], a0_b=a0_b[0], a2_f=a2_f[0], a2_b=a2_b[0], g2=g2[0],
             k_k=k_k[0], k_a=k_a[0], r_k=r_k[0].reshape(-1), ln_x_w=ln_x_w[0], ln_x_b=ln_x_b[0],
             w_up_att=w_up_att[0], w_up_rwkv=w_up_rwkv[0], w_out=w_out[0], w_router=w_router[0],
             b_router=b_router[0], w_gate_up=w_gate_up[0], b_gate_up=b_gate_up[0], w_down=w_down[0],
             b_down=b_down[0])
    return _layer(x, c, ctx, c_ctx, p)
```

```python
import functools

import jax
import jax.numpy as jnp
import numpy as np
from jax import lax
from jax.experimental import pallas as pl
from jax.experimental.pallas import tpu as pltpu

F32 = jnp.float32
BF16 = jnp.bfloat16

GRID_W = 64
HEAD_DIM = 64
ATT_HEADS = 8
ATT_KV_HEADS = 2
ATT_GROUPS = ATT_HEADS // ATT_KV_HEADS
ATT_DIM = ATT_HEADS * HEAD_DIM
KV_DIM = ATT_KV_HEADS * HEAD_DIM
WINDOW = 128
BLOCK = 128
ROPE_BASE = 10000.0
ATT_SCALE = HEAD_DIM ** -0.5
NEG_INF = -1e30
RWKV_HEADS = 8
RWKV_DIM = RWKV_HEADS * HEAD_DIM
DECAY_LORA = 64
ICL_LORA = 64
GATE_LORA = 128
GN_EPS = 64e-5
N_EXPERTS = 32
TOP_K = 4
SWIGLU_LIMIT = 7.0
SWIGLU_ALPHA = 1.702
RMS_EPS = 1e-6
ATT_COLS = ATT_DIM + 2 * KV_DIM
RWKV_COLS = 3 * RWKV_DIM + 2 * DECAY_LORA + 2 * ICL_LORA + GATE_LORA
ROPE_COLS = ATT_DIM + KV_DIM

LANES = 128
SUBLANES = 8
VMEM_LIMIT = 48 * 1024 * 1024

ROW_TILE = 256
SCAN_CHUNK = 64
SCAN_GROUP = 4
SCAN_BATCH = 4
MOE_ROWS = 512


def _cparams(*sem):
    return pltpu.CompilerParams(dimension_semantics=sem, vmem_limit_bytes=VMEM_LIMIT)


def _mm(a, b):
    return jnp.dot(a.astype(BF16), b.astype(BF16), preferred_element_type=F32)


def _mm_nt(a, b):
    return lax.dot_general(a.astype(BF16), b.astype(BF16), (((1,), (1,)), ((), ())),
                           preferred_element_type=F32)


def _split2(x):
    hi = x.astype(BF16)
    lo = (x - hi.astype(F32)).astype(BF16)
    return hi, lo


def _split3(x):
    hi = x.astype(BF16)
    r1 = x - hi.astype(F32)
    mid = r1.astype(BF16)
    lo = (r1 - mid.astype(F32)).astype(BF16)
    return hi, mid, lo


def _mm_f32(a, b):
    ah, al = _split2(a)
    bh, bl = _split2(b)
    d = functools.partial(jnp.dot, preferred_element_type=F32)
    return d(ah, bh) + d(ah, bl) + d(al, bh)


def _seg_sum(x, eseg):
    hi, lo = _split2(x)
    d = functools.partial(jnp.dot, preferred_element_type=F32)
    return d(hi, eseg) + d(lo, eseg)


def _rms(x):
    return x * lax.rsqrt(jnp.mean(x * x, axis=-1, keepdims=True) + RMS_EPS)


def _sigmoid(x):
    return 1.0 / (1.0 + jnp.exp(-x))


def _ada_kernel(c_ref, w_ref, b_ref, o_ref):
    c = c_ref[...]
    o_ref[...] = _mm_f32(c * _sigmoid(c), w_ref[...]) + b_ref[...]


def _ada(c_rows, w, b):
    m, d = c_rows.shape
    n = w.shape[1]
    tn = 1536
    return pl.pallas_call(
        _ada_kernel,
        grid=(n // tn,),
        in_specs=[pl.BlockSpec((m, d), lambda j: (0, 0)),
                  pl.BlockSpec((d, tn), lambda j: (0, j)),
                  pl.BlockSpec((1, tn), lambda j: (0, j))],
        out_specs=pl.BlockSpec((m, tn), lambda j: (0, j)),
        out_shape=jax.ShapeDtypeStruct((m, n), F32),
        compiler_params=_cparams("arbitrary"),
        name="ada",
    )(c_rows, w, b.reshape(1, n))


def _proj_kernel(*refs, rope, tiles_per_seq):
    n_rope = 2 if rope else 0
    (x_ref, xp_ref, xn_ref, mod_ref, g_ref, wa_ref, wr_ref, wg_ref, ba_ref, br_ref, bg_ref) = refs[:11]
    cos_ref, sin_ref = refs[11:11 + n_rope] if rope else (None, None)
    (mup_ref, mun_ref, w2_ref, w0_ref, a2_ref, a0_ref, g2_ref, kk_ref, ka_ref, eseg_ref) = refs[11 + n_rope:21 + n_rope]
    za_ref, zg_ref = refs[21 + n_rope:23 + n_rope]
    r_o, v_o, kf_o, kb_o, lwf_o, lwb_o, kkn_o, bf_o, bb_o, gate_o = refs[23 + n_rope:]
    mod = mod_ref[0]
    modulated = lambda xv: _rms(xv) * g_ref[...] * (1.0 + mod[1:2]) + mod[0:1]
    h = modulated(x_ref[...])
    hb = h.astype(BF16)
    za = jnp.dot(hb, wa_ref[...], preferred_element_type=F32) + ba_ref[...]
    if rope:
        qk = za[:, :ROPE_COLS]
        lane = lax.broadcasted_iota(jnp.int32, qk.shape, 1)
        low = (lane % 32) < 16
        partner = jnp.where(low, pltpu.roll(qk, ROPE_COLS - 16, 1), pltpu.roll(qk, 16, 1))
        za_ref[:, :ROPE_COLS] = (qk * cos_ref[...] + partner * sin_ref[...]).astype(za_ref.dtype)
        za_ref[:, ROPE_COLS:] = za[:, ROPE_COLS:].astype(za_ref.dtype)
    else:
        za_ref[...] = za.astype(za_ref.dtype)
    zg_ref[...] = (jnp.dot(hb, wg_ref[...], preferred_element_type=F32) + bg_ref[...]).astype(zg_ref.dtype)

    tm = x_ref.shape[0]
    h_ext = jnp.concatenate([modulated(xp_ref[...]), h, modulated(xn_ref[...])], axis=0).astype(BF16)
    z_ext = jnp.dot(h_ext, wr_ref[...], preferred_element_type=F32) + br_ref[...]
    z = z_ext[SUBLANES:SUBLANES + tm]
    ti = pl.program_id(0) % tiles_per_seq
    row = lax.broadcasted_iota(jnp.int32, (tm, 1), 0)
    prev_halo = jnp.where(ti == 0, 0.0, z_ext[SUBLANES - 1:SUBLANES])
    next_halo = jnp.where(ti == tiles_per_seq - 1, 0.0, z_ext[SUBLANES + tm:SUBLANES + tm + 1])
    prev = jnp.where(row == 0, prev_halo, pltpu.roll(z, 1, 0))
    nxt = jnp.where(row == tm - 1, next_halo, pltpu.roll(z, tm - 1, 0))
    zs = z + mup_ref[...] * (prev - z) + mun_ref[...] * (nxt - z)

    d = RWKV_DIM
    r = zs[:, 0:d]
    k = zs[:, d:2 * d]
    v = zs[:, 2 * d:3 * d]
    o = 3 * d
    wl = zs[:, o:o + 2 * DECAY_LORA]
    al = zs[:, o + 2 * DECAY_LORA:o + 2 * DECAY_LORA + 2 * ICL_LORA]
    gl = zs[:, o + 2 * DECAY_LORA + 2 * ICL_LORA:]

    w = w0_ref[...] + _mm(jnp.tanh(wl), w2_ref[...])
    nw = -w
    softplus = jnp.maximum(nw, 0.0) + jnp.log(1.0 + jnp.exp(-jnp.abs(nw)))
    lw = -jnp.exp(-softplus - 0.5)
    icl = _sigmoid(a0_ref[...] + _mm(al, a2_ref[...]))
    kk0 = k * kk_ref[...]
    ss = _seg_sum(kk0 * kk0, eseg_ref[...])
    kk = kk0 / jnp.maximum(jnp.sqrt(ss), 1e-12)
    ka = ka_ref[...]
    icl_f = icl[:, :d]
    icl_b = icl[:, d:]

    r_o[...] = r.astype(r_o.dtype)
    v_o[...] = v.astype(v_o.dtype)
    kf_o[...] = (k * (1.0 + (icl_f - 1.0) * ka)).astype(kf_o.dtype)
    kb_o[...] = (k * (1.0 + (icl_b - 1.0) * ka)).astype(kb_o.dtype)
    lwf_o[...] = lw[:, :d]
    lwb_o[...] = lw[:, d:]
    kkn_o[...] = kk.astype(kkn_o.dtype)
    bf_o[...] = (kk * icl_f).astype(bf_o.dtype)
    bb_o[...] = (kk * icl_b).astype(bb_o.dtype)
    gate_o[...] = _mm(_sigmoid(gl), g2_ref[...]).astype(gate_o.dtype)


def _rope_tables(seq):
    n_rows = seq // GRID_W
    row = jnp.repeat(jnp.arange(n_rows, dtype=F32), GRID_W, total_repeat_length=seq)
    col = jnp.tile(jnp.arange(GRID_W, dtype=F32), n_rows)
    half = HEAD_DIM // 2
    inv_freq = ROPE_BASE ** (-jnp.arange(0, half, 2, dtype=F32) / half)
    ang_r = row[:, None] * inv_freq[None, :]
    ang_c = col[:, None] * inv_freq[None, :]
    cos_h = jnp.concatenate([jnp.cos(ang_r), jnp.cos(ang_r), jnp.cos(ang_c), jnp.cos(ang_c)], axis=1)
    sin_h = jnp.concatenate([-jnp.sin(ang_r), jnp.sin(ang_r), -jnp.sin(ang_c), jnp.sin(ang_c)], axis=1)
    reps = ROPE_COLS // HEAD_DIM
    return jnp.tile(cos_h, (1, reps)), jnp.tile(sin_h, (1, reps))


def _project(x2, mods, mod_row, seq, g_pre, w_att, w_rwkv, w_gate, b_att, b_rwkv, b_gate, feat_params, rope):
    rows, d = x2.shape
    tm = ROW_TILE
    tps = seq // tm
    hb = tm // SUBLANES
    nhb = rows // SUBLANES
    rd = RWKV_DIM
    const = lambda i: (0, 0)
    in_specs = [pl.BlockSpec((tm, d), lambda i: (i, 0)),
                pl.BlockSpec((SUBLANES, d), lambda i: (jnp.maximum(i * hb - 1, 0), 0)),
                pl.BlockSpec((SUBLANES, d), lambda i: (jnp.minimum((i + 1) * hb, nhb - 1), 0)),
                pl.BlockSpec((1, SUBLANES, d), lambda i: (mod_row(i // tps), 0, 0)),
                pl.BlockSpec((1, d), const),
                pl.BlockSpec(w_att.shape, const), pl.BlockSpec(w_rwkv.shape, const),
                pl.BlockSpec(w_gate.shape, const),
                pl.BlockSpec((1, ATT_COLS), const), pl.BlockSpec((1, RWKV_COLS), const),
                pl.BlockSpec((1, w_gate.shape[1]), const)]
    args = [x2, x2, x2, mods, g_pre, w_att, w_rwkv, w_gate, b_att, b_rwkv, b_gate]
    if rope:
        cos, sin = _rope_tables(seq)
        in_specs += [pl.BlockSpec((tm, ROPE_COLS), lambda i: (i % tps, 0))] * 2
        args += [cos, sin]
    in_specs += [pl.BlockSpec(a.shape, const) for a in feat_params]
    args += feat_params
    feat_spec = pl.BlockSpec((tm, rd), lambda i: (i, 0))
    outs = pl.pallas_call(
        functools.partial(_proj_kernel, rope=rope, tiles_per_seq=tps),
        grid=(rows // tm,),
        in_specs=in_specs,
        out_specs=[pl.BlockSpec((tm, ATT_COLS), lambda i: (i, 0)),
                   pl.BlockSpec((tm, w_gate.shape[1]), lambda i: (i, 0))] + [feat_spec] * len(FEAT_NAMES),
        out_shape=[jax.ShapeDtypeStruct((rows, ATT_COLS), BF16),
                   jax.ShapeDtypeStruct((rows, w_gate.shape[1]), BF16)]
                  + [jax.ShapeDtypeStruct((rows, rd), BF16 if name in FEAT_BF16 else F32) for name in FEAT_NAMES],
        compiler_params=_cparams("parallel"),
        name="proj",
    )(*args)
    return outs[0], outs[1], dict(zip(FEAT_NAMES, outs[2:]))


FEAT_NAMES = ('r', 'v', 'kf', 'kb', 'lwf', 'lwb', 'kk', 'bf', 'bb', 'gate')
FEAT_BF16 = ('r', 'v', 'kf', 'kb', 'kk', 'bf', 'bb', 'gate')


def _block_diag2(a, b):
    za = jnp.zeros_like(a)
    zb = jnp.zeros_like(b)
    return jnp.concatenate([jnp.concatenate([a, zb], axis=1), jnp.concatenate([za, b], axis=1)], axis=0)


def _seg_matrix():
    h = np.arange(RWKV_DIM) // HEAD_DIM
    return jnp.asarray((h[:, None] == h[None, :]).astype(np.float32), BF16)


def _feature_params(p):
    d = RWKV_DIM
    return [p['mu_prev'].reshape(1, -1), p['mu_next'].reshape(1, -1),
            _block_diag2(p['w2_f'], p['w2_b']).astype(BF16),
            jnp.concatenate([p['w0_f'], p['w0_b']]).reshape(1, 2 * d),
            _block_diag2(p['a2_f'], p['a2_b']).astype(BF16),
            jnp.concatenate([p['a0_f'], p['a0_b']]).reshape(1, 2 * d),
            p['g2'].astype(BF16), p['k_k'].reshape(1, d), p['k_a'].reshape(1, d), _seg_matrix()]


def _scan_kernel(rf, vf, kf, lwf, kkf, bf, rb, vb, kb, lwb, kkb, bb, s0f_ref, s0b_ref,
                 yf_ref, yb_ref, sTf_ref, sTb_ref, state_ref, *, chunk, n_chunks):
    c = pl.program_id(1)
    C = chunk
    hd = HEAD_DIM
    gw = SCAN_GROUP * hd
    n_groups = RWKV_HEADS // SCAN_GROUP
    assert C == hd, "the triangular masks below are shared between time and channel blocks"

    n_rows = state_ref.shape[0]

    @pl.when(c == 0)
    def _():
        for n in range(n_rows):
            for d, s0_ref in enumerate((s0f_ref, s0b_ref)):
                for g in range(n_groups):
                    state_ref[n, d, g] = jnp.zeros((gw, gw), F32)
                    for j in range(SCAN_GROUP):
                        state_ref[n, d, g, j * hd:(j + 1) * hd, j * hd:(j + 1) * hd] = (
                            s0_ref[n, g * SCAN_GROUP + j])

    ti = lax.broadcasted_iota(jnp.int32, (C, C), 0)
    si = lax.broadcasted_iota(jnp.int32, (C, C), 1)
    tg = lax.broadcasted_iota(jnp.int32, (C, gw), 0)
    sg = lax.broadcasted_iota(jnp.int32, (C, gw), 1) % C
    eye = (tg == sg).astype(F32)
    same_head = (lax.broadcasted_iota(jnp.int32, (gw, gw), 0) // hd
                 == lax.broadcasted_iota(jnp.int32, (gw, gw), 1) // hd)
    same_head_b = same_head.astype(BF16)
    n_double = int(np.log2(C)) - 1

    def bdiag(x_cat):
        return jnp.concatenate([x_cat.astype(BF16)] * SCAN_GROUP, axis=0) * same_head_b

    dirs = ((rf, vf, kf, lwf, kkf, bf), (rb, vb, kb, lwb, kkb, bb))
    units = []
    masks = [(((si <= ti) if d == 0 else (si >= ti)).astype(BF16),
              (sg <= tg) if d == 0 else (sg >= tg),
              (sg < tg) if d == 0 else (sg > tg)) for d in range(2)]
    for n, d in [(n, d) for n in range(n_rows) for d in range(2)]:
        r_ref, v_ref, k_ref, lw_ref, kk_ref, b_ref = (ref.at[n] for ref in dirs[d])
        tri, incl_g, strict_g = masks[d]
        lw = lw_ref[...]
        lh, lm, ll = _split3(lw)
        dd = functools.partial(jnp.dot, preferred_element_type=F32)
        cum = dd(tri, lh) + dd(tri, lm) + dd(tri, ll)
        cumx = cum - lw
        cum_end = cum[C - 1:C] if d == 0 else cum[0:1]
        e_neg = jnp.exp(-cum)
        e_end = jnp.exp(cum_end - cum)
        bv = b_ref[...].astype(F32)
        kv = k_ref[...].astype(F32)
        vv = v_ref[...].astype(F32)
        a_t = (-kk_ref[...].astype(F32) * jnp.exp(cumx)).astype(BF16)
        r_t = (r_ref[...].astype(F32) * jnp.exp(cum)).astype(BF16)
        b_t = (bv * e_neg).astype(BF16)
        k_t = (kv * e_neg).astype(BF16)
        b_q = (bv * e_end).astype(BF16)
        k_q = (kv * e_end).astype(BF16)
        g_end = jnp.exp(cum_end)
        for g in range(n_groups):
            sl = slice(g * gw, (g + 1) * gw)
            units.append(dict(
                n=n, d=d, g=g, sl=sl, incl=incl_g, strict=strict_g,
                P=jnp.concatenate([a_t[:, sl], r_t[:, sl]], axis=0),
                Q=jnp.concatenate([bdiag(b_t[:, sl]), bdiag(k_t[:, sl])], axis=0),
                Qq=jnp.concatenate([b_q[:, sl], k_q[:, sl]], axis=0),
                V=vv[:, sl], g_end=g_end[:, sl]))

    for u in units:
        u['G'] = _mm_nt(u['P'], u['Q'])
    for u in units:
        u['S0'] = state_ref[u['n'], u['d'], u['g']]
        u['PH'] = _mm_nt(u['P'], u['S0'])
        u['Vd'] = bdiag(u['V'])
    for u in units:
        G = u.pop('G')
        u['N'] = jnp.where(u['strict'], G[:C, :gw], 0.0)
        u['a_ak'] = jnp.where(u['strict'], G[:C, gw:], 0.0)
        u['a_rb'] = jnp.where(u['incl'], G[C:, :gw], 0.0)
        u['a_rk'] = jnp.where(u['incl'], G[C:, gw:], 0.0)
    for u in units:
        u['T'] = eye + u['N']
        u['Pw'] = _mm(u['N'], bdiag(u['N']))
        u['rhs'] = u['PH'][:C] + _mm(u['a_ak'], u['Vd'])
    for lvl in range(n_double):
        for u in units:
            pw = bdiag(u['Pw'])
            u['T'] = u['T'] + _mm(u['T'], pw)
            if lvl + 1 < n_double:
                u['Pw'] = _mm(u['Pw'], pw)
    for u in units:
        u['U'] = _mm(u['T'], bdiag(u['rhs']))
    y_refs = (yf_ref, yb_ref)
    for u in units:
        y_refs[u['d']][u['n'], :, u['sl']] = (u['PH'][C:] + _mm(u['a_rb'], bdiag(u['U']))
                                              + _mm(u['a_rk'], u['Vd']))
        uv = jnp.concatenate([u['U'], u['V']], axis=0)
        state_ref[u['n'], u['d'], u['g']] = jnp.where(
            same_head, u['S0'] * u['g_end'] + _mm(uv.T, u['Qq']), 0.0)

    @pl.when(c == n_chunks - 1)
    def _():
        for n in range(n_rows):
            for d, sT_ref in enumerate((sTf_ref, sTb_ref)):
                for g in range(n_groups):
                    for j in range(SCAN_GROUP):
                        sT_ref[n, g * SCAN_GROUP + j] = (
                            state_ref[n, d, g, j * hd:(j + 1) * hd, j * hd:(j + 1) * hd])


def _scan(f, batch, seq, s0_f, s0_b):
    C = SCAN_CHUNK
    nC = seq // C
    d = RWKV_DIM
    nb = max(n for n in range(1, SCAN_BATCH + 1) if batch % n == 0)
    fwd = pl.BlockSpec((nb, C, d), lambda b, c: (b, c, 0))
    bwd = pl.BlockSpec((nb, C, d), lambda b, c: (b, nC - 1 - c, 0))
    st = pl.BlockSpec((nb, RWKV_HEADS, HEAD_DIM, HEAD_DIM), lambda b, c: (b, 0, 0, 0))
    st_shape = jax.ShapeDtypeStruct((batch, RWKV_HEADS, HEAD_DIM, HEAD_DIM), F32)
    y_shape = jax.ShapeDtypeStruct((batch, seq, d), F32)
    gw = SCAN_GROUP * HEAD_DIM
    arr = lambda name: f[name].reshape(batch, seq, d)
    y_f, y_b, sT_f, sT_b = pl.pallas_call(
        functools.partial(_scan_kernel, chunk=C, n_chunks=nC),
        grid=(batch // nb, nC),
        in_specs=[fwd] * 6 + [bwd] * 6 + [st, st],
        out_specs=[fwd, bwd, st, st],
        out_shape=[y_shape, y_shape, st_shape, st_shape],
        scratch_shapes=[pltpu.VMEM((nb, 2, RWKV_HEADS // SCAN_GROUP, gw, gw), F32)],
        compiler_params=_cparams("parallel", "arbitrary"),
        name="scan",
    )(arr('r'), arr('v'), arr('kf'), arr('lwf'), arr('kk'), arr('bf'),
      arr('r'), arr('v'), arr('kb'), arr('lwb'), arr('kk'), arr('bb'), s0_f, s0_b)
    return y_f.reshape(batch * seq, d), y_b.reshape(batch * seq, d), sT_f, sT_b


def _attn_kernel(sink_ref, q_ref, kp_ref, km_ref, kn_ref, vp_ref, vm_ref, vn_ref, kc_ref, vc_ref, o_ref,
                 *, n_blocks):
    n = pl.program_id(1)
    rows = ATT_GROUPS * BLOCK
    q = q_ref[...] * ATT_SCALE
    qi = lax.broadcasted_iota(jnp.int32, (rows, BLOCK), 0) % BLOCK
    kj = lax.broadcasted_iota(jnp.int32, (rows, BLOCK), 1)
    ok_prev = (kj >= qi) & (n > 0)
    ok_next = (kj <= qi) & (n < n_blocks - 1)
    rowh = lax.broadcasted_iota(jnp.int32, (rows, 1), 0) // BLOCK
    outs = [None] * ATT_HEADS
    for g in range(ATT_KV_HEADS):
        ks = slice(g * HEAD_DIM, (g + 1) * HEAD_DIM)
        heads = [g * ATT_GROUPS + j for j in range(ATT_GROUPS)]
        Qs = jnp.concatenate([q[:, h * HEAD_DIM:(h + 1) * HEAD_DIM] for h in heads], axis=0)
        scores = [jnp.where(ok_prev, _mm_nt(Qs, kp_ref[:, ks]), NEG_INF),
                  _mm_nt(Qs, km_ref[:, ks]),
                  jnp.where(ok_next, _mm_nt(Qs, kn_ref[:, ks]), NEG_INF),
                  _mm_nt(Qs, kc_ref[:, ks])]
        values = [vp_ref, vm_ref, vn_ref, vc_ref]
        sink = jnp.zeros((rows, 1), F32)
        for j, h in enumerate(heads):
            sink = jnp.where(rowh == j, sink_ref[h], sink)
        folded = None
        for sc in scores:
            for c0 in range(0, sc.shape[1], BLOCK):
                blk = sc[:, c0:c0 + BLOCK]
                folded = blk if folded is None else jnp.maximum(folded, blk)
        m = jnp.maximum(sink, jnp.max(folded, axis=-1, keepdims=True))
        acc = jnp.zeros((rows, 2 * HEAD_DIM), F32)
        for sc, v_ref in zip(scores, values):
            one_col = (lax.broadcasted_iota(jnp.int32, (v_ref.shape[0], HEAD_DIM), 1) == 0).astype(BF16)
            v_ext = jnp.concatenate([v_ref[:, ks], one_col], axis=1)
            acc = acc + jnp.dot(jnp.exp((sc - m).astype(BF16)), v_ext, preferred_element_type=F32)
        den = acc[:, HEAD_DIM:HEAD_DIM + 1] + jnp.exp(sink - m)
        O = acc[:, :HEAD_DIM] / den
        for j, h in enumerate(heads):
            outs[h] = O[j * BLOCK:(j + 1) * BLOCK]
    o_ref[...] = jnp.concatenate(outs, axis=1).astype(o_ref.dtype)


def _attention(z_att, zc_att, sinks, batch, seq):
    nb = seq // BLOCK
    ctx_len = zc_att.shape[0] // batch
    kcol = ATT_DIM // KV_DIM
    vcol = kcol + 1

    def kv_spec(col, off):
        return pl.BlockSpec((BLOCK, KV_DIM), lambda b, n: (b * nb + jnp.clip(n + off, 0, nb - 1), col))

    return pl.pallas_call(
        functools.partial(_attn_kernel, n_blocks=nb),
        grid=(batch, nb),
        in_specs=[pl.BlockSpec(memory_space=pltpu.SMEM),
                  pl.BlockSpec((BLOCK, ATT_DIM), lambda b, n: (b * nb + n, 0)),
                  kv_spec(kcol, -1), kv_spec(kcol, 0), kv_spec(kcol, 1),
                  kv_spec(vcol, -1), kv_spec(vcol, 0), kv_spec(vcol, 1),
                  pl.BlockSpec((ctx_len, KV_DIM), lambda b, n: (b, kcol)),
                  pl.BlockSpec((ctx_len, KV_DIM), lambda b, n: (b, vcol))],
        out_specs=pl.BlockSpec((BLOCK, ATT_DIM), lambda b, n: (b * nb + n, 0)),
        out_shape=jax.ShapeDtypeStruct((batch * seq, ATT_DIM), BF16),
        compiler_params=_cparams("parallel", "parallel"),
        name="attn",
    )(sinks, z_att, z_att, z_att, z_att, z_att, z_att, z_att, zc_att, zc_att)


def _merge_kernel(yf_ref, yb_ref, r_ref, kf_ref, kb_ref, v_ref, gate_ref, att_ref, zg_ref, x_ref, mod_ref,
                  lnw_ref, lnb_ref, rk_ref, eseg_ref, wua_ref, wur_ref, wo_ref, gpm_ref, gpf_ref,
                  wr_ref, br_ref,
                  x1_ref, hf_ref, idx_ref, gt_ref, rank_ref, cnt_ref):
    eseg = eseg_ref[...]
    inv_n = 1.0 / HEAD_DIM
    y = yf_ref[...] + yb_ref[...]
    mean = _seg_sum(y, eseg) * inv_n
    dy = y - mean
    var = _seg_sum(dy * dy, eseg) * inv_n
    yn = dy * lax.rsqrt(var + GN_EPS) * lnw_ref[...] + lnb_ref[...]
    f32 = lambda ref: ref[...].astype(F32)
    bonus = _seg_sum(f32(r_ref) * (f32(kf_ref) + f32(kb_ref)) * rk_ref[...], eseg) * f32(v_ref)
    rwk = (yn + bonus) * f32(gate_ref)

    d = x_ref.shape[1]
    zg = zg_ref[...].astype(F32)
    merged = (_sigmoid(zg[:, :d]) * jnp.dot(att_ref[...], wua_ref[...], preferred_element_type=F32)
              + _sigmoid(zg[:, d:]) * _mm(rwk, wur_ref[...]))
    mix = _mm(merged, wo_ref[...])
    mod = mod_ref[0]
    x1 = x_ref[...] + mod[2:3] * (_rms(mix) * gpm_ref[...])
    x1_ref[...] = x1
    hf = _rms(x1) * gpf_ref[...] * (1.0 + mod[4:5]) + mod[3:4]
    hf_ref[...] = hf.astype(hf_ref.dtype)

    logits = _mm_f32(hf, wr_ref[...]) + br_ref[...]
    lane = lax.broadcasted_iota(jnp.int32, logits.shape, 1).astype(F32)
    idx_out = jnp.zeros(logits.shape, F32)
    val_out = jnp.zeros(logits.shape, F32)
    picked = jnp.zeros(logits.shape, F32)
    sels = []
    top = None
    den = None
    for kth in range(TOP_K):
        m = jnp.max(logits, axis=-1, keepdims=True)
        sel = jnp.min(jnp.where(logits == m, lane, float(LANES)), axis=-1, keepdims=True)
        hit = lane == sel
        logits = jnp.where(hit, -jnp.inf, logits)
        picked = jnp.where(hit, 1.0, picked)
        sels.append(hit)
        if kth == 0:
            top = m
        e = jnp.exp(m - top)
        den = e if kth == 0 else den + e
        idx_out = jnp.where(lane == float(kth), sel, idx_out)
        val_out = jnp.where(lane == float(kth), e, val_out)
    idx_ref[...] = idx_out.astype(jnp.int32)
    gt_ref[...] = val_out / den

    tm = picked.shape[0]
    earlier = (lax.broadcasted_iota(jnp.int32, (tm, tm), 1)
               < lax.broadcasted_iota(jnp.int32, (tm, tm), 0)).astype(BF16)
    before = jnp.dot(earlier, picked.astype(BF16), preferred_element_type=F32)
    rank_out = jnp.zeros(logits.shape, F32)
    for kth, hit in enumerate(sels):
        rk = jnp.sum(jnp.where(hit, before, 0.0), axis=-1, keepdims=True)
        rank_out = jnp.where(lane == float(kth), rk, rank_out)
    rank_ref[...] = rank_out.astype(jnp.int32)
    cnt_ref[0] = jnp.broadcast_to(jnp.sum(picked, axis=0, keepdims=True), cnt_ref.shape[1:]).astype(jnp.int32)

def _moe_blocks(n_tok, n_tiles):
    return -(-(n_tok * TOP_K + (SUBLANES - 1) * N_EXPERTS * n_tiles) // MOE_ROWS) + N_EXPERTS


def _merge(y_f, y_b, f, att, z_gate, x2, mods, seq, p):
    rows, d = x2.shape
    tm = ROW_TILE
    tps = seq // tm
    rd = RWKV_DIM
    const = lambda i: (0, 0)
    row = lambda w: pl.BlockSpec((tm, w), lambda i: (i, 0))
    vec = lambda w: pl.BlockSpec((1, w), const)
    w_router = jnp.pad(p['w_router'], ((0, 0), (0, LANES - N_EXPERTS)))
    b_router = jnp.pad(p['b_router'], (0, LANES - N_EXPERTS), constant_values=NEG_INF).reshape(1, LANES)
    return pl.pallas_call(
        _merge_kernel,
        grid=(rows // tm,),
        in_specs=[row(rd)] * 7 + [row(ATT_DIM), row(2 * d), row(d),
                  pl.BlockSpec((1, SUBLANES, d), lambda i: (i // tps, 0, 0)),
                  vec(rd), vec(rd), vec(rd), pl.BlockSpec((rd, rd), const),
                  pl.BlockSpec((ATT_DIM, d), const), pl.BlockSpec((rd, d), const), pl.BlockSpec((d, d), const),
                  vec(d), vec(d), pl.BlockSpec((d, LANES), const), vec(LANES)],
        out_specs=[row(d), row(d), row(LANES), row(LANES), row(LANES),
                   pl.BlockSpec((1, SUBLANES, LANES), lambda i: (i, 0, 0))],
        out_shape=[jax.ShapeDtypeStruct((rows, d), F32), jax.ShapeDtypeStruct((rows, d), BF16),
                   jax.ShapeDtypeStruct((rows, LANES), jnp.int32), jax.ShapeDtypeStruct((rows, LANES), F32),
                   jax.ShapeDtypeStruct((rows, LANES), jnp.int32),
                   jax.ShapeDtypeStruct((rows // tm, SUBLANES, LANES), jnp.int32)],
        compiler_params=_cparams("parallel"),
        name="merge",
    )(y_f, y_b, f['r'], f['kf'], f['kb'], f['v'], f['gate'], att, z_gate, x2, mods,
      p['ln_x_w'].reshape(1, rd), p['ln_x_b'].reshape(1, rd), p['r_k'].reshape(1, rd), _seg_matrix(),
      p['w_up_att'].astype(BF16), p['w_up_rwkv'].astype(BF16), p['w_out'].astype(BF16),
      p['g_post_mix'].reshape(1, d), p['g_pre_ffn'].reshape(1, d), w_router, b_router)


def _moe_kernel(be_ref, nused_ref, x_ref, wgu_ref, bgu_ref, wdn_f32_ref, bdn_ref, o_ref, wdn_ref):
    i = pl.program_id(0)

    @pl.when((i == 0) | (be_ref[i] != be_ref[jnp.maximum(i - 1, 0)]))
    def _():
        wdn_ref[...] = wdn_f32_ref[...].astype(BF16)

    @pl.when(i < nused_ref[0])
    def _():
        de = wdn_ref.shape[0]
        d = x_ref.shape[1]
        x = x_ref[...]
        bgu = bgu_ref[...]
        gate = jnp.minimum(_mm_nt(x, wgu_ref[:, :d]) + bgu[:, :de], SWIGLU_LIMIT)
        up = jnp.clip(_mm_nt(x, wgu_ref[:, d:]) + bgu[:, de:], -SWIGLU_LIMIT, SWIGLU_LIMIT)
        act = (up + 1.0) * (gate * _sigmoid(SWIGLU_ALPHA * gate))
        o_ref[...] = _mm(act, wdn_ref[...]) + bdn_ref[...]

    @pl.when(i >= nused_ref[0])
    def _():
        o_ref[...] = jnp.zeros(o_ref.shape, o_ref.dtype)


def _wprep_kernel(w_ref, o_ref, t_ref):
    d = w_ref.shape[1]
    half = o_ref.shape[1]
    wt = w_ref[0].T
    for c in range(d // LANES):
        cols = slice(c * LANES, (c + 1) * LANES)
        t_ref[c] = wt[:, cols]
        o_ref[0, :, cols] = t_ref[c, pl.ds(0, half, stride=2), :].astype(o_ref.dtype)
        o_ref[0, :, d + c * LANES:d + (c + 1) * LANES] = t_ref[c, pl.ds(1, half, stride=2), :].astype(o_ref.dtype)


def _gate_up_rows(w_gu):
    n_exp, d, de2 = w_gu.shape
    tc = 4 * LANES
    return pl.pallas_call(
        _wprep_kernel,
        grid=(n_exp, de2 // tc),
        in_specs=[pl.BlockSpec((1, d, tc), lambda e, j: (e, 0, j))],
        out_specs=pl.BlockSpec((1, tc // 2, 2 * d), lambda e, j: (e, j, 0)),
        out_shape=jax.ShapeDtypeStruct((n_exp, de2 // 2, 2 * d), BF16),
        scratch_shapes=[pltpu.VMEM((d // LANES, tc, LANES), F32)],
        compiler_params=_cparams("parallel", "parallel"),
        name="wprep",
    )(w_gu)


def _experts(xb, block_exp, n_used, w_gu, b_gu, w_dn, b_dn):
    n_slots, d = xb.shape
    bm = MOE_ROWS
    n_blocks = n_slots // bm
    de = w_dn.shape[1]
    de2 = 2 * de
    grid_spec = pltpu.PrefetchScalarGridSpec(
        num_scalar_prefetch=2,
        grid=(n_blocks,),
        in_specs=[pl.BlockSpec((bm, d), lambda i, be, nu: (jnp.minimum(i, nu[0] - 1), 0)),
                  pl.BlockSpec((None, de, 2 * d), lambda i, be, nu: (be[i], 0, 0)),
                  pl.BlockSpec((None, 1, de2), lambda i, be, nu: (be[i], 0, 0)),
                  pl.BlockSpec((None, de, d), lambda i, be, nu: (be[i], 0, 0)),
                  pl.BlockSpec((None, 1, d), lambda i, be, nu: (be[i], 0, 0))],
        out_specs=pl.BlockSpec((bm, d), lambda i, be, nu: (i, 0)),
        scratch_shapes=[pltpu.VMEM((de, d), BF16)],
    )
    return pl.pallas_call(
        _moe_kernel,
        grid_spec=grid_spec,
        out_shape=jax.ShapeDtypeStruct((n_slots, d), F32),
        compiler_params=_cparams("arbitrary"),
        name="moe",
    )(block_exp, n_used, xb, w_gu, b_gu, w_dn, b_dn)


def _local_rows(tm):
    rows = tm * TOP_K + (SUBLANES - 1) * N_EXPERTS
    assert rows % SUBLANES == 0
    return rows


def _pick_matrix(loc, weights, n_cols):
    col = lax.broadcasted_iota(jnp.int32, (loc.shape[0], n_cols), 1)
    out = jnp.zeros(col.shape, F32)
    for k in range(TOP_K):
        out = jnp.where(col == loc[:, k:k + 1], weights[k], out)
    return out


def _segment(seg_ref, tile, e):
    base = (tile * N_EXPERTS + e) * 3
    return tuple(pl.multiple_of(seg_ref[base + i], SUBLANES) for i in range(3))


def _dispatch_kernel(seg_ref, nch_ref, tail_ref, hf_ref, loc_ref, xb_ref, xc_ref, zero_ref, sem, zsem):
    j = pl.program_id(0)
    n_tiles = pl.num_programs(0)
    slot = j % 2

    def tail_copy(e):
        row = pl.multiple_of(tail_ref[2 * e], SUBLANES)
        n = pl.multiple_of(tail_ref[2 * e + 1], SUBLANES)
        return n, pltpu.make_async_copy(zero_ref.at[pl.ds(0, n), :], xb_ref.at[pl.ds(row, n), :], zsem)

    def spare_block_copy(b):
        rows = xb_ref.at[pl.ds(pl.multiple_of(b * MOE_ROWS, MOE_ROWS), MOE_ROWS), :]
        return pltpu.make_async_copy(zero_ref, rows, zsem)

    first_spare = tail_ref[2 * N_EXPERTS]
    n_blocks = xb_ref.shape[0] // MOE_ROWS

    @pl.when(j == 0)
    def _():
        zero_ref[...] = jnp.zeros(zero_ref.shape, zero_ref.dtype)

        def body(e, carry):
            n, cp = tail_copy(e)

            @pl.when(n > 0)
            def _():
                cp.start()
            return carry
        lax.fori_loop(0, N_EXPERTS, body, 0)

        def spare(b, carry):
            spare_block_copy(b).start()
            return carry
        lax.fori_loop(first_spare, n_blocks, spare, 0)

    def drain(tile, s):
        rows = pl.multiple_of(nch_ref[tile] * SUBLANES, SUBLANES)
        done = xc_ref.at[s, pl.ds(0, rows), :]
        pltpu.make_async_copy(done, done, sem.at[s]).wait()

    sel = _pick_matrix(loc_ref[...], [1.0] * TOP_K, xc_ref.shape[1]).astype(BF16)
    xc_ref[slot] = lax.dot_general(sel, hf_ref[...], (((0,), (0,)), ((), ())), preferred_element_type=F32)

    def start(e, carry):
        off, row, n = _segment(seg_ref, j, e)

        @pl.when(n > 0)
        def _():
            pltpu.make_async_copy(xc_ref.at[slot, pl.ds(off, n), :], xb_ref.at[pl.ds(row, n), :],
                                  sem.at[slot]).start()
        return carry
    lax.fori_loop(0, N_EXPERTS, start, 0)

    @pl.when(j > 0)
    def _():
        drain(j - 1, 1 - slot)

    @pl.when(j == n_tiles - 1)
    def _():
        drain(j, slot)

    @pl.when(j == 0)
    def _():
        def body(e, carry):
            n, cp = tail_copy(e)

            @pl.when(n > 0)
            def _():
                cp.wait()
            return carry
        lax.fori_loop(0, N_EXPERTS, body, 0)

        def spare(b, carry):
            spare_block_copy(b).wait()
            return carry
        lax.fori_loop(first_spare, n_blocks, spare, 0)


def _dispatch(hf, loc, segments, nch, tails):
    rows, d = hf.shape
    tm = ROW_TILE
    n_slots = _moe_blocks(rows, rows // tm) * MOE_ROWS
    grid_spec = pltpu.PrefetchScalarGridSpec(
        num_scalar_prefetch=3,
        grid=(rows // tm,),
        in_specs=[pl.BlockSpec((tm, d), lambda i, sg, nc, tl: (i, 0)),
                  pl.BlockSpec((tm, LANES), lambda i, sg, nc, tl: (i, 0))],
        out_specs=pl.BlockSpec(memory_space=pl.ANY),
        scratch_shapes=[pltpu.VMEM((2, _local_rows(tm), d), F32), pltpu.VMEM((MOE_ROWS, d), F32),
                        pltpu.SemaphoreType.DMA((2,)), pltpu.SemaphoreType.DMA(())],
    )
    return pl.pallas_call(
        _dispatch_kernel,
        grid_spec=grid_spec,
        out_shape=jax.ShapeDtypeStruct((n_slots, d), F32),
        compiler_params=_cparams("arbitrary"),
        name="dispatch",
    )(segments, nch, tails, hf, loc)


def _combine_kernel(seg_ref, nch_ref, yb_ref, loc_ref, gt_ref, x1_ref, mod_ref, g_ref, o_ref, yc_ref, sem):
    j = pl.program_id(0)
    n_tiles = pl.num_programs(0)

    def issue(tile, s):
        def body(e, carry):
            off, row, n = _segment(seg_ref, tile, e)

            @pl.when(n > 0)
            def _():
                pltpu.make_async_copy(yb_ref.at[pl.ds(row, n), :], yc_ref.at[s, pl.ds(off, n), :],
                                      sem.at[s]).start()
            return carry
        lax.fori_loop(0, N_EXPERTS, body, 0)

    @pl.when(j == 0)
    def _():
        yc_ref[...] = jnp.zeros(yc_ref.shape, yc_ref.dtype)
        issue(0, 0)

    @pl.when(j + 1 < n_tiles)
    def _():
        issue(j + 1, (j + 1) % 2)

    slot = j % 2

    rows = pl.multiple_of(nch_ref[j] * SUBLANES, SUBLANES)
    done = yc_ref.at[slot, pl.ds(0, rows), :]
    pltpu.make_async_copy(done, done, sem.at[slot]).wait()

    gt = gt_ref[...]
    w = _pick_matrix(loc_ref[...], [gt[:, k:k + 1] for k in range(TOP_K)], yc_ref.shape[1])
    y = _mm(w, yc_ref[slot])
    mod = mod_ref[0]
    o_ref[...] = x1_ref[...] + mod[5:6] * (_rms(y) * g_ref[...])


def _combine(yb, loc, segments, nch, gt, x1, mods, seq, g_post):
    rows, d = x1.shape
    tm = ROW_TILE
    tps = seq // tm
    row = lambda w: pl.BlockSpec((tm, w), lambda i, gd, nc: (i, 0))
    grid_spec = pltpu.PrefetchScalarGridSpec(
        num_scalar_prefetch=2,
        grid=(rows // tm,),
        in_specs=[pl.BlockSpec(memory_space=pl.ANY), row(LANES), row(LANES), row(d),
                  pl.BlockSpec((1, SUBLANES, d), lambda i, gd, nc: (i // tps, 0, 0)),
                  pl.BlockSpec((1, d), lambda i, gd, nc: (0, 0))],
        out_specs=row(d),
        scratch_shapes=[pltpu.VMEM((2, _local_rows(tm), d), F32), pltpu.SemaphoreType.DMA((2,))],
    )
    return pl.pallas_call(
        _combine_kernel,
        grid_spec=grid_spec,
        out_shape=jax.ShapeDtypeStruct((rows, d), F32),
        compiler_params=_cparams("arbitrary"),
        name="combine",
    )(segments, nch, yb, loc, gt, x1, mods, g_post.reshape(1, d))


def _route_slots(top_idx, rank, tile_counts, n_tok):
    bm = MOE_ROWS
    n_tiles = tile_counts.shape[0]
    tm = n_tok // n_tiles
    seg_len = -(-tile_counts // SUBLANES) * SUBLANES
    seg_off = jnp.cumsum(seg_len, axis=1) - seg_len
    per_expert = jnp.sum(seg_len, axis=0)
    padded = -(-per_expert // bm) * bm
    p_end = jnp.cumsum(padded)
    seg_slot = (p_end - padded)[None, :] + jnp.cumsum(seg_len, axis=0) - seg_len
    hit = top_idx[:, :, None] == jnp.arange(N_EXPERTS, dtype=jnp.int32)[None, None, :]
    loc = rank + jnp.sum(jnp.where(hit, jnp.repeat(seg_off, tm, axis=0)[:, None, :], 0), axis=-1)
    segments = jnp.stack([seg_off, seg_slot, seg_len], axis=-1).reshape(-1).astype(jnp.int32)
    tails = jnp.stack([p_end - padded + per_expert, padded - per_expert], axis=-1).reshape(-1)
    tails = jnp.concatenate([tails, p_end[-1:] // bm]).astype(jnp.int32)
    n_chunks = jnp.sum(seg_len, axis=1) // SUBLANES
    starts = jnp.arange(_moe_blocks(n_tok, n_tiles), dtype=jnp.int32) * bm
    block_exp = jnp.minimum(jnp.sum(p_end[None, :] <= starts[:, None], axis=1), N_EXPERTS - 1).astype(jnp.int32)
    n_used = (p_end[-1] // bm).astype(jnp.int32).reshape(1)
    loc = jnp.pad(loc.astype(jnp.int32), ((0, 0), (0, LANES - TOP_K)))
    return loc, segments, n_chunks.astype(jnp.int32), tails, block_exp, n_used


def _moe(hf, loc, segments, n_chunks, tails, block_exp, n_used, p):
    xb = _dispatch(hf, loc, segments, n_chunks, tails)
    w_gu = _gate_up_rows(p['w_gate_up'])
    b_gu = p['b_gate_up']
    b_gu = jnp.concatenate([b_gu[..., 0::2], b_gu[..., 1::2]], axis=-1)[:, None, :]
    return _experts(xb, block_exp, n_used, w_gu, b_gu, p['w_down'], p['b_down'][:, None, :])


def _layer(x, c, ctx, c_ctx, p):
    batch, seq, d = x.shape
    ctx_len = ctx.shape[1]
    n_mod = p['w_ada'].shape[1] // d

    c_rows = jnp.zeros((2 * SUBLANES, d), F32).at[:batch].set(c).at[batch].set(c_ctx)
    mods = _ada(c_rows, p['w_ada'], p['b_ada']).reshape(2 * SUBLANES, n_mod, d)
    mods = jnp.pad(mods, ((0, 0), (0, SUBLANES - n_mod), (0, 0)))

    w_in = p['w_in'].astype(BF16)
    w_att, w_rwkv, w_gate = (w_in[:, :ATT_COLS], w_in[:, ATT_COLS:ATT_COLS + RWKV_COLS],
                             w_in[:, ATT_COLS + RWKV_COLS:])
    b_in = p['b_in'].reshape(1, -1)
    b_att, b_rwkv, b_gate = (b_in[:, :ATT_COLS], b_in[:, ATT_COLS:ATT_COLS + RWKV_COLS],
                             b_in[:, ATT_COLS + RWKV_COLS:])
    g_pre = p['g_pre_mix'].reshape(1, d)
    proj = functools.partial(_project, g_pre=g_pre, w_att=w_att, w_rwkv=w_rwkv, w_gate=w_gate,
                             b_att=b_att, b_rwkv=b_rwkv, b_gate=b_gate, feat_params=_feature_params(p))

    zc_att, _, fc = proj(ctx.reshape(batch * ctx_len, d), mods, lambda b: batch, ctx_len, rope=False)
    zero_state = jnp.zeros((batch, RWKV_HEADS, HEAD_DIM, HEAD_DIM), F32)
    _, _, sc_f, sc_b = _scan(fc, batch, ctx_len, zero_state, zero_state)

    x2 = x.reshape(batch * seq, d)
    z_att, z_gate, fx = proj(x2, mods, lambda b: b, seq, rope=True)
    y_f, y_b, _, _ = _scan(fx, batch, seq, sc_f, sc_b)
    att = _attention(z_att, zc_att, p['att_sinks'], batch, seq)
    x1, hf, idx, gt, rank, cnt = _merge(y_f, y_b, fx, att, z_gate, x2, mods, seq, p)

    loc, segments, n_chunks, tails, block_exp, n_used = _route_slots(
        idx[:, :TOP_K], rank[:, :TOP_K], cnt[:, 0, :N_EXPERTS], batch * seq)
    yb = _moe(hf, loc, segments, n_chunks, tails, block_exp, n_used, p)
    out = _combine(yb, loc, segments, n_chunks, gt, x1, mods, seq, p['g_post_ffn'])
    return out.reshape(batch, seq, d)


def kernel(x, c, ctx, c_ctx, w_ada, b_ada, g_pre_mix, g_post_mix, g_pre_ffn, g_post_ffn, w_in, b_in, mu_prev, mu_next, att_sinks, w0_f, w0_b, w2_f, w2_b, a0_f, a0_b, a2_f, a2_b, g2, k_k, k_a, r_k, ln_x_w, ln_x_b, w_up_att, w_up_rwkv, w_out, w_router, b_router, w_gate_up, b_gate_up, w_down, b_down):
    assert w_ada.shape[0] == 1, "single-layer problem: the context stream update is never consumed"
    p = dict(w_ada=w_ada[0], b_ada=b_ada[0], g_pre_mix=g_pre_mix[0], g_post_mix=g_post_mix[0],
             g_pre_ffn=g_pre_ffn[0], g_post_ffn=g_post_ffn[0], w_in=w_in[0], b_in=b_in[0],
             mu_prev=mu_prev[0], mu_next=mu_next[0], att_sinks=att_sinks[0], w0_f=w0_f[0], w0_b=w0_b[0],
             w2_f=w2_f[0], w2_b=w2_b[0], a0_f=a0_f[0], a0_b=a0_b[0], a2_f=a2_f[0], a2_b=a2_b[0], g2=g2[0],
             k_k=k_k[0], k_a=k_a[0], r_k=r_k[0].reshape(-1), ln_x_w=ln_x_w[0], ln_x_b=ln_x_b[0],
             w_up_att=w_up_att[0], w_up_rwkv=w_up_rwkv[0], w_out=w_out[0], w_router=w_router[0],
             b_router=b_router[0], w_gate_up=w_gate_up[0], b_gate_up=b_gate_up[0], w_down=w_down[0],
             b_down=b_down[0])
    return _layer(x, c, ctx, c_ctx, p)
```

```python
import functools

import jax
import jax.numpy as jnp
import numpy as np
from jax import lax
from jax.experimental import pallas as pl
from jax.experimental.pallas import tpu as pltpu

F32 = jnp.float32
BF16 = jnp.bfloat16

GRID_W = 64
HEAD_DIM = 64
ATT_HEADS = 8
ATT_KV_HEADS = 2
ATT_GROUPS = ATT_HEADS // ATT_KV_HEADS
ATT_DIM = ATT_HEADS * HEAD_DIM
KV_DIM = ATT_KV_HEADS * HEAD_DIM
WINDOW = 128
BLOCK = 128
ROPE_BASE = 10000.0
ATT_SCALE = HEAD_DIM ** -0.5
NEG_INF = -1e30
RWKV_HEADS = 8
RWKV_DIM = RWKV_HEADS * HEAD_DIM
DECAY_LORA = 64
ICL_LORA = 64
GATE_LORA = 128
GN_EPS = 64e-5
N_EXPERTS = 32
TOP_K = 4
SWIGLU_LIMIT = 7.0
SWIGLU_ALPHA = 1.702
RMS_EPS = 1e-6
ATT_COLS = ATT_DIM + 2 * KV_DIM
RWKV_COLS = 3 * RWKV_DIM + 2 * DECAY_LORA + 2 * ICL_LORA + GATE_LORA
ROPE_COLS = ATT_DIM + KV_DIM

LANES = 128
SUBLANES = 8
VMEM_LIMIT = 48 * 1024 * 1024

ROW_TILE = 256
SCAN_CHUNK = 64
SCAN_GROUP = 4
SCAN_BATCH = 4
MOE_ROWS = 512


def _cparams(*sem):
    return pltpu.CompilerParams(dimension_semantics=sem, vmem_limit_bytes=VMEM_LIMIT)


def _mm(a, b):
    return jnp.dot(a.astype(BF16), b.astype(BF16), preferred_element_type=F32)


def _mm_nt(a, b):
    return lax.dot_general(a.astype(BF16), b.astype(BF16), (((1,), (1,)), ((), ())),
                           preferred_element_type=F32)


def _split2(x):
    hi = x.astype(BF16)
    lo = (x - hi.astype(F32)).astype(BF16)
    return hi, lo


def _split3(x):
    hi = x.astype(BF16)
    r1 = x - hi.astype(F32)
    mid = r1.astype(BF16)
    lo = (r1 - mid.astype(F32)).astype(BF16)
    return hi, mid, lo


def _mm_f32(a, b):
    ah, al = _split2(a)
    bh, bl = _split2(b)
    d = functools.partial(jnp.dot, preferred_element_type=F32)
    return d(ah, bh) + d(ah, bl) + d(al, bh)


def _seg_sum(x, eseg):
    hi, lo = _split2(x)
    d = functools.partial(jnp.dot, preferred_element_type=F32)
    return d(hi, eseg) + d(lo, eseg)


def _rms(x):
    return x * lax.rsqrt(jnp.mean(x * x, axis=-1, keepdims=True) + RMS_EPS)


def _sigmoid(x):
    return 1.0 / (1.0 + jnp.exp(-x))


def _ada_kernel(c_ref, w_ref, b_ref, o_ref):
    c = c_ref[...]
    o_ref[...] = _mm_f32(c * _sigmoid(c), w_ref[...]) + b_ref[...]


def _ada(c_rows, w, b):
    m, d = c_rows.shape
    n = w.shape[1]
    tn = 1536
    return pl.pallas_call(
        _ada_kernel,
        grid=(n // tn,),
        in_specs=[pl.BlockSpec((m, d), lambda j: (0, 0)),
                  pl.BlockSpec((d, tn), lambda j: (0, j)),
                  pl.BlockSpec((1, tn), lambda j: (0, j))],
        out_specs=pl.BlockSpec((m, tn), lambda j: (0, j)),
        out_shape=jax.ShapeDtypeStruct((m, n), F32),
        compiler_params=_cparams("arbitrary"),
        name="ada",
    )(c_rows, w, b.reshape(1, n))


def _proj_kernel(*refs, rope, tiles_per_seq):
    n_rope = 2 if rope else 0
    (x_ref, xp_ref, xn_ref, mod_ref, g_ref, wa_ref, wr_ref, wg_ref, ba_ref, br_ref, bg_ref) = refs[:11]
    cos_ref, sin_ref = refs[11:11 + n_rope] if rope else (None, None)
    (mup_ref, mun_ref, w2_ref, w0_ref, a2_ref, a0_ref, g2_ref, kk_ref, ka_ref, eseg_ref) = refs[11 + n_rope:21 + n_rope]
    za_ref, zg_ref = refs[21 + n_rope:23 + n_rope]
    r_o, v_o, kf_o, kb_o, lwf_o, lwb_o, kkn_o, bf_o, bb_o, gate_o = refs[23 + n_rope:]
    mod = mod_ref[0]
    modulated = lambda xv: _rms(xv) * g_ref[...] * (1.0 + mod[1:2]) + mod[0:1]
    h = modulated(x_ref[...])
    hb = h.astype(BF16)
    za = jnp.dot(hb, wa_ref[...], preferred_element_type=F32) + ba_ref[...]
    if rope:
        qk = za[:, :ROPE_COLS]
        lane = lax.broadcasted_iota(jnp.int32, qk.shape, 1)
        low = (lane % 32) < 16
        partner = jnp.where(low, pltpu.roll(qk, ROPE_COLS - 16, 1), pltpu.roll(qk, 16, 1))
        za_ref[:, :ROPE_COLS] = (qk * cos_ref[...] + partner * sin_ref[...]).astype(za_ref.dtype)
        za_ref[:, ROPE_COLS:] = za[:, ROPE_COLS:].astype(za_ref.dtype)
    else:
        za_ref[...] = za.astype(za_ref.dtype)
    zg_ref[...] = (jnp.dot(hb, wg_ref[...], preferred_element_type=F32) + bg_ref[...]).astype(zg_ref.dtype)

    tm = x_ref.shape[0]
    h_ext = jnp.concatenate([modulated(xp_ref[...]), h, modulated(xn_ref[...])], axis=0).astype(BF16)
    z_ext = jnp.dot(h_ext, wr_ref[...], preferred_element_type=F32) + br_ref[...]
    z = z_ext[SUBLANES:SUBLANES + tm]
    ti = pl.program_id(0) % tiles_per_seq
    row = lax.broadcasted_iota(jnp.int32, (tm, 1), 0)
    prev_halo = jnp.where(ti == 0, 0.0, z_ext[SUBLANES - 1:SUBLANES])
    next_halo = jnp.where(ti == tiles_per_seq - 1, 0.0, z_ext[SUBLANES + tm:SUBLANES + tm + 1])
    prev = jnp.where(row == 0, prev_halo, pltpu.roll(z, 1, 0))
    nxt = jnp.where(row == tm - 1, next_halo, pltpu.roll(z, tm - 1, 0))
    zs = z + mup_ref[...] * (prev - z) + mun_ref[...] * (nxt - z)

    d = RWKV_DIM
    r = zs[:, 0:d]
    k = zs[:, d:2 * d]
    v = zs[:, 2 * d:3 * d]
    o = 3 * d
    wl = zs[:, o:o + 2 * DECAY_LORA]
    al = zs[:, o + 2 * DECAY_LORA:o + 2 * DECAY_LORA + 2 * ICL_LORA]
    gl = zs[:, o + 2 * DECAY_LORA + 2 * ICL_LORA:]

    w = w0_ref[...] + _mm(jnp.tanh(wl), w2_ref[...])
    lw = -float(np.exp(-0.5)) * _sigmoid(w)
    icl = _sigmoid(a0_ref[...] + _mm(al, a2_ref[...]))
    kk0 = k * kk_ref[...]
    ss = _seg_sum(kk0 * kk0, eseg_ref[...])
    kk = kk0 / jnp.maximum(jnp.sqrt(ss), 1e-12)
    ka = ka_ref[...]
    icl_f = icl[:, :d]
    icl_b = icl[:, d:]

    r_o[...] = r.astype(r_o.dtype)
    v_o[...] = v.astype(v_o.dtype)
    kf_o[...] = (k * (1.0 + (icl_f - 1.0) * ka)).astype(kf_o.dtype)
    kb_o[...] = (k * (1.0 + (icl_b - 1.0) * ka)).astype(kb_o.dtype)
    lwf_o[...] = lw[:, :d]
    lwb_o[...] = lw[:, d:]
    kkn_o[...] = kk.astype(kkn_o.dtype)
    bf_o[...] = (kk * icl_f).astype(bf_o.dtype)
    bb_o[...] = (kk * icl_b).astype(bb_o.dtype)
    gate_o[...] = _mm(_sigmoid(gl), g2_ref[...]).astype(gate_o.dtype)


def _rope_tables(seq):
    n_rows = seq // GRID_W
    row = jnp.repeat(jnp.arange(n_rows, dtype=F32), GRID_W, total_repeat_length=seq)
    col = jnp.tile(jnp.arange(GRID_W, dtype=F32), n_rows)
    half = HEAD_DIM // 2
    inv_freq = ROPE_BASE ** (-jnp.arange(0, half, 2, dtype=F32) / half)
    ang_r = row[:, None] * inv_freq[None, :]
    ang_c = col[:, None] * inv_freq[None, :]
    cos_h = jnp.concatenate([jnp.cos(ang_r), jnp.cos(ang_r), jnp.cos(ang_c), jnp.cos(ang_c)], axis=1)
    sin_h = jnp.concatenate([-jnp.sin(ang_r), jnp.sin(ang_r), -jnp.sin(ang_c), jnp.sin(ang_c)], axis=1)
    reps = ROPE_COLS // HEAD_DIM
    return jnp.tile(cos_h, (1, reps)), jnp.tile(sin_h, (1, reps))


def _project(x2, mods, mod_row, seq, g_pre, w_att, w_rwkv, w_gate, b_att, b_rwkv, b_gate, feat_params, rope):
    rows, d = x2.shape
    tm = ROW_TILE
    tps = seq // tm
    hb = tm // SUBLANES
    nhb = rows // SUBLANES
    rd = RWKV_DIM
    const = lambda i: (0, 0)
    in_specs = [pl.BlockSpec((tm, d), lambda i: (i, 0)),
                pl.BlockSpec((SUBLANES, d), lambda i: (jnp.maximum(i * hb - 1, 0), 0)),
                pl.BlockSpec((SUBLANES, d), lambda i: (jnp.minimum((i + 1) * hb, nhb - 1), 0)),
                pl.BlockSpec((1, SUBLANES, d), lambda i: (mod_row(i // tps), 0, 0)),
                pl.BlockSpec((1, d), const),
                pl.BlockSpec(w_att.shape, const), pl.BlockSpec(w_rwkv.shape, const),
                pl.BlockSpec(w_gate.shape, const),
                pl.BlockSpec((1, ATT_COLS), const), pl.BlockSpec((1, RWKV_COLS), const),
                pl.BlockSpec((1, w_gate.shape[1]), const)]
    args = [x2, x2, x2, mods, g_pre, w_att, w_rwkv, w_gate, b_att, b_rwkv, b_gate]
    if rope:
        cos, sin = _rope_tables(seq)
        in_specs += [pl.BlockSpec((tm, ROPE_COLS), lambda i: (i % tps, 0))] * 2
        args += [cos, sin]
    in_specs += [pl.BlockSpec(a.shape, const) for a in feat_params]
    args += feat_params
    feat_spec = pl.BlockSpec((tm, rd), lambda i: (i, 0))
    outs = pl.pallas_call(
        functools.partial(_proj_kernel, rope=rope, tiles_per_seq=tps),
        grid=(rows // tm,),
        in_specs=in_specs,
        out_specs=[pl.BlockSpec((tm, ATT_COLS), lambda i: (i, 0)),
                   pl.BlockSpec((tm, w_gate.shape[1]), lambda i: (i, 0))] + [feat_spec] * len(FEAT_NAMES),
        out_shape=[jax.ShapeDtypeStruct((rows, ATT_COLS), BF16),
                   jax.ShapeDtypeStruct((rows, w_gate.shape[1]), BF16)]
                  + [jax.ShapeDtypeStruct((rows, rd), BF16 if name in FEAT_BF16 else F32) for name in FEAT_NAMES],
        compiler_params=_cparams("parallel"),
        name="proj",
    )(*args)
    return outs[0], outs[1], dict(zip(FEAT_NAMES, outs[2:]))


FEAT_NAMES = ('r', 'v', 'kf', 'kb', 'lwf', 'lwb', 'kk', 'bf', 'bb', 'gate')
FEAT_BF16 = ('r', 'v', 'kf', 'kb', 'kk', 'bf', 'bb', 'gate')


def _block_diag2(a, b):
    za = jnp.zeros_like(a)
    zb = jnp.zeros_like(b)
    return jnp.concatenate([jnp.concatenate([a, zb], axis=1), jnp.concatenate([za, b], axis=1)], axis=0)


def _seg_matrix():
    h = np.arange(RWKV_DIM) // HEAD_DIM
    return jnp.asarray((h[:, None] == h[None, :]).astype(np.float32), BF16)


def _feature_params(p):
    d = RWKV_DIM
    return [p['mu_prev'].reshape(1, -1), p['mu_next'].reshape(1, -1),
            _block_diag2(p['w2_f'], p['w2_b']).astype(BF16),
            jnp.concatenate([p['w0_f'], p['w0_b']]).reshape(1, 2 * d),
            _block_diag2(p['a2_f'], p['a2_b']).astype(BF16),
            jnp.concatenate([p['a0_f'], p['a0_b']]).reshape(1, 2 * d),
            p['g2'].astype(BF16), p['k_k'].reshape(1, d), p['k_a'].reshape(1, d), _seg_matrix()]


def _scan_kernel(rf, vf, kf, lwf, kkf, bf, rb, vb, kb, lwb, kkb, bb, s0f_ref, s0b_ref,
                 yf_ref, yb_ref, sTf_ref, sTb_ref, state_ref, *, chunk, n_chunks):
    c = pl.program_id(1)
    C = chunk
    hd = HEAD_DIM
    gw = SCAN_GROUP * hd
    n_groups = RWKV_HEADS // SCAN_GROUP
    assert C == hd, "the triangular masks below are shared between time and channel blocks"

    n_rows = state_ref.shape[0]

    @pl.when(c == 0)
    def _():
        for n in range(n_rows):
            for d, s0_ref in enumerate((s0f_ref, s0b_ref)):
                for g in range(n_groups):
                    state_ref[n, d, g] = jnp.zeros((gw, gw), F32)
                    for j in range(SCAN_GROUP):
                        state_ref[n, d, g, j * hd:(j + 1) * hd, j * hd:(j + 1) * hd] = (
                            s0_ref[n, g * SCAN_GROUP + j])

    ti = lax.broadcasted_iota(jnp.int32, (C, C), 0)
    si = lax.broadcasted_iota(jnp.int32, (C, C), 1)
    tg = lax.broadcasted_iota(jnp.int32, (C, gw), 0)
    sg = lax.broadcasted_iota(jnp.int32, (C, gw), 1) % C
    eye = (tg == sg).astype(F32)
    same_head = (lax.broadcasted_iota(jnp.int32, (gw, gw), 0) // hd
                 == lax.broadcasted_iota(jnp.int32, (gw, gw), 1) // hd)
    same_head_b = same_head.astype(BF16)
    n_double = int(np.log2(C)) - 1

    def bdiag(x_cat):
        return jnp.concatenate([x_cat.astype(BF16)] * SCAN_GROUP, axis=0) * same_head_b

    dirs = ((rf, vf, kf, lwf, kkf, bf), (rb, vb, kb, lwb, kkb, bb))
    units = []
    masks = [(((si <= ti) if d == 0 else (si >= ti)).astype(BF16),
              (sg <= tg) if d == 0 else (sg >= tg),
              (sg < tg) if d == 0 else (sg > tg)) for d in range(2)]
    for n, d in [(n, d) for n in range(n_rows) for d in range(2)]:
        r_ref, v_ref, k_ref, lw_ref, kk_ref, b_ref = (ref.at[n] for ref in dirs[d])
        tri, incl_g, strict_g = masks[d]
        lw = lw_ref[...]
        lh, lm, ll = _split3(lw)
        dd = functools.partial(jnp.dot, preferred_element_type=F32)
        cum = dd(tri, lh) + dd(tri, lm) + dd(tri, ll)
        cumx = cum - lw
        cum_end = cum[C - 1:C] if d == 0 else cum[0:1]
        e_neg = jnp.exp(-cum)
        e_end = jnp.exp(cum_end - cum)
        bv = b_ref[...].astype(F32)
        kv = k_ref[...].astype(F32)
        vv = v_ref[...].astype(F32)
        a_t = (-kk_ref[...].astype(F32) * jnp.exp(cumx)).astype(BF16)
        r_t = (r_ref[...].astype(F32) * jnp.exp(cum)).astype(BF16)
        b_t = (bv * e_neg).astype(BF16)
        k_t = (kv * e_neg).astype(BF16)
        b_q = (bv * e_end).astype(BF16)
        k_q = (kv * e_end).astype(BF16)
        g_end = jnp.exp(cum_end)
        for g in range(n_groups):
            sl = slice(g * gw, (g + 1) * gw)
            units.append(dict(
                n=n, d=d, g=g, sl=sl, incl=incl_g, strict=strict_g,
                P=jnp.concatenate([a_t[:, sl], r_t[:, sl]], axis=0),
                Q=jnp.concatenate([bdiag(b_t[:, sl]), bdiag(k_t[:, sl])], axis=0),
                Qq=jnp.concatenate([b_q[:, sl], k_q[:, sl]], axis=0),
                V=vv[:, sl], g_end=g_end[:, sl]))

    for u in units:
        u['G'] = _mm_nt(u['P'], u['Q'])
    for u in units:
        u['S0'] = state_ref[u['n'], u['d'], u['g']]
        u['PH'] = _mm_nt(u['P'], u['S0'])
        u['Vd'] = bdiag(u['V'])
    for u in units:
        G = u.pop('G')
        u['N'] = jnp.where(u['strict'], G[:C, :gw], 0.0)
        u['a_ak'] = jnp.where(u['strict'], G[:C, gw:], 0.0)
        u['a_rb'] = jnp.where(u['incl'], G[C:, :gw], 0.0)
        u['a_rk'] = jnp.where(u['incl'], G[C:, gw:], 0.0)
    for u in units:
        u['T'] = eye + u['N']
        u['Pw'] = _mm(u['N'], bdiag(u['N']))
        av = _mm(jnp.concatenate([u['a_ak'], u['a_rk']], axis=0), u['Vd'])
        u['rhs'] = u['PH'][:C] + av[:C]
        u['y0'] = u['PH'][C:] + av[C:]
    for lvl in range(n_double):
        for u in units:
            pw = bdiag(u['Pw'])
            if lvl + 1 < n_double:
                both = _mm(jnp.concatenate([u['T'], u['Pw']], axis=0), pw)
                u['T'] = u['T'] + both[:C]
                u['Pw'] = both[C:]
            else:
                u['T'] = u['T'] + _mm(u['T'], pw)
    for u in units:
        u['U'] = _mm(u['T'], bdiag(u['rhs']))
    y_refs = (yf_ref, yb_ref)
    for u in units:
        y_refs[u['d']][u['n'], :, u['sl']] = u['y0'] + _mm(u['a_rb'], bdiag(u['U']))
        uv = jnp.concatenate([u['U'], u['V']], axis=0)
        state_ref[u['n'], u['d'], u['g']] = jnp.where(
            same_head, u['S0'] * u['g_end'] + _mm(uv.T, u['Qq']), 0.0)

    @pl.when(c == n_chunks - 1)
    def _():
        for n in range(n_rows):
            for d, sT_ref in enumerate((sTf_ref, sTb_ref)):
                for g in range(n_groups):
                    for j in range(SCAN_GROUP):
                        sT_ref[n, g * SCAN_GROUP + j] = (
                            state_ref[n, d, g, j * hd:(j + 1) * hd, j * hd:(j + 1) * hd])


def _scan(f, batch, seq, s0_f, s0_b):
    C = SCAN_CHUNK
    nC = seq // C
    d = RWKV_DIM
    nb = max(n for n in range(1, SCAN_BATCH + 1) if batch % n == 0)
    fwd = pl.BlockSpec((nb, C, d), lambda b, c: (b, c, 0))
    bwd = pl.BlockSpec((nb, C, d), lambda b, c: (b, nC - 1 - c, 0))
    st = pl.BlockSpec((nb, RWKV_HEADS, HEAD_DIM, HEAD_DIM), lambda b, c: (b, 0, 0, 0))
    st_shape = jax.ShapeDtypeStruct((batch, RWKV_HEADS, HEAD_DIM, HEAD_DIM), F32)
    y_shape = jax.ShapeDtypeStruct((batch, seq, d), F32)
    gw = SCAN_GROUP * HEAD_DIM
    arr = lambda name: f[name].reshape(batch, seq, d)
    y_f, y_b, sT_f, sT_b = pl.pallas_call(
        functools.partial(_scan_kernel, chunk=C, n_chunks=nC),
        grid=(batch // nb, nC),
        in_specs=[fwd] * 6 + [bwd] * 6 + [st, st],
        out_specs=[fwd, bwd, st, st],
        out_shape=[y_shape, y_shape, st_shape, st_shape],
        scratch_shapes=[pltpu.VMEM((nb, 2, RWKV_HEADS // SCAN_GROUP, gw, gw), F32)],
        compiler_params=_cparams("parallel", "arbitrary"),
        name="scan",
    )(arr('r'), arr('v'), arr('kf'), arr('lwf'), arr('kk'), arr('bf'),
      arr('r'), arr('v'), arr('kb'), arr('lwb'), arr('kk'), arr('bb'), s0_f, s0_b)
    return y_f.reshape(batch * seq, d), y_b.reshape(batch * seq, d), sT_f, sT_b


def _attn_kernel(sink_ref, q_ref, kp_ref, km_ref, kn_ref, vp_ref, vm_ref, vn_ref, kc_ref, vc_ref, o_ref,
                 *, n_blocks):
    n = pl.program_id(1)
    rows = ATT_GROUPS * BLOCK
    q = q_ref[...] * ATT_SCALE
    qi = lax.broadcasted_iota(jnp.int32, (rows, BLOCK), 0) % BLOCK
    kj = lax.broadcasted_iota(jnp.int32, (rows, BLOCK), 1)
    ok_prev = (kj >= qi) & (n > 0)
    ok_next = (kj <= qi) & (n < n_blocks - 1)
    rowh = lax.broadcasted_iota(jnp.int32, (rows, 1), 0) // BLOCK
    outs = [None] * ATT_HEADS
    for g in range(ATT_KV_HEADS):
        ks = slice(g * HEAD_DIM, (g + 1) * HEAD_DIM)
        heads = [g * ATT_GROUPS + j for j in range(ATT_GROUPS)]
        Qs = jnp.concatenate([q[:, h * HEAD_DIM:(h + 1) * HEAD_DIM] for h in heads], axis=0)
        scores = [jnp.where(ok_prev, _mm_nt(Qs, kp_ref[:, ks]), NEG_INF),
                  _mm_nt(Qs, km_ref[:, ks]),
                  jnp.where(ok_next, _mm_nt(Qs, kn_ref[:, ks]), NEG_INF),
                  _mm_nt(Qs, kc_ref[:, ks])]
        values = [vp_ref, vm_ref, vn_ref, vc_ref]
        sink = jnp.zeros((rows, 1), F32)
        for j, h in enumerate(heads):
            sink = jnp.where(rowh == j, sink_ref[h], sink)
        folded = None
        for sc in scores:
            for c0 in range(0, sc.shape[1], BLOCK):
                blk = sc[:, c0:c0 + BLOCK]
                folded = blk if folded is None else jnp.maximum(folded, blk)
        m = jnp.maximum(sink, jnp.max(folded, axis=-1, keepdims=True))
        acc = jnp.zeros((rows, 2 * HEAD_DIM), F32)
        for sc, v_ref in zip(scores, values):
            one_col = (lax.broadcasted_iota(jnp.int32, (v_ref.shape[0], HEAD_DIM), 1) == 0).astype(BF16)
            v_ext = jnp.concatenate([v_ref[:, ks], one_col], axis=1)
            acc = acc + jnp.dot(jnp.exp((sc - m).astype(BF16)), v_ext, preferred_element_type=F32)
        den = acc[:, HEAD_DIM:HEAD_DIM + 1] + jnp.exp(sink - m)
        O = acc[:, :HEAD_DIM] / den
        for j, h in enumerate(heads):
            outs[h] = O[j * BLOCK:(j + 1) * BLOCK]
    o_ref[...] = jnp.concatenate(outs, axis=1).astype(o_ref.dtype)


def _attention(z_att, zc_att, sinks, batch, seq):
    nb = seq // BLOCK
    ctx_len = zc_att.shape[0] // batch
    kcol = ATT_DIM // KV_DIM
    vcol = kcol + 1

    def kv_spec(col, off):
        return pl.BlockSpec((BLOCK, KV_DIM), lambda b, n: (b * nb + jnp.clip(n + off, 0, nb - 1), col))

    return pl.pallas_call(
        functools.partial(_attn_kernel, n_blocks=nb),
        grid=(batch, nb),
        in_specs=[pl.BlockSpec(memory_space=pltpu.SMEM),
                  pl.BlockSpec((BLOCK, ATT_DIM), lambda b, n: (b * nb + n, 0)),
                  kv_spec(kcol, -1), kv_spec(kcol, 0), kv_spec(kcol, 1),
                  kv_spec(vcol, -1), kv_spec(vcol, 0), kv_spec(vcol, 1),
                  pl.BlockSpec((ctx_len, KV_DIM), lambda b, n: (b, kcol)),
                  pl.BlockSpec((ctx_len, KV_DIM), lambda b, n: (b, vcol))],
        out_specs=pl.BlockSpec((BLOCK, ATT_DIM), lambda b, n: (b * nb + n, 0)),
        out_shape=jax.ShapeDtypeStruct((batch * seq, ATT_DIM), BF16),
        compiler_params=_cparams("parallel", "parallel"),
        name="attn",
    )(sinks, z_att, z_att, z_att, z_att, z_att, z_att, z_att, zc_att, zc_att)


def _merge_kernel(yf_ref, yb_ref, r_ref, kf_ref, kb_ref, v_ref, gate_ref, att_ref, zg_ref, x_ref, mod_ref,
                  lnw_ref, lnb_ref, rk_ref, eseg_ref, wua_ref, wur_ref, wo_ref, gpm_ref, gpf_ref,
                  wr_ref, br_ref,
                  x1_ref, hf_ref, idx_ref, gt_ref, rank_ref, cnt_ref):
    eseg = eseg_ref[...]
    inv_n = 1.0 / HEAD_DIM
    y = yf_ref[...] + yb_ref[...]
    mean = _seg_sum(y, eseg) * inv_n
    dy = y - mean
    var = _seg_sum(dy * dy, eseg) * inv_n
    yn = dy * lax.rsqrt(var + GN_EPS) * lnw_ref[...] + lnb_ref[...]
    f32 = lambda ref: ref[...].astype(F32)
    bonus = _seg_sum(f32(r_ref) * (f32(kf_ref) + f32(kb_ref)) * rk_ref[...], eseg) * f32(v_ref)
    rwk = (yn + bonus) * f32(gate_ref)

    d = x_ref.shape[1]
    zg = zg_ref[...].astype(F32)
    merged = (_sigmoid(zg[:, :d]) * jnp.dot(att_ref[...], wua_ref[...], preferred_element_type=F32)
              + _sigmoid(zg[:, d:]) * _mm(rwk, wur_ref[...]))
    mix = _mm(merged, wo_ref[...])
    mod = mod_ref[0]
    x1 = x_ref[...] + mod[2:3] * (_rms(mix) * gpm_ref[...])
    x1_ref[...] = x1
    hf = _rms(x1) * gpf_ref[...] * (1.0 + mod[4:5]) + mod[3:4]
    hf_ref[...] = hf.astype(hf_ref.dtype)

    logits = _mm_f32(hf, wr_ref[...]) + br_ref[...]
    lane = lax.broadcasted_iota(jnp.int32, logits.shape, 1).astype(F32)
    idx_out = jnp.zeros(logits.shape, F32)
    val_out = jnp.zeros(logits.shape, F32)
    picked = jnp.zeros(logits.shape, F32)
    sels = []
    top = None
    den = None
    for kth in range(TOP_K):
        m = jnp.max(logits, axis=-1, keepdims=True)
        sel = jnp.min(jnp.where(logits == m, lane, float(LANES)), axis=-1, keepdims=True)
        hit = lane == sel
        logits = jnp.where(hit, -jnp.inf, logits)
        picked = jnp.where(hit, 1.0, picked)
        sels.append(hit)
        if kth == 0:
            top = m
        e = jnp.exp(m - top)
        den = e if kth == 0 else den + e
        idx_out = jnp.where(lane == float(kth), sel, idx_out)
        val_out = jnp.where(lane == float(kth), e, val_out)
    idx_ref[...] = idx_out.astype(jnp.int32)
    gt_ref[...] = val_out / den

    tm = picked.shape[0]
    earlier = (lax.broadcasted_iota(jnp.int32, (tm, tm), 1)
               < lax.broadcasted_iota(jnp.int32, (tm, tm), 0)).astype(BF16)
    before = jnp.dot(earlier, picked.astype(BF16), preferred_element_type=F32)
    rank_out = jnp.zeros(logits.shape, F32)
    for kth, hit in enumerate(sels):
        rk = jnp.sum(jnp.where(hit, before, 0.0), axis=-1, keepdims=True)
        rank_out = jnp.where(lane == float(kth), rk, rank_out)
    rank_ref[...] = rank_out.astype(jnp.int32)
    cnt_ref[0] = jnp.broadcast_to(jnp.sum(picked, axis=0, keepdims=True), cnt_ref.shape[1:]).astype(jnp.int32)

def _moe_blocks(n_tok, n_tiles):
    return -(-(n_tok * TOP_K + (SUBLANES - 1) * N_EXPERTS * n_tiles) // MOE_ROWS) + N_EXPERTS


def _merge(y_f, y_b, f, att, z_gate, x2, mods, seq, p):
    rows, d = x2.shape
    tm = ROW_TILE
    tps = seq // tm
    rd = RWKV_DIM
    const = lambda i: (0, 0)
    row = lambda w: pl.BlockSpec((tm, w), lambda i: (i, 0))
    vec = lambda w: pl.BlockSpec((1, w), const)
    w_router = jnp.pad(p['w_router'], ((0, 0), (0, LANES - N_EXPERTS)))
    b_router = jnp.pad(p['b_router'], (0, LANES - N_EXPERTS), constant_values=NEG_INF).reshape(1, LANES)
    return pl.pallas_call(
        _merge_kernel,
        grid=(rows // tm,),
        in_specs=[row(rd)] * 7 + [row(ATT_DIM), row(2 * d), row(d),
                  pl.BlockSpec((1, SUBLANES, d), lambda i: (i // tps, 0, 0)),
                  vec(rd), vec(rd), vec(rd), pl.BlockSpec((rd, rd), const),
                  pl.BlockSpec((ATT_DIM, d), const), pl.BlockSpec((rd, d), const), pl.BlockSpec((d, d), const),
                  vec(d), vec(d), pl.BlockSpec((d, LANES), const), vec(LANES)],
        out_specs=[row(d), row(d), row(LANES), row(LANES), row(LANES),
                   pl.BlockSpec((1, SUBLANES, LANES), lambda i: (i, 0, 0))],
        out_shape=[jax.ShapeDtypeStruct((rows, d), F32), jax.ShapeDtypeStruct((rows, d), BF16),
                   jax.ShapeDtypeStruct((rows, LANES), jnp.int32), jax.ShapeDtypeStruct((rows, LANES), F32),
                   jax.ShapeDtypeStruct((rows, LANES), jnp.int32),
                   jax.ShapeDtypeStruct((rows // tm, SUBLANES, LANES), jnp.int32)],
        compiler_params=_cparams("parallel"),
        name="merge",
    )(y_f, y_b, f['r'], f['kf'], f['kb'], f['v'], f['gate'], att, z_gate, x2, mods,
      p['ln_x_w'].reshape(1, rd), p['ln_x_b'].reshape(1, rd), p['r_k'].reshape(1, rd), _seg_matrix(),
      p['w_up_att'].astype(BF16), p['w_up_rwkv'].astype(BF16), p['w_out'].astype(BF16),
      p['g_post_mix'].reshape(1, d), p['g_pre_ffn'].reshape(1, d), w_router, b_router)


def _moe_kernel(be_ref, nused_ref, x_ref, wgu_ref, bgu_ref, wdn_f32_ref, bdn_ref, o_ref, wdn_ref):
    i = pl.program_id(0)

    @pl.when((i == 0) | (be_ref[i] != be_ref[jnp.maximum(i - 1, 0)]))
    def _():
        wdn_ref[...] = wdn_f32_ref[...].astype(BF16)

    @pl.when(i < nused_ref[0])
    def _():
        de = wdn_ref.shape[0]
        d = x_ref.shape[1]
        x = x_ref[...]
        bgu = bgu_ref[...]
        gate = jnp.minimum(_mm_nt(x, wgu_ref[:, :d]) + bgu[:, :de], SWIGLU_LIMIT)
        up = jnp.clip(_mm_nt(x, wgu_ref[:, d:]) + bgu[:, de:], -SWIGLU_LIMIT, SWIGLU_LIMIT)
        act = (up + 1.0) * (gate * _sigmoid(SWIGLU_ALPHA * gate))
        o_ref[...] = _mm(act, wdn_ref[...]) + bdn_ref[...]

    @pl.when(i >= nused_ref[0])
    def _():
        o_ref[...] = jnp.zeros(o_ref.shape, o_ref.dtype)


def _wprep_kernel(w_ref, o_ref, t_ref):
    d = w_ref.shape[1]
    half = o_ref.shape[1]
    wt = w_ref[0].T
    for c in range(d // LANES):
        cols = slice(c * LANES, (c + 1) * LANES)
        t_ref[c] = wt[:, cols]
        o_ref[0, :, cols] = t_ref[c, pl.ds(0, half, stride=2), :].astype(o_ref.dtype)
        o_ref[0, :, d + c * LANES:d + (c + 1) * LANES] = t_ref[c, pl.ds(1, half, stride=2), :].astype(o_ref.dtype)


def _gate_up_rows(w_gu):
    n_exp, d, de2 = w_gu.shape
    tc = 4 * LANES
    return pl.pallas_call(
        _wprep_kernel,
        grid=(n_exp, de2 // tc),
        in_specs=[pl.BlockSpec((1, d, tc), lambda e, j: (e, 0, j))],
        out_specs=pl.BlockSpec((1, tc // 2, 2 * d), lambda e, j: (e, j, 0)),
        out_shape=jax.ShapeDtypeStruct((n_exp, de2 // 2, 2 * d), BF16),
        scratch_shapes=[pltpu.VMEM((d // LANES, tc, LANES), F32)],
        compiler_params=_cparams("parallel", "parallel"),
        name="wprep",
    )(w_gu)


def _experts(xb, block_exp, n_used, w_gu, b_gu, w_dn, b_dn):
    n_slots, d = xb.shape
    bm = MOE_ROWS
    n_blocks = n_slots // bm
    de = w_dn.shape[1]
    de2 = 2 * de
    grid_spec = pltpu.PrefetchScalarGridSpec(
        num_scalar_prefetch=2,
        grid=(n_blocks,),
        in_specs=[pl.BlockSpec((bm, d), lambda i, be, nu: (jnp.minimum(i, nu[0] - 1), 0)),
                  pl.BlockSpec((None, de, 2 * d), lambda i, be, nu: (be[i], 0, 0)),
                  pl.BlockSpec((None, 1, de2), lambda i, be, nu: (be[i], 0, 0)),
                  pl.BlockSpec((None, de, d), lambda i, be, nu: (be[i], 0, 0)),
                  pl.BlockSpec((None, 1, d), lambda i, be, nu: (be[i], 0, 0))],
        out_specs=pl.BlockSpec((bm, d), lambda i, be, nu: (i, 0)),
        scratch_shapes=[pltpu.VMEM((de, d), BF16)],
    )
    return pl.pallas_call(
        _moe_kernel,
        grid_spec=grid_spec,
        out_shape=jax.ShapeDtypeStruct((n_slots, d), F32),
        compiler_params=_cparams("arbitrary"),
        name="moe",
    )(block_exp, n_used, xb, w_gu, b_gu, w_dn, b_dn)


def _local_rows(tm):
    rows = tm * TOP_K + (SUBLANES - 1) * N_EXPERTS
    assert rows % SUBLANES == 0
    return rows


def _pick_matrix(loc, weights, n_cols):
    col = lax.broadcasted_iota(jnp.int32, (loc.shape[0], n_cols), 1)
    out = jnp.zeros(col.shape, F32)
    for k in range(TOP_K):
        out = jnp.where(col == loc[:, k:k + 1], weights[k], out)
    return out


def _segment(seg_ref, tile, e):
    base = (tile * N_EXPERTS + e) * 3
    return tuple(pl.multiple_of(seg_ref[base + i], SUBLANES) for i in range(3))


def _dispatch_kernel(seg_ref, nch_ref, tail_ref, hf_ref, loc_ref, xb_ref, xc_ref, zero_ref, sem, zsem):
    j = pl.program_id(0)
    n_tiles = pl.num_programs(0)
    slot = j % 2

    def tail_copy(e):
        row = pl.multiple_of(tail_ref[2 * e], SUBLANES)
        n = pl.multiple_of(tail_ref[2 * e + 1], SUBLANES)
        return n, pltpu.make_async_copy(zero_ref.at[pl.ds(0, n), :], xb_ref.at[pl.ds(row, n), :], zsem)

    def spare_block_copy(b):
        rows = xb_ref.at[pl.ds(pl.multiple_of(b * MOE_ROWS, MOE_ROWS), MOE_ROWS), :]
        return pltpu.make_async_copy(zero_ref, rows, zsem)

    first_spare = tail_ref[2 * N_EXPERTS]
    n_blocks = xb_ref.shape[0] // MOE_ROWS

    @pl.when(j == 0)
    def _():
        zero_ref[...] = jnp.zeros(zero_ref.shape, zero_ref.dtype)

        def body(e, carry):
            n, cp = tail_copy(e)

            @pl.when(n > 0)
            def _():
                cp.start()
            return carry
        lax.fori_loop(0, N_EXPERTS, body, 0)

        def spare(b, carry):
            spare_block_copy(b).start()
            return carry
        lax.fori_loop(first_spare, n_blocks, spare, 0)

    def drain(tile, s):
        rows = pl.multiple_of(nch_ref[tile] * SUBLANES, SUBLANES)
        done = xc_ref.at[s, pl.ds(0, rows), :]
        pltpu.make_async_copy(done, done, sem.at[s]).wait()

    sel = _pick_matrix(loc_ref[...], [1.0] * TOP_K, xc_ref.shape[1]).astype(BF16)
    xc_ref[slot] = lax.dot_general(sel, hf_ref[...], (((0,), (0,)), ((), ())), preferred_element_type=F32)

    def start(e, carry):
        off, row, n = _segment(seg_ref, j, e)

        @pl.when(n > 0)
        def _():
            pltpu.make_async_copy(xc_ref.at[slot, pl.ds(off, n), :], xb_ref.at[pl.ds(row, n), :],
                                  sem.at[slot]).start()
        return carry
    lax.fori_loop(0, N_EXPERTS, start, 0)

    @pl.when(j > 0)
    def _():
        drain(j - 1, 1 - slot)

    @pl.when(j == n_tiles - 1)
    def _():
        drain(j, slot)

        def body(e, carry):
            n, cp = tail_copy(e)

            @pl.when(n > 0)
            def _():
                cp.wait()
            return carry
        lax.fori_loop(0, N_EXPERTS, body, 0)

        def spare(b, carry):
            spare_block_copy(b).wait()
            return carry
        lax.fori_loop(first_spare, n_blocks, spare, 0)


def _dispatch(hf, loc, segments, nch, tails):
    rows, d = hf.shape
    tm = ROW_TILE
    n_slots = _moe_blocks(rows, rows // tm) * MOE_ROWS
    grid_spec = pltpu.PrefetchScalarGridSpec(
        num_scalar_prefetch=3,
        grid=(rows // tm,),
        in_specs=[pl.BlockSpec((tm, d), lambda i, sg, nc, tl: (i, 0)),
                  pl.BlockSpec((tm, LANES), lambda i, sg, nc, tl: (i, 0))],
        out_specs=pl.BlockSpec(memory_space=pl.ANY),
        scratch_shapes=[pltpu.VMEM((2, _local_rows(tm), d), F32), pltpu.VMEM((MOE_ROWS, d), F32),
                        pltpu.SemaphoreType.DMA((2,)), pltpu.SemaphoreType.DMA(())],
    )
    return pl.pallas_call(
        _dispatch_kernel,
        grid_spec=grid_spec,
        out_shape=jax.ShapeDtypeStruct((n_slots, d), F32),
        compiler_params=_cparams("arbitrary"),
        name="dispatch",
    )(segments, nch, tails, hf, loc)


def _combine_kernel(seg_ref, nch_ref, yb_ref, loc_ref, gt_ref, x1_ref, mod_ref, g_ref, o_ref, yc_ref, sem):
    j = pl.program_id(0)
    n_tiles = pl.num_programs(0)

    def issue(tile, s):
        def body(e, carry):
            off, row, n = _segment(seg_ref, tile, e)

            @pl.when(n > 0)
            def _():
                pltpu.make_async_copy(yb_ref.at[pl.ds(row, n), :], yc_ref.at[s, pl.ds(off, n), :],
                                      sem.at[s]).start()
            return carry
        lax.fori_loop(0, N_EXPERTS, body, 0)

    @pl.when(j == 0)
    def _():
        yc_ref[...] = jnp.zeros(yc_ref.shape, yc_ref.dtype)
        issue(0, 0)

    @pl.when(j + 1 < n_tiles)
    def _():
        issue(j + 1, (j + 1) % 2)

    slot = j % 2

    rows = pl.multiple_of(nch_ref[j] * SUBLANES, SUBLANES)
    done = yc_ref.at[slot, pl.ds(0, rows), :]
    pltpu.make_async_copy(done, done, sem.at[slot]).wait()

    gt = gt_ref[...]
    w = _pick_matrix(loc_ref[...], [gt[:, k:k + 1] for k in range(TOP_K)], yc_ref.shape[1])
    y = _mm(w, yc_ref[slot])
    mod = mod_ref[0]
    o_ref[...] = x1_ref[...] + mod[5:6] * (_rms(y) * g_ref[...])


def _combine(yb, loc, segments, nch, gt, x1, mods, seq, g_post):
    rows, d = x1.shape
    tm = ROW_TILE
    tps = seq // tm
    row = lambda w: pl.BlockSpec((tm, w), lambda i, gd, nc: (i, 0))
    grid_spec = pltpu.PrefetchScalarGridSpec(
        num_scalar_prefetch=2,
        grid=(rows // tm,),
        in_specs=[pl.BlockSpec(memory_space=pl.ANY), row(LANES), row(LANES), row(d),
                  pl.BlockSpec((1, SUBLANES, d), lambda i, gd, nc: (i // tps, 0, 0)),
                  pl.BlockSpec((1, d), lambda i, gd, nc: (0, 0))],
        out_specs=row(d),
        scratch_shapes=[pltpu.VMEM((2, _local_rows(tm), d), F32), pltpu.SemaphoreType.DMA((2,))],
    )
    return pl.pallas_call(
        _combine_kernel,
        grid_spec=grid_spec,
        out_shape=jax.ShapeDtypeStruct((rows, d), F32),
        compiler_params=_cparams("arbitrary"),
        name="combine",
    )(segments, nch, yb, loc, gt, x1, mods, g_post.reshape(1, d))


def _route_slots(top_idx, rank, tile_counts, n_tok):
    bm = MOE_ROWS
    n_tiles = tile_counts.shape[0]
    tm = n_tok // n_tiles
    seg_len = -(-tile_counts // SUBLANES) * SUBLANES
    seg_off = jnp.cumsum(seg_len, axis=1) - seg_len
    per_expert = jnp.sum(seg_len, axis=0)
    padded = -(-per_expert // bm) * bm
    p_end = jnp.cumsum(padded)
    seg_slot = (p_end - padded)[None, :] + jnp.cumsum(seg_len, axis=0) - seg_len
    hit = top_idx[:, :, None] == jnp.arange(N_EXPERTS, dtype=jnp.int32)[None, None, :]
    loc = rank + jnp.sum(jnp.where(hit, jnp.repeat(seg_off, tm, axis=0)[:, None, :], 0), axis=-1)
    segments = jnp.stack([seg_off, seg_slot, seg_len], axis=-1).reshape(-1).astype(jnp.int32)
    tails = jnp.stack([p_end - padded + per_expert, padded - per_expert], axis=-1).reshape(-1)
    tails = jnp.concatenate([tails, p_end[-1:] // bm]).astype(jnp.int32)
    n_chunks = jnp.sum(seg_len, axis=1) // SUBLANES
    starts = jnp.arange(_moe_blocks(n_tok, n_tiles), dtype=jnp.int32) * bm
    block_exp = jnp.minimum(jnp.sum(p_end[None, :] <= starts[:, None], axis=1), N_EXPERTS - 1).astype(jnp.int32)
    n_used = (p_end[-1] // bm).astype(jnp.int32).reshape(1)
    loc = jnp.pad(loc.astype(jnp.int32), ((0, 0), (0, LANES - TOP_K)))
    return loc, segments, n_chunks.astype(jnp.int32), tails, block_exp, n_used


def _moe(hf, loc, segments, n_chunks, tails, block_exp, n_used, p):
    xb = _dispatch(hf, loc, segments, n_chunks, tails)
    w_gu = _gate_up_rows(p['w_gate_up'])
    b_gu = p['b_gate_up']
    b_gu = jnp.concatenate([b_gu[..., 0::2], b_gu[..., 1::2]], axis=-1)[:, None, :]
    return _experts(xb, block_exp, n_used, w_gu, b_gu, p['w_down'], p['b_down'][:, None, :])


def _layer(x, c, ctx, c_ctx, p):
    batch, seq, d = x.shape
    ctx_len = ctx.shape[1]
    n_mod = p['w_ada'].shape[1] // d

    c_rows = jnp.zeros((2 * SUBLANES, d), F32).at[:batch].set(c).at[batch].set(c_ctx)
    mods = _ada(c_rows, p['w_ada'], p['b_ada']).reshape(2 * SUBLANES, n_mod, d)
    mods = jnp.pad(mods, ((0, 0), (0, SUBLANES - n_mod), (0, 0)))

    w_in = p['w_in'].astype(BF16)
    w_att, w_rwkv, w_gate = (w_in[:, :ATT_COLS], w_in[:, ATT_COLS:ATT_COLS + RWKV_COLS],
                             w_in[:, ATT_COLS + RWKV_COLS:])
    b_in = p['b_in'].reshape(1, -1)
    b_att, b_rwkv, b_gate = (b_in[:, :ATT_COLS], b_in[:, ATT_COLS:ATT_COLS + RWKV_COLS],
                             b_in[:, ATT_COLS + RWKV_COLS:])
    g_pre = p['g_pre_mix'].reshape(1, d)
    proj = functools.partial(_project, g_pre=g_pre, w_att=w_att, w_rwkv=w_rwkv, w_gate=w_gate,
                             b_att=b_att, b_rwkv=b_rwkv, b_gate=b_gate, feat_params=_feature_params(p))

    zc_att, _, fc = proj(ctx.reshape(batch * ctx_len, d), mods, lambda b: batch, ctx_len, rope=False)
    zero_state = jnp.zeros((batch, RWKV_HEADS, HEAD_DIM, HEAD_DIM), F32)
    _, _, sc_f, sc_b = _scan(fc, batch, ctx_len, zero_state, zero_state)

    x2 = x.reshape(batch * seq, d)
    z_att, z_gate, fx = proj(x2, mods, lambda b: b, seq, rope=True)
    y_f, y_b, _, _ = _scan(fx, batch, seq, sc_f, sc_b)
    att = _attention(z_att, zc_att, p['att_sinks'], batch, seq)
    x1, hf, idx, gt, rank, cnt = _merge(y_f, y_b, fx, att, z_gate, x2, mods, seq, p)

    loc, segments, n_chunks, tails, block_exp, n_used = _route_slots(
        idx[:, :TOP_K], rank[:, :TOP_K], cnt[:, 0, :N_EXPERTS], batch * seq)
    yb = _moe(hf, loc, segments, n_chunks, tails, block_exp, n_used, p)
    out = _combine(yb, loc, segments, n_chunks, gt, x1, mods, seq, p['g_post_ffn'])
    return out.reshape(batch, seq, d)


def kernel(x, c, ctx, c_ctx, w_ada, b_ada, g_pre_mix, g_post_mix, g_pre_ffn, g_post_ffn, w_in, b_in, mu_prev, mu_next, att_sinks, w0_f, w0_b, w2_f, w2_b, a0_f, a0_b, a2_f, a2_b, g2, k_k, k_a, r_k, ln_x_w, ln_x_b, w_up_att, w_up_rwkv, w_out, w_router, b_router, w_gate_up, b_gate_up, w_down, b_down):
    assert w_ada.shape[0] == 1, "single-layer problem: the context stream update is never consumed"
    p = dict(w_ada=w_ada[0], b_ada=b_ada[0], g_pre_mix=g_pre_mix[0], g_post_mix=g_post_mix[0],
             g_pre_ffn=g_pre_ffn[0], g_post_ffn=g_post_ffn[0], w_in=w_in[0], b_in=b_in[0],
             mu_prev=mu_prev[0], mu_next=mu_next[0], att_sinks=att_sinks[0], w0_f=w0_f[0], w0_b=w0_b[0],
             w2_f=w2_f[0], w2_b=w2_b[0], a0_f=a0_f[0], a0_b=a0_b[0], a2_f=a2_f[0], a2_b=a2_b[0], g2=g2[0],
             k_k=k_k[0], k_a=k_a[0], r_k=r_k[0].reshape(-1), ln_x_w=ln_x_w[0], ln_x_b=ln_x_b[0],
             w_up_att=w_up_att[0], w_up_rwkv=w_up_rwkv[0], w_out=w_out[0], w_router=w_router[0],
             b_router=b_router[0], w_gate_up=w_gate_up[0], b_gate_up=b_gate_up[0], w_down=w_down[0],
             b_down=b_down[0])
    return _layer(x, c, ctx, c_ctx, p)
```

```python
import functools

import jax
import jax.numpy as jnp
import numpy as np
from jax import lax
from jax.experimental import pallas as pl
from jax.experimental.pallas import tpu as pltpu

F32 = jnp.float32
BF16 = jnp.bfloat16

GRID_W = 64
HEAD_DIM = 64
ATT_HEADS = 8
ATT_KV_HEADS = 2
ATT_GROUPS = ATT_HEADS // ATT_KV_HEADS
ATT_DIM = ATT_HEADS * HEAD_DIM
KV_DIM = ATT_KV_HEADS * HEAD_DIM
WINDOW = 128
BLOCK = 128
ROPE_BASE = 10000.0
ATT_SCALE = HEAD_DIM ** -0.5
NEG_INF = -1e30
RWKV_HEADS = 8
RWKV_DIM = RWKV_HEADS * HEAD_DIM
DECAY_LORA = 64
ICL_LORA = 64
GATE_LORA = 128
GN_EPS = 64e-5
N_EXPERTS = 32
TOP_K = 4
SWIGLU_LIMIT = 7.0
SWIGLU_ALPHA = 1.702
RMS_EPS = 1e-6
ATT_COLS = ATT_DIM + 2 * KV_DIM
RWKV_COLS = 3 * RWKV_DIM + 2 * DECAY_LORA + 2 * ICL_LORA + GATE_LORA
ROPE_COLS = ATT_DIM + KV_DIM

LANES = 128
SUBLANES = 8
VMEM_LIMIT = 48 * 1024 * 1024

ROW_TILE = 256
SCAN_CHUNK = 64
SCAN_GROUP = 4
SCAN_BATCH = 4
MOE_ROWS = 512


def _cparams(*sem):
    return pltpu.CompilerParams(dimension_semantics=sem, vmem_limit_bytes=VMEM_LIMIT)


def _mm(a, b):
    return jnp.dot(a.astype(BF16), b.astype(BF16), preferred_element_type=F32)


def _mm_nt(a, b):
    return lax.dot_general(a.astype(BF16), b.astype(BF16), (((1,), (1,)), ((), ())),
                           preferred_element_type=F32)


def _split2(x):
    hi = x.astype(BF16)
    lo = (x - hi.astype(F32)).astype(BF16)
    return hi, lo


def _split3(x):
    hi = x.astype(BF16)
    r1 = x - hi.astype(F32)
    mid = r1.astype(BF16)
    lo = (r1 - mid.astype(F32)).astype(BF16)
    return hi, mid, lo


def _mm_f32(a, b):
    ah, al = _split2(a)
    bh, bl = _split2(b)
    d = functools.partial(jnp.dot, preferred_element_type=F32)
    return d(ah, bh) + d(ah, bl) + d(al, bh)


def _seg_sum(x, eseg):
    hi, lo = _split2(x)
    d = functools.partial(jnp.dot, preferred_element_type=F32)
    return d(hi, eseg) + d(lo, eseg)


def _rms(x):
    return x * lax.rsqrt(jnp.mean(x * x, axis=-1, keepdims=True) + RMS_EPS)


def _sigmoid(x):
    return 0.5 * jnp.tanh(0.5 * x) + 0.5


def _ada_kernel(c_ref, w_ref, b_ref, o_ref):
    c = c_ref[...]
    o_ref[...] = _mm_f32(c * _sigmoid(c), w_ref[...]) + b_ref[...]


def _ada(c_rows, w, b):
    m, d = c_rows.shape
    n = w.shape[1]
    tn = 1536
    return pl.pallas_call(
        _ada_kernel,
        grid=(n // tn,),
        in_specs=[pl.BlockSpec((m, d), lambda j: (0, 0)),
                  pl.BlockSpec((d, tn), lambda j: (0, j)),
                  pl.BlockSpec((1, tn), lambda j: (0, j))],
        out_specs=pl.BlockSpec((m, tn), lambda j: (0, j)),
        out_shape=jax.ShapeDtypeStruct((m, n), F32),
        compiler_params=_cparams("arbitrary"),
        name="ada",
    )(c_rows, w, b.reshape(1, n))


def _proj_kernel(*refs, rope, tiles_per_seq):
    n_rope = 2 if rope else 0
    (x_ref, xp_ref, xn_ref, mod_ref, g_ref, wa_ref, wr_ref, wg_ref, ba_ref, br_ref, bg_ref) = refs[:11]
    cos_ref, sin_ref = refs[11:11 + n_rope] if rope else (None, None)
    (mup_ref, mun_ref, w2_ref, w0_ref, a2_ref, a0_ref, g2_ref, kk_ref, ka_ref, eseg_ref) = refs[11 + n_rope:21 + n_rope]
    za_ref, zg_ref = refs[21 + n_rope:23 + n_rope]
    r_o, v_o, kf_o, kb_o, lwf_o, lwb_o, kkn_o, bf_o, bb_o, gate_o = refs[23 + n_rope:]
    mod = mod_ref[0]
    modulated = lambda xv: _rms(xv) * g_ref[...] * (1.0 + mod[1:2]) + mod[0:1]
    h = modulated(x_ref[...])
    hb = h.astype(BF16)
    za = jnp.dot(hb, wa_ref[...], preferred_element_type=F32) + ba_ref[...]
    if rope:
        qk = za[:, :ROPE_COLS]
        lane = lax.broadcasted_iota(jnp.int32, qk.shape, 1)
        low = (lane % 32) < 16
        partner = jnp.where(low, pltpu.roll(qk, ROPE_COLS - 16, 1), pltpu.roll(qk, 16, 1))
        za_ref[:, :ROPE_COLS] = (qk * cos_ref[...] + partner * sin_ref[...]).astype(za_ref.dtype)
        za_ref[:, ROPE_COLS:] = za[:, ROPE_COLS:].astype(za_ref.dtype)
    else:
        za_ref[...] = za.astype(za_ref.dtype)
    zg_ref[...] = (jnp.dot(hb, wg_ref[...], preferred_element_type=F32) + bg_ref[...]).astype(zg_ref.dtype)

    tm = x_ref.shape[0]
    h_ext = jnp.concatenate([modulated(xp_ref[...]), h, modulated(xn_ref[...])], axis=0).astype(BF16)
    z_ext = jnp.dot(h_ext, wr_ref[...], preferred_element_type=F32) + br_ref[...]
    z = z_ext[SUBLANES:SUBLANES + tm]
    ti = pl.program_id(0) % tiles_per_seq
    row = lax.broadcasted_iota(jnp.int32, (tm, 1), 0)
    prev_halo = jnp.where(ti == 0, 0.0, z_ext[SUBLANES - 1:SUBLANES])
    next_halo = jnp.where(ti == tiles_per_seq - 1, 0.0, z_ext[SUBLANES + tm:SUBLANES + tm + 1])
    prev = jnp.where(row == 0, prev_halo, pltpu.roll(z, 1, 0))
    nxt = jnp.where(row == tm - 1, next_halo, pltpu.roll(z, tm - 1, 0))
    zs = z + mup_ref[...] * (prev - z) + mun_ref[...] * (nxt - z)

    d = RWKV_DIM
    r = zs[:, 0:d]
    k = zs[:, d:2 * d]
    v = zs[:, 2 * d:3 * d]
    o = 3 * d
    wl = zs[:, o:o + 2 * DECAY_LORA]
    al = zs[:, o + 2 * DECAY_LORA:o + 2 * DECAY_LORA + 2 * ICL_LORA]
    gl = zs[:, o + 2 * DECAY_LORA + 2 * ICL_LORA:]

    w = w0_ref[...] + _mm(jnp.tanh(wl), w2_ref[...])
    lw = -float(np.exp(-0.5)) * _sigmoid(w)
    icl = _sigmoid(a0_ref[...] + _mm(al, a2_ref[...]))
    kk0 = k * kk_ref[...]
    ss = _seg_sum(kk0 * kk0, eseg_ref[...])
    kk = kk0 / jnp.maximum(jnp.sqrt(ss), 1e-12)
    ka = ka_ref[...]
    icl_f = icl[:, :d]
    icl_b = icl[:, d:]

    r_o[...] = r.astype(r_o.dtype)
    v_o[...] = v.astype(v_o.dtype)
    kf_o[...] = (k * (1.0 + (icl_f - 1.0) * ka)).astype(kf_o.dtype)
    kb_o[...] = (k * (1.0 + (icl_b - 1.0) * ka)).astype(kb_o.dtype)
    lwf_o[...] = lw[:, :d]
    lwb_o[...] = lw[:, d:]
    kkn_o[...] = kk.astype(kkn_o.dtype)
    bf_o[...] = (kk * icl_f).astype(bf_o.dtype)
    bb_o[...] = (kk * icl_b).astype(bb_o.dtype)
    gate_o[...] = _mm(_sigmoid(gl), g2_ref[...]).astype(gate_o.dtype)


def _rope_tables(seq):
    n_rows = seq // GRID_W
    row = jnp.repeat(jnp.arange(n_rows, dtype=F32), GRID_W, total_repeat_length=seq)
    col = jnp.tile(jnp.arange(GRID_W, dtype=F32), n_rows)
    half = HEAD_DIM // 2
    inv_freq = ROPE_BASE ** (-jnp.arange(0, half, 2, dtype=F32) / half)
    ang_r = row[:, None] * inv_freq[None, :]
    ang_c = col[:, None] * inv_freq[None, :]
    cos_h = jnp.concatenate([jnp.cos(ang_r), jnp.cos(ang_r), jnp.cos(ang_c), jnp.cos(ang_c)], axis=1)
    sin_h = jnp.concatenate([-jnp.sin(ang_r), jnp.sin(ang_r), -jnp.sin(ang_c), jnp.sin(ang_c)], axis=1)
    reps = ROPE_COLS // HEAD_DIM
    return jnp.tile(cos_h, (1, reps)), jnp.tile(sin_h, (1, reps))


def _project(x2, mods, mod_row, seq, g_pre, w_att, w_rwkv, w_gate, b_att, b_rwkv, b_gate, feat_params, rope):
    rows, d = x2.shape
    tm = ROW_TILE
    tps = seq // tm
    hb = tm // SUBLANES
    nhb = rows // SUBLANES
    rd = RWKV_DIM
    const = lambda i: (0, 0)
    in_specs = [pl.BlockSpec((tm, d), lambda i: (i, 0)),
                pl.BlockSpec((SUBLANES, d), lambda i: (jnp.maximum(i * hb - 1, 0), 0)),
                pl.BlockSpec((SUBLANES, d), lambda i: (jnp.minimum((i + 1) * hb, nhb - 1), 0)),
                pl.BlockSpec((1, SUBLANES, d), lambda i: (mod_row(i // tps), 0, 0)),
                pl.BlockSpec((1, d), const),
                pl.BlockSpec(w_att.shape, const), pl.BlockSpec(w_rwkv.shape, const),
                pl.BlockSpec(w_gate.shape, const),
                pl.BlockSpec((1, ATT_COLS), const), pl.BlockSpec((1, RWKV_COLS), const),
                pl.BlockSpec((1, w_gate.shape[1]), const)]
    args = [x2, x2, x2, mods, g_pre, w_att, w_rwkv, w_gate, b_att, b_rwkv, b_gate]
    if rope:
        cos, sin = _rope_tables(seq)
        in_specs += [pl.BlockSpec((tm, ROPE_COLS), lambda i: (i % tps, 0))] * 2
        args += [cos, sin]
    in_specs += [pl.BlockSpec(a.shape, const) for a in feat_params]
    args += feat_params
    feat_spec = pl.BlockSpec((tm, rd), lambda i: (i, 0))
    outs = pl.pallas_call(
        functools.partial(_proj_kernel, rope=rope, tiles_per_seq=tps),
        grid=(rows // tm,),
        in_specs=in_specs,
        out_specs=[pl.BlockSpec((tm, ATT_COLS), lambda i: (i, 0)),
                   pl.BlockSpec((tm, w_gate.shape[1]), lambda i: (i, 0))] + [feat_spec] * len(FEAT_NAMES),
        out_shape=[jax.ShapeDtypeStruct((rows, ATT_COLS), BF16),
                   jax.ShapeDtypeStruct((rows, w_gate.shape[1]), BF16)]
                  + [jax.ShapeDtypeStruct((rows, rd), BF16 if name in FEAT_BF16 else F32) for name in FEAT_NAMES],
        compiler_params=_cparams("parallel"),
        name="proj",
    )(*args)
    return outs[0], outs[1], dict(zip(FEAT_NAMES, outs[2:]))


FEAT_NAMES = ('r', 'v', 'kf', 'kb', 'lwf', 'lwb', 'kk', 'bf', 'bb', 'gate')
FEAT_BF16 = ('r', 'v', 'kf', 'kb', 'kk', 'bf', 'bb', 'gate')


def _block_diag2(a, b):
    za = jnp.zeros_like(a)
    zb = jnp.zeros_like(b)
    return jnp.concatenate([jnp.concatenate([a, zb], axis=1), jnp.concatenate([za, b], axis=1)], axis=0)


def _seg_matrix():
    h = np.arange(RWKV_DIM) // HEAD_DIM
    return jnp.asarray((h[:, None] == h[None, :]).astype(np.float32), BF16)


def _feature_params(p):
    d = RWKV_DIM
    return [p['mu_prev'].reshape(1, -1), p['mu_next'].reshape(1, -1),
            _block_diag2(p['w2_f'], p['w2_b']).astype(BF16),
            jnp.concatenate([p['w0_f'], p['w0_b']]).reshape(1, 2 * d),
            _block_diag2(p['a2_f'], p['a2_b']).astype(BF16),
            jnp.concatenate([p['a0_f'], p['a0_b']]).reshape(1, 2 * d),
            p['g2'].astype(BF16), p['k_k'].reshape(1, d), p['k_a'].reshape(1, d), _seg_matrix()]


def _scan_kernel(rf, vf, kf, lwf, kkf, bf, rb, vb, kb, lwb, kkb, bb, s0f_ref, s0b_ref,
                 yf_ref, yb_ref, sTf_ref, sTb_ref, state_ref, *, chunk, n_chunks):
    c = pl.program_id(1)
    C = chunk
    hd = HEAD_DIM
    gw = SCAN_GROUP * hd
    n_groups = RWKV_HEADS // SCAN_GROUP
    assert C == hd, "the triangular masks below are shared between time and channel blocks"

    n_rows = state_ref.shape[0]

    @pl.when(c == 0)
    def _():
        for n in range(n_rows):
            for d, s0_ref in enumerate((s0f_ref, s0b_ref)):
                for g in range(n_groups):
                    state_ref[n, d, g] = jnp.zeros((gw, gw), F32)
                    for j in range(SCAN_GROUP):
                        state_ref[n, d, g, j * hd:(j + 1) * hd, j * hd:(j + 1) * hd] = (
                            s0_ref[n, g * SCAN_GROUP + j])

    ti = lax.broadcasted_iota(jnp.int32, (C, C), 0)
    si = lax.broadcasted_iota(jnp.int32, (C, C), 1)
    tg = lax.broadcasted_iota(jnp.int32, (C, gw), 0)
    sg = lax.broadcasted_iota(jnp.int32, (C, gw), 1) % C
    eye = (tg == sg).astype(F32)
    same_head = (lax.broadcasted_iota(jnp.int32, (gw, gw), 0) // hd
                 == lax.broadcasted_iota(jnp.int32, (gw, gw), 1) // hd)
    same_head_b = same_head.astype(BF16)
    n_double = int(np.log2(C)) - 1

    def bdiag(x_cat):
        return jnp.concatenate([x_cat.astype(BF16)] * SCAN_GROUP, axis=0) * same_head_b

    dirs = ((rf, vf, kf, lwf, kkf, bf), (rb, vb, kb, lwb, kkb, bb))
    units = []
    masks = [(((si <= ti) if d == 0 else (si >= ti)).astype(BF16),
              (sg <= tg) if d == 0 else (sg >= tg),
              (sg < tg) if d == 0 else (sg > tg)) for d in range(2)]
    for n, d in [(n, d) for n in range(n_rows) for d in range(2)]:
        r_ref, v_ref, k_ref, lw_ref, kk_ref, b_ref = (ref.at[n] for ref in dirs[d])
        tri, incl_g, strict_g = masks[d]
        lw = lw_ref[...]
        lh, lm, ll = _split3(lw)
        dd = functools.partial(jnp.dot, preferred_element_type=F32)
        cum = dd(tri, lh) + dd(tri, lm) + dd(tri, ll)
        cumx = cum - lw
        cum_end = cum[C - 1:C] if d == 0 else cum[0:1]
        e_neg = jnp.exp(-cum)
        e_end = jnp.exp(cum_end - cum)
        bv = b_ref[...].astype(F32)
        kv = k_ref[...].astype(F32)
        vv = v_ref[...].astype(F32)
        a_t = (-kk_ref[...].astype(F32) * jnp.exp(cumx)).astype(BF16)
        r_t = (r_ref[...].astype(F32) * jnp.exp(cum)).astype(BF16)
        b_t = (bv * e_neg).astype(BF16)
        k_t = (kv * e_neg).astype(BF16)
        b_q = (bv * e_end).astype(BF16)
        k_q = (kv * e_end).astype(BF16)
        g_end = jnp.exp(cum_end)
        for g in range(n_groups):
            sl = slice(g * gw, (g + 1) * gw)
            units.append(dict(
                n=n, d=d, g=g, sl=sl, incl=incl_g, strict=strict_g,
                P=jnp.concatenate([a_t[:, sl], r_t[:, sl]], axis=0),
                Q=jnp.concatenate([bdiag(b_t[:, sl]), bdiag(k_t[:, sl])], axis=0),
                Qq=jnp.concatenate([b_q[:, sl], k_q[:, sl]], axis=0),
                V=vv[:, sl], g_end=g_end[:, sl]))

    for u in units:
        u['G'] = _mm_nt(u['P'], u['Q'])
    for u in units:
        u['S0'] = state_ref[u['n'], u['d'], u['g']]
        u['PH'] = _mm_nt(u['P'], u['S0'])
        u['Vd'] = bdiag(u['V'])
    for u in units:
        G = u.pop('G')
        u['N'] = jnp.where(u['strict'], G[:C, :gw], 0.0)
        u['a_ak'] = jnp.where(u['strict'], G[:C, gw:], 0.0)
        u['a_rb'] = jnp.where(u['incl'], G[C:, :gw], 0.0)
        u['a_rk'] = jnp.where(u['incl'], G[C:, gw:], 0.0)
    for u in units:
        u['T'] = eye + u['N']
        u['Pw'] = _mm(u['N'], bdiag(u['N']))
        av = _mm(jnp.concatenate([u['a_ak'], u['a_rk']], axis=0), u['Vd'])
        u['rhs'] = u['PH'][:C] + av[:C]
        u['y0'] = u['PH'][C:] + av[C:]
    for lvl in range(n_double):
        for u in units:
            pw = bdiag(u['Pw'])
            if lvl + 1 < n_double:
                both = _mm(jnp.concatenate([u['T'], u['Pw']], axis=0), pw)
                u['T'] = u['T'] + both[:C]
                u['Pw'] = both[C:]
            else:
                u['T'] = u['T'] + _mm(u['T'], pw)
    for u in units:
        u['U'] = _mm(u['T'], bdiag(u['rhs']))
    y_refs = (yf_ref, yb_ref)
    for u in units:
        y_refs[u['d']][u['n'], :, u['sl']] = u['y0'] + _mm(u['a_rb'], bdiag(u['U']))
        uv = jnp.concatenate([u['U'], u['V']], axis=0)
        state_ref[u['n'], u['d'], u['g']] = jnp.where(
            same_head, u['S0'] * u['g_end'] + _mm(uv.T, u['Qq']), 0.0)

    @pl.when(c == n_chunks - 1)
    def _():
        for n in range(n_rows):
            for d, sT_ref in enumerate((sTf_ref, sTb_ref)):
                for g in range(n_groups):
                    for j in range(SCAN_GROUP):
                        sT_ref[n, g * SCAN_GROUP + j] = (
                            state_ref[n, d, g, j * hd:(j + 1) * hd, j * hd:(j + 1) * hd])


def _scan(f, batch, seq, s0_f, s0_b):
    C = SCAN_CHUNK
    nC = seq // C
    d = RWKV_DIM
    nb = max(n for n in range(1, SCAN_BATCH + 1) if batch % n == 0)
    fwd = pl.BlockSpec((nb, C, d), lambda b, c: (b, c, 0))
    bwd = pl.BlockSpec((nb, C, d), lambda b, c: (b, nC - 1 - c, 0))
    st = pl.BlockSpec((nb, RWKV_HEADS, HEAD_DIM, HEAD_DIM), lambda b, c: (b, 0, 0, 0))
    st_shape = jax.ShapeDtypeStruct((batch, RWKV_HEADS, HEAD_DIM, HEAD_DIM), F32)
    y_shape = jax.ShapeDtypeStruct((batch, seq, d), F32)
    gw = SCAN_GROUP * HEAD_DIM
    arr = lambda name: f[name].reshape(batch, seq, d)
    y_f, y_b, sT_f, sT_b = pl.pallas_call(
        functools.partial(_scan_kernel, chunk=C, n_chunks=nC),
        grid=(batch // nb, nC),
        in_specs=[fwd] * 6 + [bwd] * 6 + [st, st],
        out_specs=[fwd, bwd, st, st],
        out_shape=[y_shape, y_shape, st_shape, st_shape],
        scratch_shapes=[pltpu.VMEM((nb, 2, RWKV_HEADS // SCAN_GROUP, gw, gw), F32)],
        compiler_params=_cparams("parallel", "arbitrary"),
        name="scan",
    )(arr('r'), arr('v'), arr('kf'), arr('lwf'), arr('kk'), arr('bf'),
      arr('r'), arr('v'), arr('kb'), arr('lwb'), arr('kk'), arr('bb'), s0_f, s0_b)
    return y_f.reshape(batch * seq, d), y_b.reshape(batch * seq, d), sT_f, sT_b


def _attn_kernel(sink_ref, q_ref, kp_ref, km_ref, kn_ref, vp_ref, vm_ref, vn_ref, kc_ref, vc_ref, o_ref,
                 *, n_blocks):
    n = pl.program_id(1)
    rows = ATT_GROUPS * BLOCK
    q = q_ref[...] * ATT_SCALE
    qi = lax.broadcasted_iota(jnp.int32, (rows, BLOCK), 0) % BLOCK
    kj = lax.broadcasted_iota(jnp.int32, (rows, BLOCK), 1)
    ok_prev = (kj >= qi) & (n > 0)
    ok_next = (kj <= qi) & (n < n_blocks - 1)
    rowh = lax.broadcasted_iota(jnp.int32, (rows, 1), 0) // BLOCK
    outs = [None] * ATT_HEADS
    for g in range(ATT_KV_HEADS):
        ks = slice(g * HEAD_DIM, (g + 1) * HEAD_DIM)
        heads = [g * ATT_GROUPS + j for j in range(ATT_GROUPS)]
        Qs = jnp.concatenate([q[:, h * HEAD_DIM:(h + 1) * HEAD_DIM] for h in heads], axis=0)
        scores = [jnp.where(ok_prev, _mm_nt(Qs, kp_ref[:, ks]), NEG_INF),
                  _mm_nt(Qs, km_ref[:, ks]),
                  jnp.where(ok_next, _mm_nt(Qs, kn_ref[:, ks]), NEG_INF),
                  _mm_nt(Qs, kc_ref[:, ks])]
        values = [vp_ref, vm_ref, vn_ref, vc_ref]
        sink = jnp.zeros((rows, 1), F32)
        for j, h in enumerate(heads):
            sink = jnp.where(rowh == j, sink_ref[h], sink)
        folded = None
        for sc in scores:
            for c0 in range(0, sc.shape[1], BLOCK):
                blk = sc[:, c0:c0 + BLOCK]
                folded = blk if folded is None else jnp.maximum(folded, blk)
        m = jnp.maximum(sink, jnp.max(folded, axis=-1, keepdims=True))
        acc = jnp.zeros((rows, 2 * HEAD_DIM), F32)
        for sc, v_ref in zip(scores, values):
            one_col = (lax.broadcasted_iota(jnp.int32, (v_ref.shape[0], HEAD_DIM), 1) == 0).astype(BF16)
            v_ext = jnp.concatenate([v_ref[:, ks], one_col], axis=1)
            acc = acc + jnp.dot(jnp.exp((sc - m).astype(BF16)), v_ext, preferred_element_type=F32)
        den = acc[:, HEAD_DIM:HEAD_DIM + 1] + jnp.exp(sink - m)
        O = acc[:, :HEAD_DIM] / den
        for j, h in enumerate(heads):
            outs[h] = O[j * BLOCK:(j + 1) * BLOCK]
    o_ref[...] = jnp.concatenate(outs, axis=1).astype(o_ref.dtype)


def _attention(z_att, zc_att, sinks, batch, seq):
    nb = seq // BLOCK
    ctx_len = zc_att.shape[0] // batch
    kcol = ATT_DIM // KV_DIM
    vcol = kcol + 1

    def kv_spec(col, off):
        return pl.BlockSpec((BLOCK, KV_DIM), lambda b, n: (b * nb + jnp.clip(n + off, 0, nb - 1), col))

    return pl.pallas_call(
        functools.partial(_attn_kernel, n_blocks=nb),
        grid=(batch, nb),
        in_specs=[pl.BlockSpec(memory_space=pltpu.SMEM),
                  pl.BlockSpec((BLOCK, ATT_DIM), lambda b, n: (b * nb + n, 0)),
                  kv_spec(kcol, -1), kv_spec(kcol, 0), kv_spec(kcol, 1),
                  kv_spec(vcol, -1), kv_spec(vcol, 0), kv_spec(vcol, 1),
                  pl.BlockSpec((ctx_len, KV_DIM), lambda b, n: (b, kcol)),
                  pl.BlockSpec((ctx_len, KV_DIM), lambda b, n: (b, vcol))],
        out_specs=pl.BlockSpec((BLOCK, ATT_DIM), lambda b, n: (b * nb + n, 0)),
        out_shape=jax.ShapeDtypeStruct((batch * seq, ATT_DIM), BF16),
        compiler_params=_cparams("parallel", "parallel"),
        name="attn",
    )(sinks, z_att, z_att, z_att, z_att, z_att, z_att, z_att, zc_att, zc_att)


def _merge_kernel(yf_ref, yb_ref, r_ref, kf_ref, kb_ref, v_ref, gate_ref, att_ref, zg_ref, x_ref, mod_ref,
                  lnw_ref, lnb_ref, rk_ref, eseg_ref, wua_ref, wur_ref, wo_ref, gpm_ref, gpf_ref,
                  wr_ref, br_ref,
                  x1_ref, hf_ref, gt_ref, loc_ref, cnt_ref):
    eseg = eseg_ref[...]
    inv_n = 1.0 / HEAD_DIM
    y = yf_ref[...] + yb_ref[...]
    mean = _seg_sum(y, eseg) * inv_n
    dy = y - mean
    var = _seg_sum(dy * dy, eseg) * inv_n
    yn = dy * lax.rsqrt(var + GN_EPS) * lnw_ref[...] + lnb_ref[...]
    f32 = lambda ref: ref[...].astype(F32)
    bonus = _seg_sum(f32(r_ref) * (f32(kf_ref) + f32(kb_ref)) * rk_ref[...], eseg) * f32(v_ref)
    rwk = (yn + bonus) * f32(gate_ref)

    d = x_ref.shape[1]
    zg = zg_ref[...].astype(F32)
    merged = (_sigmoid(zg[:, :d]) * jnp.dot(att_ref[...], wua_ref[...], preferred_element_type=F32)
              + _sigmoid(zg[:, d:]) * _mm(rwk, wur_ref[...]))
    mix = _mm(merged, wo_ref[...])
    mod = mod_ref[0]
    x1 = x_ref[...] + mod[2:3] * (_rms(mix) * gpm_ref[...])
    x1_ref[...] = x1
    hf = _rms(x1) * gpf_ref[...] * (1.0 + mod[4:5]) + mod[3:4]
    hf_ref[...] = hf.astype(hf_ref.dtype)

    logits = _mm_f32(hf, wr_ref[...]) + br_ref[...]
    lane = lax.broadcasted_iota(jnp.int32, logits.shape, 1).astype(F32)
    val_out = jnp.zeros(logits.shape, F32)
    picked = jnp.zeros(logits.shape, F32)
    sels = []
    top = None
    den = None
    for kth in range(TOP_K):
        m = jnp.max(logits, axis=-1, keepdims=True)
        sel = jnp.min(jnp.where(logits == m, lane, float(LANES)), axis=-1, keepdims=True)
        hit = lane == sel
        logits = jnp.where(hit, -jnp.inf, logits)
        picked = jnp.where(hit, 1.0, picked)
        sels.append(hit)
        if kth == 0:
            top = m
        e = jnp.exp(m - top)
        den = e if kth == 0 else den + e
        val_out = jnp.where(lane == float(kth), e, val_out)
    gt_ref[...] = val_out / den

    tm = picked.shape[0]
    earlier = (lax.broadcasted_iota(jnp.int32, (tm, tm), 1)
               < lax.broadcasted_iota(jnp.int32, (tm, tm), 0)).astype(BF16)
    before = jnp.dot(earlier, picked.astype(BF16), preferred_element_type=F32)
    counts = jnp.sum(picked, axis=0, keepdims=True)
    seg_len = jnp.floor((counts + (SUBLANES - 1)) * (1.0 / SUBLANES)) * SUBLANES
    lower_expert = (lax.broadcasted_iota(jnp.int32, (LANES, LANES), 0)
                    < lax.broadcasted_iota(jnp.int32, (LANES, LANES), 1)).astype(BF16)
    seg_off = jnp.dot(jnp.broadcast_to(seg_len, (SUBLANES, LANES)).astype(BF16), lower_expert,
                      preferred_element_type=F32)[0:1]
    loc_out = jnp.zeros(logits.shape, F32)
    for kth, hit in enumerate(sels):
        row = jnp.sum(jnp.where(hit, before + seg_off, 0.0), axis=-1, keepdims=True)
        loc_out = jnp.where(lane == float(kth), row, loc_out)
    loc_ref[...] = loc_out.astype(jnp.int32)
    cnt_ref[0] = jnp.broadcast_to(counts, cnt_ref.shape[1:]).astype(jnp.int32)


def _moe_blocks(n_tok, n_tiles):
    return -(-(n_tok * TOP_K + (SUBLANES - 1) * N_EXPERTS * n_tiles) // MOE_ROWS) + N_EXPERTS


def _merge(y_f, y_b, f, att, z_gate, x2, mods, seq, p):
    rows, d = x2.shape
    tm = ROW_TILE
    tps = seq // tm
    rd = RWKV_DIM
    const = lambda i: (0, 0)
    row = lambda w: pl.BlockSpec((tm, w), lambda i: (i, 0))
    vec = lambda w: pl.BlockSpec((1, w), const)
    w_router = jnp.pad(p['w_router'], ((0, 0), (0, LANES - N_EXPERTS)))
    b_router = jnp.pad(p['b_router'], (0, LANES - N_EXPERTS), constant_values=NEG_INF).reshape(1, LANES)
    return pl.pallas_call(
        _merge_kernel,
        grid=(rows // tm,),
        in_specs=[row(rd)] * 7 + [row(ATT_DIM), row(2 * d), row(d),
                  pl.BlockSpec((1, SUBLANES, d), lambda i: (i // tps, 0, 0)),
                  vec(rd), vec(rd), vec(rd), pl.BlockSpec((rd, rd), const),
                  pl.BlockSpec((ATT_DIM, d), const), pl.BlockSpec((rd, d), const), pl.BlockSpec((d, d), const),
                  vec(d), vec(d), pl.BlockSpec((d, LANES), const), vec(LANES)],
        out_specs=[row(d), row(d), row(LANES), row(LANES),
                   pl.BlockSpec((1, SUBLANES, LANES), lambda i: (i, 0, 0))],
        out_shape=[jax.ShapeDtypeStruct((rows, d), F32), jax.ShapeDtypeStruct((rows, d), BF16),
                   jax.ShapeDtypeStruct((rows, LANES), F32),
                   jax.ShapeDtypeStruct((rows, LANES), jnp.int32),
                   jax.ShapeDtypeStruct((rows // tm, SUBLANES, LANES), jnp.int32)],
        compiler_params=_cparams("parallel"),
        name="merge",
    )(y_f, y_b, f['r'], f['kf'], f['kb'], f['v'], f['gate'], att, z_gate, x2, mods,
      p['ln_x_w'].reshape(1, rd), p['ln_x_b'].reshape(1, rd), p['r_k'].reshape(1, rd), _seg_matrix(),
      p['w_up_att'].astype(BF16), p['w_up_rwkv'].astype(BF16), p['w_out'].astype(BF16),
      p['g_post_mix'].reshape(1, d), p['g_pre_ffn'].reshape(1, d), w_router, b_router)


def _moe_kernel(be_ref, nused_ref, x_ref, wgu_ref, bgu_ref, wdn_f32_ref, bdn_ref, o_ref, wdn_ref):
    i = pl.program_id(0)

    @pl.when((i == 0) | (be_ref[i] != be_ref[jnp.maximum(i - 1, 0)]))
    def _():
        wdn_ref[...] = wdn_f32_ref[...].astype(BF16)

    @pl.when(i < nused_ref[0])
    def _():
        de = wdn_ref.shape[0]
        d = x_ref.shape[1]
        x = x_ref[...].astype(BF16)
        bgu = bgu_ref[...]
        gate = jnp.minimum(_mm_nt(x, wgu_ref[:, :d]) + bgu[:, :de], SWIGLU_LIMIT)
        up = jnp.clip(_mm_nt(x, wgu_ref[:, d:]) + bgu[:, de:], -SWIGLU_LIMIT, SWIGLU_LIMIT)
        act = (up + 1.0) * (gate * _sigmoid(SWIGLU_ALPHA * gate))
        o_ref[...] = _mm(act, wdn_ref[...]) + bdn_ref[...]

    @pl.when(i >= nused_ref[0])
    def _():
        o_ref[...] = jnp.zeros(o_ref.shape, o_ref.dtype)


def _wprep_kernel(w_ref, o_ref, t_ref):
    d = w_ref.shape[1]
    half = o_ref.shape[1]
    wt = w_ref[0].T
    for c in range(d // LANES):
        cols = slice(c * LANES, (c + 1) * LANES)
        t_ref[c] = wt[:, cols]
        o_ref[0, :, cols] = t_ref[c, pl.ds(0, half, stride=2), :].astype(o_ref.dtype)
        o_ref[0, :, d + c * LANES:d + (c + 1) * LANES] = t_ref[c, pl.ds(1, half, stride=2), :].astype(o_ref.dtype)


def _gate_up_rows(w_gu):
    n_exp, d, de2 = w_gu.shape
    tc = 4 * LANES
    return pl.pallas_call(
        _wprep_kernel,
        grid=(n_exp, de2 // tc),
        in_specs=[pl.BlockSpec((1, d, tc), lambda e, j: (e, 0, j))],
        out_specs=pl.BlockSpec((1, tc // 2, 2 * d), lambda e, j: (e, j, 0)),
        out_shape=jax.ShapeDtypeStruct((n_exp, de2 // 2, 2 * d), BF16),
        scratch_shapes=[pltpu.VMEM((d // LANES, tc, LANES), F32)],
        compiler_params=_cparams("parallel", "parallel"),
        name="wprep",
    )(w_gu)


def _experts(xb, block_exp, n_used, w_gu, b_gu, w_dn, b_dn):
    n_slots, d = xb.shape
    bm = MOE_ROWS
    n_blocks = n_slots // bm
    de = w_dn.shape[1]
    de2 = 2 * de
    grid_spec = pltpu.PrefetchScalarGridSpec(
        num_scalar_prefetch=2,
        grid=(n_blocks,),
        in_specs=[pl.BlockSpec((bm, d), lambda i, be, nu: (jnp.minimum(i, nu[0] - 1), 0)),
                  pl.BlockSpec((None, de, 2 * d), lambda i, be, nu: (be[i], 0, 0)),
                  pl.BlockSpec((None, 1, de2), lambda i, be, nu: (be[i], 0, 0)),
                  pl.BlockSpec((None, de, d), lambda i, be, nu: (be[i], 0, 0)),
                  pl.BlockSpec((None, 1, d), lambda i, be, nu: (be[i], 0, 0))],
        out_specs=pl.BlockSpec((bm, d), lambda i, be, nu: (i, 0)),
        scratch_shapes=[pltpu.VMEM((de, d), BF16)],
    )
    return pl.pallas_call(
        _moe_kernel,
        grid_spec=grid_spec,
        out_shape=jax.ShapeDtypeStruct((n_slots, d), F32),
        compiler_params=_cparams("arbitrary"),
        name="moe",
    )(block_exp, n_used, xb, w_gu, b_gu, w_dn, b_dn)


def _local_rows(tm):
    rows = tm * TOP_K + (SUBLANES - 1) * N_EXPERTS
    assert rows % SUBLANES == 0
    return rows


def _pick_matrix(loc, weights, n_cols):
    col = lax.broadcasted_iota(jnp.int32, (loc.shape[0], n_cols), 1)
    out = jnp.zeros(col.shape, F32)
    for k in range(TOP_K):
        out = jnp.where(col == loc[:, k:k + 1], weights[k], out)
    return out


def _segment(seg_ref, tile, e):
    base = (tile * N_EXPERTS + e) * 3
    return tuple(pl.multiple_of(seg_ref[base + i], SUBLANES) for i in range(3))


def _dispatch_kernel(seg_ref, nch_ref, tail_ref, hf_ref, loc_ref, xb_ref, xc_ref, zero_ref, sem, zsem):
    j = pl.program_id(0)
    n_tiles = pl.num_programs(0)
    slot = j % 2

    def tail_copy(e):
        row = pl.multiple_of(tail_ref[2 * e], SUBLANES)
        n = pl.multiple_of(tail_ref[2 * e + 1], SUBLANES)
        return n, pltpu.make_async_copy(zero_ref.at[pl.ds(0, n), :], xb_ref.at[pl.ds(row, n), :], zsem)

    def spare_block_copy(b):
        rows = xb_ref.at[pl.ds(pl.multiple_of(b * MOE_ROWS, MOE_ROWS), MOE_ROWS), :]
        return pltpu.make_async_copy(zero_ref, rows, zsem)

    first_spare = tail_ref[2 * N_EXPERTS]
    n_blocks = xb_ref.shape[0] // MOE_ROWS

    @pl.when(j == 0)
    def _():
        zero_ref[...] = jnp.zeros(zero_ref.shape, zero_ref.dtype)

        def body(e, carry):
            n, cp = tail_copy(e)

            @pl.when(n > 0)
            def _():
                cp.start()
            return carry
        lax.fori_loop(0, N_EXPERTS, body, 0)

        def spare(b, carry):
            spare_block_copy(b).start()
            return carry
        lax.fori_loop(first_spare, n_blocks, spare, 0)

    def drain(tile, s):
        rows = pl.multiple_of(nch_ref[tile] * SUBLANES, SUBLANES)
        done = xc_ref.at[s, pl.ds(0, rows), :]
        pltpu.make_async_copy(done, done, sem.at[s]).wait()

    sel = _pick_matrix(loc_ref[...], [1.0] * TOP_K, xc_ref.shape[1]).astype(BF16)
    xc_ref[slot] = lax.dot_general(sel, hf_ref[...], (((0,), (0,)), ((), ())), preferred_element_type=F32)

    def start(e, carry):
        off, row, n = _segment(seg_ref, j, e)

        @pl.when(n > 0)
        def _():
            pltpu.make_async_copy(xc_ref.at[slot, pl.ds(off, n), :], xb_ref.at[pl.ds(row, n), :],
                                  sem.at[slot]).start()
        return carry
    lax.fori_loop(0, N_EXPERTS, start, 0)

    @pl.when(j > 0)
    def _():
        drain(j - 1, 1 - slot)

    @pl.when(j == n_tiles - 1)
    def _():
        drain(j, slot)

        def body(e, carry):
            n, cp = tail_copy(e)

            @pl.when(n > 0)
            def _():
                cp.wait()
            return carry
        lax.fori_loop(0, N_EXPERTS, body, 0)

        def spare(b, carry):
            spare_block_copy(b).wait()
            return carry
        lax.fori_loop(first_spare, n_blocks, spare, 0)


def _dispatch(hf, loc, segments, nch, tails):
    rows, d = hf.shape
    tm = ROW_TILE
    n_slots = _moe_blocks(rows, rows // tm) * MOE_ROWS
    grid_spec = pltpu.PrefetchScalarGridSpec(
        num_scalar_prefetch=3,
        grid=(rows // tm,),
        in_specs=[pl.BlockSpec((tm, d), lambda i, sg, nc, tl: (i, 0)),
                  pl.BlockSpec((tm, LANES), lambda i, sg, nc, tl: (i, 0))],
        out_specs=pl.BlockSpec(memory_space=pl.ANY),
        scratch_shapes=[pltpu.VMEM((2, _local_rows(tm), d), F32), pltpu.VMEM((MOE_ROWS, d), F32),
                        pltpu.SemaphoreType.DMA((2,)), pltpu.SemaphoreType.DMA(())],
    )
    return pl.pallas_call(
        _dispatch_kernel,
        grid_spec=grid_spec,
        out_shape=jax.ShapeDtypeStruct((n_slots, d), F32),
        compiler_params=_cparams("arbitrary"),
        name="dispatch",
    )(segments, nch, tails, hf, loc)


def _combine_kernel(seg_ref, nch_ref, yb_ref, loc_ref, gt_ref, x1_ref, mod_ref, g_ref, o_ref, yc_ref, sem):
    j = pl.program_id(0)
    n_tiles = pl.num_programs(0)

    def issue(tile, s):
        def body(e, carry):
            off, row, n = _segment(seg_ref, tile, e)

            @pl.when(n > 0)
            def _():
                pltpu.make_async_copy(yb_ref.at[pl.ds(row, n), :], yc_ref.at[s, pl.ds(off, n), :],
                                      sem.at[s]).start()
            return carry
        lax.fori_loop(0, N_EXPERTS, body, 0)

    @pl.when(j == 0)
    def _():
        yc_ref[...] = jnp.zeros(yc_ref.shape, yc_ref.dtype)
        issue(0, 0)

    @pl.when(j + 1 < n_tiles)
    def _():
        issue(j + 1, (j + 1) % 2)

    slot = j % 2

    rows = pl.multiple_of(nch_ref[j] * SUBLANES, SUBLANES)
    done = yc_ref.at[slot, pl.ds(0, rows), :]
    pltpu.make_async_copy(done, done, sem.at[slot]).wait()

    gt = gt_ref[...]
    w = _pick_matrix(loc_ref[...], [gt[:, k:k + 1] for k in range(TOP_K)], yc_ref.shape[1])
    y = _mm(w, yc_ref[slot])
    mod = mod_ref[0]
    o_ref[...] = x1_ref[...] + mod[5:6] * (_rms(y) * g_ref[...])


def _combine(yb, loc, segments, nch, gt, x1, mods, seq, g_post):
    rows, d = x1.shape
    tm = ROW_TILE
    tps = seq // tm
    row = lambda w: pl.BlockSpec((tm, w), lambda i, gd, nc: (i, 0))
    grid_spec = pltpu.PrefetchScalarGridSpec(
        num_scalar_prefetch=2,
        grid=(rows // tm,),
        in_specs=[pl.BlockSpec(memory_space=pl.ANY), row(LANES), row(LANES), row(d),
                  pl.BlockSpec((1, SUBLANES, d), lambda i, gd, nc: (i // tps, 0, 0)),
                  pl.BlockSpec((1, d), lambda i, gd, nc: (0, 0))],
        out_specs=row(d),
        scratch_shapes=[pltpu.VMEM((2, _local_rows(tm), d), F32), pltpu.SemaphoreType.DMA((2,))],
    )
    return pl.pallas_call(
        _combine_kernel,
        grid_spec=grid_spec,
        out_shape=jax.ShapeDtypeStruct((rows, d), F32),
        compiler_params=_cparams("arbitrary"),
        name="combine",
    )(segments, nch, yb, loc, gt, x1, mods, g_post.reshape(1, d))


def _route_slots(tile_counts, n_tok):
    bm = MOE_ROWS
    n_tiles = tile_counts.shape[0]
    seg_len = -(-tile_counts // SUBLANES) * SUBLANES
    seg_off = jnp.cumsum(seg_len, axis=1) - seg_len
    per_expert = jnp.sum(seg_len, axis=0)
    padded = -(-per_expert // bm) * bm
    p_end = jnp.cumsum(padded)
    seg_slot = (p_end - padded)[None, :] + jnp.cumsum(seg_len, axis=0) - seg_len
    segments = jnp.stack([seg_off, seg_slot, seg_len], axis=-1).reshape(-1).astype(jnp.int32)
    tails = jnp.stack([p_end - padded + per_expert, padded - per_expert], axis=-1).reshape(-1)
    tails = jnp.concatenate([tails, p_end[-1:] // bm]).astype(jnp.int32)
    n_chunks = jnp.sum(seg_len, axis=1) // SUBLANES
    starts = jnp.arange(_moe_blocks(n_tok, n_tiles), dtype=jnp.int32) * bm
    block_exp = jnp.minimum(jnp.sum(p_end[None, :] <= starts[:, None], axis=1), N_EXPERTS - 1).astype(jnp.int32)
    n_used = (p_end[-1] // bm).astype(jnp.int32).reshape(1)
    return segments, n_chunks.astype(jnp.int32), tails, block_exp, n_used


def _moe(hf, loc, segments, n_chunks, tails, block_exp, n_used, p):
    xb = _dispatch(hf, loc, segments, n_chunks, tails)
    w_gu = _gate_up_rows(p['w_gate_up'])
    b_gu = p['b_gate_up']
    b_gu = jnp.concatenate([b_gu[..., 0::2], b_gu[..., 1::2]], axis=-1)[:, None, :]
    return _experts(xb, block_exp, n_used, w_gu, b_gu, p['w_down'], p['b_down'][:, None, :])


def _layer(x, c, ctx, c_ctx, p):
    batch, seq, d = x.shape
    ctx_len = ctx.shape[1]
    n_mod = p['w_ada'].shape[1] // d

    c_rows = jnp.zeros((2 * SUBLANES, d), F32).at[:batch].set(c).at[batch].set(c_ctx)
    mods = _ada(c_rows, p['w_ada'], p['b_ada']).reshape(2 * SUBLANES, n_mod, d)
    mods = jnp.pad(mods, ((0, 0), (0, SUBLANES - n_mod), (0, 0)))

    w_in = p['w_in'].astype(BF16)
    w_att, w_rwkv, w_gate = (w_in[:, :ATT_COLS], w_in[:, ATT_COLS:ATT_COLS + RWKV_COLS],
                             w_in[:, ATT_COLS + RWKV_COLS:])
    b_in = p['b_in'].reshape(1, -1)
    b_att, b_rwkv, b_gate = (b_in[:, :ATT_COLS], b_in[:, ATT_COLS:ATT_COLS + RWKV_COLS],
                             b_in[:, ATT_COLS + RWKV_COLS:])
    g_pre = p['g_pre_mix'].reshape(1, d)
    proj = functools.partial(_project, g_pre=g_pre, w_att=w_att, w_rwkv=w_rwkv, w_gate=w_gate,
                             b_att=b_att, b_rwkv=b_rwkv, b_gate=b_gate, feat_params=_feature_params(p))

    zc_att, _, fc = proj(ctx.reshape(batch * ctx_len, d), mods, lambda b: batch, ctx_len, rope=False)
    zero_state = jnp.zeros((batch, RWKV_HEADS, HEAD_DIM, HEAD_DIM), F32)
    _, _, sc_f, sc_b = _scan(fc, batch, ctx_len, zero_state, zero_state)

    x2 = x.reshape(batch * seq, d)
    z_att, z_gate, fx = proj(x2, mods, lambda b: b, seq, rope=True)
    y_f, y_b, _, _ = _scan(fx, batch, seq, sc_f, sc_b)
    att = _attention(z_att, zc_att, p['att_sinks'], batch, seq)
    x1, hf, gt, loc, cnt = _merge(y_f, y_b, fx, att, z_gate, x2, mods, seq, p)

    segments, n_chunks, tails, block_exp, n_used = _route_slots(cnt[:, 0, :N_EXPERTS], batch * seq)
    yb = _moe(hf, loc, segments, n_chunks, tails, block_exp, n_used, p)
    out = _combine(yb, loc, segments, n_chunks, gt, x1, mods, seq, p['g_post_ffn'])
    return out.reshape(batch, seq, d)


def kernel(x, c, ctx, c_ctx, w_ada, b_ada, g_pre_mix, g_post_mix, g_pre_ffn, g_post_ffn, w_in, b_in, mu_prev, mu_next, att_sinks, w0_f, w0_b, w2_f, w2_b, a0_f, a0_b, a2_f, a2_b, g2, k_k, k_a, r_k, ln_x_w, ln_x_b, w_up_att, w_up_rwkv, w_out, w_router, b_router, w_gate_up, b_gate_up, w_down, b_down):
    assert w_ada.shape[0] == 1, "single-layer problem: the context stream update is never consumed"
    p = dict(w_ada=w_ada[0], b_ada=b_ada[0], g_pre_mix=g_pre_mix[0], g_post_mix=g_post_mix[0],
             g_pre_ffn=g_pre_ffn[0], g_post_ffn=g_post_ffn[0], w_in=w_in[0], b_in=b_in[0],
             mu_prev=mu_prev[0], mu_next=mu_next[0], att_sinks=att_sinks[0], w0_f=w0_f[0], w0_b=w0_b[0],
             w2_f=w2_f[0], w2_b=w2_b[0], a0_f=a0_f[0], a0_b=a0_b[0], a2_f=a2_f[0], a2_b=a2_b[0], g2=g2[0],
             k_k=k_k[0], k_a=k_a[0], r_k=r_k[0].reshape(-1), ln_x_w=ln_x_w[0], ln_x_b=ln_x_b[0],
             w_up_att=w_up_att[0], w_up_rwkv=w_up_rwkv[0], w_out=w_out[0], w_router=w_router[0],
             b_router=b_router[0], w_gate_up=w_gate_up[0], b_gate_up=b_gate_up[0], w_down=w_down[0],
             b_down=b_down[0])
    return _layer(x, c, ctx, c_ctx, p)
```

```python
import functools

import jax
import jax.numpy as jnp
import numpy as np
from jax import lax
from jax.experimental import pallas as pl
from jax.experimental.pallas import tpu as pltpu

F32 = jnp.float32
BF16 = jnp.bfloat16

GRID_W = 64
HEAD_DIM = 64
ATT_HEADS = 8
ATT_KV_HEADS = 2
ATT_GROUPS = ATT_HEADS // ATT_KV_HEADS
ATT_DIM = ATT_HEADS * HEAD_DIM
KV_DIM = ATT_KV_HEADS * HEAD_DIM
WINDOW = 128
BLOCK = 128
ROPE_BASE = 10000.0
ATT_SCALE = HEAD_DIM ** -0.5
NEG_INF = -1e30
RWKV_HEADS = 8
RWKV_DIM = RWKV_HEADS * HEAD_DIM
DECAY_LORA = 64
ICL_LORA = 64
GATE_LORA = 128
GN_EPS = 64e-5
N_EXPERTS = 32
TOP_K = 4
SWIGLU_LIMIT = 7.0
SWIGLU_ALPHA = 1.702
RMS_EPS = 1e-6
ATT_COLS = ATT_DIM + 2 * KV_DIM
RWKV_COLS = 3 * RWKV_DIM + 2 * DECAY_LORA + 2 * ICL_LORA + GATE_LORA
ROPE_COLS = ATT_DIM + KV_DIM

LANES = 128
SUBLANES = 8
VMEM_LIMIT = 48 * 1024 * 1024

ROW_TILE = 256
SCAN_CHUNK = 64
SCAN_GROUP = 4
SCAN_BATCH = 4
MOE_ROWS = 512


def _cparams(*sem):
    return pltpu.CompilerParams(dimension_semantics=sem, vmem_limit_bytes=VMEM_LIMIT)


def _mm(a, b):
    return jnp.dot(a.astype(BF16), b.astype(BF16), preferred_element_type=F32)


def _mm_nt(a, b):
    return lax.dot_general(a.astype(BF16), b.astype(BF16), (((1,), (1,)), ((), ())),
                           preferred_element_type=F32)


def _split2(x):
    hi = x.astype(BF16)
    lo = (x - hi.astype(F32)).astype(BF16)
    return hi, lo


def _split3(x):
    hi = x.astype(BF16)
    r1 = x - hi.astype(F32)
    mid = r1.astype(BF16)
    lo = (r1 - mid.astype(F32)).astype(BF16)
    return hi, mid, lo


def _mm_f32(a, b):
    ah, al = _split2(a)
    bh, bl = _split2(b)
    d = functools.partial(jnp.dot, preferred_element_type=F32)
    return d(ah, bh) + d(ah, bl) + d(al, bh)


def _seg_sum(x, eseg):
    hi, lo = _split2(x)
    d = functools.partial(jnp.dot, preferred_element_type=F32)
    return d(hi, eseg) + d(lo, eseg)


def _rms(x):
    return x * lax.rsqrt(jnp.mean(x * x, axis=-1, keepdims=True) + RMS_EPS)


def _sigmoid(x):
    return 0.5 * jnp.tanh(0.5 * x) + 0.5


def _ada_kernel(c_ref, w_ref, b_ref, o_ref):
    c = c_ref[...]
    o_ref[...] = _mm_f32(c * _sigmoid(c), w_ref[...]) + b_ref[...]


def _ada(c_rows, w, b):
    m, d = c_rows.shape
    n = w.shape[1]
    tn = 1536
    return pl.pallas_call(
        _ada_kernel,
        grid=(n // tn,),
        in_specs=[pl.BlockSpec((m, d), lambda j: (0, 0)),
                  pl.BlockSpec((d, tn), lambda j: (0, j)),
                  pl.BlockSpec((1, tn), lambda j: (0, j))],
        out_specs=pl.BlockSpec((m, tn), lambda j: (0, j)),
        out_shape=jax.ShapeDtypeStruct((m, n), F32),
        compiler_params=_cparams("arbitrary"),
        name="ada",
    )(c_rows, w, b.reshape(1, n))


def _proj_kernel(*refs, rope, tiles_per_seq):
    n_rope = 2 if rope else 0
    (x_ref, xp_ref, xn_ref, mod_ref, g_ref, wa_ref, wr_ref, wg_ref, ba_ref, br_ref, bg_ref) = refs[:11]
    cos_ref, sin_ref = refs[11:11 + n_rope] if rope else (None, None)
    (mup_ref, mun_ref, w2_ref, w0_ref, a2_ref, a0_ref, g2_ref, kk_ref, ka_ref, eseg_ref) = refs[11 + n_rope:21 + n_rope]
    za_ref, zg_ref = refs[21 + n_rope:23 + n_rope]
    r_o, v_o, kf_o, kb_o, lwf_o, lwb_o, kkn_o, bf_o, bb_o, gate_o = refs[23 + n_rope:]
    mod = mod_ref[0]
    modulated = lambda xv: _rms(xv) * g_ref[...] * (1.0 + mod[1:2]) + mod[0:1]
    h = modulated(x_ref[...])
    hb = h.astype(BF16)
    za = jnp.dot(hb, wa_ref[...], preferred_element_type=F32) + ba_ref[...]
    if rope:
        qk = za[:, :ROPE_COLS]
        lane = lax.broadcasted_iota(jnp.int32, qk.shape, 1)
        low = (lane % 32) < 16
        partner = jnp.where(low, pltpu.roll(qk, ROPE_COLS - 16, 1), pltpu.roll(qk, 16, 1))
        cos = jnp.concatenate([cos_ref[...]] * (ROPE_COLS // LANES), axis=1)
        sin = jnp.concatenate([sin_ref[...]] * (ROPE_COLS // LANES), axis=1)
        za_ref[:, :ROPE_COLS] = (qk * cos + partner * sin).astype(za_ref.dtype)
        za_ref[:, ROPE_COLS:] = za[:, ROPE_COLS:].astype(za_ref.dtype)
    else:
        za_ref[...] = za.astype(za_ref.dtype)
    zg_ref[...] = (jnp.dot(hb, wg_ref[...], preferred_element_type=F32) + bg_ref[...]).astype(zg_ref.dtype)

    tm = x_ref.shape[0]
    h_ext = jnp.concatenate([modulated(xp_ref[...]), h, modulated(xn_ref[...])], axis=0).astype(BF16)
    z_ext = jnp.dot(h_ext, wr_ref[...], preferred_element_type=F32) + br_ref[...]
    z = z_ext[SUBLANES:SUBLANES + tm]
    ti = pl.program_id(0) % tiles_per_seq
    row = lax.broadcasted_iota(jnp.int32, (tm, 1), 0)
    prev_halo = jnp.where(ti == 0, 0.0, z_ext[SUBLANES - 1:SUBLANES])
    next_halo = jnp.where(ti == tiles_per_seq - 1, 0.0, z_ext[SUBLANES + tm:SUBLANES + tm + 1])
    prev = jnp.where(row == 0, prev_halo, pltpu.roll(z, 1, 0))
    nxt = jnp.where(row == tm - 1, next_halo, pltpu.roll(z, tm - 1, 0))
    zs = z + mup_ref[...] * (prev - z) + mun_ref[...] * (nxt - z)

    d = RWKV_DIM
    r = zs[:, 0:d]
    k = zs[:, d:2 * d]
    v = zs[:, 2 * d:3 * d]
    o = 3 * d
    wl = zs[:, o:o + 2 * DECAY_LORA]
    al = zs[:, o + 2 * DECAY_LORA:o + 2 * DECAY_LORA + 2 * ICL_LORA]
    gl = zs[:, o + 2 * DECAY_LORA + 2 * ICL_LORA:]

    w = w0_ref[...] + _mm(jnp.tanh(wl), w2_ref[...])
    lw = -float(np.exp(-0.5)) * _sigmoid(w)
    icl = _sigmoid(a0_ref[...] + _mm(al, a2_ref[...]))
    kk0 = k * kk_ref[...]
    ss = _seg_sum(kk0 * kk0, eseg_ref[...])
    kk = kk0 / jnp.maximum(jnp.sqrt(ss), 1e-12)
    ka = ka_ref[...]
    icl_f = icl[:, :d]
    icl_b = icl[:, d:]

    r_o[...] = r.astype(r_o.dtype)
    v_o[...] = v.astype(v_o.dtype)
    kf_o[...] = (k * (1.0 + (icl_f - 1.0) * ka)).astype(kf_o.dtype)
    kb_o[...] = (k * (1.0 + (icl_b - 1.0) * ka)).astype(kb_o.dtype)
    lwf_o[...] = lw[:, :d]
    lwb_o[...] = lw[:, d:]
    kkn_o[...] = kk.astype(kkn_o.dtype)
    bf_o[...] = (kk * icl_f).astype(bf_o.dtype)
    bb_o[...] = (kk * icl_b).astype(bb_o.dtype)
    gate_o[...] = _mm(_sigmoid(gl), g2_ref[...]).astype(gate_o.dtype)


def _rope_tables(seq):
    n_rows = seq // GRID_W
    row = jnp.repeat(jnp.arange(n_rows, dtype=F32), GRID_W, total_repeat_length=seq)
    col = jnp.tile(jnp.arange(GRID_W, dtype=F32), n_rows)
    half = HEAD_DIM // 2
    inv_freq = ROPE_BASE ** (-jnp.arange(0, half, 2, dtype=F32) / half)
    ang_r = row[:, None] * inv_freq[None, :]
    ang_c = col[:, None] * inv_freq[None, :]
    cos_h = jnp.concatenate([jnp.cos(ang_r), jnp.cos(ang_r), jnp.cos(ang_c), jnp.cos(ang_c)], axis=1)
    sin_h = jnp.concatenate([-jnp.sin(ang_r), jnp.sin(ang_r), -jnp.sin(ang_c), jnp.sin(ang_c)], axis=1)
    reps = LANES // HEAD_DIM
    return jnp.tile(cos_h, (1, reps)), jnp.tile(sin_h, (1, reps))


def _project(x2, mods, mod_row, seq, g_pre, w_att, w_rwkv, w_gate, b_att, b_rwkv, b_gate, feat_params, rope):
    rows, d = x2.shape
    tm = ROW_TILE
    tps = seq // tm
    hb = tm // SUBLANES
    nhb = rows // SUBLANES
    rd = RWKV_DIM
    const = lambda i: (0, 0)
    in_specs = [pl.BlockSpec((tm, d), lambda i: (i, 0)),
                pl.BlockSpec((SUBLANES, d), lambda i: (jnp.maximum(i * hb - 1, 0), 0)),
                pl.BlockSpec((SUBLANES, d), lambda i: (jnp.minimum((i + 1) * hb, nhb - 1), 0)),
                pl.BlockSpec((1, SUBLANES, d), lambda i: (mod_row(i // tps), 0, 0)),
                pl.BlockSpec((1, d), const),
                pl.BlockSpec(w_att.shape, const), pl.BlockSpec(w_rwkv.shape, const),
                pl.BlockSpec(w_gate.shape, const),
                pl.BlockSpec((1, ATT_COLS), const), pl.BlockSpec((1, RWKV_COLS), const),
                pl.BlockSpec((1, w_gate.shape[1]), const)]
    args = [x2, x2, x2, mods, g_pre, w_att, w_rwkv, w_gate, b_att, b_rwkv, b_gate]
    if rope:
        cos, sin = _rope_tables(seq)
        in_specs += [pl.BlockSpec((tm, LANES), lambda i: (i % tps, 0))] * 2
        args += [cos, sin]
    in_specs += [pl.BlockSpec(a.shape, const) for a in feat_params]
    args += feat_params
    feat_spec = pl.BlockSpec((tm, rd), lambda i: (i, 0))
    outs = pl.pallas_call(
        functools.partial(_proj_kernel, rope=rope, tiles_per_seq=tps),
        grid=(rows // tm,),
        in_specs=in_specs,
        out_specs=[pl.BlockSpec((tm, ATT_COLS), lambda i: (i, 0)),
                   pl.BlockSpec((tm, w_gate.shape[1]), lambda i: (i, 0))] + [feat_spec] * len(FEAT_NAMES),
        out_shape=[jax.ShapeDtypeStruct((rows, ATT_COLS), BF16),
                   jax.ShapeDtypeStruct((rows, w_gate.shape[1]), BF16)]
                  + [jax.ShapeDtypeStruct((rows, rd), BF16 if name in FEAT_BF16 else F32) for name in FEAT_NAMES],
        compiler_params=_cparams("parallel"),
        name="proj",
    )(*args)
    return outs[0], outs[1], dict(zip(FEAT_NAMES, outs[2:]))


FEAT_NAMES = ('r', 'v', 'kf', 'kb', 'lwf', 'lwb', 'kk', 'bf', 'bb', 'gate')
FEAT_BF16 = ('r', 'v', 'kf', 'kb', 'kk', 'bf', 'bb', 'gate')


def _block_diag2(a, b):
    za = jnp.zeros_like(a)
    zb = jnp.zeros_like(b)
    return jnp.concatenate([jnp.concatenate([a, zb], axis=1), jnp.concatenate([za, b], axis=1)], axis=0)


def _seg_matrix():
    h = np.arange(RWKV_DIM) // HEAD_DIM
    return jnp.asarray((h[:, None] == h[None, :]).astype(np.float32), BF16)


def _feature_params(p):
    d = RWKV_DIM
    return [p['mu_prev'].reshape(1, -1), p['mu_next'].reshape(1, -1),
            _block_diag2(p['w2_f'], p['w2_b']).astype(BF16),
            jnp.concatenate([p['w0_f'], p['w0_b']]).reshape(1, 2 * d),
            _block_diag2(p['a2_f'], p['a2_b']).astype(BF16),
            jnp.concatenate([p['a0_f'], p['a0_b']]).reshape(1, 2 * d),
            p['g2'].astype(BF16), p['k_k'].reshape(1, d), p['k_a'].reshape(1, d), _seg_matrix()]


def _scan_kernel(rf, vf, kf, lwf, kkf, bf, rb, vb, kb, lwb, kkb, bb, s0f_ref, s0b_ref,
                 yf_ref, yb_ref, sTf_ref, sTb_ref, state_ref, *, chunk, n_chunks):
    c = pl.program_id(1)
    C = chunk
    hd = HEAD_DIM
    gw = SCAN_GROUP * hd
    n_groups = RWKV_HEADS // SCAN_GROUP
    assert C == hd, "the triangular masks below are shared between time and channel blocks"

    n_rows = state_ref.shape[0]

    @pl.when(c == 0)
    def _():
        for n in range(n_rows):
            for d, s0_ref in enumerate((s0f_ref, s0b_ref)):
                for g in range(n_groups):
                    state_ref[n, d, g] = jnp.zeros((gw, gw), F32)
                    for j in range(SCAN_GROUP):
                        state_ref[n, d, g, j * hd:(j + 1) * hd, j * hd:(j + 1) * hd] = (
                            s0_ref[n, g * SCAN_GROUP + j])

    ti = lax.broadcasted_iota(jnp.int32, (C, C), 0)
    si = lax.broadcasted_iota(jnp.int32, (C, C), 1)
    tg = lax.broadcasted_iota(jnp.int32, (C, gw), 0)
    sg = lax.broadcasted_iota(jnp.int32, (C, gw), 1) % C
    eye = (tg == sg).astype(F32)
    same_head = (lax.broadcasted_iota(jnp.int32, (gw, gw), 0) // hd
                 == lax.broadcasted_iota(jnp.int32, (gw, gw), 1) // hd)
    same_head_b = same_head.astype(BF16)
    n_double = int(np.log2(C)) - 1

    def bdiag(x_cat):
        return jnp.concatenate([x_cat.astype(BF16)] * SCAN_GROUP, axis=0) * same_head_b

    dirs = ((rf, vf, kf, lwf, kkf, bf), (rb, vb, kb, lwb, kkb, bb))
    units = []
    masks = [(((si <= ti) if d == 0 else (si >= ti)).astype(BF16),
              (sg <= tg) if d == 0 else (sg >= tg),
              (sg < tg) if d == 0 else (sg > tg)) for d in range(2)]
    for n, d in [(n, d) for n in range(n_rows) for d in range(2)]:
        r_ref, v_ref, k_ref, lw_ref, kk_ref, b_ref = (ref.at[n] for ref in dirs[d])
        tri, incl_g, strict_g = masks[d]
        lw = lw_ref[...]
        lh, lm, ll = _split3(lw)
        dd = functools.partial(jnp.dot, preferred_element_type=F32)
        cum = dd(tri, lh) + dd(tri, lm) + dd(tri, ll)
        cumx = cum - lw
        cum_end = cum[C - 1:C] if d == 0 else cum[0:1]
        e_neg = jnp.exp(-cum)
        e_end = jnp.exp(cum_end - cum)
        bv = b_ref[...].astype(F32)
        kv = k_ref[...].astype(F32)
        vv = v_ref[...].astype(F32)
        a_t = (-kk_ref[...].astype(F32) * jnp.exp(cumx)).astype(BF16)
        r_t = (r_ref[...].astype(F32) * jnp.exp(cum)).astype(BF16)
        b_t = (bv * e_neg).astype(BF16)
        k_t = (kv * e_neg).astype(BF16)
        b_q = (bv * e_end).astype(BF16)
        k_q = (kv * e_end).astype(BF16)
        g_end = jnp.exp(cum_end)
        for g in range(n_groups):
            sl = slice(g * gw, (g + 1) * gw)
            units.append(dict(
                n=n, d=d, g=g, sl=sl, incl=incl_g, strict=strict_g,
                P=jnp.concatenate([a_t[:, sl], r_t[:, sl]], axis=0),
                Q=jnp.concatenate([bdiag(b_t[:, sl]), bdiag(k_t[:, sl])], axis=0),
                Qq=jnp.concatenate([b_q[:, sl], k_q[:, sl]], axis=0),
                V=vv[:, sl], g_end=g_end[:, sl]))

    for u in units:
        u['G'] = _mm_nt(u['P'], u['Q'])
    for u in units:
        u['S0'] = state_ref[u['n'], u['d'], u['g']]
        u['PH'] = _mm_nt(u['P'], u['S0'])
        u['Vd'] = bdiag(u['V'])
    for u in units:
        G = u.pop('G')
        u['N'] = jnp.where(u['strict'], G[:C, :gw], 0.0)
        u['a_ak'] = jnp.where(u['strict'], G[:C, gw:], 0.0)
        u['a_rb'] = jnp.where(u['incl'], G[C:, :gw], 0.0)
        u['a_rk'] = jnp.where(u['incl'], G[C:, gw:], 0.0)
    for u in units:
        u['T'] = eye + u['N']
        u['Pw'] = _mm(u['N'], bdiag(u['N']))
        av = _mm(jnp.concatenate([u['a_ak'], u['a_rk']], axis=0), u['Vd'])
        u['rhs'] = u['PH'][:C] + av[:C]
        u['y0'] = u['PH'][C:] + av[C:]
    for lvl in range(n_double):
        for u in units:
            pw = bdiag(u['Pw'])
            if lvl + 1 < n_double:
                both = _mm(jnp.concatenate([u['T'], u['Pw']], axis=0), pw)
                u['T'] = u['T'] + both[:C]
                u['Pw'] = both[C:]
            else:
                u['T'] = u['T'] + _mm(u['T'], pw)
    for u in units:
        u['U'] = _mm(u['T'], bdiag(u['rhs']))
    y_refs = (yf_ref, yb_ref)
    for u in units:
        y_refs[u['d']][u['n'], :, u['sl']] = u['y0'] + _mm(u['a_rb'], bdiag(u['U']))
        uv = jnp.concatenate([u['U'], u['V']], axis=0)
        state_ref[u['n'], u['d'], u['g']] = jnp.where(
            same_head, u['S0'] * u['g_end'] + _mm(uv.T, u['Qq']), 0.0)

    @pl.when(c == n_chunks - 1)
    def _():
        for n in range(n_rows):
            for d, sT_ref in enumerate((sTf_ref, sTb_ref)):
                for g in range(n_groups):
                    for j in range(SCAN_GROUP):
                        sT_ref[n, g * SCAN_GROUP + j] = (
                            state_ref[n, d, g, j * hd:(j + 1) * hd, j * hd:(j + 1) * hd])


def _scan(f, batch, seq, s0_f, s0_b):
    C = SCAN_CHUNK
    nC = seq // C
    d = RWKV_DIM
    nb = max(n for n in range(1, SCAN_BATCH + 1) if batch % n == 0)
    fwd = pl.BlockSpec((nb, C, d), lambda b, c: (b, c, 0))
    bwd = pl.BlockSpec((nb, C, d), lambda b, c: (b, nC - 1 - c, 0))
    st = pl.BlockSpec((nb, RWKV_HEADS, HEAD_DIM, HEAD_DIM), lambda b, c: (b, 0, 0, 0))
    st_shape = jax.ShapeDtypeStruct((batch, RWKV_HEADS, HEAD_DIM, HEAD_DIM), F32)
    y_shape = jax.ShapeDtypeStruct((batch, seq, d), F32)
    gw = SCAN_GROUP * HEAD_DIM
    arr = lambda name: f[name].reshape(batch, seq, d)
    y_f, y_b, sT_f, sT_b = pl.pallas_call(
        functools.partial(_scan_kernel, chunk=C, n_chunks=nC),
        grid=(batch // nb, nC),
        in_specs=[fwd] * 6 + [bwd] * 6 + [st, st],
        out_specs=[fwd, bwd, st, st],
        out_shape=[y_shape, y_shape, st_shape, st_shape],
        scratch_shapes=[pltpu.VMEM((nb, 2, RWKV_HEADS // SCAN_GROUP, gw, gw), F32)],
        compiler_params=_cparams("parallel", "arbitrary"),
        name="scan",
    )(arr('r'), arr('v'), arr('kf'), arr('lwf'), arr('kk'), arr('bf'),
      arr('r'), arr('v'), arr('kb'), arr('lwb'), arr('kk'), arr('bb'), s0_f, s0_b)
    return y_f.reshape(batch * seq, d), y_b.reshape(batch * seq, d), sT_f, sT_b


def _attn_kernel(sink_ref, q_ref, kp_ref, km_ref, kn_ref, vp_ref, vm_ref, vn_ref, kc_ref, vc_ref, o_ref,
                 *, n_blocks):
    n = pl.program_id(1)
    rows = ATT_GROUPS * BLOCK
    q = q_ref[...] * ATT_SCALE
    qi = lax.broadcasted_iota(jnp.int32, (rows, BLOCK), 0) % BLOCK
    kj = lax.broadcasted_iota(jnp.int32, (rows, BLOCK), 1)
    ok_prev = (kj >= qi) & (n > 0)
    ok_next = (kj <= qi) & (n < n_blocks - 1)
    rowh = lax.broadcasted_iota(jnp.int32, (rows, 1), 0) // BLOCK
    outs = [None] * ATT_HEADS
    for g in range(ATT_KV_HEADS):
        ks = slice(g * HEAD_DIM, (g + 1) * HEAD_DIM)
        heads = [g * ATT_GROUPS + j for j in range(ATT_GROUPS)]
        Qs = jnp.concatenate([q[:, h * HEAD_DIM:(h + 1) * HEAD_DIM] for h in heads], axis=0)
        scores = [jnp.where(ok_prev, _mm_nt(Qs, kp_ref[:, ks]), NEG_INF),
                  _mm_nt(Qs, km_ref[:, ks]),
                  jnp.where(ok_next, _mm_nt(Qs, kn_ref[:, ks]), NEG_INF),
                  _mm_nt(Qs, kc_ref[:, ks])]
        values = [vp_ref, vm_ref, vn_ref, vc_ref]
        sink = jnp.zeros((rows, 1), F32)
        for j, h in enumerate(heads):
            sink = jnp.where(rowh == j, sink_ref[h], sink)
        folded = None
        for sc in scores:
            for c0 in range(0, sc.shape[1], BLOCK):
                blk = sc[:, c0:c0 + BLOCK]
                folded = blk if folded is None else jnp.maximum(folded, blk)
        m = jnp.maximum(sink, jnp.max(folded, axis=-1, keepdims=True))
        acc = jnp.zeros((rows, 2 * HEAD_DIM), F32)
        for sc, v_ref in zip(scores, values):
            one_col = (lax.broadcasted_iota(jnp.int32, (v_ref.shape[0], HEAD_DIM), 1) == 0).astype(BF16)
            v_ext = jnp.concatenate([v_ref[:, ks], one_col], axis=1)
            acc = acc + jnp.dot(jnp.exp((sc - m).astype(BF16)), v_ext, preferred_element_type=F32)
        den = acc[:, HEAD_DIM:HEAD_DIM + 1] + jnp.exp(sink - m)
        O = acc[:, :HEAD_DIM] / den
        for j, h in enumerate(heads):
            outs[h] = O[j * BLOCK:(j + 1) * BLOCK]
    o_ref[...] = jnp.concatenate(outs, axis=1).astype(o_ref.dtype)


def _attention(z_att, zc_att, sinks, batch, seq):
    nb = seq // BLOCK
    ctx_len = zc_att.shape[0] // batch
    kcol = ATT_DIM // KV_DIM
    vcol = kcol + 1

    def kv_spec(col, off):
        return pl.BlockSpec((BLOCK, KV_DIM), lambda b, n: (b * nb + jnp.clip(n + off, 0, nb - 1), col))

    return pl.pallas_call(
        functools.partial(_attn_kernel, n_blocks=nb),
        grid=(batch, nb),
        in_specs=[pl.BlockSpec(memory_space=pltpu.SMEM),
                  pl.BlockSpec((BLOCK, ATT_DIM), lambda b, n: (b * nb + n, 0)),
                  kv_spec(kcol, -1), kv_spec(kcol, 0), kv_spec(kcol, 1),
                  kv_spec(vcol, -1), kv_spec(vcol, 0), kv_spec(vcol, 1),
                  pl.BlockSpec((ctx_len, KV_DIM), lambda b, n: (b, kcol)),
                  pl.BlockSpec((ctx_len, KV_DIM), lambda b, n: (b, vcol))],
        out_specs=pl.BlockSpec((BLOCK, ATT_DIM), lambda b, n: (b * nb + n, 0)),
        out_shape=jax.ShapeDtypeStruct((batch * seq, ATT_DIM), BF16),
        compiler_params=_cparams("parallel", "parallel"),
        name="attn",
    )(sinks, z_att, z_att, z_att, z_att, z_att, z_att, z_att, zc_att, zc_att)


def _merge_kernel(yf_ref, yb_ref, r_ref, kf_ref, kb_ref, v_ref, gate_ref, att_ref, zg_ref, x_ref, mod_ref,
                  lnw_ref, lnb_ref, rk_ref, eseg_ref, wua_ref, wur_ref, wo_ref, gpm_ref, gpf_ref,
                  wr_ref, br_ref,
                  x1_ref, hf_ref, gt_ref, loc_ref, cnt_ref):
    eseg = eseg_ref[...]
    inv_n = 1.0 / HEAD_DIM
    y = yf_ref[...] + yb_ref[...]
    mean = _seg_sum(y, eseg) * inv_n
    dy = y - mean
    var = _seg_sum(dy * dy, eseg) * inv_n
    yn = dy * lax.rsqrt(var + GN_EPS) * lnw_ref[...] + lnb_ref[...]
    f32 = lambda ref: ref[...].astype(F32)
    bonus = _seg_sum(f32(r_ref) * (f32(kf_ref) + f32(kb_ref)) * rk_ref[...], eseg) * f32(v_ref)
    rwk = (yn + bonus) * f32(gate_ref)

    d = x_ref.shape[1]
    zg = zg_ref[...].astype(F32)
    merged = (_sigmoid(zg[:, :d]) * jnp.dot(att_ref[...], wua_ref[...], preferred_element_type=F32)
              + _sigmoid(zg[:, d:]) * _mm(rwk, wur_ref[...]))
    mix = _mm(merged, wo_ref[...])
    mod = mod_ref[0]
    x1 = x_ref[...] + mod[2:3] * (_rms(mix) * gpm_ref[...])
    x1_ref[...] = x1
    hf = _rms(x1) * gpf_ref[...] * (1.0 + mod[4:5]) + mod[3:4]
    hf_ref[...] = hf.astype(hf_ref.dtype)

    logits = _mm_f32(hf, wr_ref[...]) + br_ref[...]
    lane = lax.broadcasted_iota(jnp.int32, logits.shape, 1).astype(F32)
    val_out = jnp.zeros(logits.shape, F32)
    picked = jnp.zeros(logits.shape, F32)
    sels = []
    top = None
    den = None
    for kth in range(TOP_K):
        m = jnp.max(logits, axis=-1, keepdims=True)
        sel = jnp.min(jnp.where(logits == m, lane, float(LANES)), axis=-1, keepdims=True)
        hit = lane == sel
        logits = jnp.where(hit, -jnp.inf, logits)
        picked = jnp.where(hit, 1.0, picked)
        sels.append(hit)
        if kth == 0:
            top = m
        e = jnp.exp(m - top)
        den = e if kth == 0 else den + e
        val_out = jnp.where(lane == float(kth), e, val_out)
    gt_ref[...] = val_out / den

    tm = picked.shape[0]
    earlier = (lax.broadcasted_iota(jnp.int32, (tm, tm), 1)
               < lax.broadcasted_iota(jnp.int32, (tm, tm), 0)).astype(BF16)
    before = jnp.dot(earlier, picked.astype(BF16), preferred_element_type=F32)
    counts = jnp.sum(picked, axis=0, keepdims=True)
    seg_len = jnp.floor((counts + (SUBLANES - 1)) * (1.0 / SUBLANES)) * SUBLANES
    lower_expert = (lax.broadcasted_iota(jnp.int32, (LANES, LANES), 0)
                    < lax.broadcasted_iota(jnp.int32, (LANES, LANES), 1)).astype(BF16)
    seg_off = jnp.dot(jnp.broadcast_to(seg_len, (SUBLANES, LANES)).astype(BF16), lower_expert,
                      preferred_element_type=F32)[0:1]
    loc_out = jnp.zeros(logits.shape, F32)
    for kth, hit in enumerate(sels):
        row = jnp.sum(jnp.where(hit, before + seg_off, 0.0), axis=-1, keepdims=True)
        loc_out = jnp.where(lane == float(kth), row, loc_out)
    loc_ref[...] = loc_out.astype(jnp.int32)
    cnt_ref[0] = jnp.broadcast_to(counts, cnt_ref.shape[1:]).astype(jnp.int32)


def _moe_blocks(n_tok, n_tiles):
    return -(-(n_tok * TOP_K + (SUBLANES - 1) * N_EXPERTS * n_tiles) // MOE_ROWS) + N_EXPERTS


def _merge(y_f, y_b, f, att, z_gate, x2, mods, seq, p):
    rows, d = x2.shape
    tm = ROW_TILE
    tps = seq // tm
    rd = RWKV_DIM
    const = lambda i: (0, 0)
    row = lambda w: pl.BlockSpec((tm, w), lambda i: (i, 0))
    vec = lambda w: pl.BlockSpec((1, w), const)
    w_router = jnp.pad(p['w_router'], ((0, 0), (0, LANES - N_EXPERTS)))
    b_router = jnp.pad(p['b_router'], (0, LANES - N_EXPERTS), constant_values=NEG_INF).reshape(1, LANES)
    return pl.pallas_call(
        _merge_kernel,
        grid=(rows // tm,),
        in_specs=[row(rd)] * 7 + [row(ATT_DIM), row(2 * d), row(d),
                  pl.BlockSpec((1, SUBLANES, d), lambda i: (i // tps, 0, 0)),
                  vec(rd), vec(rd), vec(rd), pl.BlockSpec((rd, rd), const),
                  pl.BlockSpec((ATT_DIM, d), const), pl.BlockSpec((rd, d), const), pl.BlockSpec((d, d), const),
                  vec(d), vec(d), pl.BlockSpec((d, LANES), const), vec(LANES)],
        out_specs=[row(d), row(d), row(LANES), row(LANES),
                   pl.BlockSpec((1, SUBLANES, LANES), lambda i: (i, 0, 0))],
        out_shape=[jax.ShapeDtypeStruct((rows, d), F32), jax.ShapeDtypeStruct((rows, d), BF16),
                   jax.ShapeDtypeStruct((rows, LANES), F32),
                   jax.ShapeDtypeStruct((rows, LANES), jnp.int32),
                   jax.ShapeDtypeStruct((rows // tm, SUBLANES, LANES), jnp.int32)],
        compiler_params=_cparams("parallel"),
        name="merge",
    )(y_f, y_b, f['r'], f['kf'], f['kb'], f['v'], f['gate'], att, z_gate, x2, mods,
      p['ln_x_w'].reshape(1, rd), p['ln_x_b'].reshape(1, rd), p['r_k'].reshape(1, rd), _seg_matrix(),
      p['w_up_att'].astype(BF16), p['w_up_rwkv'].astype(BF16), p['w_out'].astype(BF16),
      p['g_post_mix'].reshape(1, d), p['g_pre_ffn'].reshape(1, d), w_router, b_router)


def _moe_kernel(be_ref, nused_ref, x_ref, wgu_ref, bgu_ref, wdn_f32_ref, bdn_ref, o_ref, wdn_ref):
    i = pl.program_id(0)

    @pl.when((i == 0) | (be_ref[i] != be_ref[jnp.maximum(i - 1, 0)]))
    def _():
        wdn_ref[...] = wdn_f32_ref[...].astype(BF16)

    @pl.when(i < nused_ref[0])
    def _():
        de = wdn_ref.shape[0]
        d = x_ref.shape[1]
        x = x_ref[...].astype(BF16)
        bgu = bgu_ref[...]
        gate = jnp.minimum(_mm_nt(x, wgu_ref[:, :d]) + bgu[:, :de], SWIGLU_LIMIT)
        up = jnp.clip(_mm_nt(x, wgu_ref[:, d:]) + bgu[:, de:], -SWIGLU_LIMIT, SWIGLU_LIMIT)
        act = (up + 1.0) * (gate * _sigmoid(SWIGLU_ALPHA * gate))
        o_ref[...] = _mm(act, wdn_ref[...]) + bdn_ref[...]

    @pl.when(i >= nused_ref[0])
    def _():
        o_ref[...] = jnp.zeros(o_ref.shape, o_ref.dtype)


def _wprep_kernel(w_ref, o_ref, t_ref):
    d = w_ref.shape[1]
    half = o_ref.shape[1]
    wt = w_ref[0].T
    for c in range(d // LANES):
        cols = slice(c * LANES, (c + 1) * LANES)
        t_ref[c] = wt[:, cols]
        o_ref[0, :, cols] = t_ref[c, pl.ds(0, half, stride=2), :].astype(o_ref.dtype)
        o_ref[0, :, d + c * LANES:d + (c + 1) * LANES] = t_ref[c, pl.ds(1, half, stride=2), :].astype(o_ref.dtype)


def _gate_up_rows(w_gu):
    n_exp, d, de2 = w_gu.shape
    tc = 4 * LANES
    return pl.pallas_call(
        _wprep_kernel,
        grid=(n_exp, de2 // tc),
        in_specs=[pl.BlockSpec((1, d, tc), lambda e, j: (e, 0, j))],
        out_specs=pl.BlockSpec((1, tc // 2, 2 * d), lambda e, j: (e, j, 0)),
        out_shape=jax.ShapeDtypeStruct((n_exp, de2 // 2, 2 * d), BF16),
        scratch_shapes=[pltpu.VMEM((d // LANES, tc, LANES), F32)],
        compiler_params=_cparams("parallel", "parallel"),
        name="wprep",
    )(w_gu)


def _experts(xb, block_exp, n_used, w_gu, b_gu, w_dn, b_dn):
    n_slots, d = xb.shape
    bm = MOE_ROWS
    n_blocks = n_slots // bm
    de = w_dn.shape[1]
    de2 = 2 * de
    grid_spec = pltpu.PrefetchScalarGridSpec(
        num_scalar_prefetch=2,
        grid=(n_blocks,),
        in_specs=[pl.BlockSpec((bm, d), lambda i, be, nu: (jnp.minimum(i, nu[0] - 1), 0)),
                  pl.BlockSpec((None, de, 2 * d), lambda i, be, nu: (be[i], 0, 0)),
                  pl.BlockSpec((None, 1, de2), lambda i, be, nu: (be[i], 0, 0)),
                  pl.BlockSpec((None, de, d), lambda i, be, nu: (be[i], 0, 0)),
                  pl.BlockSpec((None, 1, d), lambda i, be, nu: (be[i], 0, 0))],
        out_specs=pl.BlockSpec((bm, d), lambda i, be, nu: (i, 0)),
        scratch_shapes=[pltpu.VMEM((de, d), BF16)],
    )
    return pl.pallas_call(
        _moe_kernel,
        grid_spec=grid_spec,
        out_shape=jax.ShapeDtypeStruct((n_slots, d), F32),
        compiler_params=_cparams("arbitrary"),
        name="moe",
    )(block_exp, n_used, xb, w_gu, b_gu, w_dn, b_dn)


def _local_rows(tm):
    rows = tm * TOP_K + (SUBLANES - 1) * N_EXPERTS
    assert rows % SUBLANES == 0
    return rows


def _pick_matrix(loc, weights, n_cols):
    col = lax.broadcasted_iota(jnp.int32, (loc.shape[0], n_cols), 1)
    out = jnp.zeros(col.shape, F32)
    for k in range(TOP_K):
        out = jnp.where(col == loc[:, k:k + 1], weights[k], out)
    return out


def _segment(seg_ref, tile, e):
    base = (tile * N_EXPERTS + e) * 3
    return tuple(pl.multiple_of(seg_ref[base + i], SUBLANES) for i in range(3))


def _dispatch_kernel(seg_ref, nch_ref, tail_ref, hf_ref, loc_ref, xb_ref, xc_ref, zero_ref, sem, zsem):
    j = pl.program_id(0)
    n_tiles = pl.num_programs(0)
    slot = j % 2

    def tail_copy(e):
        row = pl.multiple_of(tail_ref[2 * e], SUBLANES)
        n = pl.multiple_of(tail_ref[2 * e + 1], SUBLANES)
        return n, pltpu.make_async_copy(zero_ref.at[pl.ds(0, n), :], xb_ref.at[pl.ds(row, n), :], zsem)

    def spare_block_copy(b):
        rows = xb_ref.at[pl.ds(pl.multiple_of(b * MOE_ROWS, MOE_ROWS), MOE_ROWS), :]
        return pltpu.make_async_copy(zero_ref, rows, zsem)

    first_spare = tail_ref[2 * N_EXPERTS]
    n_blocks = xb_ref.shape[0] // MOE_ROWS

    @pl.when(j == 0)
    def _():
        zero_ref[...] = jnp.zeros(zero_ref.shape, zero_ref.dtype)

        def body(e, carry):
            n, cp = tail_copy(e)

            @pl.when(n > 0)
            def _():
                cp.start()
            return carry
        lax.fori_loop(0, N_EXPERTS, body, 0)

        def spare(b, carry):
            spare_block_copy(b).start()
            return carry
        lax.fori_loop(first_spare, n_blocks, spare, 0)

    def drain(tile, s):
        rows = pl.multiple_of(nch_ref[tile] * SUBLANES, SUBLANES)
        done = xc_ref.at[s, pl.ds(0, rows), :]
        pltpu.make_async_copy(done, done, sem.at[s]).wait()

    sel = _pick_matrix(loc_ref[...], [1.0] * TOP_K, xc_ref.shape[1]).astype(BF16)
    xc_ref[slot] = lax.dot_general(sel, hf_ref[...], (((0,), (0,)), ((), ())), preferred_element_type=F32)

    def start(e, carry):
        off, row, n = _segment(seg_ref, j, e)

        @pl.when(n > 0)
        def _():
            pltpu.make_async_copy(xc_ref.at[slot, pl.ds(off, n), :], xb_ref.at[pl.ds(row, n), :],
                                  sem.at[slot]).start()
        return carry
    lax.fori_loop(0, N_EXPERTS, start, 0)

    @pl.when(j > 0)
    def _():
        drain(j - 1, 1 - slot)

    @pl.when(j == n_tiles - 1)
    def _():
        drain(j, slot)

        def body(e, carry):
            n, cp = tail_copy(e)

            @pl.when(n > 0)
            def _():
                cp.wait()
            return carry
        lax.fori_loop(0, N_EXPERTS, body, 0)

        def spare(b, carry):
            spare_block_copy(b).wait()
            return carry
        lax.fori_loop(first_spare, n_blocks, spare, 0)


def _dispatch(hf, loc, segments, nch, tails):
    rows, d = hf.shape
    tm = ROW_TILE
    n_slots = _moe_blocks(rows, rows // tm) * MOE_ROWS
    grid_spec = pltpu.PrefetchScalarGridSpec(
        num_scalar_prefetch=3,
        grid=(rows // tm,),
        in_specs=[pl.BlockSpec((tm, d), lambda i, sg, nc, tl: (i, 0)),
                  pl.BlockSpec((tm, LANES), lambda i, sg, nc, tl: (i, 0))],
        out_specs=pl.BlockSpec(memory_space=pl.ANY),
        scratch_shapes=[pltpu.VMEM((2, _local_rows(tm), d), F32), pltpu.VMEM((MOE_ROWS, d), F32),
                        pltpu.SemaphoreType.DMA((2,)), pltpu.SemaphoreType.DMA(())],
    )
    return pl.pallas_call(
        _dispatch_kernel,
        grid_spec=grid_spec,
        out_shape=jax.ShapeDtypeStruct((n_slots, d), F32),
        compiler_params=_cparams("arbitrary"),
        name="dispatch",
    )(segments, nch, tails, hf, loc)


def _combine_kernel(seg_ref, nch_ref, yb_ref, loc_ref, gt_ref, x1_ref, mod_ref, g_ref, o_ref, yc_ref, sem):
    j = pl.program_id(0)
    n_tiles = pl.num_programs(0)

    def issue(tile, s):
        def body(e, carry):
            off, row, n = _segment(seg_ref, tile, e)

            @pl.when(n > 0)
            def _():
                pltpu.make_async_copy(yb_ref.at[pl.ds(row, n), :], yc_ref.at[s, pl.ds(off, n), :],
                                      sem.at[s]).start()
            return carry
        lax.fori_loop(0, N_EXPERTS, body, 0)

    @pl.when(j == 0)
    def _():
        yc_ref[...] = jnp.zeros(yc_ref.shape, yc_ref.dtype)
        issue(0, 0)

    @pl.when(j + 1 < n_tiles)
    def _():
        issue(j + 1, (j + 1) % 2)

    slot = j % 2

    rows = pl.multiple_of(nch_ref[j] * SUBLANES, SUBLANES)
    done = yc_ref.at[slot, pl.ds(0, rows), :]
    pltpu.make_async_copy(done, done, sem.at[slot]).wait()

    gt = gt_ref[...]
    w = _pick_matrix(loc_ref[...], [gt[:, k:k + 1] for k in range(TOP_K)], yc_ref.shape[1])
    y = _mm(w, yc_ref[slot])
    mod = mod_ref[0]
    o_ref[...] = x1_ref[...] + mod[5:6] * (_rms(y) * g_ref[...])


def _combine(yb, loc, segments, nch, gt, x1, mods, seq, g_post):
    rows, d = x1.shape
    tm = ROW_TILE
    tps = seq // tm
    row = lambda w: pl.BlockSpec((tm, w), lambda i, gd, nc: (i, 0))
    grid_spec = pltpu.PrefetchScalarGridSpec(
        num_scalar_prefetch=2,
        grid=(rows // tm,),
        in_specs=[pl.BlockSpec(memory_space=pl.ANY), row(LANES), row(LANES), row(d),
                  pl.BlockSpec((1, SUBLANES, d), lambda i, gd, nc: (i // tps, 0, 0)),
                  pl.BlockSpec((1, d), lambda i, gd, nc: (0, 0))],
        out_specs=row(d),
        scratch_shapes=[pltpu.VMEM((2, _local_rows(tm), d), F32), pltpu.SemaphoreType.DMA((2,))],
    )
    return pl.pallas_call(
        _combine_kernel,
        grid_spec=grid_spec,
        out_shape=jax.ShapeDtypeStruct((rows, d), F32),
        compiler_params=_cparams("arbitrary"),
        name="combine",
    )(segments, nch, yb, loc, gt, x1, mods, g_post.reshape(1, d))


def _route_slots(tile_counts, n_tok):
    bm = MOE_ROWS
    n_tiles = tile_counts.shape[0]
    seg_len = -(-tile_counts // SUBLANES) * SUBLANES
    seg_off = jnp.cumsum(seg_len, axis=1) - seg_len
    per_expert = jnp.sum(seg_len, axis=0)
    padded = -(-per_expert // bm) * bm
    p_end = jnp.cumsum(padded)
    seg_slot = (p_end - padded)[None, :] + jnp.cumsum(seg_len, axis=0) - seg_len
    segments = jnp.stack([seg_off, seg_slot, seg_len], axis=-1).reshape(-1).astype(jnp.int32)
    tails = jnp.stack([p_end - padded + per_expert, padded - per_expert], axis=-1).reshape(-1)
    tails = jnp.concatenate([tails, p_end[-1:] // bm]).astype(jnp.int32)
    n_chunks = jnp.sum(seg_len, axis=1) // SUBLANES
    starts = jnp.arange(_moe_blocks(n_tok, n_tiles), dtype=jnp.int32) * bm
    block_exp = jnp.minimum(jnp.sum(p_end[None, :] <= starts[:, None], axis=1), N_EXPERTS - 1).astype(jnp.int32)
    n_used = (p_end[-1] // bm).astype(jnp.int32).reshape(1)
    return segments, n_chunks.astype(jnp.int32), tails, block_exp, n_used


def _moe(hf, loc, segments, n_chunks, tails, block_exp, n_used, p):
    xb = _dispatch(hf, loc, segments, n_chunks, tails)
    w_gu = _gate_up_rows(p['w_gate_up'])
    b_gu = p['b_gate_up']
    b_gu = jnp.concatenate([b_gu[..., 0::2], b_gu[..., 1::2]], axis=-1)[:, None, :]
    return _experts(xb, block_exp, n_used, w_gu, b_gu, p['w_down'], p['b_down'][:, None, :])


def _layer(x, c, ctx, c_ctx, p):
    batch, seq, d = x.shape
    ctx_len = ctx.shape[1]
    n_mod = p['w_ada'].shape[1] // d

    c_rows = jnp.zeros((2 * SUBLANES, d), F32).at[:batch].set(c).at[batch].set(c_ctx)
    mods = _ada(c_rows, p['w_ada'], p['b_ada']).reshape(2 * SUBLANES, n_mod, d)
    mods = jnp.pad(mods, ((0, 0), (0, SUBLANES - n_mod), (0, 0)))

    w_in = p['w_in'].astype(BF16)
    w_att, w_rwkv, w_gate = (w_in[:, :ATT_COLS], w_in[:, ATT_COLS:ATT_COLS + RWKV_COLS],
                             w_in[:, ATT_COLS + RWKV_COLS:])
    b_in = p['b_in'].reshape(1, -1)
    b_att, b_rwkv, b_gate = (b_in[:, :ATT_COLS], b_in[:, ATT_COLS:ATT_COLS + RWKV_COLS],
                             b_in[:, ATT_COLS + RWKV_COLS:])
    g_pre = p['g_pre_mix'].reshape(1, d)
    proj = functools.partial(_project, g_pre=g_pre, w_att=w_att, w_rwkv=w_rwkv, w_gate=w_gate,
                             b_att=b_att, b_rwkv=b_rwkv, b_gate=b_gate, feat_params=_feature_params(p))

    zc_att, _, fc = proj(ctx.reshape(batch * ctx_len, d), mods, lambda b: batch, ctx_len, rope=False)
    zero_state = jnp.zeros((batch, RWKV_HEADS, HEAD_DIM, HEAD_DIM), F32)
    _, _, sc_f, sc_b = _scan(fc, batch, ctx_len, zero_state, zero_state)

    x2 = x.reshape(batch * seq, d)
    z_att, z_gate, fx = proj(x2, mods, lambda b: b, seq, rope=True)
    y_f, y_b, _, _ = _scan(fx, batch, seq, sc_f, sc_b)
    att = _attention(z_att, zc_att, p['att_sinks'], batch, seq)
    x1, hf, gt, loc, cnt = _merge(y_f, y_b, fx, att, z_gate, x2, mods, seq, p)

    segments, n_chunks, tails, block_exp, n_used = _route_slots(cnt[:, 0, :N_EXPERTS], batch * seq)
    yb = _moe(hf, loc, segments, n_chunks, tails, block_exp, n_used, p)
    out = _combine(yb, loc, segments, n_chunks, gt, x1, mods, seq, p['g_post_ffn'])
    return out.reshape(batch, seq, d)


def kernel(x, c, ctx, c_ctx, w_ada, b_ada, g_pre_mix, g_post_mix, g_pre_ffn, g_post_ffn, w_in, b_in, mu_prev, mu_next, att_sinks, w0_f, w0_b, w2_f, w2_b, a0_f, a0_b, a2_f, a2_b, g2, k_k, k_a, r_k, ln_x_w, ln_x_b, w_up_att, w_up_rwkv, w_out, w_router, b_router, w_gate_up, b_gate_up, w_down, b_down):
    assert w_ada.shape[0] == 1, "single-layer problem: the context stream update is never consumed"
    p = dict(w_ada=w_ada[0], b_ada=b_ada[0], g_pre_mix=g_pre_mix[0], g_post_mix=g_post_mix[0],
             g_pre_ffn=g_pre_ffn[0], g_post_ffn=g_post_ffn[0], w_in=w_in[0], b_in=b_in[0],
             mu_prev=mu_prev[0], mu_next=mu_next[0], att_sinks=att_sinks[0], w0_f=w0_f[0], w0_b=w0_b[0],
             w2_f=w2_f[0], w2_b=w2_b[0], a0_f=a0_f[0], a0_b=a0_b[0], a2_f=a2_f[0], a2_b=a2_b[0], g2=g2[0],
             k_k=k_k[0], k_a=k_a[0], r_k=r_k[0].reshape(-1), ln_x_w=ln_x_w[0], ln_x_b=ln_x_b[0],
             w_up_att=w_up_att[0], w_up_rwkv=w_up_rwkv[0], w_out=w_out[0], w_router=w_router[0],
             b_router=b_router[0], w_gate_up=w_gate_up[0], b_gate_up=b_gate_up[0], w_down=w_down[0],
             b_down=b_down[0])
    return _layer(x, c, ctx, c_ctx, p)
```

```python
import functools

import jax
import jax.numpy as jnp
import numpy as np
from jax import lax
from jax.experimental import pallas as pl
from jax.experimental.pallas import tpu as pltpu

F32 = jnp.float32
BF16 = jnp.bfloat16

GRID_W = 64
HEAD_DIM = 64
ATT_HEADS = 8
ATT_KV_HEADS = 2
ATT_GROUPS = ATT_HEADS // ATT_KV_HEADS
ATT_DIM = ATT_HEADS * HEAD_DIM
KV_DIM = ATT_KV_HEADS * HEAD_DIM
WINDOW = 128
BLOCK = 128
ROPE_BASE = 10000.0
ATT_SCALE = HEAD_DIM ** -0.5
NEG_INF = -1e30
RWKV_HEADS = 8
RWKV_DIM = RWKV_HEADS * HEAD_DIM
DECAY_LORA = 64
ICL_LORA = 64
GATE_LORA = 128
GN_EPS = 64e-5
N_EXPERTS = 32
TOP_K = 4
SWIGLU_LIMIT = 7.0
SWIGLU_ALPHA = 1.702
RMS_EPS = 1e-6
ATT_COLS = ATT_DIM + 2 * KV_DIM
RWKV_COLS = 3 * RWKV_DIM + 2 * DECAY_LORA + 2 * ICL_LORA + GATE_LORA
ROPE_COLS = ATT_DIM + KV_DIM

LANES = 128
SUBLANES = 8
VMEM_LIMIT = 48 * 1024 * 1024

ROW_TILE = 256
SCAN_CHUNK = 64
SCAN_GROUP = 4
SCAN_BATCH = 4
MOE_ROWS = 512


def _cparams(*sem):
    return pltpu.CompilerParams(dimension_semantics=sem, vmem_limit_bytes=VMEM_LIMIT)


def _mm(a, b):
    return jnp.dot(a.astype(BF16), b.astype(BF16), preferred_element_type=F32)


def _mm_nt(a, b):
    return lax.dot_general(a.astype(BF16), b.astype(BF16), (((1,), (1,)), ((), ())),
                           preferred_element_type=F32)


def _split2(x):
    hi = x.astype(BF16)
    lo = (x - hi.astype(F32)).astype(BF16)
    return hi, lo


def _split3(x):
    hi = x.astype(BF16)
    r1 = x - hi.astype(F32)
    mid = r1.astype(BF16)
    lo = (r1 - mid.astype(F32)).astype(BF16)
    return hi, mid, lo


def _mm_f32(a, b):
    ah, al = _split2(a)
    bh, bl = _split2(b)
    d = functools.partial(jnp.dot, preferred_element_type=F32)
    return d(ah, bh) + d(ah, bl) + d(al, bh)


def _seg_sum(x, eseg):
    hi, lo = _split2(x)
    d = functools.partial(jnp.dot, preferred_element_type=F32)
    return d(hi, eseg) + d(lo, eseg)


def _rms(x):
    return x * lax.rsqrt(jnp.mean(x * x, axis=-1, keepdims=True) + RMS_EPS)


def _sigmoid(x):
    return 0.5 * jnp.tanh(0.5 * x) + 0.5


def _ada_kernel(c_ref, w_ref, b_ref, o_ref):
    c = c_ref[...]
    o_ref[...] = _mm_f32(c * _sigmoid(c), w_ref[...]) + b_ref[...]


def _ada(c_rows, w, b):
    m, d = c_rows.shape
    n = w.shape[1]
    tn = 1536
    return pl.pallas_call(
        _ada_kernel,
        grid=(n // tn,),
        in_specs=[pl.BlockSpec((m, d), lambda j: (0, 0)),
                  pl.BlockSpec((d, tn), lambda j: (0, j)),
                  pl.BlockSpec((1, tn), lambda j: (0, j))],
        out_specs=pl.BlockSpec((m, tn), lambda j: (0, j)),
        out_shape=jax.ShapeDtypeStruct((m, n), F32),
        compiler_params=_cparams("arbitrary"),
        name="ada",
    )(c_rows, w, b.reshape(1, n))


def _proj_kernel(*refs, rope, tiles_per_seq):
    n_rope = 2 if rope else 0
    (x_ref, xp_ref, xn_ref, mod_ref, g_ref, wa_ref, wr_ref, wg_ref, ba_ref, br_ref, bg_ref) = refs[:11]
    cos_ref, sin_ref = refs[11:11 + n_rope] if rope else (None, None)
    (mup_ref, mun_ref, w2_ref, w0_ref, a2_ref, a0_ref, g2_ref, kk_ref, ka_ref, eseg_ref) = refs[11 + n_rope:21 + n_rope]
    za_ref, zg_ref = refs[21 + n_rope:23 + n_rope]
    r_o, v_o, kf_o, kb_o, lwf_o, lwb_o, kkn_o, bf_o, bb_o, gate_o = refs[23 + n_rope:]
    mod = mod_ref[0]
    modulated = lambda xv: _rms(xv) * g_ref[...] * (1.0 + mod[1:2]) + mod[0:1]
    h = modulated(x_ref[...])
    hb = h.astype(BF16)
    za = jnp.dot(hb, wa_ref[...], preferred_element_type=F32) + ba_ref[...]
    if rope:
        qk = za[:, :ROPE_COLS]
        lane = lax.broadcasted_iota(jnp.int32, qk.shape, 1)
        low = (lane % 32) < 16
        partner = jnp.where(low, pltpu.roll(qk, ROPE_COLS - 16, 1), pltpu.roll(qk, 16, 1))
        cos = jnp.concatenate([cos_ref[...]] * (ROPE_COLS // LANES), axis=1)
        sin = jnp.concatenate([sin_ref[...]] * (ROPE_COLS // LANES), axis=1)
        za_ref[:, :ROPE_COLS] = (qk * cos + partner * sin).astype(za_ref.dtype)
        za_ref[:, ROPE_COLS:] = za[:, ROPE_COLS:].astype(za_ref.dtype)
    else:
        za_ref[...] = za.astype(za_ref.dtype)
    zg_ref[...] = (jnp.dot(hb, wg_ref[...], preferred_element_type=F32) + bg_ref[...]).astype(zg_ref.dtype)

    tm = x_ref.shape[0]
    h_ext = jnp.concatenate([modulated(xp_ref[...]), h, modulated(xn_ref[...])], axis=0).astype(BF16)
    z_ext = jnp.dot(h_ext, wr_ref[...], preferred_element_type=F32) + br_ref[...]
    z = z_ext[SUBLANES:SUBLANES + tm]
    ti = pl.program_id(0) % tiles_per_seq
    row = lax.broadcasted_iota(jnp.int32, (tm, 1), 0)
    prev_halo = jnp.where(ti == 0, 0.0, z_ext[SUBLANES - 1:SUBLANES])
    next_halo = jnp.where(ti == tiles_per_seq - 1, 0.0, z_ext[SUBLANES + tm:SUBLANES + tm + 1])
    prev = jnp.where(row == 0, prev_halo, pltpu.roll(z, 1, 0))
    nxt = jnp.where(row == tm - 1, next_halo, pltpu.roll(z, tm - 1, 0))
    zs = z + mup_ref[...] * (prev - z) + mun_ref[...] * (nxt - z)

    d = RWKV_DIM
    r = zs[:, 0:d]
    k = zs[:, d:2 * d]
    v = zs[:, 2 * d:3 * d]
    o = 3 * d
    wl = zs[:, o:o + 2 * DECAY_LORA]
    al = zs[:, o + 2 * DECAY_LORA:o + 2 * DECAY_LORA + 2 * ICL_LORA]
    gl = zs[:, o + 2 * DECAY_LORA + 2 * ICL_LORA:]

    w = w0_ref[...] + _mm(jnp.tanh(wl), w2_ref[...])
    lw = -float(np.exp(-0.5)) * _sigmoid(w)
    icl = _sigmoid(a0_ref[...] + _mm(al, a2_ref[...]))
    kk0 = k * kk_ref[...]
    ss = _seg_sum(kk0 * kk0, eseg_ref[...])
    kk = kk0 / jnp.maximum(jnp.sqrt(ss), 1e-12)
    ka = ka_ref[...]
    icl_f = icl[:, :d]
    icl_b = icl[:, d:]

    r_o[...] = r.astype(r_o.dtype)
    v_o[...] = v.astype(v_o.dtype)
    kf_o[...] = (k * (1.0 + (icl_f - 1.0) * ka)).astype(kf_o.dtype)
    kb_o[...] = (k * (1.0 + (icl_b - 1.0) * ka)).astype(kb_o.dtype)
    lwf_o[...] = lw[:, :d]
    lwb_o[...] = lw[:, d:]
    kkn_o[...] = kk.astype(kkn_o.dtype)
    bf_o[...] = (kk * icl_f).astype(bf_o.dtype)
    bb_o[...] = (kk * icl_b).astype(bb_o.dtype)
    gate_o[...] = _mm(_sigmoid(gl), g2_ref[...]).astype(gate_o.dtype)


def _rope_tables(seq):
    n_rows = seq // GRID_W
    row = jnp.repeat(jnp.arange(n_rows, dtype=F32), GRID_W, total_repeat_length=seq)
    col = jnp.tile(jnp.arange(GRID_W, dtype=F32), n_rows)
    half = HEAD_DIM // 2
    inv_freq = ROPE_BASE ** (-jnp.arange(0, half, 2, dtype=F32) / half)
    ang_r = row[:, None] * inv_freq[None, :]
    ang_c = col[:, None] * inv_freq[None, :]
    cos_h = jnp.concatenate([jnp.cos(ang_r), jnp.cos(ang_r), jnp.cos(ang_c), jnp.cos(ang_c)], axis=1)
    sin_h = jnp.concatenate([-jnp.sin(ang_r), jnp.sin(ang_r), -jnp.sin(ang_c), jnp.sin(ang_c)], axis=1)
    reps = LANES // HEAD_DIM
    return jnp.tile(cos_h, (1, reps)), jnp.tile(sin_h, (1, reps))


def _project(x2, mods, mod_row, seq, g_pre, w_att, w_rwkv, w_gate, b_att, b_rwkv, b_gate, feat_params, rope):
    rows, d = x2.shape
    tm = ROW_TILE
    tps = seq // tm
    hb = tm // SUBLANES
    nhb = rows // SUBLANES
    rd = RWKV_DIM
    const = lambda i: (0, 0)
    in_specs = [pl.BlockSpec((tm, d), lambda i: (i, 0)),
                pl.BlockSpec((SUBLANES, d), lambda i: (jnp.maximum(i * hb - 1, 0), 0)),
                pl.BlockSpec((SUBLANES, d), lambda i: (jnp.minimum((i + 1) * hb, nhb - 1), 0)),
                pl.BlockSpec((1, SUBLANES, d), lambda i: (mod_row(i // tps), 0, 0)),
                pl.BlockSpec((1, d), const),
                pl.BlockSpec(w_att.shape, const), pl.BlockSpec(w_rwkv.shape, const),
                pl.BlockSpec(w_gate.shape, const),
                pl.BlockSpec((1, ATT_COLS), const), pl.BlockSpec((1, RWKV_COLS), const),
                pl.BlockSpec((1, w_gate.shape[1]), const)]
    args = [x2, x2, x2, mods, g_pre, w_att, w_rwkv, w_gate, b_att, b_rwkv, b_gate]
    if rope:
        cos, sin = _rope_tables(seq)
        in_specs += [pl.BlockSpec((tm, LANES), lambda i: (i % tps, 0))] * 2
        args += [cos, sin]
    in_specs += [pl.BlockSpec(a.shape, const) for a in feat_params]
    args += feat_params
    feat_spec = pl.BlockSpec((tm, rd), lambda i: (i, 0))
    outs = pl.pallas_call(
        functools.partial(_proj_kernel, rope=rope, tiles_per_seq=tps),
        grid=(rows // tm,),
        in_specs=in_specs,
        out_specs=[pl.BlockSpec((tm, ATT_COLS), lambda i: (i, 0)),
                   pl.BlockSpec((tm, w_gate.shape[1]), lambda i: (i, 0))] + [feat_spec] * len(FEAT_NAMES),
        out_shape=[jax.ShapeDtypeStruct((rows, ATT_COLS), BF16),
                   jax.ShapeDtypeStruct((rows, w_gate.shape[1]), BF16)]
                  + [jax.ShapeDtypeStruct((rows, rd), BF16 if name in FEAT_BF16 else F32) for name in FEAT_NAMES],
        compiler_params=_cparams("parallel"),
        name="proj",
    )(*args)
    return outs[0], outs[1], dict(zip(FEAT_NAMES, outs[2:]))


FEAT_NAMES = ('r', 'v', 'kf', 'kb', 'lwf', 'lwb', 'kk', 'bf', 'bb', 'gate')
FEAT_BF16 = ('r', 'v', 'kf', 'kb', 'kk', 'bf', 'bb', 'gate')


def _block_diag2(a, b):
    za = jnp.zeros_like(a)
    zb = jnp.zeros_like(b)
    return jnp.concatenate([jnp.concatenate([a, zb], axis=1), jnp.concatenate([za, b], axis=1)], axis=0)


def _seg_matrix():
    h = np.arange(RWKV_DIM) // HEAD_DIM
    return jnp.asarray((h[:, None] == h[None, :]).astype(np.float32), BF16)


def _feature_params(p):
    d = RWKV_DIM
    return [p['mu_prev'].reshape(1, -1), p['mu_next'].reshape(1, -1),
            _block_diag2(p['w2_f'], p['w2_b']).astype(BF16),
            jnp.concatenate([p['w0_f'], p['w0_b']]).reshape(1, 2 * d),
            _block_diag2(p['a2_f'], p['a2_b']).astype(BF16),
            jnp.concatenate([p['a0_f'], p['a0_b']]).reshape(1, 2 * d),
            p['g2'].astype(BF16), p['k_k'].reshape(1, d), p['k_a'].reshape(1, d), _seg_matrix()]


def _scan_kernel(rf, vf, kf, lwf, kkf, bf, rb, vb, kb, lwb, kkb, bb, s0f_ref, s0b_ref,
                 yf_ref, yb_ref, sTf_ref, sTb_ref, state_ref, *, chunk, n_chunks):
    c = pl.program_id(1)
    C = chunk
    hd = HEAD_DIM
    gw = SCAN_GROUP * hd
    n_groups = RWKV_HEADS // SCAN_GROUP
    assert C == hd, "the triangular masks below are shared between time and channel blocks"

    n_rows = state_ref.shape[0]

    @pl.when(c == 0)
    def _():
        for n in range(n_rows):
            for d, s0_ref in enumerate((s0f_ref, s0b_ref)):
                for g in range(n_groups):
                    state_ref[n, d, g] = jnp.zeros((gw, gw), F32)
                    for j in range(SCAN_GROUP):
                        state_ref[n, d, g, j * hd:(j + 1) * hd, j * hd:(j + 1) * hd] = (
                            s0_ref[n, g * SCAN_GROUP + j])

    ti = lax.broadcasted_iota(jnp.int32, (C, C), 0)
    si = lax.broadcasted_iota(jnp.int32, (C, C), 1)
    tg = lax.broadcasted_iota(jnp.int32, (C, gw), 0)
    sg = lax.broadcasted_iota(jnp.int32, (C, gw), 1) % C
    eye = (tg == sg).astype(F32)
    same_head = (lax.broadcasted_iota(jnp.int32, (gw, gw), 0) // hd
                 == lax.broadcasted_iota(jnp.int32, (gw, gw), 1) // hd)
    same_head_b = same_head.astype(BF16)
    n_double = int(np.log2(C)) - 1

    def bdiag(x_cat):
        return jnp.concatenate([x_cat.astype(BF16)] * SCAN_GROUP, axis=0) * same_head_b

    dirs = ((rf, vf, kf, lwf, kkf, bf), (rb, vb, kb, lwb, kkb, bb))
    units = []
    masks = [(((si <= ti) if d == 0 else (si >= ti)).astype(BF16),
              (sg <= tg) if d == 0 else (sg >= tg),
              (sg < tg) if d == 0 else (sg > tg)) for d in range(2)]
    for n, d in [(n, d) for n in range(n_rows) for d in range(2)]:
        r_ref, v_ref, k_ref, lw_ref, kk_ref, b_ref = (ref.at[n] for ref in dirs[d])
        tri, incl_g, strict_g = masks[d]
        lw = lw_ref[...]
        lh, lm, ll = _split3(lw)
        dd = functools.partial(jnp.dot, preferred_element_type=F32)
        cum = dd(tri, lh) + dd(tri, lm) + dd(tri, ll)
        cumx = cum - lw
        cum_end = cum[C - 1:C] if d == 0 else cum[0:1]
        e_neg = jnp.exp(-cum)
        e_end = jnp.exp(cum_end - cum)
        bv = b_ref[...].astype(F32)
        kv = k_ref[...].astype(F32)
        vv = v_ref[...].astype(F32)
        a_t = (-kk_ref[...].astype(F32) * jnp.exp(cumx)).astype(BF16)
        r_t = (r_ref[...].astype(F32) * jnp.exp(cum)).astype(BF16)
        b_t = (bv * e_neg).astype(BF16)
        k_t = (kv * e_neg).astype(BF16)
        b_q = (bv * e_end).astype(BF16)
        k_q = (kv * e_end).astype(BF16)
        g_end = jnp.exp(cum_end)
        for g in range(n_groups):
            sl = slice(g * gw, (g + 1) * gw)
            units.append(dict(
                n=n, d=d, g=g, sl=sl, incl=incl_g, strict=strict_g,
                P=jnp.concatenate([a_t[:, sl], r_t[:, sl]], axis=0),
                Q=jnp.concatenate([bdiag(b_t[:, sl]), bdiag(k_t[:, sl])], axis=0),
                Qq=jnp.concatenate([b_q[:, sl], k_q[:, sl]], axis=0),
                V=vv[:, sl], g_end=g_end[:, sl]))

    for u in units:
        u['G'] = _mm_nt(u['P'], u['Q'])
    for u in units:
        u['S0'] = state_ref[u['n'], u['d'], u['g']]
        u['PH'] = _mm_nt(u['P'], u['S0'])
        u['Vd'] = bdiag(u['V'])
    for u in units:
        G = u.pop('G')
        u['N'] = jnp.where(u['strict'], G[:C, :gw], 0.0)
        u['a_ak'] = jnp.where(u['strict'], G[:C, gw:], 0.0)
        u['a_rb'] = jnp.where(u['incl'], G[C:, :gw], 0.0)
        u['a_rk'] = jnp.where(u['incl'], G[C:, gw:], 0.0)
    for u in units:
        u['T'] = eye + u['N']
        u['Pw'] = _mm(u['N'], bdiag(u['N']))
        av = _mm(jnp.concatenate([u['a_ak'], u['a_rk']], axis=0), u['Vd'])
        u['rhs'] = u['PH'][:C] + av[:C]
        u['y0'] = u['PH'][C:] + av[C:]
    for lvl in range(n_double):
        for u in units:
            pw = bdiag(u['Pw'])
            if lvl + 1 < n_double:
                both = _mm(jnp.concatenate([u['T'], u['Pw']], axis=0), pw)
                u['T'] = u['T'] + both[:C]
                u['Pw'] = both[C:]
            else:
                u['T'] = u['T'] + _mm(u['T'], pw)
    for u in units:
        u['U'] = _mm(u['T'], bdiag(u['rhs']))
    y_refs = (yf_ref, yb_ref)
    for u in units:
        y_refs[u['d']][u['n'], :, u['sl']] = u['y0'] + _mm(u['a_rb'], bdiag(u['U']))
        uv = jnp.concatenate([u['U'], u['V']], axis=0)
        state_ref[u['n'], u['d'], u['g']] = jnp.where(
            same_head, u['S0'] * u['g_end'] + _mm(uv.T, u['Qq']), 0.0)

    @pl.when(c == n_chunks - 1)
    def _():
        for n in range(n_rows):
            for d, sT_ref in enumerate((sTf_ref, sTb_ref)):
                for g in range(n_groups):
                    for j in range(SCAN_GROUP):
                        sT_ref[n, g * SCAN_GROUP + j] = (
                            state_ref[n, d, g, j * hd:(j + 1) * hd, j * hd:(j + 1) * hd])


def _scan(f, batch, seq, s0_f, s0_b):
    C = SCAN_CHUNK
    nC = seq // C
    d = RWKV_DIM
    nb = max(n for n in range(1, SCAN_BATCH + 1) if batch % n == 0)
    fwd = pl.BlockSpec((nb, C, d), lambda b, c: (b, c, 0))
    bwd = pl.BlockSpec((nb, C, d), lambda b, c: (b, nC - 1 - c, 0))
    st = pl.BlockSpec((nb, RWKV_HEADS, HEAD_DIM, HEAD_DIM), lambda b, c: (b, 0, 0, 0))
    st_shape = jax.ShapeDtypeStruct((batch, RWKV_HEADS, HEAD_DIM, HEAD_DIM), F32)
    y_shape = jax.ShapeDtypeStruct((batch, seq, d), F32)
    gw = SCAN_GROUP * HEAD_DIM
    arr = lambda name: f[name].reshape(batch, seq, d)
    y_f, y_b, sT_f, sT_b = pl.pallas_call(
        functools.partial(_scan_kernel, chunk=C, n_chunks=nC),
        grid=(batch // nb, nC),
        in_specs=[fwd] * 6 + [bwd] * 6 + [st, st],
        out_specs=[fwd, bwd, st, st],
        out_shape=[y_shape, y_shape, st_shape, st_shape],
        scratch_shapes=[pltpu.VMEM((nb, 2, RWKV_HEADS // SCAN_GROUP, gw, gw), F32)],
        compiler_params=_cparams("parallel", "arbitrary"),
        name="scan",
    )(arr('r'), arr('v'), arr('kf'), arr('lwf'), arr('kk'), arr('bf'),
      arr('r'), arr('v'), arr('kb'), arr('lwb'), arr('kk'), arr('bb'), s0_f, s0_b)
    return y_f.reshape(batch * seq, d), y_b.reshape(batch * seq, d), sT_f, sT_b


def _attn_kernel(sink_ref, q_ref, kp_ref, km_ref, kn_ref, vp_ref, vm_ref, vn_ref, kc_ref, vc_ref, o_ref,
                 *, n_blocks):
    n = pl.program_id(1)
    rows = ATT_GROUPS * BLOCK
    q = q_ref[...] * ATT_SCALE
    qi = lax.broadcasted_iota(jnp.int32, (rows, BLOCK), 0) % BLOCK
    kj = lax.broadcasted_iota(jnp.int32, (rows, BLOCK), 1)
    ok_prev = (kj >= qi) & (n > 0)
    ok_next = (kj <= qi) & (n < n_blocks - 1)
    rowh = lax.broadcasted_iota(jnp.int32, (rows, 1), 0) // BLOCK
    outs = [None] * ATT_HEADS
    for g in range(ATT_KV_HEADS):
        ks = slice(g * HEAD_DIM, (g + 1) * HEAD_DIM)
        heads = [g * ATT_GROUPS + j for j in range(ATT_GROUPS)]
        Qs = jnp.concatenate([q[:, h * HEAD_DIM:(h + 1) * HEAD_DIM] for h in heads], axis=0)
        scores = [jnp.where(ok_prev, _mm_nt(Qs, kp_ref[:, ks]), NEG_INF),
                  _mm_nt(Qs, km_ref[:, ks]),
                  jnp.where(ok_next, _mm_nt(Qs, kn_ref[:, ks]), NEG_INF),
                  _mm_nt(Qs, kc_ref[:, ks])]
        values = [vp_ref, vm_ref, vn_ref, vc_ref]
        sink = jnp.zeros((rows, 1), F32)
        for j, h in enumerate(heads):
            sink = jnp.where(rowh == j, sink_ref[h], sink)
        folded = None
        for sc in scores:
            for c0 in range(0, sc.shape[1], BLOCK):
                blk = sc[:, c0:c0 + BLOCK]
                folded = blk if folded is None else jnp.maximum(folded, blk)
        m = jnp.maximum(sink, jnp.max(folded, axis=-1, keepdims=True))
        acc = jnp.zeros((rows, 2 * HEAD_DIM), F32)
        for sc, v_ref in zip(scores, values):
            one_col = (lax.broadcasted_iota(jnp.int32, (v_ref.shape[0], HEAD_DIM), 1) == 0).astype(BF16)
            v_ext = jnp.concatenate([v_ref[:, ks], one_col], axis=1)
            acc = acc + jnp.dot(jnp.exp((sc - m).astype(BF16)), v_ext, preferred_element_type=F32)
        den = acc[:, HEAD_DIM:HEAD_DIM + 1] + jnp.exp(sink - m)
        O = acc[:, :HEAD_DIM] / den
        for j, h in enumerate(heads):
            outs[h] = O[j * BLOCK:(j + 1) * BLOCK]
    o_ref[...] = jnp.concatenate(outs, axis=1).astype(o_ref.dtype)


def _attention(z_att, zc_att, sinks, batch, seq):
    nb = seq // BLOCK
    ctx_len = zc_att.shape[0] // batch
    kcol = ATT_DIM // KV_DIM
    vcol = kcol + 1

    def kv_spec(col, off):
        return pl.BlockSpec((BLOCK, KV_DIM), lambda b, n: (b * nb + jnp.clip(n + off, 0, nb - 1), col))

    return pl.pallas_call(
        functools.partial(_attn_kernel, n_blocks=nb),
        grid=(batch, nb),
        in_specs=[pl.BlockSpec(memory_space=pltpu.SMEM),
                  pl.BlockSpec((BLOCK, ATT_DIM), lambda b, n: (b * nb + n, 0)),
                  kv_spec(kcol, -1), kv_spec(kcol, 0), kv_spec(kcol, 1),
                  kv_spec(vcol, -1), kv_spec(vcol, 0), kv_spec(vcol, 1),
                  pl.BlockSpec((ctx_len, KV_DIM), lambda b, n: (b, kcol)),
                  pl.BlockSpec((ctx_len, KV_DIM), lambda b, n: (b, vcol))],
        out_specs=pl.BlockSpec((BLOCK, ATT_DIM), lambda b, n: (b * nb + n, 0)),
        out_shape=jax.ShapeDtypeStruct((batch * seq, ATT_DIM), BF16),
        compiler_params=_cparams("parallel", "parallel"),
        name="attn",
    )(sinks, z_att, z_att, z_att, z_att, z_att, z_att, z_att, zc_att, zc_att)


def _merge_kernel(yf_ref, yb_ref, r_ref, kf_ref, kb_ref, v_ref, gate_ref, att_ref, zg_ref, x_ref, mod_ref,
                  lnw_ref, lnb_ref, rk_ref, eseg_ref, wua_ref, wur_ref, wo_ref, gpm_ref, gpf_ref,
                  wr_ref, br_ref,
                  x1_ref, hf_ref, gt_ref, loc_ref, cnt_ref):
    eseg = eseg_ref[...]
    inv_n = 1.0 / HEAD_DIM
    y = yf_ref[...] + yb_ref[...]
    mean = _seg_sum(y, eseg) * inv_n
    dy = y - mean
    var = _seg_sum(dy * dy, eseg) * inv_n
    yn = dy * lax.rsqrt(var + GN_EPS) * lnw_ref[...] + lnb_ref[...]
    f32 = lambda ref: ref[...].astype(F32)
    bonus = _seg_sum(f32(r_ref) * (f32(kf_ref) + f32(kb_ref)) * rk_ref[...], eseg) * f32(v_ref)
    rwk = (yn + bonus) * f32(gate_ref)

    d = x_ref.shape[1]
    zg = zg_ref[...].astype(F32)
    merged = (_sigmoid(zg[:, :d]) * jnp.dot(att_ref[...], wua_ref[...], preferred_element_type=F32)
              + _sigmoid(zg[:, d:]) * _mm(rwk, wur_ref[...]))
    mix = _mm(merged, wo_ref[...])
    mod = mod_ref[0]
    x1 = x_ref[...] + mod[2:3] * (_rms(mix) * gpm_ref[...])
    x1_ref[...] = x1
    hf = _rms(x1) * gpf_ref[...] * (1.0 + mod[4:5]) + mod[3:4]
    hf_ref[...] = hf.astype(hf_ref.dtype)

    logits = _mm_f32(hf, wr_ref[...]) + br_ref[...]
    lane = lax.broadcasted_iota(jnp.int32, logits.shape, 1).astype(F32)
    val_out = jnp.zeros(logits.shape, F32)
    picked = jnp.zeros(logits.shape, F32)
    sels = []
    top = None
    den = None
    for kth in range(TOP_K):
        m = jnp.max(logits, axis=-1, keepdims=True)
        sel = jnp.min(jnp.where(logits == m, lane, float(LANES)), axis=-1, keepdims=True)
        hit = lane == sel
        logits = jnp.where(hit, -jnp.inf, logits)
        picked = jnp.where(hit, 1.0, picked)
        sels.append(hit)
        if kth == 0:
            top = m
        e = jnp.exp(m - top)
        den = e if kth == 0 else den + e
        val_out = jnp.where(lane == float(kth), e, val_out)
    gt_ref[...] = val_out / den

    tm = picked.shape[0]
    earlier = (lax.broadcasted_iota(jnp.int32, (tm, tm), 1)
               < lax.broadcasted_iota(jnp.int32, (tm, tm), 0)).astype(BF16)
    before = jnp.dot(earlier, picked.astype(BF16), preferred_element_type=F32)
    counts = jnp.sum(picked, axis=0, keepdims=True)
    seg_len = jnp.floor((counts + (SUBLANES - 1)) * (1.0 / SUBLANES)) * SUBLANES
    lower_expert = (lax.broadcasted_iota(jnp.int32, (LANES, LANES), 0)
                    < lax.broadcasted_iota(jnp.int32, (LANES, LANES), 1)).astype(BF16)
    seg_off = jnp.dot(jnp.broadcast_to(seg_len, (SUBLANES, LANES)).astype(BF16), lower_expert,
                      preferred_element_type=F32)[0:1]
    loc_out = jnp.zeros(logits.shape, F32)
    for kth, hit in enumerate(sels):
        row = jnp.sum(jnp.where(hit, before + seg_off, 0.0), axis=-1, keepdims=True)
        loc_out = jnp.where(lane == float(kth), row, loc_out)
    loc_ref[...] = loc_out.astype(jnp.int32)
    cnt_ref[0] = jnp.broadcast_to(counts, cnt_ref.shape[1:]).astype(jnp.int32)


def _moe_blocks(n_tok, n_tiles):
    return -(-(n_tok * TOP_K + (SUBLANES - 1) * N_EXPERTS * n_tiles) // MOE_ROWS) + N_EXPERTS


def _merge(y_f, y_b, f, att, z_gate, x2, mods, seq, p):
    rows, d = x2.shape
    tm = ROW_TILE
    tps = seq // tm
    rd = RWKV_DIM
    const = lambda i: (0, 0)
    row = lambda w: pl.BlockSpec((tm, w), lambda i: (i, 0))
    vec = lambda w: pl.BlockSpec((1, w), const)
    w_router = jnp.pad(p['w_router'], ((0, 0), (0, LANES - N_EXPERTS)))
    b_router = jnp.pad(p['b_router'], (0, LANES - N_EXPERTS), constant_values=NEG_INF).reshape(1, LANES)
    return pl.pallas_call(
        _merge_kernel,
        grid=(rows // tm,),
        in_specs=[row(rd)] * 7 + [row(ATT_DIM), row(2 * d), row(d),
                  pl.BlockSpec((1, SUBLANES, d), lambda i: (i // tps, 0, 0)),
                  vec(rd), vec(rd), vec(rd), pl.BlockSpec((rd, rd), const),
                  pl.BlockSpec((ATT_DIM, d), const), pl.BlockSpec((rd, d), const), pl.BlockSpec((d, d), const),
                  vec(d), vec(d), pl.BlockSpec((d, LANES), const), vec(LANES)],
        out_specs=[row(d), row(d), row(LANES), row(LANES),
                   pl.BlockSpec((1, SUBLANES, LANES), lambda i: (i, 0, 0))],
        out_shape=[jax.ShapeDtypeStruct((rows, d), F32), jax.ShapeDtypeStruct((rows, d), BF16),
                   jax.ShapeDtypeStruct((rows, LANES), F32),
                   jax.ShapeDtypeStruct((rows, LANES), jnp.int32),
                   jax.ShapeDtypeStruct((rows // tm, SUBLANES, LANES), jnp.int32)],
        compiler_params=_cparams("parallel"),
        name="merge",
    )(y_f, y_b, f['r'], f['kf'], f['kb'], f['v'], f['gate'], att, z_gate, x2, mods,
      p['ln_x_w'].reshape(1, rd), p['ln_x_b'].reshape(1, rd), p['r_k'].reshape(1, rd), _seg_matrix(),
      p['w_up_att'].astype(BF16), p['w_up_rwkv'].astype(BF16), p['w_out'].astype(BF16),
      p['g_post_mix'].reshape(1, d), p['g_pre_ffn'].reshape(1, d), w_router, b_router)


def _moe_kernel(be_ref, nused_ref, x_ref, wgu_ref, bgu_ref, wdn_f32_ref, bdn_ref, o_ref, wdn_ref):
    i = pl.program_id(0)

    @pl.when((i == 0) | (be_ref[i] != be_ref[jnp.maximum(i - 1, 0)]))
    def _():
        wdn_ref[...] = wdn_f32_ref[...].astype(BF16)

    @pl.when(i < nused_ref[0])
    def _():
        de = wdn_ref.shape[0]
        d = x_ref.shape[1]
        x = x_ref[...].astype(BF16)
        bgu = bgu_ref[...]
        gate = jnp.minimum(_mm_nt(x, wgu_ref[:, :d]) + bgu[:, :de], SWIGLU_LIMIT)
        up = jnp.clip(_mm_nt(x, wgu_ref[:, d:]) + bgu[:, de:], -SWIGLU_LIMIT, SWIGLU_LIMIT)
        act = (up + 1.0) * (gate * _sigmoid(SWIGLU_ALPHA * gate))
        o_ref[...] = _mm(act, wdn_ref[...]) + bdn_ref[...]

    @pl.when(i >= nused_ref[0])
    def _():
        o_ref[...] = jnp.zeros(o_ref.shape, o_ref.dtype)


def _wprep_kernel(w_ref, o_ref, t_ref):
    d = w_ref.shape[1]
    half = o_ref.shape[1]
    wt = w_ref[0].T
    for c in range(d // LANES):
        cols = slice(c * LANES, (c + 1) * LANES)
        t_ref[c] = wt[:, cols]
        o_ref[0, :, cols] = t_ref[c, pl.ds(0, half, stride=2), :].astype(o_ref.dtype)
        o_ref[0, :, d + c * LANES:d + (c + 1) * LANES] = t_ref[c, pl.ds(1, half, stride=2), :].astype(o_ref.dtype)


def _gate_up_rows(w_gu):
    n_exp, d, de2 = w_gu.shape
    tc = 4 * LANES
    return pl.pallas_call(
        _wprep_kernel,
        grid=(n_exp, de2 // tc),
        in_specs=[pl.BlockSpec((1, d, tc), lambda e, j: (e, 0, j))],
        out_specs=pl.BlockSpec((1, tc // 2, 2 * d), lambda e, j: (e, j, 0)),
        out_shape=jax.ShapeDtypeStruct((n_exp, de2 // 2, 2 * d), BF16),
        scratch_shapes=[pltpu.VMEM((d // LANES, tc, LANES), F32)],
        compiler_params=_cparams("parallel", "parallel"),
        name="wprep",
    )(w_gu)


def _experts(xb, block_exp, n_used, w_gu, b_gu, w_dn, b_dn):
    n_slots, d = xb.shape
    bm = MOE_ROWS
    n_blocks = n_slots // bm
    de = w_dn.shape[1]
    de2 = 2 * de
    grid_spec = pltpu.PrefetchScalarGridSpec(
        num_scalar_prefetch=2,
        grid=(n_blocks,),
        in_specs=[pl.BlockSpec((bm, d), lambda i, be, nu: (jnp.minimum(i, nu[0] - 1), 0)),
                  pl.BlockSpec((None, de, 2 * d), lambda i, be, nu: (be[i], 0, 0)),
                  pl.BlockSpec((None, 1, de2), lambda i, be, nu: (be[i], 0, 0)),
                  pl.BlockSpec((None, de, d), lambda i, be, nu: (be[i], 0, 0)),
                  pl.BlockSpec((None, 1, d), lambda i, be, nu: (be[i], 0, 0))],
        out_specs=pl.BlockSpec((bm, d), lambda i, be, nu: (i, 0)),
        scratch_shapes=[pltpu.VMEM((de, d), BF16)],
    )
    return pl.pallas_call(
        _moe_kernel,
        grid_spec=grid_spec,
        out_shape=jax.ShapeDtypeStruct((n_slots, d), F32),
        compiler_params=_cparams("arbitrary"),
        name="moe",
    )(block_exp, n_used, xb, w_gu, b_gu, w_dn, b_dn)


def _local_rows(tm):
    rows = tm * TOP_K + (SUBLANES - 1) * N_EXPERTS
    assert rows % SUBLANES == 0
    return rows


def _pick_matrix(loc, weights, n_cols):
    col = lax.broadcasted_iota(jnp.int32, (loc.shape[0], n_cols), 1)
    out = jnp.zeros(col.shape, F32)
    for k in range(TOP_K):
        out = jnp.where(col == loc[:, k:k + 1], weights[k], out)
    return out


def _segment(seg_ref, tile, e):
    base = (tile * N_EXPERTS + e) * 3
    return tuple(pl.multiple_of(seg_ref[base + i], SUBLANES) for i in range(3))


def _dispatch_kernel(seg_ref, nch_ref, tail_ref, hf_ref, loc_ref, xb_ref, xc_ref, zero_ref, sem, zsem):
    j = pl.program_id(0)
    n_tiles = pl.num_programs(0)
    slot = j % 2

    def tail_copy(e):
        row = pl.multiple_of(tail_ref[2 * e], SUBLANES)
        n = pl.multiple_of(tail_ref[2 * e + 1], SUBLANES)
        return n, pltpu.make_async_copy(zero_ref.at[pl.ds(0, n), :], xb_ref.at[pl.ds(row, n), :], zsem)

    def spare_block_copy(b):
        rows = xb_ref.at[pl.ds(pl.multiple_of(b * MOE_ROWS, MOE_ROWS), MOE_ROWS), :]
        return pltpu.make_async_copy(zero_ref, rows, zsem)

    first_spare = tail_ref[2 * N_EXPERTS]
    n_blocks = xb_ref.shape[0] // MOE_ROWS

    @pl.when(j == 0)
    def _():
        zero_ref[...] = jnp.zeros(zero_ref.shape, zero_ref.dtype)

        def body(e, carry):
            n, cp = tail_copy(e)

            @pl.when(n > 0)
            def _():
                cp.start()
            return carry
        lax.fori_loop(0, N_EXPERTS, body, 0)

        def spare(b, carry):
            spare_block_copy(b).start()
            return carry
        lax.fori_loop(first_spare, n_blocks, spare, 0)

    def drain(tile, s):
        rows = pl.multiple_of(nch_ref[tile] * SUBLANES, SUBLANES)
        done = xc_ref.at[s, pl.ds(0, rows), :]
        pltpu.make_async_copy(done, done, sem.at[s]).wait()

    sel = _pick_matrix(loc_ref[...], [1.0] * TOP_K, xc_ref.shape[1]).astype(BF16)
    xc_ref[slot] = lax.dot_general(sel, hf_ref[...], (((0,), (0,)), ((), ())), preferred_element_type=F32)

    def start(e, carry):
        off, row, n = _segment(seg_ref, j, e)

        @pl.when(n > 0)
        def _():
            pltpu.make_async_copy(xc_ref.at[slot, pl.ds(off, n), :], xb_ref.at[pl.ds(row, n), :],
                                  sem.at[slot]).start()
        return carry
    lax.fori_loop(0, N_EXPERTS, start, 0)

    @pl.when(j > 0)
    def _():
        drain(j - 1, 1 - slot)

    @pl.when(j == n_tiles - 1)
    def _():
        drain(j, slot)

        def body(e, carry):
            n, cp = tail_copy(e)

            @pl.when(n > 0)
            def _():
                cp.wait()
            return carry
        lax.fori_loop(0, N_EXPERTS, body, 0)

        def spare(b, carry):
            spare_block_copy(b).wait()
            return carry
        lax.fori_loop(first_spare, n_blocks, spare, 0)


def _dispatch(hf, loc, segments, nch, tails):
    rows, d = hf.shape
    tm = ROW_TILE
    n_slots = _moe_blocks(rows, rows // tm) * MOE_ROWS
    grid_spec = pltpu.PrefetchScalarGridSpec(
        num_scalar_prefetch=3,
        grid=(rows // tm,),
        in_specs=[pl.BlockSpec((tm, d), lambda i, sg, nc, tl: (i, 0)),
                  pl.BlockSpec((tm, LANES), lambda i, sg, nc, tl: (i, 0))],
        out_specs=pl.BlockSpec(memory_space=pl.ANY),
        scratch_shapes=[pltpu.VMEM((2, _local_rows(tm), d), F32), pltpu.VMEM((MOE_ROWS, d), F32),
                        pltpu.SemaphoreType.DMA((2,)), pltpu.SemaphoreType.DMA(())],
    )
    return pl.pallas_call(
        _dispatch_kernel,
        grid_spec=grid_spec,
        out_shape=jax.ShapeDtypeStruct((n_slots, d), F32),
        compiler_params=_cparams("arbitrary"),
        name="dispatch",
    )(segments, nch, tails, hf, loc)


def _combine_kernel(seg_ref, nch_ref, yb_ref, loc_ref, gt_ref, x1_ref, mod_ref, g_ref, o_ref, yc_ref, sem):
    j = pl.program_id(0)
    n_tiles = pl.num_programs(0)

    def issue(tile, s):
        def body(e, carry):
            off, row, n = _segment(seg_ref, tile, e)

            @pl.when(n > 0)
            def _():
                pltpu.make_async_copy(yb_ref.at[pl.ds(row, n), :], yc_ref.at[s, pl.ds(off, n), :],
                                      sem.at[s]).start()
            return carry
        lax.fori_loop(0, N_EXPERTS, body, 0)

    @pl.when(j == 0)
    def _():
        yc_ref[...] = jnp.zeros(yc_ref.shape, yc_ref.dtype)
        issue(0, 0)

    @pl.when(j + 1 < n_tiles)
    def _():
        issue(j + 1, (j + 1) % 2)

    slot = j % 2

    rows = pl.multiple_of(nch_ref[j] * SUBLANES, SUBLANES)
    done = yc_ref.at[slot, pl.ds(0, rows), :]
    pltpu.make_async_copy(done, done, sem.at[slot]).wait()

    gt = gt_ref[...]
    w = _pick_matrix(loc_ref[...], [gt[:, k:k + 1] for k in range(TOP_K)], yc_ref.shape[1])
    y = _mm(w, yc_ref[slot])
    mod = mod_ref[0]
    o_ref[...] = x1_ref[...] + mod[5:6] * (_rms(y) * g_ref[...])


def _combine(yb, loc, segments, nch, gt, x1, mods, seq, g_post):
    rows, d = x1.shape
    tm = ROW_TILE
    tps = seq // tm
    row = lambda w: pl.BlockSpec((tm, w), lambda i, gd, nc: (i, 0))
    grid_spec = pltpu.PrefetchScalarGridSpec(
        num_scalar_prefetch=2,
        grid=(rows // tm,),
        in_specs=[pl.BlockSpec(memory_space=pl.ANY), row(LANES), row(LANES), row(d),
                  pl.BlockSpec((1, SUBLANES, d), lambda i, gd, nc: (i // tps, 0, 0)),
                  pl.BlockSpec((1, d), lambda i, gd, nc: (0, 0))],
        out_specs=row(d),
        scratch_shapes=[pltpu.VMEM((2, _local_rows(tm), d), F32), pltpu.SemaphoreType.DMA((2,))],
    )
    return pl.pallas_call(
        _combine_kernel,
        grid_spec=grid_spec,
        out_shape=jax.ShapeDtypeStruct((rows, d), F32),
        compiler_params=_cparams("arbitrary"),
        name="combine",
    )(segments, nch, yb, loc, gt, x1, mods, g_post.reshape(1, d))


def _route_slots(tile_counts, n_tok):
    bm = MOE_ROWS
    n_tiles = tile_counts.shape[0]
    seg_len = -(-tile_counts // SUBLANES) * SUBLANES
    e_id = jnp.arange(N_EXPERTS, dtype=jnp.int32)
    j_id = jnp.arange(n_tiles, dtype=jnp.int32)
    earlier_e = e_id[:, None] < e_id[None, :]
    earlier_j = j_id[:, None] < j_id[None, :]
    seg_off = jnp.sum(jnp.where(earlier_e[None], seg_len[:, :, None], 0), axis=1)
    per_expert = jnp.sum(seg_len, axis=0)
    padded = -(-per_expert // bm) * bm
    p_end = jnp.sum(jnp.where(e_id[:, None] <= e_id[None, :], padded[:, None], 0), axis=0)
    seg_slot = ((p_end - padded)[None, :]
                + jnp.sum(jnp.where(earlier_j[:, :, None], seg_len[:, None, :], 0), axis=0))
    segments = jnp.stack([seg_off, seg_slot, seg_len], axis=-1).reshape(-1).astype(jnp.int32)
    tails = jnp.stack([p_end - padded + per_expert, padded - per_expert], axis=-1).reshape(-1)
    tails = jnp.concatenate([tails, p_end[-1:] // bm]).astype(jnp.int32)
    n_chunks = jnp.sum(seg_len, axis=1) // SUBLANES
    starts = jnp.arange(_moe_blocks(n_tok, n_tiles), dtype=jnp.int32) * bm
    block_exp = jnp.minimum(jnp.sum(p_end[None, :] <= starts[:, None], axis=1), N_EXPERTS - 1).astype(jnp.int32)
    n_used = (p_end[-1] // bm).astype(jnp.int32).reshape(1)
    return segments, n_chunks.astype(jnp.int32), tails, block_exp, n_used


def _moe(hf, loc, segments, n_chunks, tails, block_exp, n_used, p):
    xb = _dispatch(hf, loc, segments, n_chunks, tails)
    w_gu = _gate_up_rows(p['w_gate_up'])
    b_gu = p['b_gate_up']
    b_gu = jnp.concatenate([b_gu[..., 0::2], b_gu[..., 1::2]], axis=-1)[:, None, :]
    return _experts(xb, block_exp, n_used, w_gu, b_gu, p['w_down'], p['b_down'][:, None, :])


def _layer(x, c, ctx, c_ctx, p):
    batch, seq, d = x.shape
    ctx_len = ctx.shape[1]
    n_mod = p['w_ada'].shape[1] // d

    c_rows = jnp.zeros((2 * SUBLANES, d), F32).at[:batch].set(c).at[batch].set(c_ctx)
    mods = _ada(c_rows, p['w_ada'], p['b_ada']).reshape(2 * SUBLANES, n_mod, d)
    mods = jnp.pad(mods, ((0, 0), (0, SUBLANES - n_mod), (0, 0)))

    w_in = p['w_in'].astype(BF16)
    w_att, w_rwkv, w_gate = (w_in[:, :ATT_COLS], w_in[:, ATT_COLS:ATT_COLS + RWKV_COLS],
                             w_in[:, ATT_COLS + RWKV_COLS:])
    b_in = p['b_in'].reshape(1, -1)
    b_att, b_rwkv, b_gate = (b_in[:, :ATT_COLS], b_in[:, ATT_COLS:ATT_COLS + RWKV_COLS],
                             b_in[:, ATT_COLS + RWKV_COLS:])
    g_pre = p['g_pre_mix'].reshape(1, d)
    proj = functools.partial(_project, g_pre=g_pre, w_att=w_att, w_rwkv=w_rwkv, w_gate=w_gate,
                             b_att=b_att, b_rwkv=b_rwkv, b_gate=b_gate, feat_params=_feature_params(p))

    zc_att, _, fc = proj(ctx.reshape(batch * ctx_len, d), mods, lambda b: batch, ctx_len, rope=False)
    zero_state = jnp.zeros((batch, RWKV_HEADS, HEAD_DIM, HEAD_DIM), F32)
    _, _, sc_f, sc_b = _scan(fc, batch, ctx_len, zero_state, zero_state)

    x2 = x.reshape(batch * seq, d)
    z_att, z_gate, fx = proj(x2, mods, lambda b: b, seq, rope=True)
    y_f, y_b, _, _ = _scan(fx, batch, seq, sc_f, sc_b)
    att = _attention(z_att, zc_att, p['att_sinks'], batch, seq)
    x1, hf, gt, loc, cnt = _merge(y_f, y_b, fx, att, z_gate, x2, mods, seq, p)

    segments, n_chunks, tails, block_exp, n_used = _route_slots(cnt[:, 0, :N_EXPERTS], batch * seq)
    yb = _moe(hf, loc, segments, n_chunks, tails, block_exp, n_used, p)
    out = _combine(yb, loc, segments, n_chunks, gt, x1, mods, seq, p['g_post_ffn'])
    return out.reshape(batch, seq, d)


def kernel(x, c, ctx, c_ctx, w_ada, b_ada, g_pre_mix, g_post_mix, g_pre_ffn, g_post_ffn, w_in, b_in, mu_prev, mu_next, att_sinks, w0_f, w0_b, w2_f, w2_b, a0_f, a0_b, a2_f, a2_b, g2, k_k, k_a, r_k, ln_x_w, ln_x_b, w_up_att, w_up_rwkv, w_out, w_router, b_router, w_gate_up, b_gate_up, w_down, b_down):
    assert w_ada.shape[0] == 1, "single-layer problem: the context stream update is never consumed"
    p = dict(w_ada=w_ada[0], b_ada=b_ada[0], g_pre_mix=g_pre_mix[0], g_post_mix=g_post_mix[0],
             g_pre_ffn=g_pre_ffn[0], g_post_ffn=g_post_ffn[0], w_in=w_in[0], b_in=b_in[0],
             mu_prev=mu_prev[0], mu_next=mu_next[0], att_sinks=att_sinks[0], w0_f=w0_f[0], w0_b=w0_b[0],
             w2_f=w2_f[0], w2_b=w2_b[0], a0_f=a0_f[0], a0_b=a0_b[0], a2_f=a2_f[0], a2_b=a2_b[0], g2=g2[0],
             k_k=k_k[0], k_a=k_a[0], r_k=r_k[0].reshape(-1), ln_x_w=ln_x_w[0], ln_x_b=ln_x_b[0],
             w_up_att=w_up_att[0], w_up_rwkv=w_up_rwkv[0], w_out=w_out[0], w_router=w_router[0],
             b_router=b_router[0], w_gate_up=w_gate_up[0], b_gate_up=b_gate_up[0], w_down=w_down[0],
             b_down=b_down[0])
    return _layer(x, c, ctx, c_ctx, p)
```

```python
import functools

import jax
import jax.numpy as jnp
import numpy as np
from jax import lax
from jax.experimental import pallas as pl
from jax.experimental.pallas import tpu as pltpu

F32 = jnp.float32
BF16 = jnp.bfloat16

GRID_W = 64
HEAD_DIM = 64
ATT_HEADS = 8
ATT_KV_HEADS = 2
ATT_GROUPS = ATT_HEADS // ATT_KV_HEADS
ATT_DIM = ATT_HEADS * HEAD_DIM
KV_DIM = ATT_KV_HEADS * HEAD_DIM
WINDOW = 128
BLOCK = 128
ROPE_BASE = 10000.0
ATT_SCALE = HEAD_DIM ** -0.5
NEG_INF = -1e30
RWKV_HEADS = 8
RWKV_DIM = RWKV_HEADS * HEAD_DIM
DECAY_LORA = 64
ICL_LORA = 64
GATE_LORA = 128
GN_EPS = 64e-5
N_EXPERTS = 32
TOP_K = 4
SWIGLU_LIMIT = 7.0
SWIGLU_ALPHA = 1.702
RMS_EPS = 1e-6
ATT_COLS = ATT_DIM + 2 * KV_DIM
RWKV_COLS = 3 * RWKV_DIM + 2 * DECAY_LORA + 2 * ICL_LORA + GATE_LORA
ROPE_COLS = ATT_DIM + KV_DIM

LANES = 128
SUBLANES = 8
VMEM_LIMIT = 48 * 1024 * 1024

ROW_TILE = 256
SCAN_CHUNK = 64
SCAN_GROUP = 4
SCAN_BATCH = 4
MOE_ROWS = 512


def _cparams(*sem):
    return pltpu.CompilerParams(dimension_semantics=sem, vmem_limit_bytes=VMEM_LIMIT)


def _mm(a, b):
    return jnp.dot(a.astype(BF16), b.astype(BF16), preferred_element_type=F32)


def _mm_nt(a, b):
    return lax.dot_general(a.astype(BF16), b.astype(BF16), (((1,), (1,)), ((), ())),
                           preferred_element_type=F32)


def _split2(x):
    hi = x.astype(BF16)
    lo = (x - hi.astype(F32)).astype(BF16)
    return hi, lo


def _split3(x):
    hi = x.astype(BF16)
    r1 = x - hi.astype(F32)
    mid = r1.astype(BF16)
    lo = (r1 - mid.astype(F32)).astype(BF16)
    return hi, mid, lo


def _mm_f32(a, b):
    ah, al = _split2(a)
    bh, bl = _split2(b)
    d = functools.partial(jnp.dot, preferred_element_type=F32)
    return d(ah, bh) + d(ah, bl) + d(al, bh)


def _seg_sum(x, eseg):
    hi, lo = _split2(x)
    d = functools.partial(jnp.dot, preferred_element_type=F32)
    return d(hi, eseg) + d(lo, eseg)


def _rms(x):
    return x * lax.rsqrt(jnp.mean(x * x, axis=-1, keepdims=True) + RMS_EPS)


def _sigmoid(x):
    return 0.5 * jnp.tanh(0.5 * x) + 0.5


def _ada_kernel(c_ref, w_ref, b_ref, o_ref):
    c = c_ref[...]
    o_ref[...] = _mm_f32(c * _sigmoid(c), w_ref[...]) + b_ref[...]


def _ada(c_rows, w, b):
    m, d = c_rows.shape
    n = w.shape[1]
    tn = 1536
    return pl.pallas_call(
        _ada_kernel,
        grid=(n // tn,),
        in_specs=[pl.BlockSpec((m, d), lambda j: (0, 0)),
                  pl.BlockSpec((d, tn), lambda j: (0, j)),
                  pl.BlockSpec((1, tn), lambda j: (0, j))],
        out_specs=pl.BlockSpec((m, tn), lambda j: (0, j)),
        out_shape=jax.ShapeDtypeStruct((m, n), F32),
        compiler_params=_cparams("arbitrary"),
        name="ada",
    )(c_rows, w, b.reshape(1, n))


def _proj_kernel(*refs, rope, tiles_per_seq):
    n_rope = 2 if rope else 0
    (x_ref, xp_ref, xn_ref, mod_ref, g_ref, wa_ref, wr_ref, wg_ref, ba_ref, br_ref, bg_ref) = refs[:11]
    cos_ref, sin_ref = refs[11:11 + n_rope] if rope else (None, None)
    (mup_ref, mun_ref, w2_ref, w0_ref, a2_ref, a0_ref, g2_ref, kk_ref, ka_ref, eseg_ref) = refs[11 + n_rope:21 + n_rope]
    za_ref, zg_ref = refs[21 + n_rope:23 + n_rope]
    r_o, v_o, kf_o, kb_o, lwf_o, lwb_o, kkn_o, bf_o, bb_o, gate_o = refs[23 + n_rope:]
    mod = mod_ref[0]
    modulated = lambda xv: _rms(xv) * g_ref[...] * (1.0 + mod[1:2]) + mod[0:1]
    h = modulated(x_ref[...])
    hb = h.astype(BF16)
    za = jnp.dot(hb, wa_ref[...], preferred_element_type=F32) + ba_ref[...]
    if rope:
        qk = za[:, :ROPE_COLS]
        lane = lax.broadcasted_iota(jnp.int32, qk.shape, 1)
        low = (lane % 32) < 16
        partner = jnp.where(low, pltpu.roll(qk, ROPE_COLS - 16, 1), pltpu.roll(qk, 16, 1))
        cos = jnp.concatenate([cos_ref[...]] * (ROPE_COLS // LANES), axis=1)
        sin = jnp.concatenate([sin_ref[...]] * (ROPE_COLS // LANES), axis=1)
        za_ref[:, :ROPE_COLS] = (qk * cos + partner * sin).astype(za_ref.dtype)
        za_ref[:, ROPE_COLS:] = za[:, ROPE_COLS:].astype(za_ref.dtype)
    else:
        za_ref[...] = za.astype(za_ref.dtype)
    zg_ref[...] = (jnp.dot(hb, wg_ref[...], preferred_element_type=F32) + bg_ref[...]).astype(zg_ref.dtype)

    tm = x_ref.shape[0]
    h_ext = jnp.concatenate([modulated(xp_ref[...]), h, modulated(xn_ref[...])], axis=0).astype(BF16)
    z_ext = jnp.dot(h_ext, wr_ref[...], preferred_element_type=F32) + br_ref[...]
    z = z_ext[SUBLANES:SUBLANES + tm]
    ti = pl.program_id(0) % tiles_per_seq
    row = lax.broadcasted_iota(jnp.int32, (tm, 1), 0)
    prev_halo = jnp.where(ti == 0, 0.0, z_ext[SUBLANES - 1:SUBLANES])
    next_halo = jnp.where(ti == tiles_per_seq - 1, 0.0, z_ext[SUBLANES + tm:SUBLANES + tm + 1])
    prev = jnp.where(row == 0, prev_halo, pltpu.roll(z, 1, 0))
    nxt = jnp.where(row == tm - 1, next_halo, pltpu.roll(z, tm - 1, 0))
    zs = z + mup_ref[...] * (prev - z) + mun_ref[...] * (nxt - z)

    d = RWKV_DIM
    r = zs[:, 0:d]
    k = zs[:, d:2 * d]
    v = zs[:, 2 * d:3 * d]
    o = 3 * d
    wl = zs[:, o:o + 2 * DECAY_LORA]
    al = zs[:, o + 2 * DECAY_LORA:o + 2 * DECAY_LORA + 2 * ICL_LORA]
    gl = zs[:, o + 2 * DECAY_LORA + 2 * ICL_LORA:]

    w = w0_ref[...] + _mm(jnp.tanh(wl), w2_ref[...])
    lw = -float(np.exp(-0.5)) * _sigmoid(w)
    icl = _sigmoid(a0_ref[...] + _mm(al, a2_ref[...]))
    kk0 = k * kk_ref[...]
    ss = _seg_sum(kk0 * kk0, eseg_ref[...])
    kk = kk0 / jnp.maximum(jnp.sqrt(ss), 1e-12)
    ka = ka_ref[...]
    icl_f = icl[:, :d]
    icl_b = icl[:, d:]

    r_o[...] = r.astype(r_o.dtype)
    v_o[...] = v.astype(v_o.dtype)
    kf_o[...] = (k * (1.0 + (icl_f - 1.0) * ka)).astype(kf_o.dtype)
    kb_o[...] = (k * (1.0 + (icl_b - 1.0) * ka)).astype(kb_o.dtype)
    lwf_o[...] = lw[:, :d]
    lwb_o[...] = lw[:, d:]
    kkn_o[...] = kk.astype(kkn_o.dtype)
    bf_o[...] = (kk * icl_f).astype(bf_o.dtype)
    bb_o[...] = (kk * icl_b).astype(bb_o.dtype)
    gate_o[...] = _mm(_sigmoid(gl), g2_ref[...]).astype(gate_o.dtype)


def _rope_tables(seq):
    n_rows = seq // GRID_W
    row = jnp.repeat(jnp.arange(n_rows, dtype=F32), GRID_W, total_repeat_length=seq)
    col = jnp.tile(jnp.arange(GRID_W, dtype=F32), n_rows)
    half = HEAD_DIM // 2
    inv_freq = ROPE_BASE ** (-jnp.arange(0, half, 2, dtype=F32) / half)
    ang_r = row[:, None] * inv_freq[None, :]
    ang_c = col[:, None] * inv_freq[None, :]
    cos_h = jnp.concatenate([jnp.cos(ang_r), jnp.cos(ang_r), jnp.cos(ang_c), jnp.cos(ang_c)], axis=1)
    sin_h = jnp.concatenate([-jnp.sin(ang_r), jnp.sin(ang_r), -jnp.sin(ang_c), jnp.sin(ang_c)], axis=1)
    reps = LANES // HEAD_DIM
    return jnp.tile(cos_h, (1, reps)), jnp.tile(sin_h, (1, reps))


def _project(x2, mods, mod_row, seq, g_pre, w_att, w_rwkv, w_gate, b_att, b_rwkv, b_gate, feat_params, rope):
    rows, d = x2.shape
    tm = ROW_TILE
    tps = seq // tm
    hb = tm // SUBLANES
    nhb = rows // SUBLANES
    rd = RWKV_DIM
    const = lambda i: (0, 0)
    in_specs = [pl.BlockSpec((tm, d), lambda i: (i, 0)),
                pl.BlockSpec((SUBLANES, d), lambda i: (jnp.maximum(i * hb - 1, 0), 0)),
                pl.BlockSpec((SUBLANES, d), lambda i: (jnp.minimum((i + 1) * hb, nhb - 1), 0)),
                pl.BlockSpec((1, SUBLANES, d), lambda i: (mod_row(i // tps), 0, 0)),
                pl.BlockSpec((1, d), const),
                pl.BlockSpec(w_att.shape, const), pl.BlockSpec(w_rwkv.shape, const),
                pl.BlockSpec(w_gate.shape, const),
                pl.BlockSpec((1, ATT_COLS), const), pl.BlockSpec((1, RWKV_COLS), const),
                pl.BlockSpec((1, w_gate.shape[1]), const)]
    args = [x2, x2, x2, mods, g_pre, w_att, w_rwkv, w_gate, b_att, b_rwkv, b_gate]
    if rope:
        cos, sin = _rope_tables(seq)
        in_specs += [pl.BlockSpec((tm, LANES), lambda i: (i % tps, 0))] * 2
        args += [cos, sin]
    in_specs += [pl.BlockSpec(a.shape, const) for a in feat_params]
    args += feat_params
    feat_spec = pl.BlockSpec((tm, rd), lambda i: (i, 0))
    outs = pl.pallas_call(
        functools.partial(_proj_kernel, rope=rope, tiles_per_seq=tps),
        grid=(rows // tm,),
        in_specs=in_specs,
        out_specs=[pl.BlockSpec((tm, ATT_COLS), lambda i: (i, 0)),
                   pl.BlockSpec((tm, w_gate.shape[1]), lambda i: (i, 0))] + [feat_spec] * len(FEAT_NAMES),
        out_shape=[jax.ShapeDtypeStruct((rows, ATT_COLS), BF16),
                   jax.ShapeDtypeStruct((rows, w_gate.shape[1]), BF16)]
                  + [jax.ShapeDtypeStruct((rows, rd), BF16 if name in FEAT_BF16 else F32) for name in FEAT_NAMES],
        compiler_params=_cparams("parallel"),
        name="proj",
    )(*args)
    return outs[0], outs[1], dict(zip(FEAT_NAMES, outs[2:]))


FEAT_NAMES = ('r', 'v', 'kf', 'kb', 'lwf', 'lwb', 'kk', 'bf', 'bb', 'gate')
FEAT_BF16 = ('r', 'v', 'kf', 'kb', 'kk', 'bf', 'bb', 'gate')


def _block_diag2(a, b):
    za = jnp.zeros_like(a)
    zb = jnp.zeros_like(b)
    return jnp.concatenate([jnp.concatenate([a, zb], axis=1), jnp.concatenate([za, b], axis=1)], axis=0)


def _seg_matrix():
    h = np.arange(RWKV_DIM) // HEAD_DIM
    return jnp.asarray((h[:, None] == h[None, :]).astype(np.float32), BF16)


def _feature_params(p):
    d = RWKV_DIM
    return [p['mu_prev'].reshape(1, -1), p['mu_next'].reshape(1, -1),
            _block_diag2(p['w2_f'], p['w2_b']).astype(BF16),
            jnp.concatenate([p['w0_f'], p['w0_b']]).reshape(1, 2 * d),
            _block_diag2(p['a2_f'], p['a2_b']).astype(BF16),
            jnp.concatenate([p['a0_f'], p['a0_b']]).reshape(1, 2 * d),
            p['g2'].astype(BF16), p['k_k'].reshape(1, d), p['k_a'].reshape(1, d), _seg_matrix()]


def _scan_kernel(rf, vf, kf, lwf, kkf, bf, rb, vb, kb, lwb, kkb, bb, s0f_ref, s0b_ref,
                 yf_ref, yb_ref, sTf_ref, sTb_ref, state_ref, *, chunk, n_chunks):
    c = pl.program_id(1)
    C = chunk
    hd = HEAD_DIM
    gw = SCAN_GROUP * hd
    n_groups = RWKV_HEADS // SCAN_GROUP
    assert C == hd, "the triangular masks below are shared between time and channel blocks"

    n_rows = state_ref.shape[0]

    @pl.when(c == 0)
    def _():
        for n in range(n_rows):
            for d, s0_ref in enumerate((s0f_ref, s0b_ref)):
                for g in range(n_groups):
                    state_ref[n, d, g] = jnp.zeros((gw, gw), F32)
                    for j in range(SCAN_GROUP):
                        state_ref[n, d, g, j * hd:(j + 1) * hd, j * hd:(j + 1) * hd] = (
                            s0_ref[n, g * SCAN_GROUP + j])

    ti = lax.broadcasted_iota(jnp.int32, (C, C), 0)
    si = lax.broadcasted_iota(jnp.int32, (C, C), 1)
    tg = lax.broadcasted_iota(jnp.int32, (C, gw), 0)
    sg = lax.broadcasted_iota(jnp.int32, (C, gw), 1) % C
    eye = (tg == sg).astype(F32)
    same_head = (lax.broadcasted_iota(jnp.int32, (gw, gw), 0) // hd
                 == lax.broadcasted_iota(jnp.int32, (gw, gw), 1) // hd)
    same_head_b = same_head.astype(BF16)
    n_double = int(np.log2(C)) - 1

    def bdiag(x_cat):
        return jnp.concatenate([x_cat.astype(BF16)] * SCAN_GROUP, axis=0) * same_head_b

    dirs = ((rf, vf, kf, lwf, kkf, bf), (rb, vb, kb, lwb, kkb, bb))
    units = []
    masks = [(((si <= ti) if d == 0 else (si >= ti)).astype(BF16),
              (sg <= tg) if d == 0 else (sg >= tg),
              (sg < tg) if d == 0 else (sg > tg)) for d in range(2)]
    for n, d in [(n, d) for n in range(n_rows) for d in range(2)]:
        r_ref, v_ref, k_ref, lw_ref, kk_ref, b_ref = (ref.at[n] for ref in dirs[d])
        tri, incl_g, strict_g = masks[d]
        lw = lw_ref[...]
        lh, lm, ll = _split3(lw)
        dd = functools.partial(jnp.dot, preferred_element_type=F32)
        cum = dd(tri, lh) + dd(tri, lm) + dd(tri, ll)
        cumx = cum - lw
        cum_end = cum[C - 1:C] if d == 0 else cum[0:1]
        e_neg = jnp.exp(-cum)
        e_end = jnp.exp(cum_end - cum)
        bv = b_ref[...].astype(F32)
        kv = k_ref[...].astype(F32)
        vv = v_ref[...].astype(F32)
        a_t = (-kk_ref[...].astype(F32) * jnp.exp(cumx)).astype(BF16)
        r_t = (r_ref[...].astype(F32) * jnp.exp(cum)).astype(BF16)
        b_t = (bv * e_neg).astype(BF16)
        k_t = (kv * e_neg).astype(BF16)
        b_q = (bv * e_end).astype(BF16)
        k_q = (kv * e_end).astype(BF16)
        g_end = jnp.exp(cum_end)
        for g in range(n_groups):
            sl = slice(g * gw, (g + 1) * gw)
            units.append(dict(
                n=n, d=d, g=g, sl=sl, incl=incl_g, strict=strict_g,
                P=jnp.concatenate([a_t[:, sl], r_t[:, sl]], axis=0),
                Q=jnp.concatenate([bdiag(b_t[:, sl]), bdiag(k_t[:, sl])], axis=0),
                Qq=jnp.concatenate([b_q[:, sl], k_q[:, sl]], axis=0),
                V=vv[:, sl], g_end=g_end[:, sl]))

    for u in units:
        u['G'] = _mm_nt(u['P'], u['Q'])
    for u in units:
        u['S0'] = state_ref[u['n'], u['d'], u['g']]
        u['PH'] = _mm_nt(u['P'], u['S0'])
        u['Vd'] = bdiag(u['V'])
    for u in units:
        G = u.pop('G')
        u['N'] = jnp.where(u['strict'], G[:C, :gw], 0.0)
        u['a_ak'] = jnp.where(u['strict'], G[:C, gw:], 0.0)
        u['a_rb'] = jnp.where(u['incl'], G[C:, :gw], 0.0)
        u['a_rk'] = jnp.where(u['incl'], G[C:, gw:], 0.0)
    for u in units:
        u['T'] = eye + u['N']
        u['Pw'] = _mm(u['N'], bdiag(u['N']))
        av = _mm(jnp.concatenate([u['a_ak'], u['a_rk']], axis=0), u['Vd'])
        u['rhs'] = u['PH'][:C] + av[:C]
        u['y0'] = u['PH'][C:] + av[C:]
    for lvl in range(n_double):
        for u in units:
            pw = bdiag(u['Pw'])
            if lvl + 1 < n_double:
                both = _mm(jnp.concatenate([u['T'], u['Pw']], axis=0), pw)
                u['T'] = u['T'] + both[:C]
                u['Pw'] = both[C:]
            else:
                u['T'] = u['T'] + _mm(u['T'], pw)
    for u in units:
        u['U'] = _mm(u['T'], bdiag(u['rhs']))
    y_refs = (yf_ref, yb_ref)
    for u in units:
        y_refs[u['d']][u['n'], :, u['sl']] = u['y0'] + _mm(u['a_rb'], bdiag(u['U']))
        uv = jnp.concatenate([u['U'], u['V']], axis=0)
        state_ref[u['n'], u['d'], u['g']] = jnp.where(
            same_head, u['S0'] * u['g_end'] + _mm(uv.T, u['Qq']), 0.0)

    @pl.when(c == n_chunks - 1)
    def _():
        for n in range(n_rows):
            for d, sT_ref in enumerate((sTf_ref, sTb_ref)):
                for g in range(n_groups):
                    for j in range(SCAN_GROUP):
                        sT_ref[n, g * SCAN_GROUP + j] = (
                            state_ref[n, d, g, j * hd:(j + 1) * hd, j * hd:(j + 1) * hd])


def _scan(f, batch, seq, s0_f, s0_b):
    C = SCAN_CHUNK
    nC = seq // C
    d = RWKV_DIM
    nb = max(n for n in range(1, SCAN_BATCH + 1) if batch % n == 0)
    fwd = pl.BlockSpec((nb, C, d), lambda b, c: (b, c, 0))
    bwd = pl.BlockSpec((nb, C, d), lambda b, c: (b, nC - 1 - c, 0))
    st = pl.BlockSpec((nb, RWKV_HEADS, HEAD_DIM, HEAD_DIM), lambda b, c: (b, 0, 0, 0))
    st_shape = jax.ShapeDtypeStruct((batch, RWKV_HEADS, HEAD_DIM, HEAD_DIM), F32)
    y_shape = jax.ShapeDtypeStruct((batch, seq, d), F32)
    gw = SCAN_GROUP * HEAD_DIM
    arr = lambda name: f[name].reshape(batch, seq, d)
    y_f, y_b, sT_f, sT_b = pl.pallas_call(
        functools.partial(_scan_kernel, chunk=C, n_chunks=nC),
        grid=(batch // nb, nC),
        in_specs=[fwd] * 6 + [bwd] * 6 + [st, st],
        out_specs=[fwd, bwd, st, st],
        out_shape=[y_shape, y_shape, st_shape, st_shape],
        scratch_shapes=[pltpu.VMEM((nb, 2, RWKV_HEADS // SCAN_GROUP, gw, gw), F32)],
        compiler_params=_cparams("parallel", "arbitrary"),
        name="scan",
    )(arr('r'), arr('v'), arr('kf'), arr('lwf'), arr('kk'), arr('bf'),
      arr('r'), arr('v'), arr('kb'), arr('lwb'), arr('kk'), arr('bb'), s0_f, s0_b)
    return y_f.reshape(batch * seq, d), y_b.reshape(batch * seq, d), sT_f, sT_b


def _attn_kernel(sink_ref, q_ref, kp_ref, km_ref, kn_ref, vp_ref, vm_ref, vn_ref, kc_ref, vc_ref, o_ref,
                 *, n_blocks):
    n = pl.program_id(1)
    rows = ATT_GROUPS * BLOCK
    q = q_ref[...] * ATT_SCALE
    qi = lax.broadcasted_iota(jnp.int32, (rows, BLOCK), 0) % BLOCK
    kj = lax.broadcasted_iota(jnp.int32, (rows, BLOCK), 1)
    ok_prev = (kj >= qi) & (n > 0)
    ok_next = (kj <= qi) & (n < n_blocks - 1)
    rowh = lax.broadcasted_iota(jnp.int32, (rows, 1), 0) // BLOCK
    outs = [None] * ATT_HEADS
    for g in range(ATT_KV_HEADS):
        ks = slice(g * HEAD_DIM, (g + 1) * HEAD_DIM)
        heads = [g * ATT_GROUPS + j for j in range(ATT_GROUPS)]
        Qs = jnp.concatenate([q[:, h * HEAD_DIM:(h + 1) * HEAD_DIM] for h in heads], axis=0)
        scores = [jnp.where(ok_prev, _mm_nt(Qs, kp_ref[:, ks]), NEG_INF),
                  _mm_nt(Qs, km_ref[:, ks]),
                  jnp.where(ok_next, _mm_nt(Qs, kn_ref[:, ks]), NEG_INF),
                  _mm_nt(Qs, kc_ref[:, ks])]
        values = [vp_ref, vm_ref, vn_ref, vc_ref]
        sink = jnp.zeros((rows, 1), F32)
        for j, h in enumerate(heads):
            sink = jnp.where(rowh == j, sink_ref[h], sink)
        folded = None
        for sc in scores:
            for c0 in range(0, sc.shape[1], BLOCK):
                blk = sc[:, c0:c0 + BLOCK]
                folded = blk if folded is None else jnp.maximum(folded, blk)
        m = jnp.maximum(sink, jnp.max(folded, axis=-1, keepdims=True))
        acc = jnp.zeros((rows, 2 * HEAD_DIM), F32)
        for sc, v_ref in zip(scores, values):
            one_col = (lax.broadcasted_iota(jnp.int32, (v_ref.shape[0], HEAD_DIM), 1) == 0).astype(BF16)
            v_ext = jnp.concatenate([v_ref[:, ks], one_col], axis=1)
            acc = acc + jnp.dot(jnp.exp((sc - m).astype(BF16)), v_ext, preferred_element_type=F32)
        den = acc[:, HEAD_DIM:HEAD_DIM + 1] + jnp.exp(sink - m)
        O = acc[:, :HEAD_DIM] / den
        for j, h in enumerate(heads):
            outs[h] = O[j * BLOCK:(j + 1) * BLOCK]
    o_ref[...] = jnp.concatenate(outs, axis=1).astype(o_ref.dtype)


def _attention(z_att, zc_att, sinks, batch, seq):
    nb = seq // BLOCK
    ctx_len = zc_att.shape[0] // batch
    kcol = ATT_DIM // KV_DIM
    vcol = kcol + 1

    def kv_spec(col, off):
        return pl.BlockSpec((BLOCK, KV_DIM), lambda b, n: (b * nb + jnp.clip(n + off, 0, nb - 1), col))

    return pl.pallas_call(
        functools.partial(_attn_kernel, n_blocks=nb),
        grid=(batch, nb),
        in_specs=[pl.BlockSpec(memory_space=pltpu.SMEM),
                  pl.BlockSpec((BLOCK, ATT_DIM), lambda b, n: (b * nb + n, 0)),
                  kv_spec(kcol, -1), kv_spec(kcol, 0), kv_spec(kcol, 1),
                  kv_spec(vcol, -1), kv_spec(vcol, 0), kv_spec(vcol, 1),
                  pl.BlockSpec((ctx_len, KV_DIM), lambda b, n: (b, kcol)),
                  pl.BlockSpec((ctx_len, KV_DIM), lambda b, n: (b, vcol))],
        out_specs=pl.BlockSpec((BLOCK, ATT_DIM), lambda b, n: (b * nb + n, 0)),
        out_shape=jax.ShapeDtypeStruct((batch * seq, ATT_DIM), BF16),
        compiler_params=_cparams("parallel", "parallel"),
        name="attn",
    )(sinks, z_att, z_att, z_att, z_att, z_att, z_att, z_att, zc_att, zc_att)


def _merge_kernel(yf_ref, yb_ref, r_ref, kf_ref, kb_ref, v_ref, gate_ref, att_ref, zg_ref, x_ref, mod_ref,
                  lnw_ref, lnb_ref, rk_ref, eseg_ref, wua_ref, wur_ref, wo_ref, gpm_ref, gpf_ref,
                  wr_ref, br_ref,
                  x1_ref, hf_ref, gt_ref, loc_ref, cnt_ref):
    eseg = eseg_ref[...]
    inv_n = 1.0 / HEAD_DIM
    y = yf_ref[...] + yb_ref[...]
    mean = _seg_sum(y, eseg) * inv_n
    dy = y - mean
    var = _seg_sum(dy * dy, eseg) * inv_n
    yn = dy * lax.rsqrt(var + GN_EPS) * lnw_ref[...] + lnb_ref[...]
    f32 = lambda ref: ref[...].astype(F32)
    bonus = _seg_sum(f32(r_ref) * (f32(kf_ref) + f32(kb_ref)) * rk_ref[...], eseg) * f32(v_ref)
    rwk = (yn + bonus) * f32(gate_ref)

    d = x_ref.shape[1]
    zg = zg_ref[...].astype(F32)
    merged = (_sigmoid(zg[:, :d]) * jnp.dot(att_ref[...], wua_ref[...], preferred_element_type=F32)
              + _sigmoid(zg[:, d:]) * _mm(rwk, wur_ref[...]))
    mix = _mm(merged, wo_ref[...])
    mod = mod_ref[0]
    x1 = x_ref[...] + mod[2:3] * (_rms(mix) * gpm_ref[...])
    x1_ref[...] = x1
    hf = _rms(x1) * gpf_ref[...] * (1.0 + mod[4:5]) + mod[3:4]
    hf_ref[...] = hf.astype(hf_ref.dtype)

    logits = _mm_f32(hf, wr_ref[...]) + br_ref[...]
    lane = lax.broadcasted_iota(jnp.int32, logits.shape, 1).astype(F32)
    val_out = jnp.zeros(logits.shape, F32)
    picked = jnp.zeros(logits.shape, F32)
    sels = []
    top = None
    den = None
    for kth in range(TOP_K):
        m = jnp.max(logits, axis=-1, keepdims=True)
        sel = jnp.min(jnp.where(logits == m, lane, float(LANES)), axis=-1, keepdims=True)
        hit = lane == sel
        logits = jnp.where(hit, -jnp.inf, logits)
        picked = jnp.where(hit, 1.0, picked)
        sels.append(hit)
        if kth == 0:
            top = m
        e = jnp.exp(m - top)
        den = e if kth == 0 else den + e
        val_out = jnp.where(lane == float(kth), e, val_out)
    gt_ref[...] = val_out / den

    tm = picked.shape[0]
    earlier = (lax.broadcasted_iota(jnp.int32, (tm, tm), 1)
               < lax.broadcasted_iota(jnp.int32, (tm, tm), 0)).astype(BF16)
    before = jnp.dot(earlier, picked.astype(BF16), preferred_element_type=F32)
    counts = jnp.sum(picked, axis=0, keepdims=True)
    seg_len = jnp.floor((counts + (SUBLANES - 1)) * (1.0 / SUBLANES)) * SUBLANES
    lower_expert = (lax.broadcasted_iota(jnp.int32, (LANES, LANES), 0)
                    < lax.broadcasted_iota(jnp.int32, (LANES, LANES), 1)).astype(BF16)
    seg_off = jnp.dot(jnp.broadcast_to(seg_len, (SUBLANES, LANES)).astype(BF16), lower_expert,
                      preferred_element_type=F32)[0:1]
    loc_out = jnp.zeros(logits.shape, F32)
    for kth, hit in enumerate(sels):
        row = jnp.sum(jnp.where(hit, before + seg_off, 0.0), axis=-1, keepdims=True)
        loc_out = jnp.where(lane == float(kth), row, loc_out)
    loc_ref[...] = loc_out.astype(jnp.int32)
    cnt_ref[0] = jnp.broadcast_to(counts, cnt_ref.shape[1:]).astype(jnp.int32)


def _moe_blocks(n_tok, n_tiles):
    return -(-(n_tok * TOP_K + (SUBLANES - 1) * N_EXPERTS * n_tiles) // MOE_ROWS) + N_EXPERTS


def _merge(y_f, y_b, f, att, z_gate, x2, mods, seq, p):
    rows, d = x2.shape
    tm = ROW_TILE
    tps = seq // tm
    rd = RWKV_DIM
    const = lambda i: (0, 0)
    row = lambda w: pl.BlockSpec((tm, w), lambda i: (i, 0))
    vec = lambda w: pl.BlockSpec((1, w), const)
    w_router = jnp.pad(p['w_router'], ((0, 0), (0, LANES - N_EXPERTS)))
    b_router = jnp.pad(p['b_router'], (0, LANES - N_EXPERTS), constant_values=NEG_INF).reshape(1, LANES)
    return pl.pallas_call(
        _merge_kernel,
        grid=(rows // tm,),
        in_specs=[row(rd)] * 7 + [row(ATT_DIM), row(2 * d), row(d),
                  pl.BlockSpec((1, SUBLANES, d), lambda i: (i // tps, 0, 0)),
                  vec(rd), vec(rd), vec(rd), pl.BlockSpec((rd, rd), const),
                  pl.BlockSpec((ATT_DIM, d), const), pl.BlockSpec((rd, d), const), pl.BlockSpec((d, d), const),
                  vec(d), vec(d), pl.BlockSpec((d, LANES), const), vec(LANES)],
        out_specs=[row(d), row(d), row(LANES), row(LANES),
                   pl.BlockSpec((1, SUBLANES, LANES), lambda i: (i, 0, 0))],
        out_shape=[jax.ShapeDtypeStruct((rows, d), F32), jax.ShapeDtypeStruct((rows, d), BF16),
                   jax.ShapeDtypeStruct((rows, LANES), F32),
                   jax.ShapeDtypeStruct((rows, LANES), jnp.int32),
                   jax.ShapeDtypeStruct((rows // tm, SUBLANES, LANES), jnp.int32)],
        compiler_params=_cparams("parallel"),
        name="merge",
    )(y_f, y_b, f['r'], f['kf'], f['kb'], f['v'], f['gate'], att, z_gate, x2, mods,
      p['ln_x_w'].reshape(1, rd), p['ln_x_b'].reshape(1, rd), p['r_k'].reshape(1, rd), _seg_matrix(),
      p['w_up_att'].astype(BF16), p['w_up_rwkv'].astype(BF16), p['w_out'].astype(BF16),
      p['g_post_mix'].reshape(1, d), p['g_pre_ffn'].reshape(1, d), w_router, b_router)


def _moe_kernel(be_ref, nused_ref, x_ref, wgu_ref, bgu_ref, wdn_f32_ref, bdn_ref, o_ref, wdn_ref):
    i = pl.program_id(0)

    @pl.when((i == 0) | (be_ref[i] != be_ref[jnp.maximum(i - 1, 0)]))
    def _():
        wdn_ref[...] = wdn_f32_ref[...].astype(BF16)

    @pl.when(i < nused_ref[0])
    def _():
        de = wdn_ref.shape[0]
        d = x_ref.shape[1]
        x = x_ref[...].astype(BF16)
        bgu = bgu_ref[...]
        gate = jnp.minimum(_mm_nt(x, wgu_ref[:, :d]) + bgu[:, :de], SWIGLU_LIMIT)
        up = jnp.clip(_mm_nt(x, wgu_ref[:, d:]) + bgu[:, de:], -SWIGLU_LIMIT, SWIGLU_LIMIT)
        act = (up + 1.0) * (gate * _sigmoid(SWIGLU_ALPHA * gate))
        o_ref[...] = _mm(act, wdn_ref[...]) + bdn_ref[...]

    @pl.when(i >= nused_ref[0])
    def _():
        o_ref[...] = jnp.zeros(o_ref.shape, o_ref.dtype)


def _wprep_kernel(w_ref, o_ref, t_ref):
    d = w_ref.shape[1]
    half = o_ref.shape[1]
    wt = w_ref[0].T
    for c in range(d // LANES):
        cols = slice(c * LANES, (c + 1) * LANES)
        t_ref[c] = wt[:, cols]
        o_ref[0, :, cols] = t_ref[c, pl.ds(0, half, stride=2), :].astype(o_ref.dtype)
        o_ref[0, :, d + c * LANES:d + (c + 1) * LANES] = t_ref[c, pl.ds(1, half, stride=2), :].astype(o_ref.dtype)


def _gate_up_rows(w_gu):
    n_exp, d, de2 = w_gu.shape
    tc = 4 * LANES
    return pl.pallas_call(
        _wprep_kernel,
        grid=(n_exp, de2 // tc),
        in_specs=[pl.BlockSpec((1, d, tc), lambda e, j: (e, 0, j))],
        out_specs=pl.BlockSpec((1, tc // 2, 2 * d), lambda e, j: (e, j, 0)),
        out_shape=jax.ShapeDtypeStruct((n_exp, de2 // 2, 2 * d), BF16),
        scratch_shapes=[pltpu.VMEM((d // LANES, tc, LANES), F32)],
        compiler_params=_cparams("parallel", "parallel"),
        name="wprep",
    )(w_gu)


def _experts(xb, block_exp, n_used, w_gu, b_gu, w_dn, b_dn):
    n_slots, d = xb.shape
    bm = MOE_ROWS
    n_blocks = n_slots // bm
    de = w_dn.shape[1]
    de2 = 2 * de
    grid_spec = pltpu.PrefetchScalarGridSpec(
        num_scalar_prefetch=2,
        grid=(n_blocks,),
        in_specs=[pl.BlockSpec((bm, d), lambda i, be, nu: (jnp.minimum(i, nu[0] - 1), 0)),
                  pl.BlockSpec((None, de, 2 * d), lambda i, be, nu: (be[i], 0, 0)),
                  pl.BlockSpec((None, 1, de2), lambda i, be, nu: (be[i], 0, 0)),
                  pl.BlockSpec((None, de, d), lambda i, be, nu: (be[i], 0, 0)),
                  pl.BlockSpec((None, 1, d), lambda i, be, nu: (be[i], 0, 0))],
        out_specs=pl.BlockSpec((bm, d), lambda i, be, nu: (i, 0)),
        scratch_shapes=[pltpu.VMEM((de, d), BF16)],
    )
    return pl.pallas_call(
        _moe_kernel,
        grid_spec=grid_spec,
        out_shape=jax.ShapeDtypeStruct((n_slots, d), F32),
        compiler_params=_cparams("arbitrary"),
        name="moe",
    )(block_exp, n_used, xb, w_gu, b_gu, w_dn, b_dn)


def _local_rows(tm):
    rows = tm * TOP_K + (SUBLANES - 1) * N_EXPERTS
    assert rows % SUBLANES == 0
    return rows


def _pick_matrix(loc, weights, n_cols):
    col = lax.broadcasted_iota(jnp.int32, (loc.shape[0], n_cols), 1)
    out = jnp.zeros(col.shape, F32)
    for k in range(TOP_K):
        out = jnp.where(col == loc[:, k:k + 1], weights[k], out)
    return out


def _segment(seg_ref, tile, e):
    base = (tile * N_EXPERTS + e) * 3
    return tuple(pl.multiple_of(seg_ref[base + i], SUBLANES) for i in range(3))


def _dispatch_kernel(seg_ref, nch_ref, tail_ref, hf_ref, loc_ref, xb_ref, xc_ref, zero_ref, sem, zsem):
    j = pl.program_id(0)
    n_tiles = pl.num_programs(0)
    slot = j % 2

    def tail_copy(e):
        row = pl.multiple_of(tail_ref[2 * e], SUBLANES)
        n = pl.multiple_of(tail_ref[2 * e + 1], SUBLANES)
        return n, pltpu.make_async_copy(zero_ref.at[pl.ds(0, n), :], xb_ref.at[pl.ds(row, n), :], zsem)

    def spare_block_copy(b):
        rows = xb_ref.at[pl.ds(pl.multiple_of(b * MOE_ROWS, MOE_ROWS), MOE_ROWS), :]
        return pltpu.make_async_copy(zero_ref, rows, zsem)

    first_spare = tail_ref[2 * N_EXPERTS]
    n_blocks = xb_ref.shape[0] // MOE_ROWS

    @pl.when(j == 0)
    def _():
        zero_ref[...] = jnp.zeros(zero_ref.shape, zero_ref.dtype)

        def body(e, carry):
            n, cp = tail_copy(e)

            @pl.when(n > 0)
            def _():
                cp.start()
            return carry
        lax.fori_loop(0, N_EXPERTS, body, 0)

        def spare(b, carry):
            spare_block_copy(b).start()
            return carry
        lax.fori_loop(first_spare, n_blocks, spare, 0)

    def drain(tile, s):
        rows = pl.multiple_of(nch_ref[tile] * SUBLANES, SUBLANES)
        done = xc_ref.at[s, pl.ds(0, rows), :]
        pltpu.make_async_copy(done, done, sem.at[s]).wait()

    sel = _pick_matrix(loc_ref[...], [1.0] * TOP_K, xc_ref.shape[1]).astype(BF16)
    xc_ref[slot] = lax.dot_general(sel, hf_ref[...], (((0,), (0,)), ((), ())), preferred_element_type=F32)

    for e in range(N_EXPERTS):
        off, row, n = _segment(seg_ref, j, e)

        @pl.when(n > 0)
        def _(off=off, row=row, n=n, e=e):
            pltpu.make_async_copy(xc_ref.at[slot, pl.ds(off, n), :], xb_ref.at[pl.ds(row, n), :],
                                  sem.at[slot]).start(priority=e % 2)

    @pl.when(j > 0)
    def _():
        drain(j - 1, 1 - slot)

    @pl.when(j == n_tiles - 1)
    def _():
        drain(j, slot)

        def body(e, carry):
            n, cp = tail_copy(e)

            @pl.when(n > 0)
            def _():
                cp.wait()
            return carry
        lax.fori_loop(0, N_EXPERTS, body, 0)

        def spare(b, carry):
            spare_block_copy(b).wait()
            return carry
        lax.fori_loop(first_spare, n_blocks, spare, 0)


def _dispatch(hf, loc, segments, nch, tails):
    rows, d = hf.shape
    tm = ROW_TILE
    n_slots = _moe_blocks(rows, rows // tm) * MOE_ROWS
    grid_spec = pltpu.PrefetchScalarGridSpec(
        num_scalar_prefetch=3,
        grid=(rows // tm,),
        in_specs=[pl.BlockSpec((tm, d), lambda i, sg, nc, tl: (i, 0)),
                  pl.BlockSpec((tm, LANES), lambda i, sg, nc, tl: (i, 0))],
        out_specs=pl.BlockSpec(memory_space=pl.ANY),
        scratch_shapes=[pltpu.VMEM((2, _local_rows(tm), d), F32), pltpu.VMEM((MOE_ROWS, d), F32),
                        pltpu.SemaphoreType.DMA((2,)), pltpu.SemaphoreType.DMA(())],
    )
    return pl.pallas_call(
        _dispatch_kernel,
        grid_spec=grid_spec,
        out_shape=jax.ShapeDtypeStruct((n_slots, d), F32),
        compiler_params=_cparams("arbitrary"),
        name="dispatch",
    )(segments, nch, tails, hf, loc)


def _combine_kernel(seg_ref, nch_ref, yb_ref, loc_ref, gt_ref, x1_ref, mod_ref, g_ref, o_ref, yc_ref, sem):
    j = pl.program_id(0)
    n_tiles = pl.num_programs(0)

    def issue(tile, s):
        for e in range(N_EXPERTS):
            off, row, n = _segment(seg_ref, tile, e)

            @pl.when(n > 0)
            def _(off=off, row=row, n=n, e=e):
                pltpu.make_async_copy(yb_ref.at[pl.ds(row, n), :], yc_ref.at[s, pl.ds(off, n), :],
                                      sem.at[s]).start(priority=e % 2)

    @pl.when(j == 0)
    def _():
        yc_ref[...] = jnp.zeros(yc_ref.shape, yc_ref.dtype)
        issue(0, 0)

    @pl.when(j + 1 < n_tiles)
    def _():
        issue(j + 1, (j + 1) % 2)

    slot = j % 2

    rows = pl.multiple_of(nch_ref[j] * SUBLANES, SUBLANES)
    done = yc_ref.at[slot, pl.ds(0, rows), :]
    pltpu.make_async_copy(done, done, sem.at[slot]).wait()

    gt = gt_ref[...]
    w = _pick_matrix(loc_ref[...], [gt[:, k:k + 1] for k in range(TOP_K)], yc_ref.shape[1])
    y = _mm(w, yc_ref[slot])
    mod = mod_ref[0]
    o_ref[...] = x1_ref[...] + mod[5:6] * (_rms(y) * g_ref[...])


def _combine(yb, loc, segments, nch, gt, x1, mods, seq, g_post):
    rows, d = x1.shape
    tm = ROW_TILE
    tps = seq // tm
    row = lambda w: pl.BlockSpec((tm, w), lambda i, gd, nc: (i, 0))
    grid_spec = pltpu.PrefetchScalarGridSpec(
        num_scalar_prefetch=2,
        grid=(rows // tm,),
        in_specs=[pl.BlockSpec(memory_space=pl.ANY), row(LANES), row(LANES), row(d),
                  pl.BlockSpec((1, SUBLANES, d), lambda i, gd, nc: (i // tps, 0, 0)),
                  pl.BlockSpec((1, d), lambda i, gd, nc: (0, 0))],
        out_specs=row(d),
        scratch_shapes=[pltpu.VMEM((2, _local_rows(tm), d), F32), pltpu.SemaphoreType.DMA((2,))],
    )
    return pl.pallas_call(
        _combine_kernel,
        grid_spec=grid_spec,
        out_shape=jax.ShapeDtypeStruct((rows, d), F32),
        compiler_params=_cparams("arbitrary"),
        name="combine",
    )(segments, nch, yb, loc, gt, x1, mods, g_post.reshape(1, d))


def _route_slots(tile_counts, n_tok):
    bm = MOE_ROWS
    n_tiles = tile_counts.shape[0]
    seg_len = -(-tile_counts // SUBLANES) * SUBLANES
    e_id = jnp.arange(N_EXPERTS, dtype=jnp.int32)
    j_id = jnp.arange(n_tiles, dtype=jnp.int32)
    earlier_e = e_id[:, None] < e_id[None, :]
    earlier_j = j_id[:, None] < j_id[None, :]
    seg_off = jnp.sum(jnp.where(earlier_e[None], seg_len[:, :, None], 0), axis=1)
    per_expert = jnp.sum(seg_len, axis=0)
    padded = -(-per_expert // bm) * bm
    p_end = jnp.sum(jnp.where(e_id[:, None] <= e_id[None, :], padded[:, None], 0), axis=0)
    seg_slot = ((p_end - padded)[None, :]
                + jnp.sum(jnp.where(earlier_j[:, :, None], seg_len[:, None, :], 0), axis=0))
    segments = jnp.stack([seg_off, seg_slot, seg_len], axis=-1).reshape(-1).astype(jnp.int32)
    tails = jnp.stack([p_end - padded + per_expert, padded - per_expert], axis=-1).reshape(-1)
    tails = jnp.concatenate([tails, p_end[-1:] // bm]).astype(jnp.int32)
    n_chunks = jnp.sum(seg_len, axis=1) // SUBLANES
    starts = jnp.arange(_moe_blocks(n_tok, n_tiles), dtype=jnp.int32) * bm
    block_exp = jnp.minimum(jnp.sum(p_end[None, :] <= starts[:, None], axis=1), N_EXPERTS - 1).astype(jnp.int32)
    n_used = (p_end[-1] // bm).astype(jnp.int32).reshape(1)
    return segments, n_chunks.astype(jnp.int32), tails, block_exp, n_used


def _moe(hf, loc, segments, n_chunks, tails, block_exp, n_used, p):
    xb = _dispatch(hf, loc, segments, n_chunks, tails)
    w_gu = _gate_up_rows(p['w_gate_up'])
    b_gu = p['b_gate_up']
    b_gu = jnp.concatenate([b_gu[..., 0::2], b_gu[..., 1::2]], axis=-1)[:, None, :]
    return _experts(xb, block_exp, n_used, w_gu, b_gu, p['w_down'], p['b_down'][:, None, :])


def _layer(x, c, ctx, c_ctx, p):
    batch, seq, d = x.shape
    ctx_len = ctx.shape[1]
    n_mod = p['w_ada'].shape[1] // d

    c_rows = jnp.zeros((2 * SUBLANES, d), F32).at[:batch].set(c).at[batch].set(c_ctx)
    mods = _ada(c_rows, p['w_ada'], p['b_ada']).reshape(2 * SUBLANES, n_mod, d)
    mods = jnp.pad(mods, ((0, 0), (0, SUBLANES - n_mod), (0, 0)))

    w_in = p['w_in'].astype(BF16)
    w_att, w_rwkv, w_gate = (w_in[:, :ATT_COLS], w_in[:, ATT_COLS:ATT_COLS + RWKV_COLS],
                             w_in[:, ATT_COLS + RWKV_COLS:])
    b_in = p['b_in'].reshape(1, -1)
    b_att, b_rwkv, b_gate = (b_in[:, :ATT_COLS], b_in[:, ATT_COLS:ATT_COLS + RWKV_COLS],
                             b_in[:, ATT_COLS + RWKV_COLS:])
    g_pre = p['g_pre_mix'].reshape(1, d)
    proj = functools.partial(_project, g_pre=g_pre, w_att=w_att, w_rwkv=w_rwkv, w_gate=w_gate,
                             b_att=b_att, b_rwkv=b_rwkv, b_gate=b_gate, feat_params=_feature_params(p))

    zc_att, _, fc = proj(ctx.reshape(batch * ctx_len, d), mods, lambda b: batch, ctx_len, rope=False)
    zero_state = jnp.zeros((batch, RWKV_HEADS, HEAD_DIM, HEAD_DIM), F32)
    _, _, sc_f, sc_b = _scan(fc, batch, ctx_len, zero_state, zero_state)

    x2 = x.reshape(batch * seq, d)
    z_att, z_gate, fx = proj(x2, mods, lambda b: b, seq, rope=True)
    y_f, y_b, _, _ = _scan(fx, batch, seq, sc_f, sc_b)
    att = _attention(z_att, zc_att, p['att_sinks'], batch, seq)
    x1, hf, gt, loc, cnt = _merge(y_f, y_b, fx, att, z_gate, x2, mods, seq, p)

    segments, n_chunks, tails, block_exp, n_used = _route_slots(cnt[:, 0, :N_EXPERTS], batch * seq)
    yb = _moe(hf, loc, segments, n_chunks, tails, block_exp, n_used, p)
    out = _combine(yb, loc, segments, n_chunks, gt, x1, mods, seq, p['g_post_ffn'])
    return out.reshape(batch, seq, d)


def kernel(x, c, ctx, c_ctx, w_ada, b_ada, g_pre_mix, g_post_mix, g_pre_ffn, g_post_ffn, w_in, b_in, mu_prev, mu_next, att_sinks, w0_f, w0_b, w2_f, w2_b, a0_f, a0_b, a2_f, a2_b, g2, k_k, k_a, r_k, ln_x_w, ln_x_b, w_up_att, w_up_rwkv, w_out, w_router, b_router, w_gate_up, b_gate_up, w_down, b_down):
    assert w_ada.shape[0] == 1, "single-layer problem: the context stream update is never consumed"
    p = dict(w_ada=w_ada[0], b_ada=b_ada[0], g_pre_mix=g_pre_mix[0], g_post_mix=g_post_mix[0],
             g_pre_ffn=g_pre_ffn[0], g_post_ffn=g_post_ffn[0], w_in=w_in[0], b_in=b_in[0],
             mu_prev=mu_prev[0], mu_next=mu_next[0], att_sinks=att_sinks[0], w0_f=w0_f[0], w0_b=w0_b[0],
             w2_f=w2_f[0], w2_b=w2_b[0], a0_f=a0_f[0], a0_b=a0_b[0], a2_f=a2_f[0], a2_b=a2_b[0], g2=g2[0],
             k_k=k_k[0], k_a=k_a[0], r_k=r_k[0].reshape(-1), ln_x_w=ln_x_w[0], ln_x_b=ln_x_b[0],
             w_up_att=w_up_att[0], w_up_rwkv=w_up_rwkv[0], w_out=w_out[0], w_router=w_router[0],
             b_router=b_router[0], w_gate_up=w_gate_up[0], b_gate_up=b_gate_up[0], w_down=w_down[0],
             b_down=b_down[0])
    return _layer(x, c, ctx, c_ctx, p)
```
